```python
import math
import jax, jax.numpy as jnp
from jax import lax
import numpy as np

D_MODEL = 1024
BATCH = 8
SEQ = 2048
DEPTH = 4
DEC_BATCH = 128
DEC_SEQ = 1
PAST_LEN = 8192
PAGE_SIZE = 128

N_EVEN = (DEPTH + 1) // 2
N_ODD = DEPTH // 2
EPS = 1e-6
NEG_INF = -1e30
D_FF = 2816

WINDOW = 128
BLOCK = 128
H_A = 8
KV_A = 2
G_A = H_A // KV_A
HD_A = 64
N_BUCKETS = 32
MAX_DIST = 128

H_B = 4
DK_B = 128
DV_B = 128
CONV_W = 4
GDN_CHUNK = 64

D_INNER = 2 * D_MODEL
P_C = 64
H_C = D_INNER // P_C
N_C = 128
G_C = 4
HPG = H_C // G_C
SSD_CHUNK = 128

A_Q = H_A * HD_A
A_K = KV_A * HD_A
A_V = KV_A * HD_A
B_Q = H_B * DK_B
B_K = H_B * DK_B
B_V = H_B * DV_B
B_Z = H_B * DV_B
GDN_CONV_CH = B_Q + B_K + B_V
EVEN_IN = A_Q + A_K + A_V + GDN_CONV_CH + B_Z + 2 * H_B
EVEN_MIX = A_Q + B_V
SSD_CONV_CH = D_INNER + 2 * G_C * N_C
ODD_IN = D_INNER + SSD_CONV_CH + H_C

kernel_name = "hybrid_swa_gdn_ssd_macaron_step"


def rmsnorm(x, g):
    xf = x.astype(jnp.float32)
    y = xf * lax.rsqrt(jnp.mean(xf * xf, axis=-1, keepdims=True) + EPS)
    return (y * g.astype(jnp.float32)).astype(x.dtype)


def l2norm(x):
    xf = x.astype(jnp.float32)
    return xf * lax.rsqrt(jnp.sum(xf * xf, axis=-1, keepdims=True) + EPS)


def swiglu(x, wg, wu, wd):
    return (jax.nn.silu(x @ wg) * (x @ wu)) @ wd


def pad_seq(t, lp):
    return jnp.pad(t, [(0, 0), (0, lp - t.shape[1])] + [(0, 0)] * (t.ndim - 2))


def causal_conv(x, hist, w):
    L = x.shape[1]
    xx = jnp.concatenate([hist.astype(x.dtype), x], axis=1)
    out = xx[:, 0:L] * w[0]
    for i in range(1, CONV_W):
        out = out + xx[:, i:i + L] * w[i]
    return out, xx[:, xx.shape[1] - (CONV_W - 1):]


def t5_bucket(dist):
    max_exact = N_BUCKETS // 2
    df = jnp.maximum(dist, max_exact).astype(jnp.float32)
    large = max_exact + (jnp.log(df / max_exact) / math.log(MAX_DIST / max_exact)
                         * (N_BUCKETS - max_exact)).astype(jnp.int32)
    return jnp.where(dist < max_exact, dist, jnp.minimum(large, N_BUCKETS - 1))


def band_bias(n_q, n_k, offset, rel_bias):
    d = offset + jnp.arange(n_q)[:, None] - jnp.arange(n_k)[None, :]
    valid = (d >= 0) & (d <= WINDOW)
    b = jnp.take(rel_bias.astype(jnp.float32), t5_bucket(jnp.clip(d, 0, WINDOW)), axis=0)
    b = jnp.where(valid[..., None], b, NEG_INF)
    return jnp.moveaxis(b, -1, 0).reshape(KV_A, G_A, n_q, n_k)


def sink_attention(q, k, v, bias, sinks):
    s = jnp.einsum('...qhgd,...khd->...hgqk', q, k).astype(jnp.float32) * (HD_A ** -0.5) + bias
    sk = sinks.astype(jnp.float32)[..., None, None]
    m = jnp.maximum(jnp.max(s, axis=-1, keepdims=True), sk)
    p = jnp.exp(s - m)
    denom = jnp.sum(p, axis=-1, keepdims=True) + jnp.exp(sk - m)
    return jnp.einsum('...hgqk,...khd->...qhgd', (p / denom).astype(v.dtype), v)


def gated_delta_chunked(q, k, v, g, beta, S0):
    Bn, L = q.shape[:2]
    C = min(GDN_CHUNK, L)
    lp = -(-L // C) * C
    nc = lp // C
    q, k, v, g, beta = [pad_seq(t.astype(jnp.float32), lp) for t in (q, k, v, g, beta)]

    def chunks(t):
        return jnp.moveaxis(t.reshape(Bn, nc, C, *t.shape[2:]), 2, 3)

    qc, kc, vc, bc = chunks(q), chunks(k), chunks(v), chunks(beta)
    gc = jnp.cumsum(chunks(g), axis=-1)
    tril = jnp.tril(jnp.ones((C, C), bool))
    strict = jnp.tril(jnp.ones((C, C), bool), -1)
    decay = jnp.exp(jnp.where(tril, gc[..., :, None] - gc[..., None, :], -jnp.inf))
    kb = kc * bc[..., None]
    M = jnp.where(strict, jnp.einsum('...id,...jd->...ij', kb, kc) * decay, 0.0)
    rhs = jnp.concatenate([vc * bc[..., None], kb * jnp.exp(gc)[..., None]], axis=-1)
    sol = lax.linalg.triangular_solve(M + jnp.eye(C, dtype=jnp.float32), rhs,
                                      left_side=True, lower=True, unit_diagonal=True)
    u, w = sol[..., :DV_B], sol[..., DV_B:]
    Aqk = jnp.einsum('...id,...jd->...ij', qc, kc) * decay
    qg = qc * jnp.exp(gc)[..., None]
    kd = kc * jnp.exp(gc[..., -1:] - gc)[..., None]
    gl = jnp.exp(gc[..., -1])

    def step(S, inp):
        u_i, w_i, qg_i, kd_i, A_i, gl_i = inp
        v_new = u_i - jnp.einsum('bhcd,bhde->bhce', w_i, S)
        o = jnp.einsum('bhcd,bhde->bhce', qg_i, S) + jnp.einsum('bhij,bhje->bhie', A_i, v_new)
        S = S * gl_i[..., None, None] + jnp.einsum('bhcd,bhce->bhde', kd_i, v_new)
        return S, o

    mv = lambda t: jnp.moveaxis(t, 1, 0)
    S_fin, o = lax.scan(step, S0.astype(jnp.float32), (mv(u), mv(w), mv(qg), mv(kd), mv(Aqk), mv(gl)))
    o = jnp.transpose(o, (1, 0, 3, 2, 4)).reshape(Bn, lp, H_B, DV_B)[:, :L]
    return o, S_fin


def ssd_chunked(x, dt, A, Bm, Cm, S0):
    Bn, L = x.shape[:2]
    C = min(SSD_CHUNK, L)
    lp = -(-L // C) * C
    nc = lp // C
    x, dt, Bm, Cm = [pad_seq(t.astype(jnp.float32), lp) for t in (x, dt, Bm, Cm)]
    a = (dt * A.astype(jnp.float32)).reshape(Bn, nc, C, G_C, HPG)
    xdt = (x * dt[..., None]).reshape(Bn, nc, C, G_C, HPG, P_C)
    Bc = Bm.reshape(Bn, nc, C, G_C, N_C)
    Cc = Cm.reshape(Bn, nc, C, G_C, N_C)
    acum = jnp.cumsum(a, axis=2)
    tril = jnp.tril(jnp.ones((C, C), bool))[:, :, None, None]
    diff = acum[:, :, :, None] - acum[:, :, None, :]
    Lmat = jnp.exp(jnp.where(tril, diff, -jnp.inf))
    scores = jnp.einsum('bclgn,bcsgn->bclsg', Cc, Bc)[..., None] * Lmat
    y = jnp.einsum('bclsgh,bcsghp->bclghp', scores, xdt)
    decay_to_end = jnp.exp(acum[:, :, -1:] - acum)
    chunk_states = jnp.einsum('bcsgn,bcsgh,bcsghp->bcghpn', Bc, decay_to_end, xdt)
    chunk_decay = jnp.exp(acum[:, :, -1])

    def step(S, inp):
        C_i, acum_i, cs_i, cd_i = inp
        y_off = jnp.einsum('blgn,bghpn,blgh->blghp', C_i, S, jnp.exp(acum_i))
        S = S * cd_i[..., None, None] + cs_i
        return S, y_off

    mv = lambda t: jnp.moveaxis(t, 1, 0)
    S0g = S0.astype(jnp.float32).reshape(Bn, G_C, HPG, P_C, N_C)
    S_fin, y_off = lax.scan(step, S0g, (mv(Cc), mv(acum), mv(chunk_states), mv(chunk_decay)))
    y = (y + jnp.moveaxis(y_off, 0, 1)).reshape(Bn, lp, H_C, P_C)[:, :L]
    return y, S_fin.reshape(Bn, H_C, P_C, N_C)


def even_mixer(h, w_in, w_out, sinks, rel_bias, conv_w, A_log, dt_bias, norm_g, win_k, win_v, conv_hist, S0):
    Bn, L, _ = h.shape
    sizes = [A_Q, A_K, A_V, GDN_CONV_CH, B_Z, H_B, H_B]
    idx = [int(i) for i in np.cumsum(sizes)[:-1]]
    q_a, k_a, v_a, qkv_b, z_b, b_b, a_b = jnp.split(h @ w_in, idx, axis=-1)
    q_a = q_a.reshape(Bn, L, KV_A, G_A, HD_A)
    k_a = k_a.reshape(Bn, L, KV_A, HD_A)
    v_a = v_a.reshape(Bn, L, KV_A, HD_A)
    sk = sinks.reshape(KV_A, G_A)
    if win_k is None:
        nb = L // BLOCK
        qb = q_a.reshape(Bn, nb, BLOCK, KV_A, G_A, HD_A)

        def band(t):
            tb = t.reshape(Bn, nb, BLOCK, KV_A, HD_A)
            prev = jnp.pad(tb, ((0, 0), (1, 0), (0, 0), (0, 0), (0, 0)))[:, :-1]
            return jnp.concatenate([prev, tb], axis=2)

        bias = band_bias(BLOCK, 2 * BLOCK, BLOCK, rel_bias)
        pad_keys = (jnp.arange(nb)[:, None] == 0) & (jnp.arange(2 * BLOCK)[None, :] < BLOCK)
        bias = jnp.where(pad_keys[:, None, None, None, :], NEG_INF, bias[None])
        o_a = sink_attention(qb, band(k_a), band(v_a), bias, sk)
        new_k, new_v = k_a[:, L - WINDOW:], v_a[:, L - WINDOW:]
    else:
        lc = win_k.shape[1]
        kk = jnp.concatenate([win_k.astype(k_a.dtype), k_a], axis=1)
        vv = jnp.concatenate([win_v.astype(v_a.dtype), v_a], axis=1)
        bias = band_bias(L, lc + L, lc, rel_bias)
        o_a = sink_attention(q_a, kk, vv, bias, sk)
        new_k, new_v = kk[:, L:], vv[:, L:]
    o_a = o_a.reshape(Bn, L, A_Q)
    conv, new_conv = causal_conv(qkv_b, conv_hist, conv_w)
    conv = jax.nn.silu(conv)
    q_b, k_b, v_b = jnp.split(conv, [B_Q, B_Q + B_K], axis=-1)
    q_b = l2norm(q_b.reshape(Bn, L, H_B, DK_B)) * (DK_B ** -0.5)
    k_b = l2norm(k_b.reshape(Bn, L, H_B, DK_B))
    v_b = v_b.reshape(Bn, L, H_B, DV_B)
    beta = jax.nn.sigmoid(b_b.astype(jnp.float32))
    g = -jnp.exp(A_log.astype(jnp.float32)) * jax.nn.softplus(a_b.astype(jnp.float32) + dt_bias.astype(jnp.float32))
    o_b, S_new = gated_delta_chunked(q_b, k_b, v_b, g, beta, S0)
    o_b = rmsnorm(o_b.astype(h.dtype), norm_g) * jax.nn.silu(z_b.reshape(Bn, L, H_B, DV_B))
    o = jnp.concatenate([o_a, o_b.reshape(Bn, L, B_V)], axis=-1) @ w_out
    return o, (new_k, new_v, new_conv, S_new.astype(S0.dtype))


def odd_mixer(h, w_in, w_out, conv_w, conv_b, dt_bias, A_log, D_skip, norm_g, conv_hist, S0):
    Bn, L, _ = h.shape
    z, xbc, dt = jnp.split(h @ w_in, [D_INNER, D_INNER + SSD_CONV_CH], axis=-1)
    xbc, new_conv = causal_conv(xbc, conv_hist, conv_w)
    xbc = jax.nn.silu(xbc + conv_b)
    xs, Bm, Cm = jnp.split(xbc, [D_INNER, D_INNER + G_C * N_C], axis=-1)
    xs = xs.reshape(Bn, L, H_C, P_C)
    Bm = Bm.reshape(Bn, L, G_C, N_C)
    Cm = Cm.reshape(Bn, L, G_C, N_C)
    dt = jax.nn.softplus(dt.astype(jnp.float32) + dt_bias.astype(jnp.float32))
    A = -jnp.exp(A_log.astype(jnp.float32))
    y, S_new = ssd_chunked(xs, dt, A, Bm, Cm, S0)
    y = y + D_skip.astype(jnp.float32)[:, None] * xs.astype(jnp.float32)
    y = y.reshape(Bn, L, D_INNER).astype(h.dtype) * jax.nn.silu(z)
    gs = D_INNER // G_C
    y = rmsnorm(y.reshape(Bn, L, G_C, gs), norm_g.reshape(G_C, gs)).reshape(Bn, L, D_INNER)
    return y @ w_out, (new_conv, S_new.astype(S0.dtype))


def trunk(x, states, W):
    (rel_bias, norm_ff1, norm_mix, norm_ff2, norm_final,
     ff1_gate, ff1_up, ff1_down, ff2_gate, ff2_up, ff2_down,
     even_w_in, even_w_out, swa_sinks, gdn_conv_w, gdn_A_log, gdn_dt_bias, gdn_norm,
     ssd_w_in, ssd_w_out, ssd_conv_w, ssd_conv_b, ssd_dt_bias, ssd_A_log, ssd_D, ssd_norm) = W
    Bn = x.shape[0]
    ks, vs, gconv, gssm, sconv, sssm = [], [], [], [], [], []
    for l in range(DEPTH):
        x = x + 0.5 * swiglu(rmsnorm(x, norm_ff1[l]), ff1_gate[l], ff1_up[l], ff1_down[l])
        h = rmsnorm(x, norm_mix[l])
        if l % 2 == 0:
            e = l // 2
            if states is None:
                wk = wv = None
                ch = jnp.zeros((Bn, CONV_W - 1, GDN_CONV_CH), x.dtype)
                S0 = jnp.zeros((Bn, H_B, DK_B, DV_B), x.dtype)
            else:
                wk, wv, ch, S0 = states[0][e], states[1][e], states[2][e], states[3][e]
            o, (nk, nv, nc_, ns) = even_mixer(h, even_w_in[e], even_w_out[e], swa_sinks[e], rel_bias,
                                              gdn_conv_w[e], gdn_A_log[e], gdn_dt_bias[e], gdn_norm[e],
                                              wk, wv, ch, S0)
            ks.append(nk); vs.append(nv); gconv.append(nc_); gssm.append(ns)
        else:
            e = l // 2
            if states is None:
                ch = jnp.zeros((Bn, CONV_W - 1, SSD_CONV_CH), x.dtype)
                S0 = jnp.zeros((Bn, H_C, P_C, N_C), x.dtype)
            else:
                ch, S0 = states[4][e], states[5][e]
            o, (nc_, ns) = odd_mixer(h, ssd_w_in[e], ssd_w_out[e], ssd_conv_w[e], ssd_conv_b[e],
                                     ssd_dt_bias[e], ssd_A_log[e], ssd_D[e], ssd_norm[e], ch, S0)
            sconv.append(nc_); sssm.append(ns)
        x = x + o
        x = x + 0.5 * swiglu(rmsnorm(x, norm_ff2[l]), ff2_gate[l], ff2_up[l], ff2_down[l])
    y = rmsnorm(x, norm_final)
    return (y, jnp.stack(ks), jnp.stack(vs), jnp.stack(gconv), jnp.stack(gssm),
            jnp.stack(sconv), jnp.stack(sssm))


def setup_inputs(seed: int = 0) -> dict:
    key = jax.random.key(seed)
    ks = iter(jax.random.split(key, 64))

    def nrm(shape, s=1.0):
        return s * jax.random.normal(next(ks), shape, jnp.float32)

    def gain(shape):
        return 1.0 + nrm(shape, 0.05)

    def dt_bias_init(shape):
        u = jax.random.uniform(next(ks), shape, jnp.float32)
        dt = jnp.exp(u * (math.log(0.1) - math.log(0.001)) + math.log(0.001))
        return dt + jnp.log(-jnp.expm1(-dt))

    def a_log_init(shape):
        return jnp.log(jax.random.uniform(next(ks), shape, jnp.float32, 1.0, 16.0))

    win = min(WINDOW, PAST_LEN)
    d = D_MODEL
    return {
        "x_prompt": nrm((BATCH, SEQ, d)),
        "x_sample": nrm((DEC_BATCH, DEC_SEQ, d)),
        "cache_swa_k": nrm((N_EVEN, DEC_BATCH, win, KV_A, HD_A)),
        "cache_swa_v": nrm((N_EVEN, DEC_BATCH, win, KV_A, HD_A)),
        "state_gdn_conv": nrm((N_EVEN, DEC_BATCH, CONV_W - 1, GDN_CONV_CH)),
        "state_gdn_ssm": nrm((N_EVEN, DEC_BATCH, H_B, DK_B, DV_B), 0.1),
        "state_ssd_conv": nrm((N_ODD, DEC_BATCH, CONV_W - 1, SSD_CONV_CH)),
        "state_ssd_ssm": nrm((N_ODD, DEC_BATCH, H_C, P_C, N_C), 0.3),
        "rel_bias": nrm((N_BUCKETS, H_A), 0.5),
        "norm_ff1": gain((DEPTH, d)),
        "norm_mix": gain((DEPTH, d)),
        "norm_ff2": gain((DEPTH, d)),
        "norm_final": gain((d,)),
        "ff1_gate": nrm((DEPTH, d, D_FF), d ** -0.5),
        "ff1_up": nrm((DEPTH, d, D_FF), d ** -0.5),
        "ff1_down": nrm((DEPTH, D_FF, d), D_FF ** -0.5),
        "ff2_gate": nrm((DEPTH, d, D_FF), d ** -0.5),
        "ff2_up": nrm((DEPTH, d, D_FF), d ** -0.5),
        "ff2_down": nrm((DEPTH, D_FF, d), D_FF ** -0.5),
        "even_w_in": nrm((N_EVEN, d, EVEN_IN), d ** -0.5),
        "even_w_out": nrm((N_EVEN, EVEN_MIX, d), EVEN_MIX ** -0.5),
        "swa_sinks": nrm((N_EVEN, H_A), 0.5),
        "gdn_conv_w": nrm((N_EVEN, CONV_W, GDN_CONV_CH), CONV_W ** -0.5),
        "gdn_A_log": a_log_init((N_EVEN, H_B)),
        "gdn_dt_bias": dt_bias_init((N_EVEN, H_B)),
        "gdn_norm": gain((N_EVEN, DV_B)),
        "ssd_w_in": nrm((N_ODD, d, ODD_IN), d ** -0.5),
        "ssd_w_out": nrm((N_ODD, D_INNER, d), D_INNER ** -0.5),
        "ssd_conv_w": nrm((N_ODD, CONV_W, SSD_CONV_CH), CONV_W ** -0.5),
        "ssd_conv_b": nrm((N_ODD, SSD_CONV_CH), 0.01),
        "ssd_dt_bias": dt_bias_init((N_ODD, H_C)),
        "ssd_A_log": a_log_init((N_ODD, H_C)),
        "ssd_D": 1.0 + nrm((N_ODD, H_C), 0.1),
        "ssd_norm": gain((N_ODD, D_INNER)),
    }


def reference(x_prompt, x_sample, cache_swa_k, cache_swa_v, state_gdn_conv, state_gdn_ssm,
              state_ssd_conv, state_ssd_ssm, rel_bias, norm_ff1, norm_mix, norm_ff2, norm_final,
              ff1_gate, ff1_up, ff1_down, ff2_gate, ff2_up, ff2_down,
              even_w_in, even_w_out, swa_sinks, gdn_conv_w, gdn_A_log, gdn_dt_bias, gdn_norm,
              ssd_w_in, ssd_w_out, ssd_conv_w, ssd_conv_b, ssd_dt_bias, ssd_A_log, ssd_D, ssd_norm):
    W = (rel_bias, norm_ff1, norm_mix, norm_ff2, norm_final,
         ff1_gate, ff1_up, ff1_down, ff2_gate, ff2_up, ff2_down,
         even_w_in, even_w_out, swa_sinks, gdn_conv_w, gdn_A_log, gdn_dt_bias, gdn_norm,
         ssd_w_in, ssd_w_out, ssd_conv_w, ssd_conv_b, ssd_dt_bias, ssd_A_log, ssd_D, ssd_norm)
    y_prompt, p_k, p_v, p_gconv, p_gssm, p_sconv, p_sssm = trunk(x_prompt, None, W)
    states = (cache_swa_k, cache_swa_v, state_gdn_conv, state_gdn_ssm, state_ssd_conv, state_ssd_ssm)
    y_sample, s_k, s_v, s_gconv, s_gssm, s_sconv, s_sssm = trunk(x_sample, states, W)
    return (y_prompt, y_sample, p_k, p_v, p_gconv, p_gssm, p_sconv, p_sssm,
            s_k, s_v, s_gconv, s_gssm, s_sconv, s_sssm)
```

```python
import functools
import math

import numpy as np
import jax
import jax.numpy as jnp
from jax import lax
from jax.experimental import pallas as pl
from jax.experimental.pallas import tpu as pltpu

F32 = jnp.float32
BF16 = jnp.bfloat16
HI = lax.Precision.HIGHEST

EPS = 1e-6
NEG_INF = -1e30
D_MODEL = 1024
D_FF = 2816
FF_CHUNK = 256
WINDOW = 128
BLOCK = 128
H_A, KV_A, G_A, HD_A = 8, 2, 4, 64
N_BUCKETS, MAX_DIST = 32, 128
H_B, DK_B, DV_B = 4, 128, 128
CONV_W = 4
GDN_CHUNK = 64
D_INNER = 2048
P_C, H_C, N_C, G_C = 64, 32, 128, 4
HPG = H_C // G_C
SSD_CHUNK = 128
A_Q = H_A * HD_A
B_QKV = 3 * H_B * DK_B
B_Z = H_B * DV_B
SSD_CONV_CH = D_INNER + 2 * G_C * N_C
LANES = 128
SUBLANES = 8
VMEM_LIMIT_BYTES = 56 * 1024 * 1024


def _cp(*sem):
    return pltpu.CompilerParams(dimension_semantics=sem, vmem_limit_bytes=VMEM_LIMIT_BYTES)


def _whole(shape):
    nd = len(shape)
    return pl.BlockSpec(shape, lambda *_: (0,) * nd, pipeline_mode=pl.Buffered(1))


def _rms(x, g):
    return x * lax.rsqrt(jnp.mean(x * x, axis=-1, keepdims=True) + EPS) * g


def _silu(x):
    return x * jax.nn.sigmoid(x)


def _softplus(x):
    return jnp.maximum(x, 0.0) + jnp.log1p(jnp.exp(-jnp.abs(x)))


def _dot(a, b):
    return jnp.dot(a, b, preferred_element_type=F32)


def _dot_nt(a, b):
    return lax.dot_general(a, b, (((1,), (1,)), ((), ())), preferred_element_type=F32)


def _dot_tn(a, b):
    return lax.dot_general(a, b, (((0,), (0,)), ((), ())), preferred_element_type=F32)


def _dot_hi(a, b):
    return jnp.dot(a, b, precision=HI, preferred_element_type=F32)


def _expand(x, e):
    hi = x.astype(BF16)
    r = x - hi.astype(F32)
    mid = r.astype(BF16)
    lo = (r - mid.astype(F32)).astype(BF16)
    return _dot(hi, e) + _dot(mid, e) + _dot(lo, e)


def _tri(n, kind):
    r = lax.broadcasted_iota(jnp.int32, (n, n), 0)
    c = lax.broadcasted_iota(jnp.int32, (n, n), 1)
    return {"lower": r >= c, "strict_lower": r > c, "upper": r <= c}[kind]


def _ffn_body(*refs, nf, final):
    if final:
        x_ref, g_ref, wg_ref, wu_ref, wd_ref, gf_ref, o_ref, hn_ref, acc_ref = refs
    else:
        x_ref, g_ref, wg_ref, wu_ref, wd_ref, o_ref, hn_ref, acc_ref = refs
    x = x_ref[...]
    hn_ref[...] = _rms(x, g_ref[...]).astype(BF16)
    acc_ref[...] = jnp.zeros_like(acc_ref)

    def step(f, carry):
        hn = hn_ref[...]
        gate = _dot(hn, wg_ref[f])
        up = _dot(hn, wu_ref[f])
        act = (_silu(gate) * up).astype(BF16)
        acc_ref[...] += _dot(act, wd_ref[f])
        return carry

    lax.fori_loop(0, nf, step, 0)
    y = x + 0.5 * acc_ref[...]
    if final:
        y = _rms(y, gf_ref[...])
    o_ref[...] = y


def _ffn(x, g, wg, wu, wd, g_final=None):
    t, d = x.shape
    tm = min(t, 512)
    nf = wg.shape[0]
    final = g_final is not None
    row = pl.BlockSpec((tm, d), lambda i: (i, 0))
    in_specs = [row, _whole((1, d)), _whole(wg.shape), _whole(wu.shape), _whole(wd.shape)]
    args = [x, g.reshape(1, d), wg, wu, wd]
    if final:
        in_specs.append(_whole((1, d)))
        args.append(g_final.reshape(1, d))
    return pl.pallas_call(
        functools.partial(_ffn_body, nf=nf, final=final),
        grid=(t // tm,),
        in_specs=in_specs,
        out_specs=row,
        out_shape=jax.ShapeDtypeStruct((t, d), F32),
        scratch_shapes=[pltpu.VMEM((tm, d), BF16), pltpu.VMEM((tm, d), F32)],
        compiler_params=_cp("parallel"),
        name="ffn",
    )(*args)


def _inproj_body(x_ref, g_ref, w_ref, ws_ref, wst_ref, *outs, splits):
    hn = _rms(x_ref[...], g_ref[...]).astype(BF16)
    off = 0
    for o_ref, n in zip(outs[:-2], splits):
        o_ref[...] = _dot(hn, w_ref[:, off:off + n])
        off += n
    outs[-2][...] = _dot(hn, ws_ref[...])
    outs[-1][...] = _dot_nt(wst_ref[...], hn)


def _inproj(x, g, w, ws, wst, splits):
    t, d = x.shape
    tm = min(t, 256)
    ns = wst.shape[0]
    row = pl.BlockSpec((tm, d), lambda i: (i, 0))
    out_specs = [pl.BlockSpec((tm, n), lambda i: (i, 0)) for n in splits]
    out_specs += [pl.BlockSpec((tm, LANES), lambda i: (i, 0)), pl.BlockSpec((ns, tm), lambda i: (0, i))]
    out_shape = [jax.ShapeDtypeStruct((t, n), F32) for n in splits]
    out_shape += [jax.ShapeDtypeStruct((t, LANES), F32), jax.ShapeDtypeStruct((ns, t), F32)]
    return pl.pallas_call(
        functools.partial(_inproj_body, splits=splits),
        grid=(t // tm,),
        in_specs=[row, _whole((1, d)), _whole(w.shape), _whole(ws.shape), _whole(wst.shape)],
        out_specs=out_specs,
        out_shape=out_shape,
        compiler_params=_cp("parallel"),
        name="inproj",
    )(x, g.reshape(1, d), w, ws, wst)


def _outproj_body(x_ref, m_ref, w_ref, o_ref):
    o_ref[...] = x_ref[...] + _dot(m_ref[...].astype(BF16), w_ref[...])


def _outproj(x, m, w):
    t, d = x.shape
    k = m.shape[1]
    tm = min(t, 512)
    row = pl.BlockSpec((tm, d), lambda i: (i, 0))
    return pl.pallas_call(
        _outproj_body,
        grid=(t // tm,),
        in_specs=[row, pl.BlockSpec((tm, k), lambda i: (i, 0)), _whole(w.shape)],
        out_specs=row,
        out_shape=jax.ShapeDtypeStruct((t, d), F32),
        compiler_params=_cp("parallel"),
        name="outproj",
    )(x, m, w)


def _t5_bucket_np(dist):
    max_exact = N_BUCKETS // 2
    df = np.maximum(dist, max_exact).astype(np.float32)
    large = max_exact + (np.log(df / np.float32(max_exact)) / np.float32(math.log(MAX_DIST / max_exact))
                         * np.float32(N_BUCKETS - max_exact)).astype(np.int32)
    return np.where(dist < max_exact, dist, np.minimum(large, N_BUCKETS - 1)).astype(np.int32)


def _band_bucket_ids(n_q, n_k, offset):
    d = offset + np.arange(n_q)[:, None] - np.arange(n_k)[None, :]
    valid = (d >= 0) & (d <= WINDOW)
    return np.where(valid, _t5_bucket_np(np.clip(d, 0, WINDOW)), -1).astype(np.int32)


def _bias_from_buckets(bid, rb_ref, h):
    acc = jnp.full(bid.shape, NEG_INF, F32)
    for bk in range(N_BUCKETS):
        acc = jnp.where(bid == bk, rb_ref[bk, h], acc)
    return acc


def _swa_prompt_body(bid_ref, rb_ref, sk_ref, q_ref, kvp_ref, kvc_ref, o_ref, bias_ref):
    first_step = (pl.program_id(0) == 0) & (pl.program_id(1) == 0)

    @pl.when(first_step)
    def _():
        bid = bid_ref[...]
        for h in range(H_A):
            bias_ref[h] = _bias_from_buckets(bid, rb_ref, h)

    col = lax.broadcasted_iota(jnp.int32, (BLOCK, 2 * BLOCK), 1)
    pad_keys = (pl.program_id(1) == 0) & (col < BLOCK)
    kvp = kvp_ref[...]
    kvc = kvc_ref[...]
    scale = HD_A ** -0.5
    outs = []
    for kv in range(KV_A):
        ks = slice(kv * HD_A, (kv + 1) * HD_A)
        vs = slice(KV_A * HD_A + kv * HD_A, KV_A * HD_A + (kv + 1) * HD_A)
        k = jnp.concatenate([kvp[:, ks], kvc[:, ks]], axis=0).astype(BF16)
        v = jnp.concatenate([kvp[:, vs], kvc[:, vs]], axis=0).astype(BF16)
        for g in range(G_A):
            h = kv * G_A + g
            q = q_ref[:, h * HD_A:(h + 1) * HD_A].astype(BF16)
            s = _dot_nt(q, k) * scale + jnp.where(pad_keys, NEG_INF, bias_ref[h])
            sk = sk_ref[h]
            m = jnp.maximum(jnp.max(s, axis=-1, keepdims=True), sk)
            p = jnp.exp(s - m)
            denom = jnp.sum(p, axis=-1, keepdims=True) + jnp.exp(sk - m)
            outs.append(_dot((p / denom).astype(BF16), v))
    o_ref[...] = jnp.concatenate(outs, axis=1).astype(BF16)


def _swa_prompt(qa, kv, rel_bias, sinks, bn, l):
    nb = l // BLOCK
    t = bn * l
    bid = jnp.asarray(_band_bucket_ids(BLOCK, 2 * BLOCK, BLOCK))
    kv_blk = (BLOCK, 2 * KV_A * HD_A)
    return pl.pallas_call(
        _swa_prompt_body,
        grid=(bn, nb),
        in_specs=[
            _whole((BLOCK, 2 * BLOCK)),
            pl.BlockSpec(memory_space=pltpu.SMEM),
            pl.BlockSpec(memory_space=pltpu.SMEM),
            pl.BlockSpec((BLOCK, A_Q), lambda b, i: (b * nb + i, 0)),
            pl.BlockSpec(kv_blk, lambda b, i: (jnp.maximum(b * nb + i - 1, 0), 0)),
            pl.BlockSpec(kv_blk, lambda b, i: (b * nb + i, 0)),
        ],
        out_specs=pl.BlockSpec((BLOCK, A_Q), lambda b, i: (b * nb + i, 0)),
        out_shape=jax.ShapeDtypeStruct((t, A_Q), BF16),
        scratch_shapes=[pltpu.VMEM((H_A, BLOCK, 2 * BLOCK), F32)],
        compiler_params=_cp("arbitrary", "arbitrary"),
        name="swa_prompt",
    )(bid, rel_bias, sinks, qa, kv, kv)


def _swa_decode_body(bid_ref, rb_ref, sk_ref, q_ref, kvn_ref, ck_ref, cv_ref, o_ref):
    bid = bid_ref[...]
    row = lax.broadcasted_iota(jnp.int32, (H_A, 1), 0)
    bias = jnp.zeros((H_A, bid.shape[1]), F32)
    sk = jnp.zeros((H_A, 1), F32)
    for h in range(H_A):
        bias = jnp.where(row == h, _bias_from_buckets(bid, rb_ref, h), bias)
        sk = jnp.where(row == h, sk_ref[h], sk)
    bias_c = bias[:, :WINDOW]
    bias_n = bias[:, WINDOW:WINDOW + 1]
    scale = HD_A ** -0.5
    q = q_ref[...]
    kvn = kvn_ref[...]
    width = KV_A * HD_A
    k_new = kvn[:, None, :width]
    v_new = kvn[:, None, width:]
    s = lax.dot_general(q.astype(BF16), ck_ref[...].astype(BF16), (((2,), (2,)), ((0,), (0,))),
                        preferred_element_type=F32) * scale + bias_c[None]
    s_n = jnp.sum(q * k_new, axis=-1, keepdims=True) * scale + bias_n[None]
    m = jnp.maximum(jnp.maximum(jnp.max(s, axis=-1, keepdims=True), s_n), sk[None])
    p = jnp.exp(s - m)
    p_n = jnp.exp(s_n - m)
    denom = jnp.sum(p, axis=-1, keepdims=True) + p_n + jnp.exp(sk[None] - m)
    o = lax.dot_general((p / denom).astype(BF16), cv_ref[...].astype(BF16), (((2,), (1,)), ((0,), (0,))),
                        preferred_element_type=F32)
    o_ref[...] = o + (p_n / denom) * v_new


def _swa_decode(q_pad, kv_new, cache_k, cache_v, rel_bias, sinks):
    bn = q_pad.shape[0]
    bs = 32
    width = KV_A * HD_A
    ids = _band_bucket_ids(1, WINDOW + 1, WINDOW)
    bid = np.full((1, WINDOW + LANES), -1, np.int32)
    bid[:, :WINDOW + 1] = ids
    return pl.pallas_call(
        _swa_decode_body,
        grid=(bn // bs,),
        in_specs=[
            _whole((1, WINDOW + LANES)),
            pl.BlockSpec(memory_space=pltpu.SMEM),
            pl.BlockSpec(memory_space=pltpu.SMEM),
            pl.BlockSpec((bs, H_A, width), lambda i: (i, 0, 0)),
            pl.BlockSpec((bs, 2 * width), lambda i: (i, 0)),
            pl.BlockSpec((bs, WINDOW, width), lambda i: (i, 0, 0)),
            pl.BlockSpec((bs, WINDOW, width), lambda i: (i, 0, 0)),
        ],
        out_specs=pl.BlockSpec((bs, H_A, width), lambda i: (i, 0, 0)),
        out_shape=jax.ShapeDtypeStruct((bn, H_A, width), F32),
        compiler_params=_cp("parallel"),
        name="swa_decode",
    )(jnp.asarray(bid), rel_bias, sinks, q_pad, kv_new, cache_k, cache_v)


def _conv_chunk(x_ref, xbuf_ref, cw_ref, rows):
    xbuf_ref[SUBLANES:SUBLANES + rows, :] = x_ref[...]
    out = xbuf_ref[5:5 + rows, :] * cw_ref[0:1, :]
    for i in range(1, CONV_W):
        out = out + xbuf_ref[5 + i:5 + i + rows, :] * cw_ref[i:i + 1, :]
    xbuf_ref[0:SUBLANES, :] = xbuf_ref[rows:rows + SUBLANES, :]
    return out


def _l2norm(x):
    return x * lax.rsqrt(jnp.sum(x * x, axis=-1, keepdims=True) + EPS)


def _gdn_prompt_body(qkv_ref, z_ref, sm_ref, smt_ref, cw_ref, arow_ref, dtrow_ref, acol_ref, dtcol_ref,
                     gn_ref, o_ref, s_out_ref, xbuf_ref, s_ref):
    c = pl.program_id(1)
    rows = qkv_ref.shape[0]

    @pl.when(c == 0)
    def _():
        xbuf_ref[0:SUBLANES, :] = jnp.zeros((SUBLANES, xbuf_ref.shape[1]), F32)
        s_ref[...] = jnp.zeros_like(s_ref)

    conv = _silu(_conv_chunk(qkv_ref, xbuf_ref, cw_ref, rows))
    nq = H_B * DK_B
    sm = sm_ref[...]
    beta_all = jax.nn.sigmoid(sm)
    g_all = -jnp.exp(arow_ref[...]) * _softplus(sm + dtrow_ref[...])
    gt_all = -jnp.exp(acol_ref[...]) * _softplus(smt_ref[...] + dtcol_ref[...])
    cc = GDN_CHUNK
    lower = _tri(cc, "lower")
    strict = _tri(cc, "strict_lower")
    tril_f = lower.astype(F32)
    triu_f = _tri(cc, "upper").astype(F32)
    eye_f = (lower & ~strict).astype(F32)
    gn = gn_ref[...]
    z = z_ref[...]
    for ci in range(rows // cc):
        r = slice(ci * cc, (ci + 1) * cc)
        gc = _dot_hi(tril_f, g_all[r])
        gct = _dot_hi(gt_all[:, r], triu_f)
        outs = []
        for h in range(H_B):
            hs = slice(h * DK_B, (h + 1) * DK_B)
            qh = _l2norm(conv[r, hs]) * (DK_B ** -0.5)
            kh = _l2norm(conv[r, nq + h * DK_B:nq + (h + 1) * DK_B])
            vh = conv[r, 2 * nq + h * DV_B:2 * nq + (h + 1) * DV_B]
            beta = beta_all[r, h:h + 1]
            gcol = gc[:, H_B + h:H_B + h + 1]
            grow = gct[H_B + h:H_B + h + 1, :]
            decay = jnp.exp(jnp.where(lower, gcol - grow, -jnp.inf))
            kb = kh * beta
            kh16 = kh.astype(BF16)
            m = jnp.where(strict, _dot_nt(kb.astype(BF16), kh16) * decay, 0.0)
            p = -m
            t = eye_f + p
            for _ in range(int(math.log2(cc)) - 1):
                p = _dot_hi(p, p)
                t = t + _dot_hi(t, p)
            egc = jnp.exp(gcol)
            sol = _dot_hi(t, jnp.concatenate([vh * beta, kb * egc], axis=1))
            u, w = sol[:, :DV_B], sol[:, DV_B:]
            aqk = _dot_nt(qh.astype(BF16), kh16) * decay
            glast = gcol[cc - 1:cc, :]
            kd = kh * jnp.exp(glast - gcol)
            s = s_ref[h]
            s16 = s.astype(BF16)
            v_new = u - _dot(w.astype(BF16), s16)
            v16 = v_new.astype(BF16)
            o = _dot((qh * egc).astype(BF16), s16) + _dot(aqk.astype(BF16), v16)
            s_ref[h] = s * jnp.exp(glast) + _dot_tn(kd.astype(BF16), v16)
            outs.append(_rms(o, gn) * _silu(z[r, h * DV_B:(h + 1) * DV_B]))
        o_ref[r, :] = jnp.concatenate(outs, axis=1).astype(BF16)

    @pl.when(c == pl.num_programs(1) - 1)
    def _():
        s_out_ref[0] = s_ref[...]


def _gdn_prompt(qkvb, z, small, small_t, conv_w, a_log, dt_bias, gnorm, bn, l):
    rows = 2 * GDN_CHUNK
    nc = l // rows
    t = bn * l
    ns = small_t.shape[0]
    pad_row = lambda v: jnp.zeros((1, LANES), F32).at[0, H_B:2 * H_B].set(v)
    pad_col = lambda v: jnp.zeros((ns, 1), F32).at[H_B:2 * H_B, 0].set(v)
    tok = lambda n: pl.BlockSpec((rows, n), lambda b, c: (b * nc + c, 0))
    return pl.pallas_call(
        _gdn_prompt_body,
        grid=(bn, nc),
        in_specs=[
            tok(B_QKV), tok(B_Z), tok(LANES),
            pl.BlockSpec((ns, rows), lambda b, c: (0, b * nc + c)),
            _whole((CONV_W, B_QKV)), _whole((1, LANES)), _whole((1, LANES)),
            _whole((ns, 1)), _whole((ns, 1)), _whole((1, DV_B)),
        ],
        out_specs=[tok(B_Z), pl.BlockSpec((1, H_B, DK_B, DV_B), lambda b, c: (b, 0, 0, 0))],
        out_shape=[jax.ShapeDtypeStruct((t, B_Z), BF16), jax.ShapeDtypeStruct((bn, H_B, DK_B, DV_B), F32)],
        scratch_shapes=[pltpu.VMEM((rows + SUBLANES, B_QKV), F32), pltpu.VMEM((H_B, DK_B, DV_B), F32)],
        compiler_params=_cp("arbitrary", "arbitrary"),
        name="gdn_prompt",
    )(qkvb, z, small, small_t, conv_w, pad_row(a_log), pad_row(dt_bias), pad_col(a_log), pad_col(dt_bias),
      gnorm.reshape(1, DV_B))


def _conv_step(x, hist_ref, cw_ref, new_ref, ch):
    out = hist_ref[:, 0:ch] * cw_ref[0:1, :]
    for i in range(1, CONV_W - 1):
        out = out + hist_ref[:, i * ch:(i + 1) * ch] * cw_ref[i:i + 1, :]
    out = out + x * cw_ref[CONV_W - 1:CONV_W, :]
    new_ref[:, 0:(CONV_W - 2) * ch] = hist_ref[:, ch:(CONV_W - 1) * ch]
    new_ref[:, (CONV_W - 2) * ch:] = x
    return out


def _gdn_decode_pre_body(qkv_ref, hist_ref, sm_ref, cw_ref, arow_ref, dtrow_ref,
                         new_ref, w_ref, qg_ref, k_ref, u_ref, qk_ref, gl_ref):
    conv = _silu(_conv_step(qkv_ref[...], hist_ref, cw_ref, new_ref, B_QKV))
    nq = H_B * DK_B
    sm = sm_ref[...]
    beta_all = jax.nn.sigmoid(sm)
    g_all = -jnp.exp(arow_ref[...]) * _softplus(sm + dtrow_ref[...])
    for h in range(H_B):
        hs = slice(h * DK_B, (h + 1) * DK_B)
        qh = _l2norm(conv[:, hs]) * (DK_B ** -0.5)
        kh = _l2norm(conv[:, nq + h * DK_B:nq + (h + 1) * DK_B])
        vh = conv[:, 2 * nq + h * DV_B:2 * nq + (h + 1) * DV_B]
        beta = beta_all[:, h:h + 1]
        eg = jnp.exp(g_all[:, H_B + h:H_B + h + 1])
        w_ref[:, hs] = kh * beta * eg
        qg_ref[:, hs] = qh * eg
        k_ref[:, hs] = kh
        u_ref[:, hs] = vh * beta
        qk_ref[:, hs] = jnp.broadcast_to(jnp.sum(qh * kh, axis=-1, keepdims=True), qh.shape)
        gl_ref[:, hs] = jnp.broadcast_to(eg, qh.shape)


def _gdn_decode_state_body(s_ref, w_ref, qg_ref, k_ref, u_ref, qk_ref, gl_ref, z_ref, gn_ref,
                           s_out_ref, o_ref):
    nb = s_ref.shape[0]
    row = lax.broadcasted_iota(jnp.int32, (nb, 1), 0)
    wq = jnp.concatenate([w_ref[...], qg_ref[...]], axis=0).astype(BF16)
    ws = jnp.zeros((nb, DV_B), F32)
    qs = jnp.zeros((nb, DV_B), F32)
    for bb in range(nb):
        res = _dot(wq, s_ref[bb, 0].astype(BF16))
        ws = jnp.where(row == bb, res[:nb], ws)
        qs = jnp.where(row == bb, res[nb:], qs)
    v_new = u_ref[...] - ws
    o = qs + qk_ref[...] * v_new
    o_ref[...] = _rms(o, gn_ref[...]) * _silu(z_ref[...])
    k = k_ref[...]
    v16 = v_new.astype(BF16)
    gl = gl_ref[...]
    for bb in range(nb):
        k_one = jnp.where(row == bb, k, 0.0).astype(BF16)
        s_out_ref[bb, 0] = s_ref[bb, 0] * gl[bb:bb + 1, :] + _dot_tn(k_one, v16)


def _gdn_decode(qkvb, hist, z, small, conv_w, a_log, dt_bias, gnorm, s0):
    bn = qkvb.shape[0]
    pad_row = lambda v: jnp.zeros((1, LANES), F32).at[0, H_B:2 * H_B].set(v)
    wide = jax.ShapeDtypeStruct((bn, H_B * DK_B), F32)
    new_conv, w, qg, k, u, qk, gl = pl.pallas_call(
        _gdn_decode_pre_body,
        out_shape=[jax.ShapeDtypeStruct(hist.shape, F32)] + [wide] * 6,
        compiler_params=pltpu.CompilerParams(vmem_limit_bytes=VMEM_LIMIT_BYTES),
        name="gdn_decode_pre",
    )(qkvb, hist, small, conv_w, pad_row(a_log), pad_row(dt_bias))
    nb = SUBLANES
    vec = pl.BlockSpec((nb, DK_B), lambda i, h: (i, h))
    st = pl.BlockSpec((nb, 1, DK_B, DV_B), lambda i, h: (i, h, 0, 0))
    s_new, o = pl.pallas_call(
        _gdn_decode_state_body,
        grid=(bn // nb, H_B),
        in_specs=[st, vec, vec, vec, vec, vec, vec, vec, _whole((1, DV_B))],
        out_specs=[st, vec],
        out_shape=[jax.ShapeDtypeStruct(s0.shape, F32), wide],
        compiler_params=_cp("parallel", "parallel"),
        name="gdn_decode_state",
    )(s0, w, qg, k, u, qk, gl, z, gnorm.reshape(1, DV_B))
    return o, new_conv, s_new


def _head_expander(width):
    e = np.zeros((LANES, H_C * width), np.float32)
    for h in range(H_C):
        e[h, h * width:(h + 1) * width] = 1.0
    return jnp.asarray(e, BF16)


def _group_rms(y, g):
    gs = D_INNER // G_C
    parts = [_rms(y[:, i * gs:(i + 1) * gs], g[:, i * gs:(i + 1) * gs]) for i in range(G_C)]
    return jnp.concatenate(parts, axis=1)


def _ssd_prompt_body(z_ref, xbc_ref, sm_ref, smt_ref, cw_ref, cb_ref, dtrow_ref, arow_ref, dtcol_ref,
                     acol_ref, dx_ref, gn_ref, e_ref, o_ref, s_out_ref, xbuf_ref, s_ref):
    c = pl.program_id(1)
    rows = xbc_ref.shape[0]

    @pl.when(c == 0)
    def _():
        xbuf_ref[0:SUBLANES, :] = jnp.zeros((SUBLANES, xbuf_ref.shape[1]), F32)
        s_ref[...] = jnp.zeros_like(s_ref)

    xbc = _silu(_conv_chunk(xbc_ref, xbuf_ref, cw_ref, rows) + cb_ref[...])
    xs = xbc[:, :D_INNER]
    bm = xbc[:, D_INNER:D_INNER + G_C * N_C]
    cm = xbc[:, D_INNER + G_C * N_C:]
    e = e_ref[...]
    lower = _tri(rows, "lower")
    dt = _softplus(sm_ref[...] + dtrow_ref[...])
    acum = _dot_hi(lower.astype(F32), dt * -jnp.exp(arow_ref[...]))
    a_t = _softplus(smt_ref[...] + dtcol_ref[...]) * -jnp.exp(acol_ref[...])
    acum_t = _dot_hi(a_t, _tri(rows, "upper").astype(F32))
    xdt = xs * _expand(dt, e)
    acum_x = _expand(acum, e)
    last_x = acum_x[rows - 1:rows, :]
    xdte = (xdt * jnp.exp(last_x - acum_x)).astype(BF16)
    scale_y = jnp.exp(acum_x)
    chunk_decay = jnp.exp(last_x)
    xdt16 = xdt.astype(BF16)
    gw = HPG * P_C
    ys = []
    for g in range(G_C):
        bg = bm[:, g * N_C:(g + 1) * N_C]
        cg16 = cm[:, g * N_C:(g + 1) * N_C].astype(BF16)
        cb = _dot_nt(cg16, bg.astype(BF16))
        yg = []
        for hh in range(HPG):
            h = g * HPG + hh
            lmat = jnp.exp(jnp.where(lower, acum[:, h:h + 1] - acum_t[h:h + 1, :], -jnp.inf))
            yg.append(_dot((cb * lmat).astype(BF16), xdt16[:, h * P_C:(h + 1) * P_C]))
        gs = slice(g * gw, (g + 1) * gw)
        sg = s_ref[:, gs]
        y_off = _dot(cg16, sg.astype(BF16)) * scale_y[:, gs]
        ys.append(jnp.concatenate(yg, axis=1) + y_off)
        s_ref[:, gs] = sg * chunk_decay[:, gs] + _dot(bg.T.astype(BF16), xdte[:, gs])
    y = jnp.concatenate(ys, axis=1) + dx_ref[...] * xs
    y = y * _silu(z_ref[...])
    o_ref[...] = _group_rms(y, gn_ref[...]).astype(BF16)

    @pl.when(c == pl.num_programs(1) - 1)
    def _():
        s_out_ref[0] = s_ref[...].T.reshape(H_C, P_C, N_C)


def _ssd_small_params(dt_bias, a_log, ns):
    row = lambda v: jnp.zeros((1, LANES), F32).at[0, :H_C].set(v)
    col = lambda v: jnp.zeros((ns, 1), F32).at[:H_C, 0].set(v)
    return row(dt_bias), row(a_log), col(dt_bias), col(a_log)


def _ssd_prompt(z, xbc, small, small_t, conv_w, conv_b, dt_bias, a_log, d_skip, gnorm, bn, l):
    rows = SSD_CHUNK
    nc = l // rows
    t = bn * l
    ns = small_t.shape[0]
    dtrow, arow, dtcol, acol = _ssd_small_params(dt_bias, a_log, ns)
    tok = lambda n: pl.BlockSpec((rows, n), lambda b, c: (b * nc + c, 0))
    return pl.pallas_call(
        _ssd_prompt_body,
        grid=(bn, nc),
        in_specs=[
            tok(D_INNER), tok(SSD_CONV_CH), tok(LANES),
            pl.BlockSpec((ns, rows), lambda b, c: (0, b * nc + c)),
            _whole((CONV_W, SSD_CONV_CH)), _whole((1, SSD_CONV_CH)),
            _whole((1, LANES)), _whole((1, LANES)), _whole((ns, 1)), _whole((ns, 1)),
            _whole((1, D_INNER)), _whole((1, D_INNER)), _whole((LANES, D_INNER)),
        ],
        out_specs=[tok(D_INNER), pl.BlockSpec((1, H_C, P_C, N_C), lambda b, c: (b, 0, 0, 0))],
        out_shape=[jax.ShapeDtypeStruct((t, D_INNER), BF16), jax.ShapeDtypeStruct((bn, H_C, P_C, N_C), F32)],
        scratch_shapes=[pltpu.VMEM((rows + SUBLANES, SSD_CONV_CH), F32), pltpu.VMEM((N_C, D_INNER), F32)],
        compiler_params=_cp("arbitrary", "arbitrary"),
        name="ssd_prompt",
    )(z, xbc, small, small_t, conv_w, conv_b.reshape(1, SSD_CONV_CH), dtrow, arow, dtcol, acol,
      jnp.repeat(d_skip, P_C).reshape(1, D_INNER), gnorm.reshape(1, D_INNER), _head_expander(P_C))


def _ssd_decode_pre_body(xbc_ref, hist_ref, sm_ref, cw_ref, cb_ref, dtrow_ref, arow_ref, e_ref, en_ref,
                         new_ref, xs_ref, xdt_ref, b_ref, c_ref, dax_ref, dan_ref):
    xbc = _silu(_conv_step(xbc_ref[...], hist_ref, cw_ref, new_ref, SSD_CONV_CH) + cb_ref[...])
    xs = xbc[:, :D_INNER]
    dt = _softplus(sm_ref[...] + dtrow_ref[...])
    a = dt * -jnp.exp(arow_ref[...])
    xs_ref[...] = xs
    xdt_ref[...] = xs * _expand(dt, e_ref[...])
    b_ref[...] = xbc[:, D_INNER:D_INNER + G_C * N_C]
    c_ref[...] = xbc[:, D_INNER + G_C * N_C:]
    dax_ref[...] = jnp.exp(_expand(a, e_ref[...]))
    dan_ref[...] = jnp.exp(_expand(a, en_ref[...]))


def _ssd_decode_state_body(s_ref, xdt_ref, b_ref, c_ref, dan_ref, s_out_ref, yoff_ref):
    nb = s_ref.shape[0]
    gw = HPG * P_C
    row = lax.broadcasted_iota(jnp.int32, (nb, 1), 0)
    xdt = xdt_ref[...]
    b16 = b_ref[...].astype(BF16)
    c16 = c_ref[...].astype(BF16)
    dan = dan_ref[...]
    yoff = jnp.zeros((nb, gw), F32)
    for bb in range(nb):
        s = s_ref[bb].reshape(gw, N_C)
        res = _dot_nt(c16, s.astype(BF16))
        yoff = jnp.where(row == bb, res, yoff)
        x_one = jnp.where(row == bb, xdt, 0.0).astype(BF16)
        upd = _dot_tn(x_one, b16)
        for hh in range(HPG):
            rs = slice(hh * P_C, (hh + 1) * P_C)
            s_out_ref[bb, hh] = s[rs] * dan[bb:bb + 1, hh * N_C:(hh + 1) * N_C] + upd[rs]
    yoff_ref[...] = yoff


def _ssd_decode_post_body(yoff_ref, dax_ref, xdt_ref, xs_ref, b_ref, c_ref, z_ref, dx_ref, gn_ref, o_ref):
    gw = HPG * P_C
    bc = b_ref[...] * c_ref[...]
    cbx = [jnp.broadcast_to(jnp.sum(bc[:, g * N_C:(g + 1) * N_C], axis=-1, keepdims=True), (bc.shape[0], gw))
           for g in range(G_C)]
    y = yoff_ref[...] * dax_ref[...] + jnp.concatenate(cbx, axis=1) * xdt_ref[...]
    y = y + dx_ref[...] * xs_ref[...]
    y = y * _silu(z_ref[...])
    o_ref[...] = _group_rms(y, gn_ref[...])


def _ssd_decode(z, xbc, hist, small, conv_w, conv_b, dt_bias, a_log, d_skip, gnorm, s0):
    bn = xbc.shape[0]
    dtrow, arow, _, _ = _ssd_small_params(dt_bias, a_log, H_C)
    wide = jax.ShapeDtypeStruct((bn, D_INNER), F32)
    grp = jax.ShapeDtypeStruct((bn, G_C * N_C), F32)
    plain = pltpu.CompilerParams(vmem_limit_bytes=VMEM_LIMIT_BYTES)
    new_conv, xs, xdt, bm, cm, dax, dan = pl.pallas_call(
        _ssd_decode_pre_body,
        out_shape=[jax.ShapeDtypeStruct(hist.shape, F32), wide, wide, grp, grp, wide,
                   jax.ShapeDtypeStruct((bn, H_C * N_C), F32)],
        compiler_params=plain,
        name="ssd_decode_pre",
    )(xbc, hist, small, conv_w, conv_b.reshape(1, SSD_CONV_CH), dtrow, arow,
      _head_expander(P_C), _head_expander(N_C))
    nb = SUBLANES
    gw = HPG * P_C
    st = pl.BlockSpec((nb, HPG, P_C, N_C), lambda i, g: (i, g, 0, 0))
    s_new, yoff = pl.pallas_call(
        _ssd_decode_state_body,
        grid=(bn // nb, G_C),
        in_specs=[st, pl.BlockSpec((nb, gw), lambda i, g: (i, g)), pl.BlockSpec((nb, N_C), lambda i, g: (i, g)),
                  pl.BlockSpec((nb, N_C), lambda i, g: (i, g)), pl.BlockSpec((nb, HPG * N_C), lambda i, g: (i, g))],
        out_specs=[st, pl.BlockSpec((nb, gw), lambda i, g: (i, g))],
        out_shape=[jax.ShapeDtypeStruct(s0.shape, F32), wide],
        compiler_params=_cp("parallel", "parallel"),
        name="ssd_decode_state",
    )(s0, xdt, bm, cm, dan)
    mix = pl.pallas_call(
        _ssd_decode_post_body,
        out_shape=wide,
        compiler_params=plain,
        name="ssd_decode_post",
    )(yoff, dax, xdt, xs, bm, cm, z, jnp.repeat(d_skip, P_C).reshape(1, D_INNER), gnorm.reshape(1, D_INNER))
    return mix, new_conv, s_new


def _ffn_weights(wg, wu, wd):
    nf = D_FF // FF_CHUNK
    split_cols = lambda w: w.astype(BF16).reshape(D_MODEL, nf, FF_CHUNK).transpose(1, 0, 2)
    return split_cols(wg), split_cols(wu), wd.astype(BF16).reshape(nf, FF_CHUNK, D_MODEL)


def _narrow_weights(w_small):
    n = w_small.shape[1]
    ws = jnp.zeros((D_MODEL, LANES), BF16).at[:, :n].set(w_small.astype(BF16))
    ns = -(-n // 16) * 16
    wst = jnp.zeros((ns, D_MODEL), BF16).at[:n, :].set(w_small.T.astype(BF16))
    return ws, wst


def _even_q_pad(qa):
    bn = qa.shape[0]
    q = qa.reshape(bn, KV_A, G_A, HD_A)
    out = jnp.zeros((bn, KV_A, G_A, KV_A, HD_A), F32)
    for kv in range(KV_A):
        out = out.at[:, kv, :, kv, :].set(q[:, kv])
    return out.reshape(bn, H_A, KV_A * HD_A)


def _even_o_unpad(o8):
    bn = o8.shape[0]
    o = o8.reshape(bn, KV_A, G_A, KV_A, HD_A)
    return jnp.concatenate([o[:, kv, :, kv, :].reshape(bn, G_A * HD_A) for kv in range(KV_A)], axis=1)


def _trunk(x3, states, wts):
    (rel_bias, norm_ff1, norm_mix, norm_ff2, norm_final,
     ff1, ff2, even_w_in, even_w_out, swa_sinks, gdn_conv_w, gdn_A_log, gdn_dt_bias, gdn_norm,
     ssd_w_in, ssd_w_out, ssd_conv_w, ssd_conv_b, ssd_dt_bias, ssd_A_log, ssd_D, ssd_norm) = wts
    bn, l, d = x3.shape
    t = bn * l
    x = x3.reshape(t, d)
    decode = states is not None
    ks, vs, gconv, gssm, sconv, sssm = [], [], [], [], [], []
    depth = norm_ff1.shape[0]
    width = KV_A * HD_A
    for layer in range(depth):
        x = _ffn(x, norm_ff1[layer], *ff1[layer])
        e = layer // 2
        if layer % 2 == 0:
            w_main, ws, wst = even_w_in[e]
            qa, kv, qkvb, z, small, small_t = _inproj(x, norm_mix[layer], w_main, ws, wst,
                                                      (A_Q, 2 * width, B_QKV, B_Z))
            if decode:
                ck = states[0][e].reshape(bn, WINDOW, width)
                cv = states[1][e].reshape(bn, WINDOW, width)
                o8 = _swa_decode(_even_q_pad(qa), kv, ck, cv, rel_bias, swa_sinks[e])
                o_a = _even_o_unpad(o8)
                hist = states[2][e].reshape(bn, (CONV_W - 1) * B_QKV)
                o_b, new_conv, s_new = _gdn_decode(qkvb, hist, z, small, gdn_conv_w[e], gdn_A_log[e],
                                                   gdn_dt_bias[e], gdn_norm[e], states[3][e])
                mix = jnp.concatenate([o_a, o_b], axis=1)
                new_k = jnp.concatenate([ck[:, 1:], kv[:, None, :width]], axis=1)
                new_v = jnp.concatenate([cv[:, 1:], kv[:, None, width:]], axis=1)
                new_conv = new_conv.reshape(bn, CONV_W - 1, B_QKV)
            else:
                o_a = _swa_prompt(qa, kv, rel_bias, swa_sinks[e], bn, l)
                o_b, s_new = _gdn_prompt(qkvb, z, small, small_t, gdn_conv_w[e], gdn_A_log[e],
                                         gdn_dt_bias[e], gdn_norm[e], bn, l)
                mix = jnp.concatenate([o_a, o_b], axis=1)
                kv3 = kv.reshape(bn, l, 2 * width)
                new_k = kv3[:, l - WINDOW:, :width]
                new_v = kv3[:, l - WINDOW:, width:]
                new_conv = qkvb.reshape(bn, l, B_QKV)[:, l - (CONV_W - 1):]
            ks.append(new_k.reshape(bn, WINDOW, KV_A, HD_A))
            vs.append(new_v.reshape(bn, WINDOW, KV_A, HD_A))
            gconv.append(new_conv)
            gssm.append(s_new)
            x = _outproj(x, mix, even_w_out[e])
        else:
            w_main, ws, wst = ssd_w_in[e]
            z, xbc, small, small_t = _inproj(x, norm_mix[layer], w_main, ws, wst, (D_INNER, SSD_CONV_CH))
            if decode:
                hist = states[4][e].reshape(bn, (CONV_W - 1) * SSD_CONV_CH)
                mix, new_conv, s_new = _ssd_decode(z, xbc, hist, small, ssd_conv_w[e], ssd_conv_b[e],
                                                   ssd_dt_bias[e], ssd_A_log[e], ssd_D[e], ssd_norm[e],
                                                   states[5][e])
                new_conv = new_conv.reshape(bn, CONV_W - 1, SSD_CONV_CH)
            else:
                mix, s_new = _ssd_prompt(z, xbc, small, small_t, ssd_conv_w[e], ssd_conv_b[e], ssd_dt_bias[e],
                                         ssd_A_log[e], ssd_D[e], ssd_norm[e], bn, l)
                new_conv = xbc.reshape(bn, l, SSD_CONV_CH)[:, l - (CONV_W - 1):]
            sconv.append(new_conv)
            sssm.append(s_new)
            x = _outproj(x, mix, ssd_w_out[e])
        x = _ffn(x, norm_ff2[layer], *ff2[layer], g_final=norm_final if layer == depth - 1 else None)
    return (x.reshape(bn, l, d), jnp.stack(ks), jnp.stack(vs), jnp.stack(gconv), jnp.stack(gssm),
            jnp.stack(sconv), jnp.stack(sssm))


def kernel(x_prompt, x_sample, cache_swa_k, cache_swa_v, state_gdn_conv, state_gdn_ssm, state_ssd_conv, state_ssd_ssm, rel_bias, norm_ff1, norm_mix, norm_ff2, norm_final, ff1_gate, ff1_up, ff1_down, ff2_gate, ff2_up, ff2_down, even_w_in, even_w_out, swa_sinks, gdn_conv_w, gdn_A_log, gdn_dt_bias, gdn_norm, ssd_w_in, ssd_w_out, ssd_conv_w, ssd_conv_b, ssd_dt_bias, ssd_A_log, ssd_D, ssd_norm):
    depth = norm_ff1.shape[0]
    ff1 = [_ffn_weights(ff1_gate[i], ff1_up[i], ff1_down[i]) for i in range(depth)]
    ff2 = [_ffn_weights(ff2_gate[i], ff2_up[i], ff2_down[i]) for i in range(depth)]
    n_even_main = A_Q + 2 * KV_A * HD_A + B_QKV + B_Z
    even_in = [(even_w_in[e][:, :n_even_main].astype(BF16),) + _narrow_weights(even_w_in[e][:, n_even_main:])
               for e in range(even_w_in.shape[0])]
    n_odd_main = D_INNER + SSD_CONV_CH
    odd_in = [(ssd_w_in[e][:, :n_odd_main].astype(BF16),) + _narrow_weights(ssd_w_in[e][:, n_odd_main:])
              for e in range(ssd_w_in.shape[0])]
    wts = (rel_bias, norm_ff1, norm_mix, norm_ff2, norm_final, ff1, ff2,
           even_in, even_w_out.astype(BF16), swa_sinks, gdn_conv_w, gdn_A_log, gdn_dt_bias, gdn_norm,
           odd_in, ssd_w_out.astype(BF16), ssd_conv_w, ssd_conv_b, ssd_dt_bias, ssd_A_log, ssd_D, ssd_norm)
    y_p, p_k, p_v, p_gconv, p_gssm, p_sconv, p_sssm = _trunk(x_prompt, None, wts)
    states = (cache_swa_k, cache_swa_v, state_gdn_conv, state_gdn_ssm, state_ssd_conv, state_ssd_ssm)
    y_s, s_k, s_v, s_gconv, s_gssm, s_sconv, s_sssm = _trunk(x_sample, states, wts)
    return (y_p, y_s, p_k, p_v, p_gconv, p_gssm, p_sconv, p_sssm,
            s_k, s_v, s_gconv, s_gssm, s_sconv, s_sssm)
```

```python
import functools
import math

import numpy as np
import jax
import jax.numpy as jnp
from jax import lax
from jax.experimental import pallas as pl
from jax.experimental.pallas import tpu as pltpu

F32 = jnp.float32
BF16 = jnp.bfloat16
HI = lax.Precision.HIGHEST

EPS = 1e-6
NEG_INF = -1e30
D_MODEL = 1024
D_FF = 2816
FF_CHUNK = 256
WINDOW = 128
BLOCK = 128
H_A, KV_A, G_A, HD_A = 8, 2, 4, 64
N_BUCKETS, MAX_DIST = 32, 128
H_B, DK_B, DV_B = 4, 128, 128
CONV_W = 4
GDN_CHUNK = 64
D_INNER = 2048
P_C, H_C, N_C, G_C = 64, 32, 128, 4
HPG = H_C // G_C
SSD_CHUNK = 128
A_Q = H_A * HD_A
B_QKV = 3 * H_B * DK_B
B_Z = H_B * DV_B
SSD_CONV_CH = D_INNER + 2 * G_C * N_C
LANES = 128
SUBLANES = 8
VMEM_LIMIT_BYTES = 56 * 1024 * 1024


def _cp(*sem):
    return pltpu.CompilerParams(dimension_semantics=sem, vmem_limit_bytes=VMEM_LIMIT_BYTES)


def _whole(shape):
    nd = len(shape)
    return pl.BlockSpec(shape, lambda *_: (0,) * nd, pipeline_mode=pl.Buffered(1))


def _rms(x, g):
    return x * lax.rsqrt(jnp.mean(x * x, axis=-1, keepdims=True) + EPS) * g


def _silu(x):
    return x * jax.nn.sigmoid(x)


def _softplus(x):
    return jnp.maximum(x, 0.0) + jnp.log1p(jnp.exp(-jnp.abs(x)))


def _dot(a, b):
    return jnp.dot(a, b, preferred_element_type=F32)


def _dot_nt(a, b):
    return lax.dot_general(a, b, (((1,), (1,)), ((), ())), preferred_element_type=F32)


def _dot_tn(a, b):
    return lax.dot_general(a, b, (((0,), (0,)), ((), ())), preferred_element_type=F32)


def _dot_hi(a, b):
    return jnp.dot(a, b, precision=HI, preferred_element_type=F32)


def _expand(x, e):
    hi = x.astype(BF16)
    r = x - hi.astype(F32)
    mid = r.astype(BF16)
    lo = (r - mid.astype(F32)).astype(BF16)
    return _dot(hi, e) + _dot(mid, e) + _dot(lo, e)


def _tri(n, kind):
    r = lax.broadcasted_iota(jnp.int32, (n, n), 0)
    c = lax.broadcasted_iota(jnp.int32, (n, n), 1)
    return {"lower": r >= c, "strict_lower": r > c, "upper": r <= c}[kind]


def _ffn_body(*refs, nf, final):
    if final:
        x_ref, g_ref, wg_ref, wu_ref, wd_ref, gf_ref, o_ref, hn_ref, acc_ref = refs
    else:
        x_ref, g_ref, wg_ref, wu_ref, wd_ref, o_ref, hn_ref, acc_ref = refs
    x = x_ref[...]
    hn_ref[...] = _rms(x, g_ref[...]).astype(BF16)
    acc_ref[...] = jnp.zeros_like(acc_ref)

    def step(f, carry):
        hn = hn_ref[...]
        gate = _dot(hn, wg_ref[f])
        up = _dot(hn, wu_ref[f])
        act = (_silu(gate) * up).astype(BF16)
        acc_ref[...] += _dot(act, wd_ref[f])
        return carry

    lax.fori_loop(0, nf, step, 0)
    y = x + 0.5 * acc_ref[...]
    if final:
        y = _rms(y, gf_ref[...])
    o_ref[...] = y


def _ffn(x, g, wg, wu, wd, g_final=None):
    t, d = x.shape
    tm = min(t, 512)
    nf = wg.shape[0]
    final = g_final is not None
    row = pl.BlockSpec((tm, d), lambda i: (i, 0))
    in_specs = [row, _whole((1, d)), _whole(wg.shape), _whole(wu.shape), _whole(wd.shape)]
    args = [x, g.reshape(1, d), wg, wu, wd]
    if final:
        in_specs.append(_whole((1, d)))
        args.append(g_final.reshape(1, d))
    return pl.pallas_call(
        functools.partial(_ffn_body, nf=nf, final=final),
        grid=(t // tm,),
        in_specs=in_specs,
        out_specs=row,
        out_shape=jax.ShapeDtypeStruct((t, d), F32),
        scratch_shapes=[pltpu.VMEM((tm, d), BF16), pltpu.VMEM((tm, d), F32)],
        compiler_params=_cp("parallel"),
        name="ffn",
    )(*args)


def _inproj_body(x_ref, g_ref, w_ref, ws_ref, wst_ref, *outs, splits):
    hn = _rms(x_ref[...], g_ref[...]).astype(BF16)
    off = 0
    for o_ref, n in zip(outs[:-2], splits):
        o_ref[...] = _dot(hn, w_ref[:, off:off + n])
        off += n
    outs[-2][...] = _dot(hn, ws_ref[...])
    outs[-1][...] = _dot_nt(wst_ref[...], hn)


def _inproj(x, g, w, ws, wst, splits):
    t, d = x.shape
    tm = min(t, 256)
    ns = wst.shape[0]
    row = pl.BlockSpec((tm, d), lambda i: (i, 0))
    out_specs = [pl.BlockSpec((tm, n), lambda i: (i, 0)) for n in splits]
    out_specs += [pl.BlockSpec((tm, LANES), lambda i: (i, 0)), pl.BlockSpec((ns, tm), lambda i: (0, i))]
    out_shape = [jax.ShapeDtypeStruct((t, n), F32) for n in splits]
    out_shape += [jax.ShapeDtypeStruct((t, LANES), F32), jax.ShapeDtypeStruct((ns, t), F32)]
    return pl.pallas_call(
        functools.partial(_inproj_body, splits=splits),
        grid=(t // tm,),
        in_specs=[row, _whole((1, d)), _whole(w.shape), _whole(ws.shape), _whole(wst.shape)],
        out_specs=out_specs,
        out_shape=out_shape,
        compiler_params=_cp("parallel"),
        name="inproj",
    )(x, g.reshape(1, d), w, ws, wst)


def _outproj_body(x_ref, m_ref, w_ref, o_ref):
    o_ref[...] = x_ref[...] + _dot(m_ref[...].astype(BF16), w_ref[...])


def _outproj(x, m, w):
    t, d = x.shape
    k = m.shape[1]
    tm = min(t, 512)
    row = pl.BlockSpec((tm, d), lambda i: (i, 0))
    return pl.pallas_call(
        _outproj_body,
        grid=(t // tm,),
        in_specs=[row, pl.BlockSpec((tm, k), lambda i: (i, 0)), _whole(w.shape)],
        out_specs=row,
        out_shape=jax.ShapeDtypeStruct((t, d), F32),
        compiler_params=_cp("parallel"),
        name="outproj",
    )(x, m, w)


def _t5_bucket_np(dist):
    max_exact = N_BUCKETS // 2
    df = np.maximum(dist, max_exact).astype(np.float32)
    large = max_exact + (np.log(df / np.float32(max_exact)) / np.float32(math.log(MAX_DIST / max_exact))
                         * np.float32(N_BUCKETS - max_exact)).astype(np.int32)
    return np.where(dist < max_exact, dist, np.minimum(large, N_BUCKETS - 1)).astype(np.int32)


def _band_bucket_ids(n_q, n_k, offset):
    d = offset + np.arange(n_q)[:, None] - np.arange(n_k)[None, :]
    valid = (d >= 0) & (d <= WINDOW)
    return np.where(valid, _t5_bucket_np(np.clip(d, 0, WINDOW)), -1).astype(np.int32)


def _bias_from_buckets(bid, rb_ref, h):
    acc = jnp.full(bid.shape, NEG_INF, F32)
    for bk in range(N_BUCKETS):
        acc = jnp.where(bid == bk, rb_ref[bk, h], acc)
    return acc


def _swa_prompt_body(bid_ref, rb_ref, sk_ref, q_ref, kvp_ref, kvc_ref, o_ref, bias_ref):
    first_step = (pl.program_id(0) == 0) & (pl.program_id(1) == 0)

    @pl.when(first_step)
    def _():
        bid = bid_ref[...]
        for h in range(H_A):
            bias_ref[h] = _bias_from_buckets(bid, rb_ref, h)

    col = lax.broadcasted_iota(jnp.int32, (BLOCK, 2 * BLOCK), 1)
    pad_keys = (pl.program_id(1) == 0) & (col < BLOCK)
    kvp = kvp_ref[...]
    kvc = kvc_ref[...]
    scale = HD_A ** -0.5
    outs = []
    for kv in range(KV_A):
        ks = slice(kv * HD_A, (kv + 1) * HD_A)
        vs = slice(KV_A * HD_A + kv * HD_A, KV_A * HD_A + (kv + 1) * HD_A)
        k = jnp.concatenate([kvp[:, ks], kvc[:, ks]], axis=0).astype(BF16)
        v = jnp.concatenate([kvp[:, vs], kvc[:, vs]], axis=0).astype(BF16)
        for g in range(G_A):
            h = kv * G_A + g
            q = q_ref[:, h * HD_A:(h + 1) * HD_A].astype(BF16)
            s = _dot_nt(q, k) * scale + jnp.where(pad_keys, NEG_INF, bias_ref[h])
            sk = sk_ref[h]
            m = jnp.maximum(jnp.max(s, axis=-1, keepdims=True), sk)
            p = jnp.exp(s - m)
            denom = jnp.sum(p, axis=-1, keepdims=True) + jnp.exp(sk - m)
            outs.append(_dot((p / denom).astype(BF16), v))
    o_ref[...] = jnp.concatenate(outs, axis=1).astype(BF16)


def _swa_prompt(qa, kv, rel_bias, sinks, bn, l):
    nb = l // BLOCK
    t = bn * l
    bid = jnp.asarray(_band_bucket_ids(BLOCK, 2 * BLOCK, BLOCK))
    kv_blk = (BLOCK, 2 * KV_A * HD_A)
    return pl.pallas_call(
        _swa_prompt_body,
        grid=(bn, nb),
        in_specs=[
            _whole((BLOCK, 2 * BLOCK)),
            pl.BlockSpec(memory_space=pltpu.SMEM),
            pl.BlockSpec(memory_space=pltpu.SMEM),
            pl.BlockSpec((BLOCK, A_Q), lambda b, i: (b * nb + i, 0)),
            pl.BlockSpec(kv_blk, lambda b, i: (jnp.maximum(b * nb + i - 1, 0), 0)),
            pl.BlockSpec(kv_blk, lambda b, i: (b * nb + i, 0)),
        ],
        out_specs=pl.BlockSpec((BLOCK, A_Q), lambda b, i: (b * nb + i, 0)),
        out_shape=jax.ShapeDtypeStruct((t, A_Q), BF16),
        scratch_shapes=[pltpu.VMEM((H_A, BLOCK, 2 * BLOCK), F32)],
        compiler_params=_cp("arbitrary", "arbitrary"),
        name="swa_prompt",
    )(bid, rel_bias, sinks, qa, kv, kv)


def _swa_decode_body(bid_ref, rb_ref, sk_ref, q_ref, kvn_ref, ck_ref, cv_ref, o_ref):
    bid = bid_ref[...]
    row = lax.broadcasted_iota(jnp.int32, (H_A, 1), 0)
    bias = jnp.zeros((H_A, bid.shape[1]), F32)
    sk = jnp.zeros((H_A, 1), F32)
    for h in range(H_A):
        bias = jnp.where(row == h, _bias_from_buckets(bid, rb_ref, h), bias)
        sk = jnp.where(row == h, sk_ref[h], sk)
    bias_c = bias[:, :WINDOW]
    bias_n = bias[:, WINDOW:WINDOW + 1]
    scale = HD_A ** -0.5
    q = q_ref[...]
    kvn = kvn_ref[...]
    width = KV_A * HD_A
    k_new = kvn[:, None, :width]
    v_new = kvn[:, None, width:]
    s = lax.dot_general(q.astype(BF16), ck_ref[...].astype(BF16), (((2,), (2,)), ((0,), (0,))),
                        preferred_element_type=F32) * scale + bias_c[None]
    s_n = jnp.sum(q * k_new, axis=-1, keepdims=True) * scale + bias_n[None]
    m = jnp.maximum(jnp.maximum(jnp.max(s, axis=-1, keepdims=True), s_n), sk[None])
    p = jnp.exp(s - m)
    p_n = jnp.exp(s_n - m)
    denom = jnp.sum(p, axis=-1, keepdims=True) + p_n + jnp.exp(sk[None] - m)
    o = lax.dot_general((p / denom).astype(BF16), cv_ref[...].astype(BF16), (((2,), (1,)), ((0,), (0,))),
                        preferred_element_type=F32)
    o_ref[...] = o + (p_n / denom) * v_new


def _swa_decode(q_pad, kv_new, cache_k, cache_v, rel_bias, sinks):
    bn = q_pad.shape[0]
    bs = 32
    width = KV_A * HD_A
    ids = _band_bucket_ids(1, WINDOW + 1, WINDOW)
    bid = np.full((1, WINDOW + LANES), -1, np.int32)
    bid[:, :WINDOW + 1] = ids
    return pl.pallas_call(
        _swa_decode_body,
        grid=(bn // bs,),
        in_specs=[
            _whole((1, WINDOW + LANES)),
            pl.BlockSpec(memory_space=pltpu.SMEM),
            pl.BlockSpec(memory_space=pltpu.SMEM),
            pl.BlockSpec((bs, H_A, width), lambda i: (i, 0, 0)),
            pl.BlockSpec((bs, 2 * width), lambda i: (i, 0)),
            pl.BlockSpec((bs, WINDOW, width), lambda i: (i, 0, 0)),
            pl.BlockSpec((bs, WINDOW, width), lambda i: (i, 0, 0)),
        ],
        out_specs=pl.BlockSpec((bs, H_A, width), lambda i: (i, 0, 0)),
        out_shape=jax.ShapeDtypeStruct((bn, H_A, width), F32),
        compiler_params=_cp("parallel"),
        name="swa_decode",
    )(jnp.asarray(bid), rel_bias, sinks, q_pad, kv_new, cache_k, cache_v)


def _conv_rows(x_ref, xbuf_ref, cw_ref, rows):
    xbuf_ref[SUBLANES:SUBLANES + rows, :] = x_ref[...]
    out = xbuf_ref[5:5 + rows, :] * cw_ref[0:1, :]
    for i in range(1, CONV_W):
        out = out + xbuf_ref[5 + i:5 + i + rows, :] * cw_ref[i:i + 1, :]
    return out


def _conv_chunk(x_ref, xbuf_ref, cw_ref, rows):
    out = _conv_rows(x_ref, xbuf_ref, cw_ref, rows)
    xbuf_ref[0:SUBLANES, :] = xbuf_ref[rows:rows + SUBLANES, :]
    return out


def _l2norm(x):
    return x * lax.rsqrt(jnp.sum(x * x, axis=-1, keepdims=True) + EPS)


def _chunk_masks(n, cc):
    r = lax.broadcasted_iota(jnp.int32, (n, n), 0)
    c = lax.broadcasted_iota(jnp.int32, (n, n), 1)
    shift = int(math.log2(cc))
    same = lax.shift_right_logical(r, shift) == lax.shift_right_logical(c, shift)
    return same, same & (r >= c), same & (r > c), same & (r <= c)


def _dot_bf16x3(a, b):
    a_hi = a.astype(BF16)
    a_lo = (a - a_hi.astype(F32)).astype(BF16)
    b_hi = b.astype(BF16)
    b_lo = (b - b_hi.astype(F32)).astype(BF16)
    return _dot(a_hi, b_hi) + _dot(a_hi, b_lo) + _dot(a_lo, b_hi)


def _gdn_prep_body(qkv_ref, halo_ref, sm_ref, smt_ref, cw_ref, arow_ref, dtrow_ref, acol_ref, dtcol_ref,
                   u_ref, w_ref, qg_ref, kd_ref, aqk_ref, gl_ref, xbuf_ref):
    rows = qkv_ref.shape[0]
    cc = GDN_CHUNK
    xbuf_ref[0:SUBLANES, :] = jnp.where(pl.program_id(1) == 0, 0.0, halo_ref[...])
    conv = _silu(_conv_rows(qkv_ref, xbuf_ref, cw_ref, rows))
    nq = H_B * DK_B
    sm = sm_ref[...]
    beta_all = jax.nn.sigmoid(sm)
    g_all = -jnp.exp(arow_ref[...]) * _softplus(sm + dtrow_ref[...])
    gt_all = -jnp.exp(acol_ref[...]) * _softplus(smt_ref[...] + dtcol_ref[...])
    same, lower, strict, upper = _chunk_masks(rows, cc)
    gc = _dot_hi(lower.astype(F32), g_all)
    gct = _dot_hi(gt_all, upper.astype(F32))
    gsum = _dot_hi(same.astype(F32), g_all)
    heads = range(H_B)
    qh, kh16, kb, decay, egc = [], [], [], [], []
    for h in heads:
        q = _l2norm(conv[:, h * DK_B:(h + 1) * DK_B]) * (DK_B ** -0.5)
        k = _l2norm(conv[:, nq + h * DK_B:nq + (h + 1) * DK_B])
        v = conv[:, 2 * nq + h * DV_B:2 * nq + (h + 1) * DV_B]
        beta = beta_all[:, h:h + 1]
        gcol = gc[:, H_B + h:H_B + h + 1]
        grow = gct[H_B + h:H_B + h + 1, :]
        decay.append(jnp.exp(jnp.where(lower, gcol - grow, -jnp.inf)))
        eg = jnp.exp(gcol)
        qg_ref[:, h * DK_B:(h + 1) * DK_B] = (q * eg).astype(BF16)
        kd_ref[:, h * DK_B:(h + 1) * DK_B] = (k * jnp.exp(gsum[:, H_B + h:H_B + h + 1] - gcol)).astype(BF16)
        qh.append(q.astype(BF16))
        kh16.append(k.astype(BF16))
        kb.append(k * beta)
        egc.append((v * beta, eg))
    a1 = [-jnp.where(strict, _dot_nt(kb[h].astype(BF16), kh16[h]) * decay[h], 0.0) for h in heads]
    for h in heads:
        aqk = _dot_nt(qh[h], kh16[h]) * decay[h]
        blocks = [aqk[i * cc:(i + 1) * cc, i * cc:(i + 1) * cc] for i in range(rows // cc)]
        aqk_ref[:, h * cc:(h + 1) * cc] = jnp.concatenate(blocks, axis=0).astype(BF16)
    sq = lambda xs: [_dot(x.astype(BF16), x.astype(BF16)) for x in xs]
    pair = lambda xs, ys: [x + y + _dot(x.astype(BF16), y.astype(BF16)) for x, y in zip(xs, ys)]
    a2 = sq(a1)
    a4 = sq(a2)
    x1 = pair(a1, a2)
    a8 = sq(a4)
    a16 = sq(a8)
    x2 = pair(a4, a8)
    a32 = sq(a16)
    x12 = pair(x1, x2)
    x3 = pair(a16, a32)
    r = pair(x12, x3)
    for h in heads:
        vb, eg = egc[h]
        rhs = jnp.concatenate([vb, kb[h] * eg], axis=1)
        sol = rhs + _dot_bf16x3(r[h], rhs)
        u_ref[:, h * DV_B:(h + 1) * DV_B] = sol[:, :DV_B]
        w_ref[:, h * DK_B:(h + 1) * DK_B] = sol[:, DV_B:].astype(BF16)
    for i in range(rows // cc):
        gl = [jnp.broadcast_to(jnp.exp(gsum[i * cc:i * cc + 1, H_B + h:H_B + h + 1]), (SUBLANES, DV_B))
              for h in heads]
        gl_ref[0, i] = jnp.concatenate(gl, axis=1)


def _gdn_scan_body(u_ref, w_ref, qg_ref, kd_ref, aqk_ref, gl_ref, z_ref, gn_ref, o_ref, s_out_ref, s_ref):
    c = pl.program_id(0)
    bn = u_ref.shape[0]
    cc = GDN_CHUNK

    @pl.when(c == 0)
    def _():
        s_ref[...] = jnp.zeros_like(s_ref)

    gn = gn_ref[...]
    group = 2
    for b0 in range(0, bn, group):
        combos = [(b, h) for b in range(b0, b0 + group) for h in range(H_B)]
        res = {}
        for b, h in combos:
            hs = slice(h * DK_B, (h + 1) * DK_B)
            wq = jnp.concatenate([w_ref[b, :, hs], qg_ref[b, :, hs]], axis=0)
            res[b, h] = _dot(wq, s_ref[b, h].astype(BF16))
        v16 = {}
        for b, h in combos:
            hs = slice(h * DV_B, (h + 1) * DV_B)
            v16[b, h] = (u_ref[b, :, hs] - res[b, h][:cc]).astype(BF16)
        for b, h in combos:
            hs = slice(h * DV_B, (h + 1) * DV_B)
            o = res[b, h][cc:] + _dot(aqk_ref[b, :, h * cc:(h + 1) * cc], v16[b, h])
            upd = _dot_tn(kd_ref[b, :, hs], v16[b, h])
            s_ref[b, h] = s_ref[b, h] * gl_ref[b, 0, 0:1, hs] + upd
            o_ref[b, :, hs] = (_rms(o, gn) * _silu(z_ref[b, :, hs])).astype(BF16)

    @pl.when(c == pl.num_programs(0) - 1)
    def _():
        s_out_ref[...] = s_ref[...]


def _gdn_prompt(qkvb, z, small, small_t, conv_w, a_log, dt_bias, gnorm, bn, l):
    rows = 4 * GDN_CHUNK
    cc = GDN_CHUNK
    nb = l // rows
    nc = l // cc
    t = bn * l
    ns = small_t.shape[0]
    nq = H_B * DK_B
    pad_row = lambda v: jnp.zeros((1, LANES), F32).at[0, H_B:2 * H_B].set(v)
    pad_col = lambda v: jnp.zeros((ns, 1), F32).at[H_B:2 * H_B, 0].set(v)
    tok = lambda n: pl.BlockSpec((rows, n), lambda b, i: (b * nb + i, 0))
    halo = pl.BlockSpec((SUBLANES, B_QKV), lambda b, i: (jnp.maximum((b * nb + i) * (rows // SUBLANES) - 1, 0), 0))
    u, w, qg, kd, aqk, gl = pl.pallas_call(
        _gdn_prep_body,
        grid=(bn, nb),
        in_specs=[
            tok(B_QKV), halo, tok(LANES),
            pl.BlockSpec((ns, rows), lambda b, i: (0, b * nb + i)),
            _whole((CONV_W, B_QKV)), _whole((1, LANES)), _whole((1, LANES)), _whole((ns, 1)), _whole((ns, 1)),
        ],
        out_specs=[tok(nq), tok(nq), tok(nq), tok(nq), tok(H_B * cc),
                   pl.BlockSpec((1, rows // cc, SUBLANES, nq), lambda b, i: (b, i, 0, 0))],
        out_shape=[jax.ShapeDtypeStruct((t, nq), F32)] + [jax.ShapeDtypeStruct((t, nq), BF16)] * 3
        + [jax.ShapeDtypeStruct((t, H_B * cc), BF16), jax.ShapeDtypeStruct((bn, nc, SUBLANES, nq), F32)],
        scratch_shapes=[pltpu.VMEM((rows + SUBLANES, B_QKV), F32)],
        compiler_params=_cp("parallel", "parallel"),
        name="gdn_prep",
    )(qkvb, qkvb, small, small_t, conv_w, pad_row(a_log), pad_row(dt_bias), pad_col(a_log), pad_col(dt_bias))
    seq = lambda n: pl.BlockSpec((bn, cc, n), lambda c: (0, c, 0))
    r3 = lambda a: a.reshape(bn, l, a.shape[1])
    o, s_new = pl.pallas_call(
        _gdn_scan_body,
        grid=(nc,),
        in_specs=[seq(nq), seq(nq), seq(nq), seq(nq), seq(H_B * cc),
                  pl.BlockSpec((bn, 1, SUBLANES, nq), lambda c: (0, c, 0, 0)), seq(B_Z), _whole((1, DV_B))],
        out_specs=[seq(B_Z), pl.BlockSpec((bn, H_B, DK_B, DV_B), lambda c: (0, 0, 0, 0))],
        out_shape=[jax.ShapeDtypeStruct((bn, l, B_Z), BF16), jax.ShapeDtypeStruct((bn, H_B, DK_B, DV_B), F32)],
        scratch_shapes=[pltpu.VMEM((bn, H_B, DK_B, DV_B), F32)],
        compiler_params=_cp("arbitrary"),
        name="gdn_scan",
    )(r3(u), r3(w), r3(qg), r3(kd), r3(aqk), gl, r3(z), gnorm.reshape(1, DV_B))
    return o.reshape(t, B_Z), s_new


def _conv_step(x, hist_ref, cw_ref, new_ref, ch):
    out = hist_ref[:, 0:ch] * cw_ref[0:1, :]
    for i in range(1, CONV_W - 1):
        out = out + hist_ref[:, i * ch:(i + 1) * ch] * cw_ref[i:i + 1, :]
    out = out + x * cw_ref[CONV_W - 1:CONV_W, :]
    new_ref[:, 0:(CONV_W - 2) * ch] = hist_ref[:, ch:(CONV_W - 1) * ch]
    new_ref[:, (CONV_W - 2) * ch:] = x
    return out


def _gdn_decode_pre_body(qkv_ref, hist_ref, sm_ref, cw_ref, arow_ref, dtrow_ref,
                         new_ref, w_ref, qg_ref, k_ref, u_ref, qk_ref, gl_ref):
    conv = _silu(_conv_step(qkv_ref[...], hist_ref, cw_ref, new_ref, B_QKV))
    nq = H_B * DK_B
    sm = sm_ref[...]
    beta_all = jax.nn.sigmoid(sm)
    g_all = -jnp.exp(arow_ref[...]) * _softplus(sm + dtrow_ref[...])
    for h in range(H_B):
        hs = slice(h * DK_B, (h + 1) * DK_B)
        qh = _l2norm(conv[:, hs]) * (DK_B ** -0.5)
        kh = _l2norm(conv[:, nq + h * DK_B:nq + (h + 1) * DK_B])
        vh = conv[:, 2 * nq + h * DV_B:2 * nq + (h + 1) * DV_B]
        beta = beta_all[:, h:h + 1]
        eg = jnp.exp(g_all[:, H_B + h:H_B + h + 1])
        w_ref[:, hs] = kh * beta * eg
        qg_ref[:, hs] = qh * eg
        k_ref[:, hs] = kh
        u_ref[:, hs] = vh * beta
        qk_ref[:, hs] = jnp.broadcast_to(jnp.sum(qh * kh, axis=-1, keepdims=True), qh.shape)
        gl_ref[:, hs] = jnp.broadcast_to(eg, qh.shape)


def _gdn_decode_state_body(s_ref, w_ref, qg_ref, k_ref, u_ref, qk_ref, gl_ref, z_ref, gn_ref,
                           s_out_ref, o_ref):
    nb = s_ref.shape[0]
    row = lax.broadcasted_iota(jnp.int32, (nb, 1), 0)
    wq = jnp.concatenate([w_ref[...], qg_ref[...]], axis=0).astype(BF16)
    ws = jnp.zeros((nb, DV_B), F32)
    qs = jnp.zeros((nb, DV_B), F32)
    for bb in range(nb):
        res = _dot(wq, s_ref[bb, 0].astype(BF16))
        ws = jnp.where(row == bb, res[:nb], ws)
        qs = jnp.where(row == bb, res[nb:], qs)
    v_new = u_ref[...] - ws
    o = qs + qk_ref[...] * v_new
    o_ref[...] = _rms(o, gn_ref[...]) * _silu(z_ref[...])
    k = k_ref[...]
    v16 = v_new.astype(BF16)
    gl = gl_ref[...]
    for bb in range(nb):
        k_one = jnp.where(row == bb, k, 0.0).astype(BF16)
        s_out_ref[bb, 0] = s_ref[bb, 0] * gl[bb:bb + 1, :] + _dot_tn(k_one, v16)


def _gdn_decode(qkvb, hist, z, small, conv_w, a_log, dt_bias, gnorm, s0):
    bn = qkvb.shape[0]
    pad_row = lambda v: jnp.zeros((1, LANES), F32).at[0, H_B:2 * H_B].set(v)
    wide = jax.ShapeDtypeStruct((bn, H_B * DK_B), F32)
    new_conv, w, qg, k, u, qk, gl = pl.pallas_call(
        _gdn_decode_pre_body,
        out_shape=[jax.ShapeDtypeStruct(hist.shape, F32)] + [wide] * 6,
        compiler_params=pltpu.CompilerParams(vmem_limit_bytes=VMEM_LIMIT_BYTES),
        name="gdn_decode_pre",
    )(qkvb, hist, small, conv_w, pad_row(a_log), pad_row(dt_bias))
    nb = SUBLANES
    vec = pl.BlockSpec((nb, DK_B), lambda i, h: (i, h))
    st = pl.BlockSpec((nb, 1, DK_B, DV_B), lambda i, h: (i, h, 0, 0))
    s_new, o = pl.pallas_call(
        _gdn_decode_state_body,
        grid=(bn // nb, H_B),
        in_specs=[st, vec, vec, vec, vec, vec, vec, vec, _whole((1, DV_B))],
        out_specs=[st, vec],
        out_shape=[jax.ShapeDtypeStruct(s0.shape, F32), wide],
        compiler_params=_cp("parallel", "parallel"),
        name="gdn_decode_state",
    )(s0, w, qg, k, u, qk, gl, z, gnorm.reshape(1, DV_B))
    return o, new_conv, s_new


def _head_expander(width):
    e = np.zeros((LANES, H_C * width), np.float32)
    for h in range(H_C):
        e[h, h * width:(h + 1) * width] = 1.0
    return jnp.asarray(e, BF16)


def _group_rms(y, g):
    gs = D_INNER // G_C
    parts = [_rms(y[:, i * gs:(i + 1) * gs], g[:, i * gs:(i + 1) * gs]) for i in range(G_C)]
    return jnp.concatenate(parts, axis=1)


def _ssd_prompt_body(z_ref, xbc_ref, sm_ref, smt_ref, cw_ref, cb_ref, dtrow_ref, arow_ref, dtcol_ref,
                     acol_ref, dx_ref, gn_ref, e_ref, o_ref, s_out_ref, xbuf_ref, s_ref):
    c = pl.program_id(1)
    rows = xbc_ref.shape[0]

    @pl.when(c == 0)
    def _():
        xbuf_ref[0:SUBLANES, :] = jnp.zeros((SUBLANES, xbuf_ref.shape[1]), F32)
        s_ref[...] = jnp.zeros_like(s_ref)

    xbc = _silu(_conv_chunk(xbc_ref, xbuf_ref, cw_ref, rows) + cb_ref[...])
    xs = xbc[:, :D_INNER]
    bm = xbc[:, D_INNER:D_INNER + G_C * N_C]
    cm = xbc[:, D_INNER + G_C * N_C:]
    e = e_ref[...]
    lower = _tri(rows, "lower")
    dt = _softplus(sm_ref[...] + dtrow_ref[...])
    acum = _dot_hi(lower.astype(F32), dt * -jnp.exp(arow_ref[...]))
    a_t = _softplus(smt_ref[...] + dtcol_ref[...]) * -jnp.exp(acol_ref[...])
    acum_t = _dot_hi(a_t, _tri(rows, "upper").astype(F32))
    xdt = xs * _expand(dt, e)
    acum_x = _expand(acum, e)
    last_x = acum_x[rows - 1:rows, :]
    xdte = (xdt * jnp.exp(last_x - acum_x)).astype(BF16)
    scale_y = jnp.exp(acum_x)
    chunk_decay = jnp.exp(last_x)
    xdt16 = xdt.astype(BF16)
    gw = HPG * P_C
    ys = []
    for g in range(G_C):
        bg = bm[:, g * N_C:(g + 1) * N_C]
        cg16 = cm[:, g * N_C:(g + 1) * N_C].astype(BF16)
        cb = _dot_nt(cg16, bg.astype(BF16))
        yg = []
        for hh in range(HPG):
            h = g * HPG + hh
            lmat = jnp.exp(jnp.where(lower, acum[:, h:h + 1] - acum_t[h:h + 1, :], -jnp.inf))
            yg.append(_dot((cb * lmat).astype(BF16), xdt16[:, h * P_C:(h + 1) * P_C]))
        gs = slice(g * gw, (g + 1) * gw)
        sg = s_ref[:, gs]
        y_off = _dot(cg16, sg.astype(BF16)) * scale_y[:, gs]
        ys.append(jnp.concatenate(yg, axis=1) + y_off)
        s_ref[:, gs] = sg * chunk_decay[:, gs] + _dot(bg.T.astype(BF16), xdte[:, gs])
    y = jnp.concatenate(ys, axis=1) + dx_ref[...] * xs
    y = y * _silu(z_ref[...])
    o_ref[...] = _group_rms(y, gn_ref[...]).astype(BF16)

    @pl.when(c == pl.num_programs(1) - 1)
    def _():
        s_out_ref[0] = s_ref[...].T.reshape(H_C, P_C, N_C)


def _ssd_small_params(dt_bias, a_log, ns):
    row = lambda v: jnp.zeros((1, LANES), F32).at[0, :H_C].set(v)
    col = lambda v: jnp.zeros((ns, 1), F32).at[:H_C, 0].set(v)
    return row(dt_bias), row(a_log), col(dt_bias), col(a_log)


def _ssd_prompt(z, xbc, small, small_t, conv_w, conv_b, dt_bias, a_log, d_skip, gnorm, bn, l):
    rows = SSD_CHUNK
    nc = l // rows
    t = bn * l
    ns = small_t.shape[0]
    dtrow, arow, dtcol, acol = _ssd_small_params(dt_bias, a_log, ns)
    tok = lambda n: pl.BlockSpec((rows, n), lambda b, c: (b * nc + c, 0))
    return pl.pallas_call(
        _ssd_prompt_body,
        grid=(bn, nc),
        in_specs=[
            tok(D_INNER), tok(SSD_CONV_CH), tok(LANES),
            pl.BlockSpec((ns, rows), lambda b, c: (0, b * nc + c)),
            _whole((CONV_W, SSD_CONV_CH)), _whole((1, SSD_CONV_CH)),
            _whole((1, LANES)), _whole((1, LANES)), _whole((ns, 1)), _whole((ns, 1)),
            _whole((1, D_INNER)), _whole((1, D_INNER)), _whole((LANES, D_INNER)),
        ],
        out_specs=[tok(D_INNER), pl.BlockSpec((1, H_C, P_C, N_C), lambda b, c: (b, 0, 0, 0))],
        out_shape=[jax.ShapeDtypeStruct((t, D_INNER), BF16), jax.ShapeDtypeStruct((bn, H_C, P_C, N_C), F32)],
        scratch_shapes=[pltpu.VMEM((rows + SUBLANES, SSD_CONV_CH), F32), pltpu.VMEM((N_C, D_INNER), F32)],
        compiler_params=_cp("arbitrary", "arbitrary"),
        name="ssd_prompt",
    )(z, xbc, small, small_t, conv_w, conv_b.reshape(1, SSD_CONV_CH), dtrow, arow, dtcol, acol,
      jnp.repeat(d_skip, P_C).reshape(1, D_INNER), gnorm.reshape(1, D_INNER), _head_expander(P_C))


def _ssd_decode_pre_body(xbc_ref, hist_ref, sm_ref, cw_ref, cb_ref, dtrow_ref, arow_ref, e_ref, en_ref,
                         new_ref, xs_ref, xdt_ref, b_ref, c_ref, dax_ref, dan_ref):
    xbc = _silu(_conv_step(xbc_ref[...], hist_ref, cw_ref, new_ref, SSD_CONV_CH) + cb_ref[...])
    xs = xbc[:, :D_INNER]
    dt = _softplus(sm_ref[...] + dtrow_ref[...])
    a = dt * -jnp.exp(arow_ref[...])
    xs_ref[...] = xs
    xdt_ref[...] = xs * _expand(dt, e_ref[...])
    b_ref[...] = xbc[:, D_INNER:D_INNER + G_C * N_C]
    c_ref[...] = xbc[:, D_INNER + G_C * N_C:]
    dax_ref[...] = jnp.exp(_expand(a, e_ref[...]))
    dan_ref[...] = jnp.exp(_expand(a, en_ref[...]))


def _ssd_decode_state_body(s_ref, xdt_ref, b_ref, c_ref, dan_ref, s_out_ref, yoff_ref):
    nb = s_ref.shape[0]
    gw = HPG * P_C
    row = lax.broadcasted_iota(jnp.int32, (nb, 1), 0)
    xdt = xdt_ref[...]
    b16 = b_ref[...].astype(BF16)
    c16 = c_ref[...].astype(BF16)
    dan = dan_ref[...]
    yoff = jnp.zeros((nb, gw), F32)
    for bb in range(nb):
        s = s_ref[bb].reshape(gw, N_C)
        res = _dot_nt(c16, s.astype(BF16))
        yoff = jnp.where(row == bb, res, yoff)
        x_one = jnp.where(row == bb, xdt, 0.0).astype(BF16)
        upd = _dot_tn(x_one, b16)
        for hh in range(HPG):
            rs = slice(hh * P_C, (hh + 1) * P_C)
            s_out_ref[bb, hh] = s[rs] * dan[bb:bb + 1, hh * N_C:(hh + 1) * N_C] + upd[rs]
    yoff_ref[...] = yoff


def _ssd_decode_post_body(yoff_ref, dax_ref, xdt_ref, xs_ref, b_ref, c_ref, z_ref, dx_ref, gn_ref, o_ref):
    gw = HPG * P_C
    bc = b_ref[...] * c_ref[...]
    cbx = [jnp.broadcast_to(jnp.sum(bc[:, g * N_C:(g + 1) * N_C], axis=-1, keepdims=True), (bc.shape[0], gw))
           for g in range(G_C)]
    y = yoff_ref[...] * dax_ref[...] + jnp.concatenate(cbx, axis=1) * xdt_ref[...]
    y = y + dx_ref[...] * xs_ref[...]
    y = y * _silu(z_ref[...])
    o_ref[...] = _group_rms(y, gn_ref[...])


def _ssd_decode(z, xbc, hist, small, conv_w, conv_b, dt_bias, a_log, d_skip, gnorm, s0):
    bn = xbc.shape[0]
    dtrow, arow, _, _ = _ssd_small_params(dt_bias, a_log, H_C)
    wide = jax.ShapeDtypeStruct((bn, D_INNER), F32)
    grp = jax.ShapeDtypeStruct((bn, G_C * N_C), F32)
    plain = pltpu.CompilerParams(vmem_limit_bytes=VMEM_LIMIT_BYTES)
    new_conv, xs, xdt, bm, cm, dax, dan = pl.pallas_call(
        _ssd_decode_pre_body,
        out_shape=[jax.ShapeDtypeStruct(hist.shape, F32), wide, wide, grp, grp, wide,
                   jax.ShapeDtypeStruct((bn, H_C * N_C), F32)],
        compiler_params=plain,
        name="ssd_decode_pre",
    )(xbc, hist, small, conv_w, conv_b.reshape(1, SSD_CONV_CH), dtrow, arow,
      _head_expander(P_C), _head_expander(N_C))
    nb = SUBLANES
    gw = HPG * P_C
    st = pl.BlockSpec((nb, HPG, P_C, N_C), lambda i, g: (i, g, 0, 0))
    s_new, yoff = pl.pallas_call(
        _ssd_decode_state_body,
        grid=(bn // nb, G_C),
        in_specs=[st, pl.BlockSpec((nb, gw), lambda i, g: (i, g)), pl.BlockSpec((nb, N_C), lambda i, g: (i, g)),
                  pl.BlockSpec((nb, N_C), lambda i, g: (i, g)), pl.BlockSpec((nb, HPG * N_C), lambda i, g: (i, g))],
        out_specs=[st, pl.BlockSpec((nb, gw), lambda i, g: (i, g))],
        out_shape=[jax.ShapeDtypeStruct(s0.shape, F32), wide],
        compiler_params=_cp("parallel", "parallel"),
        name="ssd_decode_state",
    )(s0, xdt, bm, cm, dan)
    mix = pl.pallas_call(
        _ssd_decode_post_body,
        out_shape=wide,
        compiler_params=plain,
        name="ssd_decode_post",
    )(yoff, dax, xdt, xs, bm, cm, z, jnp.repeat(d_skip, P_C).reshape(1, D_INNER), gnorm.reshape(1, D_INNER))
    return mix, new_conv, s_new


def _ffn_weights(wg, wu, wd):
    nf = D_FF // FF_CHUNK
    split_cols = lambda w: w.astype(BF16).reshape(D_MODEL, nf, FF_CHUNK).transpose(1, 0, 2)
    return split_cols(wg), split_cols(wu), wd.astype(BF16).reshape(nf, FF_CHUNK, D_MODEL)


def _narrow_weights(w_small):
    n = w_small.shape[1]
    ws = jnp.zeros((D_MODEL, LANES), BF16).at[:, :n].set(w_small.astype(BF16))
    ns = -(-n // 16) * 16
    wst = jnp.zeros((ns, D_MODEL), BF16).at[:n, :].set(w_small.T.astype(BF16))
    return ws, wst


def _even_q_pad(qa):
    bn = qa.shape[0]
    q = qa.reshape(bn, KV_A, G_A, HD_A)
    out = jnp.zeros((bn, KV_A, G_A, KV_A, HD_A), F32)
    for kv in range(KV_A):
        out = out.at[:, kv, :, kv, :].set(q[:, kv])
    return out.reshape(bn, H_A, KV_A * HD_A)


def _even_o_unpad(o8):
    bn = o8.shape[0]
    o = o8.reshape(bn, KV_A, G_A, KV_A, HD_A)
    return jnp.concatenate([o[:, kv, :, kv, :].reshape(bn, G_A * HD_A) for kv in range(KV_A)], axis=1)


def _trunk(x3, states, wts):
    (rel_bias, norm_ff1, norm_mix, norm_ff2, norm_final,
     ff1, ff2, even_w_in, even_w_out, swa_sinks, gdn_conv_w, gdn_A_log, gdn_dt_bias, gdn_norm,
     ssd_w_in, ssd_w_out, ssd_conv_w, ssd_conv_b, ssd_dt_bias, ssd_A_log, ssd_D, ssd_norm) = wts
    bn, l, d = x3.shape
    t = bn * l
    x = x3.reshape(t, d)
    decode = states is not None
    ks, vs, gconv, gssm, sconv, sssm = [], [], [], [], [], []
    depth = norm_ff1.shape[0]
    width = KV_A * HD_A
    for layer in range(depth):
        x = _ffn(x, norm_ff1[layer], *ff1[layer])
        e = layer // 2
        if layer % 2 == 0:
            w_main, ws, wst = even_w_in[e]
            qa, kv, qkvb, z, small, small_t = _inproj(x, norm_mix[layer], w_main, ws, wst,
                                                      (A_Q, 2 * width, B_QKV, B_Z))
            if decode:
                ck = states[0][e].reshape(bn, WINDOW, width)
                cv = states[1][e].reshape(bn, WINDOW, width)
                o8 = _swa_decode(_even_q_pad(qa), kv, ck, cv, rel_bias, swa_sinks[e])
                o_a = _even_o_unpad(o8)
                hist = states[2][e].reshape(bn, (CONV_W - 1) * B_QKV)
                o_b, new_conv, s_new = _gdn_decode(qkvb, hist, z, small, gdn_conv_w[e], gdn_A_log[e],
                                                   gdn_dt_bias[e], gdn_norm[e], states[3][e])
                mix = jnp.concatenate([o_a, o_b], axis=1)
                new_k = jnp.concatenate([ck[:, 1:], kv[:, None, :width]], axis=1)
                new_v = jnp.concatenate([cv[:, 1:], kv[:, None, width:]], axis=1)
                new_conv = new_conv.reshape(bn, CONV_W - 1, B_QKV)
            else:
                o_a = _swa_prompt(qa, kv, rel_bias, swa_sinks[e], bn, l)
                o_b, s_new = _gdn_prompt(qkvb, z, small, small_t, gdn_conv_w[e], gdn_A_log[e],
                                         gdn_dt_bias[e], gdn_norm[e], bn, l)
                mix = jnp.concatenate([o_a, o_b], axis=1)
                kv3 = kv.reshape(bn, l, 2 * width)
                new_k = kv3[:, l - WINDOW:, :width]
                new_v = kv3[:, l - WINDOW:, width:]
                new_conv = qkvb.reshape(bn, l, B_QKV)[:, l - (CONV_W - 1):]
            ks.append(new_k.reshape(bn, WINDOW, KV_A, HD_A))
            vs.append(new_v.reshape(bn, WINDOW, KV_A, HD_A))
            gconv.append(new_conv)
            gssm.append(s_new)
            x = _outproj(x, mix, even_w_out[e])
        else:
            w_main, ws, wst = ssd_w_in[e]
            z, xbc, small, small_t = _inproj(x, norm_mix[layer], w_main, ws, wst, (D_INNER, SSD_CONV_CH))
            if decode:
                hist = states[4][e].reshape(bn, (CONV_W - 1) * SSD_CONV_CH)
                mix, new_conv, s_new = _ssd_decode(z, xbc, hist, small, ssd_conv_w[e], ssd_conv_b[e],
                                                   ssd_dt_bias[e], ssd_A_log[e], ssd_D[e], ssd_norm[e],
                                                   states[5][e])
                new_conv = new_conv.reshape(bn, CONV_W - 1, SSD_CONV_CH)
            else:
                mix, s_new = _ssd_prompt(z, xbc, small, small_t, ssd_conv_w[e], ssd_conv_b[e], ssd_dt_bias[e],
                                         ssd_A_log[e], ssd_D[e], ssd_norm[e], bn, l)
                new_conv = xbc.reshape(bn, l, SSD_CONV_CH)[:, l - (CONV_W - 1):]
            sconv.append(new_conv)
            sssm.append(s_new)
            x = _outproj(x, mix, ssd_w_out[e])
        x = _ffn(x, norm_ff2[layer], *ff2[layer], g_final=norm_final if layer == depth - 1 else None)
    return (x.reshape(bn, l, d), jnp.stack(ks), jnp.stack(vs), jnp.stack(gconv), jnp.stack(gssm),
            jnp.stack(sconv), jnp.stack(sssm))


def kernel(x_prompt, x_sample, cache_swa_k, cache_swa_v, state_gdn_conv, state_gdn_ssm, state_ssd_conv, state_ssd_ssm, rel_bias, norm_ff1, norm_mix, norm_ff2, norm_final, ff1_gate, ff1_up, ff1_down, ff2_gate, ff2_up, ff2_down, even_w_in, even_w_out, swa_sinks, gdn_conv_w, gdn_A_log, gdn_dt_bias, gdn_norm, ssd_w_in, ssd_w_out, ssd_conv_w, ssd_conv_b, ssd_dt_bias, ssd_A_log, ssd_D, ssd_norm):
    depth = norm_ff1.shape[0]
    ff1 = [_ffn_weights(ff1_gate[i], ff1_up[i], ff1_down[i]) for i in range(depth)]
    ff2 = [_ffn_weights(ff2_gate[i], ff2_up[i], ff2_down[i]) for i in range(depth)]
    n_even_main = A_Q + 2 * KV_A * HD_A + B_QKV + B_Z
    even_in = [(even_w_in[e][:, :n_even_main].astype(BF16),) + _narrow_weights(even_w_in[e][:, n_even_main:])
               for e in range(even_w_in.shape[0])]
    n_odd_main = D_INNER + SSD_CONV_CH
    odd_in = [(ssd_w_in[e][:, :n_odd_main].astype(BF16),) + _narrow_weights(ssd_w_in[e][:, n_odd_main:])
              for e in range(ssd_w_in.shape[0])]
    wts = (rel_bias, norm_ff1, norm_mix, norm_ff2, norm_final, ff1, ff2,
           even_in, even_w_out.astype(BF16), swa_sinks, gdn_conv_w, gdn_A_log, gdn_dt_bias, gdn_norm,
           odd_in, ssd_w_out.astype(BF16), ssd_conv_w, ssd_conv_b, ssd_dt_bias, ssd_A_log, ssd_D, ssd_norm)
    y_p, p_k, p_v, p_gconv, p_gssm, p_sconv, p_sssm = _trunk(x_prompt, None, wts)
    states = (cache_swa_k, cache_swa_v, state_gdn_conv, state_gdn_ssm, state_ssd_conv, state_ssd_ssm)
    y_s, s_k, s_v, s_gconv, s_gssm, s_sconv, s_sssm = _trunk(x_sample, states, wts)
    return (y_p, y_s, p_k, p_v, p_gconv, p_gssm, p_sconv, p_sssm,
            s_k, s_v, s_gconv, s_gssm, s_sconv, s_sssm)
```

```python
import functools
import math

import numpy as np
import jax
import jax.numpy as jnp
from jax import lax
from jax.experimental import pallas as pl
from jax.experimental.pallas import tpu as pltpu

F32 = jnp.float32
BF16 = jnp.bfloat16
HI = lax.Precision.HIGHEST

EPS = 1e-6
NEG_INF = -1e30
D_MODEL = 1024
WINDOW = 128
BLOCK = 128
H_A, KV_A, G_A, HD_A = 8, 2, 4, 64
N_BUCKETS, MAX_DIST = 32, 128
H_B, DK_B, DV_B = 4, 128, 128
CONV_W = 4
GDN_CHUNK = 64
D_INNER = 2048
P_C, H_C, N_C, G_C = 64, 32, 128, 4
HPG = H_C // G_C
SSD_CHUNK = 128
A_Q = H_A * HD_A
B_QKV = 3 * H_B * DK_B
B_Z = H_B * DV_B
SSD_CONV_CH = D_INNER + 2 * G_C * N_C
LANES = 128
SUBLANES = 8
VMEM_LIMIT_BYTES = 56 * 1024 * 1024


def _cp(*sem):
    return pltpu.CompilerParams(dimension_semantics=sem, vmem_limit_bytes=VMEM_LIMIT_BYTES)


def _whole(shape):
    nd = len(shape)
    return pl.BlockSpec(shape, lambda *_: (0,) * nd, pipeline_mode=pl.Buffered(1))


def _rms(x, g):
    return x * lax.rsqrt(jnp.mean(x * x, axis=-1, keepdims=True) + EPS) * g


def _silu(x):
    return x * jax.nn.sigmoid(x)


def _softplus(x):
    return jnp.maximum(x, 0.0) + jnp.log1p(jnp.exp(-jnp.abs(x)))


def _dot(a, b):
    return jnp.dot(a, b, preferred_element_type=F32)


def _dot_nt(a, b):
    return lax.dot_general(a, b, (((1,), (1,)), ((), ())), preferred_element_type=F32)


def _dot_tn(a, b):
    return lax.dot_general(a, b, (((0,), (0,)), ((), ())), preferred_element_type=F32)


def _dot_hi(a, b):
    return jnp.dot(a, b, precision=HI, preferred_element_type=F32)


def _expand(x, e):
    hi = x.astype(BF16)
    r = x - hi.astype(F32)
    mid = r.astype(BF16)
    lo = (r - mid.astype(F32)).astype(BF16)
    return _dot(hi, e) + _dot(mid, e) + _dot(lo, e)


def _tri(n, kind):
    r = lax.broadcasted_iota(jnp.int32, (n, n), 0)
    c = lax.broadcasted_iota(jnp.int32, (n, n), 1)
    return {"lower": r >= c, "strict_lower": r > c, "upper": r <= c}[kind]


def _ffn_body(*refs, n_mix, final):
    refs = list(refs)
    x = refs.pop(0)[...]
    for _ in range(n_mix):
        m_ref, wo_ref = refs.pop(0), refs.pop(0)
        x = x + _dot(m_ref[...].astype(BF16), wo_ref[...])
    g_ref, wg_ref, wu_ref, wd_ref = refs[:4]
    o_ref = refs[-1]
    hn = _rms(x, g_ref[...]).astype(BF16)
    act = (_silu(_dot(hn, wg_ref[...])) * _dot(hn, wu_ref[...])).astype(BF16)
    y = x + 0.5 * _dot(act, wd_ref[...])
    if final:
        y = _rms(y, refs[4][...])
    o_ref[...] = y


def _ffn(x, g, wg, wu, wd, mixes=(), g_final=None):
    t, d = x.shape
    tm = min(t, 512)
    final = g_final is not None
    row = pl.BlockSpec((tm, d), lambda i: (i, 0))
    in_specs, args = [row], [x]
    for m, w in mixes:
        in_specs += [pl.BlockSpec((tm, m.shape[1]), lambda i: (i, 0)), _whole(w.shape)]
        args += [m, w]
    in_specs += [_whole((1, d)), _whole(wg.shape), _whole(wu.shape), _whole(wd.shape)]
    args += [g.reshape(1, d), wg, wu, wd]
    if final:
        in_specs.append(_whole((1, d)))
        args.append(g_final.reshape(1, d))
    return pl.pallas_call(
        functools.partial(_ffn_body, n_mix=len(mixes), final=final),
        grid=(t // tm,),
        in_specs=in_specs,
        out_specs=row,
        out_shape=jax.ShapeDtypeStruct((t, d), F32),
        compiler_params=_cp("parallel"),
        name="ffn",
    )(*args)


def _inproj_body(x_ref, g_ref, w_ref, ws_ref, wst_ref, *outs, splits):
    hn = _rms(x_ref[...], g_ref[...]).astype(BF16)
    off = 0
    for o_ref, n in zip(outs[:-2], splits):
        o_ref[...] = _dot(hn, w_ref[:, off:off + n])
        off += n
    outs[-2][...] = _dot(hn, ws_ref[...])
    outs[-1][...] = _dot_nt(wst_ref[...], hn)


def _inproj(x, g, w, ws, wst, splits):
    t, d = x.shape
    tm = min(t, 256)
    ns = wst.shape[0]
    row = pl.BlockSpec((tm, d), lambda i: (i, 0))
    out_specs = [pl.BlockSpec((tm, n), lambda i: (i, 0)) for n in splits]
    out_specs += [pl.BlockSpec((tm, LANES), lambda i: (i, 0)), pl.BlockSpec((ns, tm), lambda i: (0, i))]
    out_shape = [jax.ShapeDtypeStruct((t, n), F32) for n in splits]
    out_shape += [jax.ShapeDtypeStruct((t, LANES), F32), jax.ShapeDtypeStruct((ns, t), F32)]
    return pl.pallas_call(
        functools.partial(_inproj_body, splits=splits),
        grid=(t // tm,),
        in_specs=[row, _whole((1, d)), _whole(w.shape), _whole(ws.shape), _whole(wst.shape)],
        out_specs=out_specs,
        out_shape=out_shape,
        compiler_params=_cp("parallel"),
        name="inproj",
    )(x, g.reshape(1, d), w, ws, wst)


def _t5_bucket_np(dist):
    max_exact = N_BUCKETS // 2
    df = np.maximum(dist, max_exact).astype(np.float32)
    large = max_exact + (np.log(df / np.float32(max_exact)) / np.float32(math.log(MAX_DIST / max_exact))
                         * np.float32(N_BUCKETS - max_exact)).astype(np.int32)
    return np.where(dist < max_exact, dist, np.minimum(large, N_BUCKETS - 1)).astype(np.int32)


def _band_bucket_ids(n_q, n_k, offset):
    d = offset + np.arange(n_q)[:, None] - np.arange(n_k)[None, :]
    valid = (d >= 0) & (d <= WINDOW)
    return np.where(valid, _t5_bucket_np(np.clip(d, 0, WINDOW)), -1).astype(np.int32)


def _bias_from_buckets(bid, rb_ref, h):
    acc = jnp.full(bid.shape, NEG_INF, F32)
    for bk in range(N_BUCKETS):
        acc = jnp.where(bid == bk, rb_ref[bk, h], acc)
    return acc


def _swa_prompt_body(bid_ref, rb_ref, sk_ref, q_ref, kvp_ref, kvc_ref, o_ref, bias_ref):
    first_step = (pl.program_id(0) == 0) & (pl.program_id(1) == 0)

    @pl.when(first_step)
    def _():
        bid = bid_ref[...]
        for h in range(H_A):
            bias_ref[h] = _bias_from_buckets(bid, rb_ref, h)

    col = lax.broadcasted_iota(jnp.int32, (BLOCK, 2 * BLOCK), 1)
    pad_keys = (pl.program_id(1) == 0) & (col < BLOCK)
    kvp = kvp_ref[...]
    kvc = kvc_ref[...]
    scale = HD_A ** -0.5
    outs = []
    for kv in range(KV_A):
        ks = slice(kv * HD_A, (kv + 1) * HD_A)
        vs = slice(KV_A * HD_A + kv * HD_A, KV_A * HD_A + (kv + 1) * HD_A)
        k = jnp.concatenate([kvp[:, ks], kvc[:, ks]], axis=0).astype(BF16)
        v = jnp.concatenate([kvp[:, vs], kvc[:, vs]], axis=0).astype(BF16)
        for g in range(G_A):
            h = kv * G_A + g
            q = q_ref[:, h * HD_A:(h + 1) * HD_A].astype(BF16)
            s = _dot_nt(q, k) * scale + jnp.where(pad_keys, NEG_INF, bias_ref[h])
            sk = sk_ref[h]
            m = jnp.maximum(jnp.max(s, axis=-1, keepdims=True), sk)
            p = jnp.exp(s - m)
            denom = jnp.sum(p, axis=-1, keepdims=True) + jnp.exp(sk - m)
            outs.append(_dot((p / denom).astype(BF16), v))
    o_ref[...] = jnp.concatenate(outs, axis=1).astype(BF16)


def _swa_prompt(qa, kv, rel_bias, sinks, bn, l):
    nb = l // BLOCK
    t = bn * l
    bid = jnp.asarray(_band_bucket_ids(BLOCK, 2 * BLOCK, BLOCK))
    kv_blk = (BLOCK, 2 * KV_A * HD_A)
    return pl.pallas_call(
        _swa_prompt_body,
        grid=(bn, nb),
        in_specs=[
            _whole((BLOCK, 2 * BLOCK)),
            pl.BlockSpec(memory_space=pltpu.SMEM),
            pl.BlockSpec(memory_space=pltpu.SMEM),
            pl.BlockSpec((BLOCK, A_Q), lambda b, i: (b * nb + i, 0)),
            pl.BlockSpec(kv_blk, lambda b, i: (jnp.maximum(b * nb + i - 1, 0), 0)),
            pl.BlockSpec(kv_blk, lambda b, i: (b * nb + i, 0)),
        ],
        out_specs=pl.BlockSpec((BLOCK, A_Q), lambda b, i: (b * nb + i, 0)),
        out_shape=jax.ShapeDtypeStruct((t, A_Q), BF16),
        scratch_shapes=[pltpu.VMEM((H_A, BLOCK, 2 * BLOCK), F32)],
        compiler_params=_cp("arbitrary", "arbitrary"),
        name="swa_prompt",
    )(bid, rel_bias, sinks, qa, kv, kv)


def _swa_decode_body(bid_ref, rb_ref, sk_ref, q_ref, kvn_ref, ck_ref, cv_ref, o_ref):
    bid = bid_ref[...]
    row = lax.broadcasted_iota(jnp.int32, (H_A, 1), 0)
    bias = jnp.zeros((H_A, bid.shape[1]), F32)
    sk = jnp.zeros((H_A, 1), F32)
    for h in range(H_A):
        bias = jnp.where(row == h, _bias_from_buckets(bid, rb_ref, h), bias)
        sk = jnp.where(row == h, sk_ref[h], sk)
    bias_c = bias[:, :WINDOW]
    bias_n = bias[:, WINDOW:WINDOW + 1]
    scale = HD_A ** -0.5
    q = q_ref[...]
    kvn = kvn_ref[...]
    width = KV_A * HD_A
    k_new = kvn[:, None, :width]
    v_new = kvn[:, None, width:]
    s = lax.dot_general(q.astype(BF16), ck_ref[...].astype(BF16), (((2,), (2,)), ((0,), (0,))),
                        preferred_element_type=F32) * scale + bias_c[None]
    s_n = jnp.sum(q * k_new, axis=-1, keepdims=True) * scale + bias_n[None]
    m = jnp.maximum(jnp.maximum(jnp.max(s, axis=-1, keepdims=True), s_n), sk[None])
    p = jnp.exp(s - m)
    p_n = jnp.exp(s_n - m)
    denom = jnp.sum(p, axis=-1, keepdims=True) + p_n + jnp.exp(sk[None] - m)
    o = lax.dot_general((p / denom).astype(BF16), cv_ref[...].astype(BF16), (((2,), (1,)), ((0,), (0,))),
                        preferred_element_type=F32)
    o_ref[...] = o + (p_n / denom) * v_new


def _swa_decode(q_pad, kv_new, cache_k, cache_v, rel_bias, sinks):
    bn = q_pad.shape[0]
    bs = 32
    width = KV_A * HD_A
    ids = _band_bucket_ids(1, WINDOW + 1, WINDOW)
    bid = np.full((1, WINDOW + LANES), -1, np.int32)
    bid[:, :WINDOW + 1] = ids
    return pl.pallas_call(
        _swa_decode_body,
        grid=(bn // bs,),
        in_specs=[
            _whole((1, WINDOW + LANES)),
            pl.BlockSpec(memory_space=pltpu.SMEM),
            pl.BlockSpec(memory_space=pltpu.SMEM),
            pl.BlockSpec((bs, H_A, width), lambda i: (i, 0, 0)),
            pl.BlockSpec((bs, 2 * width), lambda i: (i, 0)),
            pl.BlockSpec((bs, WINDOW, width), lambda i: (i, 0, 0)),
            pl.BlockSpec((bs, WINDOW, width), lambda i: (i, 0, 0)),
        ],
        out_specs=pl.BlockSpec((bs, H_A, width), lambda i: (i, 0, 0)),
        out_shape=jax.ShapeDtypeStruct((bn, H_A, width), F32),
        compiler_params=_cp("parallel"),
        name="swa_decode",
    )(jnp.asarray(bid), rel_bias, sinks, q_pad, kv_new, cache_k, cache_v)


def _conv_rows(x_ref, xbuf_ref, cw_ref, rows):
    xbuf_ref[SUBLANES:SUBLANES + rows, :] = x_ref[...]
    out = xbuf_ref[5:5 + rows, :] * cw_ref[0:1, :]
    for i in range(1, CONV_W):
        out = out + xbuf_ref[5 + i:5 + i + rows, :] * cw_ref[i:i + 1, :]
    return out


def _conv_chunk(x_ref, xbuf_ref, cw_ref, rows):
    out = _conv_rows(x_ref, xbuf_ref, cw_ref, rows)
    xbuf_ref[0:SUBLANES, :] = xbuf_ref[rows:rows + SUBLANES, :]
    return out


def _l2norm(x):
    return x * lax.rsqrt(jnp.sum(x * x, axis=-1, keepdims=True) + EPS)


def _chunk_masks(n, cc):
    r = lax.broadcasted_iota(jnp.int32, (n, n), 0)
    c = lax.broadcasted_iota(jnp.int32, (n, n), 1)
    shift = int(math.log2(cc))
    same = lax.shift_right_logical(r, shift) == lax.shift_right_logical(c, shift)
    return same, same & (r >= c), same & (r > c), same & (r <= c)


def _dot_bf16x3(a, b):
    a_hi = a.astype(BF16)
    a_lo = (a - a_hi.astype(F32)).astype(BF16)
    b_hi = b.astype(BF16)
    b_lo = (b - b_hi.astype(F32)).astype(BF16)
    return _dot(a_hi, b_hi) + _dot(a_hi, b_lo) + _dot(a_lo, b_hi)


def _gdn_prep_body(qkv_ref, halo_ref, sm_ref, smt_ref, cw_ref, arow_ref, dtrow_ref, acol_ref, dtcol_ref,
                   u_ref, w_ref, qg_ref, kd_ref, aqk_ref, gl_ref, xbuf_ref):
    rows = qkv_ref.shape[0]
    cc = GDN_CHUNK
    xbuf_ref[0:SUBLANES, :] = jnp.where(pl.program_id(1) == 0, 0.0, halo_ref[...])
    conv = _silu(_conv_rows(qkv_ref, xbuf_ref, cw_ref, rows))
    nq = H_B * DK_B
    sm = sm_ref[...]
    beta_all = jax.nn.sigmoid(sm)
    g_all = -jnp.exp(arow_ref[...]) * _softplus(sm + dtrow_ref[...])
    gt_all = -jnp.exp(acol_ref[...]) * _softplus(smt_ref[...] + dtcol_ref[...])
    same, lower, strict, upper = _chunk_masks(rows, cc)
    gc = _dot_hi(lower.astype(F32), g_all)
    gct = _dot_hi(gt_all, upper.astype(F32))
    gsum = _dot_hi(same.astype(F32), g_all)
    heads = range(H_B)
    qh, kh16, kb, decay, egc = [], [], [], [], []
    for h in heads:
        q = _l2norm(conv[:, h * DK_B:(h + 1) * DK_B]) * (DK_B ** -0.5)
        k = _l2norm(conv[:, nq + h * DK_B:nq + (h + 1) * DK_B])
        v = conv[:, 2 * nq + h * DV_B:2 * nq + (h + 1) * DV_B]
        beta = beta_all[:, h:h + 1]
        gcol = gc[:, H_B + h:H_B + h + 1]
        grow = gct[H_B + h:H_B + h + 1, :]
        decay.append(jnp.exp(jnp.where(lower, gcol - grow, -jnp.inf)))
        eg = jnp.exp(gcol)
        qg_ref[:, h * DK_B:(h + 1) * DK_B] = (q * eg).astype(BF16)
        kd_ref[:, h * DK_B:(h + 1) * DK_B] = (k * jnp.exp(gsum[:, H_B + h:H_B + h + 1] - gcol)).astype(BF16)
        qh.append(q.astype(BF16))
        kh16.append(k.astype(BF16))
        kb.append(k * beta)
        egc.append((v * beta, eg))
    a1 = [-jnp.where(strict, _dot_nt(kb[h].astype(BF16), kh16[h]) * decay[h], 0.0) for h in heads]
    for h in heads:
        aqk = _dot_nt(qh[h], kh16[h]) * decay[h]
        blocks = [aqk[i * cc:(i + 1) * cc, i * cc:(i + 1) * cc] for i in range(rows // cc)]
        aqk_ref[:, h * cc:(h + 1) * cc] = jnp.concatenate(blocks, axis=0).astype(BF16)
    sq = lambda xs: [_dot(x.astype(BF16), x.astype(BF16)) for x in xs]
    pair = lambda xs, ys: [x + y + _dot(x.astype(BF16), y.astype(BF16)) for x, y in zip(xs, ys)]
    a2 = sq(a1)
    a4 = sq(a2)
    x1 = pair(a1, a2)
    a8 = sq(a4)
    a16 = sq(a8)
    x2 = pair(a4, a8)
    a32 = sq(a16)
    x12 = pair(x1, x2)
    x3 = pair(a16, a32)
    r = pair(x12, x3)
    for h in heads:
        vb, eg = egc[h]
        rhs = jnp.concatenate([vb, kb[h] * eg], axis=1)
        sol = rhs + _dot_bf16x3(r[h], rhs)
        u_ref[:, h * DV_B:(h + 1) * DV_B] = sol[:, :DV_B]
        w_ref[:, h * DK_B:(h + 1) * DK_B] = sol[:, DV_B:].astype(BF16)
    for i in range(rows // cc):
        gl = [jnp.broadcast_to(jnp.exp(gsum[i * cc:i * cc + 1, H_B + h:H_B + h + 1]), (SUBLANES, DV_B))
              for h in heads]
        gl_ref[0, i] = jnp.concatenate(gl, axis=1)


def _gdn_scan_body(u_ref, w_ref, qg_ref, kd_ref, aqk_ref, gl_ref, z_ref, gn_ref, o_ref, s_out_ref, s_ref):
    c = pl.program_id(0)
    bn = u_ref.shape[0]
    cc = GDN_CHUNK

    @pl.when(c == 0)
    def _():
        s_ref[...] = jnp.zeros_like(s_ref)

    gn = gn_ref[...]
    group = 2
    for b0 in range(0, bn, group):
        combos = [(b, h) for b in range(b0, b0 + group) for h in range(H_B)]
        res = {}
        for b, h in combos:
            hs = slice(h * DK_B, (h + 1) * DK_B)
            wq = jnp.concatenate([w_ref[b, :, hs], qg_ref[b, :, hs]], axis=0)
            res[b, h] = _dot(wq, s_ref[b, h].astype(BF16))
        v16 = {}
        for b, h in combos:
            hs = slice(h * DV_B, (h + 1) * DV_B)
            v16[b, h] = (u_ref[b, :, hs] - res[b, h][:cc]).astype(BF16)
        for b, h in combos:
            hs = slice(h * DV_B, (h + 1) * DV_B)
            o = res[b, h][cc:] + _dot(aqk_ref[b, :, h * cc:(h + 1) * cc], v16[b, h])
            upd = _dot_tn(kd_ref[b, :, hs], v16[b, h])
            s_ref[b, h] = s_ref[b, h] * gl_ref[b, 0, 0:1, hs] + upd
            o_ref[b, :, hs] = (_rms(o, gn) * _silu(z_ref[b, :, hs])).astype(BF16)

    @pl.when(c == pl.num_programs(0) - 1)
    def _():
        s_out_ref[...] = s_ref[...]


def _gdn_prompt(qkvb, z, small, small_t, conv_w, a_log, dt_bias, gnorm, bn, l):
    rows = 4 * GDN_CHUNK
    cc = GDN_CHUNK
    nb = l // rows
    nc = l // cc
    t = bn * l
    ns = small_t.shape[0]
    nq = H_B * DK_B
    pad_row = lambda v: jnp.zeros((1, LANES), F32).at[0, H_B:2 * H_B].set(v)
    pad_col = lambda v: jnp.zeros((ns, 1), F32).at[H_B:2 * H_B, 0].set(v)
    tok = lambda n: pl.BlockSpec((rows, n), lambda b, i: (b * nb + i, 0))
    halo = pl.BlockSpec((SUBLANES, B_QKV), lambda b, i: (jnp.maximum((b * nb + i) * (rows // SUBLANES) - 1, 0), 0))
    u, w, qg, kd, aqk, gl = pl.pallas_call(
        _gdn_prep_body,
        grid=(bn, nb),
        in_specs=[
            tok(B_QKV), halo, tok(LANES),
            pl.BlockSpec((ns, rows), lambda b, i: (0, b * nb + i)),
            _whole((CONV_W, B_QKV)), _whole((1, LANES)), _whole((1, LANES)), _whole((ns, 1)), _whole((ns, 1)),
        ],
        out_specs=[tok(nq), tok(nq), tok(nq), tok(nq), tok(H_B * cc),
                   pl.BlockSpec((1, rows // cc, SUBLANES, nq), lambda b, i: (b, i, 0, 0))],
        out_shape=[jax.ShapeDtypeStruct((t, nq), F32)] + [jax.ShapeDtypeStruct((t, nq), BF16)] * 3
        + [jax.ShapeDtypeStruct((t, H_B * cc), BF16), jax.ShapeDtypeStruct((bn, nc, SUBLANES, nq), F32)],
        scratch_shapes=[pltpu.VMEM((rows + SUBLANES, B_QKV), F32)],
        compiler_params=_cp("parallel", "parallel"),
        name="gdn_prep",
    )(qkvb, qkvb, small, small_t, conv_w, pad_row(a_log), pad_row(dt_bias), pad_col(a_log), pad_col(dt_bias))
    seq = lambda n: pl.BlockSpec((bn, cc, n), lambda c: (0, c, 0))
    r3 = lambda a: a.reshape(bn, l, a.shape[1])
    o, s_new = pl.pallas_call(
        _gdn_scan_body,
        grid=(nc,),
        in_specs=[seq(nq), seq(nq), seq(nq), seq(nq), seq(H_B * cc),
                  pl.BlockSpec((bn, 1, SUBLANES, nq), lambda c: (0, c, 0, 0)), seq(B_Z), _whole((1, DV_B))],
        out_specs=[seq(B_Z), pl.BlockSpec((bn, H_B, DK_B, DV_B), lambda c: (0, 0, 0, 0))],
        out_shape=[jax.ShapeDtypeStruct((bn, l, B_Z), BF16), jax.ShapeDtypeStruct((bn, H_B, DK_B, DV_B), F32)],
        scratch_shapes=[pltpu.VMEM((bn, H_B, DK_B, DV_B), F32)],
        compiler_params=_cp("arbitrary"),
        name="gdn_scan",
    )(r3(u), r3(w), r3(qg), r3(kd), r3(aqk), gl, r3(z), gnorm.reshape(1, DV_B))
    return o.reshape(t, B_Z), s_new


def _conv_step(x, hist_ref, cw_ref, new_ref, ch):
    out = hist_ref[:, 0:ch] * cw_ref[0:1, :]
    for i in range(1, CONV_W - 1):
        out = out + hist_ref[:, i * ch:(i + 1) * ch] * cw_ref[i:i + 1, :]
    out = out + x * cw_ref[CONV_W - 1:CONV_W, :]
    new_ref[:, 0:(CONV_W - 2) * ch] = hist_ref[:, ch:(CONV_W - 1) * ch]
    new_ref[:, (CONV_W - 2) * ch:] = x
    return out


def _gdn_decode_pre_body(qkv_ref, hist_ref, sm_ref, cw_ref, arow_ref, dtrow_ref,
                         new_ref, w_ref, qg_ref, k_ref, u_ref, qk_ref, gl_ref):
    conv = _silu(_conv_step(qkv_ref[...], hist_ref, cw_ref, new_ref, B_QKV))
    nq = H_B * DK_B
    sm = sm_ref[...]
    beta_all = jax.nn.sigmoid(sm)
    g_all = -jnp.exp(arow_ref[...]) * _softplus(sm + dtrow_ref[...])
    for h in range(H_B):
        hs = slice(h * DK_B, (h + 1) * DK_B)
        qh = _l2norm(conv[:, hs]) * (DK_B ** -0.5)
        kh = _l2norm(conv[:, nq + h * DK_B:nq + (h + 1) * DK_B])
        vh = conv[:, 2 * nq + h * DV_B:2 * nq + (h + 1) * DV_B]
        beta = beta_all[:, h:h + 1]
        eg = jnp.exp(g_all[:, H_B + h:H_B + h + 1])
        w_ref[:, hs] = kh * beta * eg
        qg_ref[:, hs] = qh * eg
        k_ref[:, hs] = kh
        u_ref[:, hs] = vh * beta
        qk_ref[:, hs] = jnp.broadcast_to(jnp.sum(qh * kh, axis=-1, keepdims=True), qh.shape)
        gl_ref[:, hs] = jnp.broadcast_to(eg, qh.shape)


def _gdn_decode_state_body(s_ref, w_ref, qg_ref, k_ref, u_ref, qk_ref, gl_ref, z_ref, gn_ref,
                           s_out_ref, o_ref):
    nb = s_ref.shape[0]
    row = lax.broadcasted_iota(jnp.int32, (nb, 1), 0)
    wq = jnp.concatenate([w_ref[...], qg_ref[...]], axis=0).astype(BF16)
    ws = jnp.zeros((nb, DV_B), F32)
    qs = jnp.zeros((nb, DV_B), F32)
    for bb in range(nb):
        res = _dot(wq, s_ref[bb, 0].astype(BF16))
        ws = jnp.where(row == bb, res[:nb], ws)
        qs = jnp.where(row == bb, res[nb:], qs)
    v_new = u_ref[...] - ws
    o = qs + qk_ref[...] * v_new
    o_ref[...] = _rms(o, gn_ref[...]) * _silu(z_ref[...])
    k = k_ref[...]
    v16 = v_new.astype(BF16)
    gl = gl_ref[...]
    for bb in range(nb):
        k_one = jnp.where(row == bb, k, 0.0).astype(BF16)
        s_out_ref[bb, 0] = s_ref[bb, 0] * gl[bb:bb + 1, :] + _dot_tn(k_one, v16)


def _gdn_decode(qkvb, hist, z, small, conv_w, a_log, dt_bias, gnorm, s_all, e):
    bn = qkvb.shape[0]
    pad_row = lambda v: jnp.zeros((1, LANES), F32).at[0, H_B:2 * H_B].set(v)
    wide = jax.ShapeDtypeStruct((bn, H_B * DK_B), F32)
    new_conv, w, qg, k, u, qk, gl = pl.pallas_call(
        _gdn_decode_pre_body,
        out_shape=[jax.ShapeDtypeStruct(hist.shape, F32)] + [wide] * 6,
        compiler_params=pltpu.CompilerParams(vmem_limit_bytes=VMEM_LIMIT_BYTES),
        name="gdn_decode_pre",
    )(qkvb, hist, small, conv_w, pad_row(a_log), pad_row(dt_bias))
    nb = SUBLANES
    vec = pl.BlockSpec((nb, DK_B), lambda i, h: (i, h))
    st = pl.BlockSpec((nb, 1, DK_B, DV_B), lambda i, h: (i, h, 0, 0))
    st_in = pl.BlockSpec((None, nb, 1, DK_B, DV_B), lambda i, h: (e, i, h, 0, 0))
    s_new, o = pl.pallas_call(
        _gdn_decode_state_body,
        grid=(bn // nb, H_B),
        in_specs=[st_in, vec, vec, vec, vec, vec, vec, vec, _whole((1, DV_B))],
        out_specs=[st, vec],
        out_shape=[jax.ShapeDtypeStruct(s_all.shape[1:], F32), wide],
        compiler_params=_cp("parallel", "parallel"),
        name="gdn_decode_state",
    )(s_all, w, qg, k, u, qk, gl, z, gnorm.reshape(1, DV_B))
    return o, new_conv, s_new


def _head_expander(width):
    e = np.zeros((LANES, H_C * width), np.float32)
    for h in range(H_C):
        e[h, h * width:(h + 1) * width] = 1.0
    return jnp.asarray(e, BF16)


def _group_rms(y, g):
    gs = D_INNER // G_C
    parts = [_rms(y[:, i * gs:(i + 1) * gs], g[:, i * gs:(i + 1) * gs]) for i in range(G_C)]
    return jnp.concatenate(parts, axis=1)


def _ssd_prompt_body(z_ref, xbc_ref, sm_ref, smt_ref, cw_ref, cb_ref, dtrow_ref, arow_ref, dtcol_ref,
                     acol_ref, dx_ref, gn_ref, e_ref, o_ref, s_out_ref, xbuf_ref, s_ref):
    c = pl.program_id(1)
    rows = xbc_ref.shape[0]

    @pl.when(c == 0)
    def _():
        xbuf_ref[0:SUBLANES, :] = jnp.zeros((SUBLANES, xbuf_ref.shape[1]), F32)
        s_ref[...] = jnp.zeros_like(s_ref)

    xbc = _silu(_conv_chunk(xbc_ref, xbuf_ref, cw_ref, rows) + cb_ref[...])
    xs = xbc[:, :D_INNER]
    bm = xbc[:, D_INNER:D_INNER + G_C * N_C]
    cm = xbc[:, D_INNER + G_C * N_C:]
    e = e_ref[...]
    lower = _tri(rows, "lower")
    dt = _softplus(sm_ref[...] + dtrow_ref[...])
    acum = _dot_hi(lower.astype(F32), dt * -jnp.exp(arow_ref[...]))
    a_t = _softplus(smt_ref[...] + dtcol_ref[...]) * -jnp.exp(acol_ref[...])
    acum_t = _dot_hi(a_t, _tri(rows, "upper").astype(F32))
    xdt = xs * _expand(dt, e)
    acum_x = _expand(acum, e)
    last_x = acum_x[rows - 1:rows, :]
    xdte = (xdt * jnp.exp(last_x - acum_x)).astype(BF16)
    scale_y = jnp.exp(acum_x)
    chunk_decay = jnp.exp(last_x)
    xdt16 = xdt.astype(BF16)
    gw = HPG * P_C
    ys = []
    for g in range(G_C):
        bg = bm[:, g * N_C:(g + 1) * N_C]
        cg16 = cm[:, g * N_C:(g + 1) * N_C].astype(BF16)
        cb = _dot_nt(cg16, bg.astype(BF16))
        yg = []
        for hh in range(HPG):
            h = g * HPG + hh
            lmat = jnp.exp(jnp.where(lower, acum[:, h:h + 1] - acum_t[h:h + 1, :], -jnp.inf))
            yg.append(_dot((cb * lmat).astype(BF16), xdt16[:, h * P_C:(h + 1) * P_C]))
        gs = slice(g * gw, (g + 1) * gw)
        sg = s_ref[:, gs]
        y_off = _dot(cg16, sg.astype(BF16)) * scale_y[:, gs]
        ys.append(jnp.concatenate(yg, axis=1) + y_off)
        s_ref[:, gs] = sg * chunk_decay[:, gs] + _dot(bg.T.astype(BF16), xdte[:, gs])
    y = jnp.concatenate(ys, axis=1) + dx_ref[...] * xs
    y = y * _silu(z_ref[...])
    o_ref[...] = _group_rms(y, gn_ref[...]).astype(BF16)

    @pl.when(c == pl.num_programs(1) - 1)
    def _():
        s_out_ref[0] = s_ref[...].T.reshape(H_C, P_C, N_C)


def _ssd_small_params(dt_bias, a_log, ns):
    row = lambda v: jnp.zeros((1, LANES), F32).at[0, :H_C].set(v)
    col = lambda v: jnp.zeros((ns, 1), F32).at[:H_C, 0].set(v)
    return row(dt_bias), row(a_log), col(dt_bias), col(a_log)


def _ssd_prompt(z, xbc, small, small_t, conv_w, conv_b, dt_bias, a_log, d_skip, gnorm, bn, l):
    rows = SSD_CHUNK
    nc = l // rows
    t = bn * l
    ns = small_t.shape[0]
    dtrow, arow, dtcol, acol = _ssd_small_params(dt_bias, a_log, ns)
    tok = lambda n: pl.BlockSpec((rows, n), lambda b, c: (b * nc + c, 0))
    return pl.pallas_call(
        _ssd_prompt_body,
        grid=(bn, nc),
        in_specs=[
            tok(D_INNER), tok(SSD_CONV_CH), tok(LANES),
            pl.BlockSpec((ns, rows), lambda b, c: (0, b * nc + c)),
            _whole((CONV_W, SSD_CONV_CH)), _whole((1, SSD_CONV_CH)),
            _whole((1, LANES)), _whole((1, LANES)), _whole((ns, 1)), _whole((ns, 1)),
            _whole((1, D_INNER)), _whole((1, D_INNER)), _whole((LANES, D_INNER)),
        ],
        out_specs=[tok(D_INNER), pl.BlockSpec((1, H_C, P_C, N_C), lambda b, c: (b, 0, 0, 0))],
        out_shape=[jax.ShapeDtypeStruct((t, D_INNER), BF16), jax.ShapeDtypeStruct((bn, H_C, P_C, N_C), F32)],
        scratch_shapes=[pltpu.VMEM((rows + SUBLANES, SSD_CONV_CH), F32), pltpu.VMEM((N_C, D_INNER), F32)],
        compiler_params=_cp("arbitrary", "arbitrary"),
        name="ssd_prompt",
    )(z, xbc, small, small_t, conv_w, conv_b.reshape(1, SSD_CONV_CH), dtrow, arow, dtcol, acol,
      jnp.repeat(d_skip, P_C).reshape(1, D_INNER), gnorm.reshape(1, D_INNER), _head_expander(P_C))


def _ssd_decode_pre_body(xbc_ref, hist_ref, sm_ref, cw_ref, cb_ref, dtrow_ref, arow_ref, e_ref, en_ref,
                         new_ref, xs_ref, xdt_ref, b_ref, c_ref, dax_ref, dan_ref):
    xbc = _silu(_conv_step(xbc_ref[...], hist_ref, cw_ref, new_ref, SSD_CONV_CH) + cb_ref[...])
    xs = xbc[:, :D_INNER]
    dt = _softplus(sm_ref[...] + dtrow_ref[...])
    a = dt * -jnp.exp(arow_ref[...])
    xs_ref[...] = xs
    xdt_ref[...] = xs * _expand(dt, e_ref[...])
    b_ref[...] = xbc[:, D_INNER:D_INNER + G_C * N_C]
    c_ref[...] = xbc[:, D_INNER + G_C * N_C:]
    dax_ref[...] = jnp.exp(_expand(a, e_ref[...]))
    dan_ref[...] = jnp.exp(_expand(a, en_ref[...]))


def _ssd_decode_state_body(s_ref, xdt_ref, b_ref, c_ref, dan_ref, s_out_ref, yoff_ref):
    nb = s_ref.shape[0]
    gw = HPG * P_C
    row = lax.broadcasted_iota(jnp.int32, (nb, 1), 0)
    xdt = xdt_ref[...]
    b16 = b_ref[...].astype(BF16)
    c16 = c_ref[...].astype(BF16)
    dan = dan_ref[...]
    yoff = jnp.zeros((nb, gw), F32)
    for bb in range(nb):
        s = s_ref[bb].reshape(gw, N_C)
        res = _dot_nt(c16, s.astype(BF16))
        yoff = jnp.where(row == bb, res, yoff)
        x_one = jnp.where(row == bb, xdt, 0.0).astype(BF16)
        upd = _dot_tn(x_one, b16)
        for hh in range(HPG):
            rs = slice(hh * P_C, (hh + 1) * P_C)
            s_out_ref[bb, hh] = s[rs] * dan[bb:bb + 1, hh * N_C:(hh + 1) * N_C] + upd[rs]
    yoff_ref[...] = yoff


def _ssd_decode_post_body(yoff_ref, dax_ref, xdt_ref, xs_ref, b_ref, c_ref, z_ref, dx_ref, gn_ref, o_ref):
    gw = HPG * P_C
    bc = b_ref[...] * c_ref[...]
    cbx = [jnp.broadcast_to(jnp.sum(bc[:, g * N_C:(g + 1) * N_C], axis=-1, keepdims=True), (bc.shape[0], gw))
           for g in range(G_C)]
    y = yoff_ref[...] * dax_ref[...] + jnp.concatenate(cbx, axis=1) * xdt_ref[...]
    y = y + dx_ref[...] * xs_ref[...]
    y = y * _silu(z_ref[...])
    o_ref[...] = _group_rms(y, gn_ref[...])


def _ssd_decode(z, xbc, hist, small, conv_w, conv_b, dt_bias, a_log, d_skip, gnorm, s_all, e):
    bn = xbc.shape[0]
    dtrow, arow, _, _ = _ssd_small_params(dt_bias, a_log, H_C)
    wide = jax.ShapeDtypeStruct((bn, D_INNER), F32)
    grp = jax.ShapeDtypeStruct((bn, G_C * N_C), F32)
    plain = pltpu.CompilerParams(vmem_limit_bytes=VMEM_LIMIT_BYTES)
    new_conv, xs, xdt, bm, cm, dax, dan = pl.pallas_call(
        _ssd_decode_pre_body,
        out_shape=[jax.ShapeDtypeStruct(hist.shape, F32), wide, wide, grp, grp, wide,
                   jax.ShapeDtypeStruct((bn, H_C * N_C), F32)],
        compiler_params=plain,
        name="ssd_decode_pre",
    )(xbc, hist, small, conv_w, conv_b.reshape(1, SSD_CONV_CH), dtrow, arow,
      _head_expander(P_C), _head_expander(N_C))
    nb = SUBLANES
    gw = HPG * P_C
    st = pl.BlockSpec((nb, HPG, P_C, N_C), lambda i, g: (i, g, 0, 0))
    st_in = pl.BlockSpec((None, nb, HPG, P_C, N_C), lambda i, g: (e, i, g, 0, 0))
    s_new, yoff = pl.pallas_call(
        _ssd_decode_state_body,
        grid=(bn // nb, G_C),
        in_specs=[st_in, pl.BlockSpec((nb, gw), lambda i, g: (i, g)), pl.BlockSpec((nb, N_C), lambda i, g: (i, g)),
                  pl.BlockSpec((nb, N_C), lambda i, g: (i, g)), pl.BlockSpec((nb, HPG * N_C), lambda i, g: (i, g))],
        out_specs=[st, pl.BlockSpec((nb, gw), lambda i, g: (i, g))],
        out_shape=[jax.ShapeDtypeStruct(s_all.shape[1:], F32), wide],
        compiler_params=_cp("parallel", "parallel"),
        name="ssd_decode_state",
    )(s_all, xdt, bm, cm, dan)
    mix = pl.pallas_call(
        _ssd_decode_post_body,
        out_shape=wide,
        compiler_params=plain,
        name="ssd_decode_post",
    )(yoff, dax, xdt, xs, bm, cm, z, jnp.repeat(d_skip, P_C).reshape(1, D_INNER), gnorm.reshape(1, D_INNER))
    return mix, new_conv, s_new


def _narrow_weights(w_small):
    n = w_small.shape[1]
    ws = jnp.zeros((D_MODEL, LANES), BF16).at[:, :n].set(w_small.astype(BF16))
    ns = -(-n // 16) * 16
    wst = jnp.zeros((ns, D_MODEL), BF16).at[:n, :].set(w_small.T.astype(BF16))
    return ws, wst


def _even_q_pad(qa):
    bn = qa.shape[0]
    q = qa.reshape(bn, KV_A, G_A, HD_A)
    out = jnp.zeros((bn, KV_A, G_A, KV_A, HD_A), F32)
    for kv in range(KV_A):
        out = out.at[:, kv, :, kv, :].set(q[:, kv])
    return out.reshape(bn, H_A, KV_A * HD_A)


def _even_o_unpad(o8):
    bn = o8.shape[0]
    o = o8.reshape(bn, KV_A, G_A, KV_A, HD_A)
    return jnp.concatenate([o[:, kv, :, kv, :].reshape(bn, G_A * HD_A) for kv in range(KV_A)], axis=1)


def _trunk(x3, states, wts):
    (rel_bias, norm_ff1, norm_mix, norm_ff2, norm_final,
     ff1, ff2, even_w_in, even_w_out, swa_sinks, gdn_conv_w, gdn_A_log, gdn_dt_bias, gdn_norm,
     ssd_w_in, ssd_w_out, ssd_conv_w, ssd_conv_b, ssd_dt_bias, ssd_A_log, ssd_D, ssd_norm) = wts
    bn, l, d = x3.shape
    t = bn * l
    x = x3.reshape(t, d)
    decode = states is not None
    ks, vs, gconv, gssm, sconv, sssm = [], [], [], [], [], []
    depth = norm_ff1.shape[0]
    width = KV_A * HD_A
    for layer in range(depth):
        x = _ffn(x, norm_ff1[layer], *ff1[layer])
        e = layer // 2
        if layer % 2 == 0:
            w_main, ws, wst = even_w_in[e]
            qa, kv, qkvb, z, small, small_t = _inproj(x, norm_mix[layer], w_main, ws, wst,
                                                      (A_Q, 2 * width, B_QKV, B_Z))
            if decode:
                ck = states[0][e].reshape(bn, WINDOW, width)
                cv = states[1][e].reshape(bn, WINDOW, width)
                o8 = _swa_decode(_even_q_pad(qa), kv, ck, cv, rel_bias, swa_sinks[e])
                o_a = _even_o_unpad(o8)
                hist = states[2][e].reshape(bn, (CONV_W - 1) * B_QKV)
                o_b, new_conv, s_new = _gdn_decode(qkvb, hist, z, small, gdn_conv_w[e], gdn_A_log[e],
                                                   gdn_dt_bias[e], gdn_norm[e], states[3], e)
                new_k = jnp.concatenate([ck[:, 1:], kv[:, None, :width]], axis=1)
                new_v = jnp.concatenate([cv[:, 1:], kv[:, None, width:]], axis=1)
                new_conv = new_conv.reshape(bn, CONV_W - 1, B_QKV)
            else:
                o_a = _swa_prompt(qa, kv, rel_bias, swa_sinks[e], bn, l)
                o_b, s_new = _gdn_prompt(qkvb, z, small, small_t, gdn_conv_w[e], gdn_A_log[e],
                                         gdn_dt_bias[e], gdn_norm[e], bn, l)
                kv3 = kv.reshape(bn, l, 2 * width)
                new_k = kv3[:, l - WINDOW:, :width]
                new_v = kv3[:, l - WINDOW:, width:]
                new_conv = qkvb.reshape(bn, l, B_QKV)[:, l - (CONV_W - 1):]
            ks.append(new_k.reshape(bn, WINDOW, KV_A, HD_A))
            vs.append(new_v.reshape(bn, WINDOW, KV_A, HD_A))
            gconv.append(new_conv)
            gssm.append(s_new)
            mixes = [(o_a, even_w_out[e][:A_Q]), (o_b, even_w_out[e][A_Q:])]
        else:
            w_main, ws, wst = ssd_w_in[e]
            z, xbc, small, small_t = _inproj(x, norm_mix[layer], w_main, ws, wst, (D_INNER, SSD_CONV_CH))
            if decode:
                hist = states[4][e].reshape(bn, (CONV_W - 1) * SSD_CONV_CH)
                mix, new_conv, s_new = _ssd_decode(z, xbc, hist, small, ssd_conv_w[e], ssd_conv_b[e],
                                                   ssd_dt_bias[e], ssd_A_log[e], ssd_D[e], ssd_norm[e],
                                                   states[5], e)
                new_conv = new_conv.reshape(bn, CONV_W - 1, SSD_CONV_CH)
            else:
                mix, s_new = _ssd_prompt(z, xbc, small, small_t, ssd_conv_w[e], ssd_conv_b[e], ssd_dt_bias[e],
                                         ssd_A_log[e], ssd_D[e], ssd_norm[e], bn, l)
                new_conv = xbc.reshape(bn, l, SSD_CONV_CH)[:, l - (CONV_W - 1):]
            sconv.append(new_conv)
            sssm.append(s_new)
            mixes = [(mix, ssd_w_out[e])]
        x = _ffn(x, norm_ff2[layer], *ff2[layer], mixes=mixes,
                 g_final=norm_final if layer == depth - 1 else None)
    return (x.reshape(bn, l, d), jnp.stack(ks), jnp.stack(vs), jnp.stack(gconv), jnp.stack(gssm),
            jnp.stack(sconv), jnp.stack(sssm))


def kernel(x_prompt, x_sample, cache_swa_k, cache_swa_v, state_gdn_conv, state_gdn_ssm, state_ssd_conv, state_ssd_ssm, rel_bias, norm_ff1, norm_mix, norm_ff2, norm_final, ff1_gate, ff1_up, ff1_down, ff2_gate, ff2_up, ff2_down, even_w_in, even_w_out, swa_sinks, gdn_conv_w, gdn_A_log, gdn_dt_bias, gdn_norm, ssd_w_in, ssd_w_out, ssd_conv_w, ssd_conv_b, ssd_dt_bias, ssd_A_log, ssd_D, ssd_norm):
    depth = norm_ff1.shape[0]
    ff1 = [(ff1_gate[i].astype(BF16), ff1_up[i].astype(BF16), ff1_down[i].astype(BF16)) for i in range(depth)]
    ff2 = [(ff2_gate[i].astype(BF16), ff2_up[i].astype(BF16), ff2_down[i].astype(BF16)) for i in range(depth)]
    n_even_main = A_Q + 2 * KV_A * HD_A + B_QKV + B_Z
    even_in = [(even_w_in[e][:, :n_even_main].astype(BF16),) + _narrow_weights(even_w_in[e][:, n_even_main:])
               for e in range(even_w_in.shape[0])]
    n_odd_main = D_INNER + SSD_CONV_CH
    odd_in = [(ssd_w_in[e][:, :n_odd_main].astype(BF16),) + _narrow_weights(ssd_w_in[e][:, n_odd_main:])
              for e in range(ssd_w_in.shape[0])]
    wts = (rel_bias, norm_ff1, norm_mix, norm_ff2, norm_final, ff1, ff2,
           even_in, even_w_out.astype(BF16), swa_sinks, gdn_conv_w, gdn_A_log, gdn_dt_bias, gdn_norm,
           odd_in, ssd_w_out.astype(BF16), ssd_conv_w, ssd_conv_b, ssd_dt_bias, ssd_A_log, ssd_D, ssd_norm)
    y_p, p_k, p_v, p_gconv, p_gssm, p_sconv, p_sssm = _trunk(x_prompt, None, wts)
    states = (cache_swa_k, cache_swa_v, state_gdn_conv, state_gdn_ssm, state_ssd_conv, state_ssd_ssm)
    y_s, s_k, s_v, s_gconv, s_gssm, s_sconv, s_sssm = _trunk(x_sample, states, wts)
    return (y_p, y_s, p_k, p_v, p_gconv, p_gssm, p_sconv, p_sssm,
            s_k, s_v, s_gconv, s_gssm, s_sconv, s_sssm)
```

```python
import functools
import math

import numpy as np
import jax
import jax.numpy as jnp
from jax import lax
from jax.experimental import pallas as pl
from jax.experimental.pallas import tpu as pltpu

F32 = jnp.float32
BF16 = jnp.bfloat16
HI = lax.Precision.HIGHEST

EPS = 1e-6
NEG_INF = -1e30
D_MODEL = 1024
WINDOW = 128
BLOCK = 128
H_A, KV_A, G_A, HD_A = 8, 2, 4, 64
N_BUCKETS, MAX_DIST = 32, 128
H_B, DK_B, DV_B = 4, 128, 128
CONV_W = 4
GDN_CHUNK = 64
D_INNER = 2048
P_C, H_C, N_C, G_C = 64, 32, 128, 4
HPG = H_C // G_C
SSD_CHUNK = 128
A_Q = H_A * HD_A
B_QKV = 3 * H_B * DK_B
B_Z = H_B * DV_B
SSD_CONV_CH = D_INNER + 2 * G_C * N_C
LANES = 128
SUBLANES = 8
VMEM_LIMIT_BYTES = 56 * 1024 * 1024


def _cp(*sem):
    return pltpu.CompilerParams(dimension_semantics=sem, vmem_limit_bytes=VMEM_LIMIT_BYTES)


def _whole(shape):
    nd = len(shape)
    return pl.BlockSpec(shape, lambda *_: (0,) * nd, pipeline_mode=pl.Buffered(1))


def _rms(x, g):
    return x * lax.rsqrt(jnp.mean(x * x, axis=-1, keepdims=True) + EPS) * g


def _silu(x):
    return x * jax.nn.sigmoid(x)


def _softplus(x):
    return jnp.maximum(x, 0.0) + jnp.log1p(jnp.exp(-jnp.abs(x)))


def _dot(a, b):
    return jnp.dot(a, b, preferred_element_type=F32)


def _dot_nt(a, b):
    return lax.dot_general(a, b, (((1,), (1,)), ((), ())), preferred_element_type=F32)


def _dot_tn(a, b):
    return lax.dot_general(a, b, (((0,), (0,)), ((), ())), preferred_element_type=F32)


def _dot_hi(a, b):
    return jnp.dot(a, b, precision=HI, preferred_element_type=F32)


def _expand(x, e3):
    hi = x.astype(BF16)
    r = x - hi.astype(F32)
    mid = r.astype(BF16)
    lo = (r - mid.astype(F32)).astype(BF16)
    return _dot(jnp.concatenate([hi, mid, lo], axis=1), e3)


def _tri(n, kind):
    r = lax.broadcasted_iota(jnp.int32, (n, n), 0)
    c = lax.broadcasted_iota(jnp.int32, (n, n), 1)
    return {"lower": r >= c, "strict_lower": r > c, "upper": r <= c}[kind]


def _ffn_body(*refs, n_mix, final):
    refs = list(refs)
    x = refs.pop(0)[...]
    for _ in range(n_mix):
        m_ref, wo_ref = refs.pop(0), refs.pop(0)
        x = x + _dot(m_ref[...].astype(BF16), wo_ref[...])
    g_ref, wg_ref, wu_ref, wd_ref = refs[:4]
    o_ref = refs[-1]
    hn = _rms(x, g_ref[...]).astype(BF16)
    act = (_silu(_dot(hn, wg_ref[...])) * _dot(hn, wu_ref[...])).astype(BF16)
    y = x + 0.5 * _dot(act, wd_ref[...])
    if final:
        y = _rms(y, refs[4][...])
    o_ref[...] = y


def _ffn(x, g, wg, wu, wd, mixes=(), g_final=None):
    t, d = x.shape
    tm = min(t, 512)
    final = g_final is not None
    row = pl.BlockSpec((tm, d), lambda i: (i, 0))
    in_specs, args = [row], [x]
    for m, w in mixes:
        in_specs += [pl.BlockSpec((tm, m.shape[1]), lambda i: (i, 0)), _whole(w.shape)]
        args += [m, w]
    in_specs += [_whole((1, d)), _whole(wg.shape), _whole(wu.shape), _whole(wd.shape)]
    args += [g.reshape(1, d), wg, wu, wd]
    if final:
        in_specs.append(_whole((1, d)))
        args.append(g_final.reshape(1, d))
    return pl.pallas_call(
        functools.partial(_ffn_body, n_mix=len(mixes), final=final),
        grid=(t // tm,),
        in_specs=in_specs,
        out_specs=row,
        out_shape=jax.ShapeDtypeStruct((t, d), F32),
        compiler_params=_cp("parallel"),
        name="ffn",
    )(*args)


def _inproj_body(x_ref, g_ref, w_ref, ws_ref, *outs, splits):
    hn = _rms(x_ref[...], g_ref[...]).astype(BF16)
    off = 0
    for o_ref, n in zip(outs[:-2], splits):
        o_ref[...] = _dot(hn, w_ref[:, off:off + n])
        off += n
    small = _dot(hn, ws_ref[...])
    outs[-2][...] = small
    outs[-1][...] = small.T[:outs[-1].shape[0]]


def _inproj(x, g, w, ws, ns, splits):
    t, d = x.shape
    tm = min(t, 256)
    row = pl.BlockSpec((tm, d), lambda i: (i, 0))
    out_specs = [pl.BlockSpec((tm, n), lambda i: (i, 0)) for n in splits]
    out_specs += [pl.BlockSpec((tm, LANES), lambda i: (i, 0)), pl.BlockSpec((ns, tm), lambda i: (0, i))]
    out_shape = [jax.ShapeDtypeStruct((t, n), F32) for n in splits]
    out_shape += [jax.ShapeDtypeStruct((t, LANES), F32), jax.ShapeDtypeStruct((ns, t), F32)]
    return pl.pallas_call(
        functools.partial(_inproj_body, splits=splits),
        grid=(t // tm,),
        in_specs=[row, _whole((1, d)), _whole(w.shape), _whole(ws.shape)],
        out_specs=out_specs,
        out_shape=out_shape,
        compiler_params=_cp("parallel"),
        name="inproj",
    )(x, g.reshape(1, d), w, ws)


def _t5_bucket_np(dist):
    max_exact = N_BUCKETS // 2
    df = np.maximum(dist, max_exact).astype(np.float32)
    large = max_exact + (np.log(df / np.float32(max_exact)) / np.float32(math.log(MAX_DIST / max_exact))
                         * np.float32(N_BUCKETS - max_exact)).astype(np.int32)
    return np.where(dist < max_exact, dist, np.minimum(large, N_BUCKETS - 1)).astype(np.int32)


def _band_bucket_ids(n_q, n_k, offset):
    d = offset + np.arange(n_q)[:, None] - np.arange(n_k)[None, :]
    valid = (d >= 0) & (d <= WINDOW)
    return np.where(valid, _t5_bucket_np(np.clip(d, 0, WINDOW)), -1).astype(np.int32)


def _bias_from_buckets(bid, rb_ref, h):
    acc = jnp.full(bid.shape, NEG_INF, F32)
    for bk in range(N_BUCKETS):
        acc = jnp.where(bid == bk, rb_ref[bk, h], acc)
    return acc


def _swa_prompt_body(bid_ref, rb_ref, sk_ref, q_ref, kvp_ref, kvc_ref, o_ref, bias_ref):
    first_step = (pl.program_id(0) == 0) & (pl.program_id(1) == 0)

    @pl.when(first_step)
    def _():
        bid = bid_ref[...]
        for h in range(H_A):
            bias_ref[h] = _bias_from_buckets(bid, rb_ref, h)

    col = lax.broadcasted_iota(jnp.int32, (BLOCK, 2 * BLOCK), 1)
    pad_keys = (pl.program_id(1) == 0) & (col < BLOCK)
    kvp = kvp_ref[...]
    kvc = kvc_ref[...]
    scale = HD_A ** -0.5
    outs = []
    for kv in range(KV_A):
        ks = slice(kv * HD_A, (kv + 1) * HD_A)
        vs = slice(KV_A * HD_A + kv * HD_A, KV_A * HD_A + (kv + 1) * HD_A)
        k = jnp.concatenate([kvp[:, ks], kvc[:, ks]], axis=0).astype(BF16)
        v = jnp.concatenate([kvp[:, vs], kvc[:, vs]], axis=0).astype(BF16)
        for g in range(G_A):
            h = kv * G_A + g
            q = q_ref[:, h * HD_A:(h + 1) * HD_A].astype(BF16)
            s = _dot_nt(q, k) * scale + jnp.where(pad_keys, NEG_INF, bias_ref[h])
            sk = sk_ref[h]
            m = jnp.maximum(jnp.max(s, axis=-1, keepdims=True), sk)
            p = jnp.exp(s - m)
            denom = jnp.sum(p, axis=-1, keepdims=True) + jnp.exp(sk - m)
            outs.append(_dot((p / denom).astype(BF16), v))
    o_ref[...] = jnp.concatenate(outs, axis=1).astype(BF16)


def _swa_prompt(qa, kv, rel_bias, sinks, bn, l):
    nb = l // BLOCK
    t = bn * l
    bid = jnp.asarray(_band_bucket_ids(BLOCK, 2 * BLOCK, BLOCK))
    kv_blk = (BLOCK, 2 * KV_A * HD_A)
    return pl.pallas_call(
        _swa_prompt_body,
        grid=(bn, nb),
        in_specs=[
            _whole((BLOCK, 2 * BLOCK)),
            pl.BlockSpec(memory_space=pltpu.SMEM),
            pl.BlockSpec(memory_space=pltpu.SMEM),
            pl.BlockSpec((BLOCK, A_Q), lambda b, i: (b * nb + i, 0)),
            pl.BlockSpec(kv_blk, lambda b, i: (jnp.maximum(b * nb + i - 1, 0), 0)),
            pl.BlockSpec(kv_blk, lambda b, i: (b * nb + i, 0)),
        ],
        out_specs=pl.BlockSpec((BLOCK, A_Q), lambda b, i: (b * nb + i, 0)),
        out_shape=jax.ShapeDtypeStruct((t, A_Q), BF16),
        scratch_shapes=[pltpu.VMEM((H_A, BLOCK, 2 * BLOCK), F32)],
        compiler_params=_cp("arbitrary", "arbitrary"),
        name="swa_prompt",
    )(bid, rel_bias, sinks, qa, kv, kv)


def _swa_decode_body(bid_ref, rb_ref, sk_ref, q_ref, kvn_ref, ck_ref, cv_ref, o_ref):
    bid = bid_ref[...]
    row = lax.broadcasted_iota(jnp.int32, (H_A, 1), 0)
    bias = jnp.zeros((H_A, bid.shape[1]), F32)
    sk = jnp.zeros((H_A, 1), F32)
    for h in range(H_A):
        bias = jnp.where(row == h, _bias_from_buckets(bid, rb_ref, h), bias)
        sk = jnp.where(row == h, sk_ref[h], sk)
    bias_c = bias[:, :WINDOW]
    bias_n = bias[:, WINDOW:WINDOW + 1]
    scale = HD_A ** -0.5
    q = q_ref[...]
    kvn = kvn_ref[...]
    width = KV_A * HD_A
    k_new = kvn[:, None, :width]
    v_new = kvn[:, None, width:]
    s = lax.dot_general(q.astype(BF16), ck_ref[...].astype(BF16), (((2,), (2,)), ((0,), (0,))),
                        preferred_element_type=F32) * scale + bias_c[None]
    s_n = jnp.sum(q * k_new, axis=-1, keepdims=True) * scale + bias_n[None]
    m = jnp.maximum(jnp.maximum(jnp.max(s, axis=-1, keepdims=True), s_n), sk[None])
    p = jnp.exp(s - m)
    p_n = jnp.exp(s_n - m)
    denom = jnp.sum(p, axis=-1, keepdims=True) + p_n + jnp.exp(sk[None] - m)
    o = lax.dot_general((p / denom).astype(BF16), cv_ref[...].astype(BF16), (((2,), (1,)), ((0,), (0,))),
                        preferred_element_type=F32)
    o_ref[...] = o + (p_n / denom) * v_new


def _swa_decode(q_pad, kv_new, cache_k, cache_v, rel_bias, sinks):
    bn = q_pad.shape[0]
    bs = 32
    width = KV_A * HD_A
    ids = _band_bucket_ids(1, WINDOW + 1, WINDOW)
    bid = np.full((1, WINDOW + LANES), -1, np.int32)
    bid[:, :WINDOW + 1] = ids
    return pl.pallas_call(
        _swa_decode_body,
        grid=(bn // bs,),
        in_specs=[
            _whole((1, WINDOW + LANES)),
            pl.BlockSpec(memory_space=pltpu.SMEM),
            pl.BlockSpec(memory_space=pltpu.SMEM),
            pl.BlockSpec((bs, H_A, width), lambda i: (i, 0, 0)),
            pl.BlockSpec((bs, 2 * width), lambda i: (i, 0)),
            pl.BlockSpec((bs, WINDOW, width), lambda i: (i, 0, 0)),
            pl.BlockSpec((bs, WINDOW, width), lambda i: (i, 0, 0)),
        ],
        out_specs=pl.BlockSpec((bs, H_A, width), lambda i: (i, 0, 0)),
        out_shape=jax.ShapeDtypeStruct((bn, H_A, width), F32),
        compiler_params=_cp("parallel"),
        name="swa_decode",
    )(jnp.asarray(bid), rel_bias, sinks, q_pad, kv_new, cache_k, cache_v)


def _cache_shift_body(c_ref, n_ref, o_ref):
    o_ref[:, 0:WINDOW - 1, :] = c_ref[:, 1:WINDOW, :]
    o_ref[:, WINDOW - 1:WINDOW, :] = n_ref[...][:, None, :]


def _cache_shift(cache_all, new_rows):
    n, bn, win, width = cache_all.shape
    bs = 32
    blk = pl.BlockSpec((None, bs, win, width), lambda e, i: (e, i, 0, 0))
    return pl.pallas_call(
        _cache_shift_body,
        grid=(n, bn // bs),
        in_specs=[blk, pl.BlockSpec((None, bs, width), lambda e, i: (e, i, 0))],
        out_specs=blk,
        out_shape=jax.ShapeDtypeStruct(cache_all.shape, F32),
        compiler_params=_cp("parallel", "parallel"),
        name="cache_shift",
    )(cache_all, new_rows)


def _conv_rows(x_ref, xbuf_ref, cw_ref, rows):
    xbuf_ref[SUBLANES:SUBLANES + rows, :] = x_ref[...]
    x = xbuf_ref[...]
    out = x * cw_ref[0:1, :]
    for i in range(1, CONV_W):
        out = pltpu.roll(out, 1, axis=0) + x * cw_ref[i:i + 1, :]
    return out[SUBLANES:]


def _conv_chunk(x_ref, xbuf_ref, cw_ref, rows):
    out = _conv_rows(x_ref, xbuf_ref, cw_ref, rows)
    xbuf_ref[0:SUBLANES, :] = xbuf_ref[rows:rows + SUBLANES, :]
    return out


def _l2norm(x):
    return x * lax.rsqrt(jnp.sum(x * x, axis=-1, keepdims=True) + EPS)


def _chunk_masks(n, cc):
    r = lax.broadcasted_iota(jnp.int32, (n, n), 0)
    c = lax.broadcasted_iota(jnp.int32, (n, n), 1)
    shift = int(math.log2(cc))
    same = lax.shift_right_logical(r, shift) == lax.shift_right_logical(c, shift)
    return same, same & (r >= c), same & (r > c), same & (r <= c)


def _dot_bf16x3(a, b):
    a_hi = a.astype(BF16)
    a_lo = (a - a_hi.astype(F32)).astype(BF16)
    b_hi = b.astype(BF16)
    b_lo = (b - b_hi.astype(F32)).astype(BF16)
    return _dot(a_hi, b_hi) + _dot(a_hi, b_lo) + _dot(a_lo, b_hi)


def _gdn_prep_body(qkv_ref, halo_ref, sm_ref, smt_ref, cw_ref, arow_ref, dtrow_ref, acol_ref, dtcol_ref,
                   u_ref, w_ref, qg_ref, kd_ref, aqk_ref, gl_ref, xbuf_ref):
    rows = qkv_ref.shape[0]
    cc = GDN_CHUNK
    xbuf_ref[0:SUBLANES, :] = jnp.where(pl.program_id(1) == 0, 0.0, halo_ref[...])
    conv = _silu(_conv_rows(qkv_ref, xbuf_ref, cw_ref, rows))
    nq = H_B * DK_B
    sm = sm_ref[...]
    beta_all = jax.nn.sigmoid(sm)
    g_all = -jnp.exp(arow_ref[...]) * _softplus(sm + dtrow_ref[...])
    gt_all = -jnp.exp(acol_ref[...]) * _softplus(smt_ref[...] + dtcol_ref[...])
    same, lower, strict, upper = _chunk_masks(rows, cc)
    gc = _dot_hi(lower.astype(F32), g_all)
    gct = _dot_hi(gt_all, upper.astype(F32))
    gsum = _dot_hi(same.astype(F32), g_all)
    neg_outside = jnp.where(lower, 0.0, -jnp.inf)
    neg_strict = jnp.where(strict, -1.0, 0.0)
    heads = range(H_B)
    qh, kh16, kb, decay, egc = [], [], [], [], []
    for h in heads:
        q = _l2norm(conv[:, h * DK_B:(h + 1) * DK_B]) * (DK_B ** -0.5)
        k = _l2norm(conv[:, nq + h * DK_B:nq + (h + 1) * DK_B])
        v = conv[:, 2 * nq + h * DV_B:2 * nq + (h + 1) * DV_B]
        beta = beta_all[:, h:h + 1]
        gcol = gc[:, H_B + h:H_B + h + 1]
        grow = gct[H_B + h:H_B + h + 1, :]
        decay.append(jnp.exp(gcol - grow + neg_outside))
        eg = jnp.exp(gcol)
        qg_ref[:, h * DK_B:(h + 1) * DK_B] = (q * eg).astype(BF16)
        kd_ref[:, h * DK_B:(h + 1) * DK_B] = (k * jnp.exp(gsum[:, H_B + h:H_B + h + 1] - gcol)).astype(BF16)
        qh.append(q.astype(BF16))
        kh16.append(k.astype(BF16))
        kb.append(k * beta)
        egc.append((v * beta, eg))
    a1 = [_dot_nt(kb[h].astype(BF16), kh16[h]) * decay[h] * neg_strict for h in heads]
    for h in heads:
        aqk = _dot_nt(qh[h], kh16[h]) * decay[h]
        blocks = [aqk[i * cc:(i + 1) * cc, i * cc:(i + 1) * cc] for i in range(rows // cc)]
        aqk_ref[:, h * cc:(h + 1) * cc] = jnp.concatenate(blocks, axis=0).astype(BF16)
    sq = lambda xs: [_dot(x.astype(BF16), x.astype(BF16)) for x in xs]
    pair = lambda xs, ys: [x + y + _dot(x.astype(BF16), y.astype(BF16)) for x, y in zip(xs, ys)]
    a2 = sq(a1)
    a4 = sq(a2)
    x1 = pair(a1, a2)
    a8 = sq(a4)
    a16 = sq(a8)
    x2 = pair(a4, a8)
    a32 = sq(a16)
    x12 = pair(x1, x2)
    x3 = pair(a16, a32)
    r = pair(x12, x3)
    for h in heads:
        vb, eg = egc[h]
        rhs = jnp.concatenate([vb, kb[h] * eg], axis=1)
        sol = rhs + _dot_bf16x3(r[h], rhs)
        u_ref[:, h * DV_B:(h + 1) * DV_B] = sol[:, :DV_B]
        w_ref[:, h * DK_B:(h + 1) * DK_B] = sol[:, DV_B:].astype(BF16)
    for i in range(rows // cc):
        gl = [jnp.broadcast_to(jnp.exp(gsum[i * cc:i * cc + 1, H_B + h:H_B + h + 1]), (SUBLANES, DV_B))
              for h in heads]
        gl_ref[0, i] = jnp.concatenate(gl, axis=1)


def _gdn_scan_body(u_ref, w_ref, qg_ref, kd_ref, aqk_ref, gl_ref, z_ref, gn_ref, o_ref, s_out_ref, s_ref):
    c = pl.program_id(0)
    bn = u_ref.shape[0]
    cc = GDN_CHUNK

    @pl.when(c == 0)
    def _():
        s_ref[...] = jnp.zeros_like(s_ref)

    gn = gn_ref[...]
    group = 2
    for b0 in range(0, bn, group):
        combos = [(b, h) for b in range(b0, b0 + group) for h in range(H_B)]
        res = {}
        for b, h in combos:
            hs = slice(h * DK_B, (h + 1) * DK_B)
            wq = jnp.concatenate([w_ref[b, :, hs], qg_ref[b, :, hs]], axis=0)
            res[b, h] = _dot(wq, s_ref[b, h].astype(BF16))
        v16 = {}
        for b, h in combos:
            hs = slice(h * DV_B, (h + 1) * DV_B)
            v16[b, h] = (u_ref[b, :, hs] - res[b, h][:cc]).astype(BF16)
        for b, h in combos:
            hs = slice(h * DV_B, (h + 1) * DV_B)
            o = res[b, h][cc:] + _dot(aqk_ref[b, :, h * cc:(h + 1) * cc], v16[b, h])
            upd = _dot_tn(kd_ref[b, :, hs], v16[b, h])
            s_ref[b, h] = s_ref[b, h] * gl_ref[b, 0, 0:1, hs] + upd
            o_ref[b, :, hs] = (_rms(o, gn) * _silu(z_ref[b, :, hs])).astype(BF16)

    @pl.when(c == pl.num_programs(0) - 1)
    def _():
        s_out_ref[...] = s_ref[...]


def _gdn_prompt(qkvb, z, small, small_t, conv_w, a_log, dt_bias, gnorm, bn, l):
    rows = 4 * GDN_CHUNK
    cc = GDN_CHUNK
    nb = l // rows
    nc = l // cc
    t = bn * l
    ns = small_t.shape[0]
    nq = H_B * DK_B
    pad_row = lambda v: jnp.zeros((1, LANES), F32).at[0, H_B:2 * H_B].set(v)
    pad_col = lambda v: jnp.zeros((ns, 1), F32).at[H_B:2 * H_B, 0].set(v)
    tok = lambda n: pl.BlockSpec((rows, n), lambda b, i: (b * nb + i, 0))
    halo = pl.BlockSpec((SUBLANES, B_QKV), lambda b, i: (jnp.maximum((b * nb + i) * (rows // SUBLANES) - 1, 0), 0))
    u, w, qg, kd, aqk, gl = pl.pallas_call(
        _gdn_prep_body,
        grid=(bn, nb),
        in_specs=[
            tok(B_QKV), halo, tok(LANES),
            pl.BlockSpec((ns, rows), lambda b, i: (0, b * nb + i)),
            _whole((CONV_W, B_QKV)), _whole((1, LANES)), _whole((1, LANES)), _whole((ns, 1)), _whole((ns, 1)),
        ],
        out_specs=[tok(nq), tok(nq), tok(nq), tok(nq), tok(H_B * cc),
                   pl.BlockSpec((1, rows // cc, SUBLANES, nq), lambda b, i: (b, i, 0, 0))],
        out_shape=[jax.ShapeDtypeStruct((t, nq), F32)] + [jax.ShapeDtypeStruct((t, nq), BF16)] * 3
        + [jax.ShapeDtypeStruct((t, H_B * cc), BF16), jax.ShapeDtypeStruct((bn, nc, SUBLANES, nq), F32)],
        scratch_shapes=[pltpu.VMEM((rows + SUBLANES, B_QKV), F32)],
        compiler_params=_cp("parallel", "parallel"),
        name="gdn_prep",
    )(qkvb, qkvb, small, small_t, conv_w, pad_row(a_log), pad_row(dt_bias), pad_col(a_log), pad_col(dt_bias))
    seq = lambda n: pl.BlockSpec((bn, cc, n), lambda c: (0, c, 0))
    r3 = lambda a: a.reshape(bn, l, a.shape[1])
    o, s_new = pl.pallas_call(
        _gdn_scan_body,
        grid=(nc,),
        in_specs=[seq(nq), seq(nq), seq(nq), seq(nq), seq(H_B * cc),
                  pl.BlockSpec((bn, 1, SUBLANES, nq), lambda c: (0, c, 0, 0)), seq(B_Z), _whole((1, DV_B))],
        out_specs=[seq(B_Z), pl.BlockSpec((bn, H_B, DK_B, DV_B), lambda c: (0, 0, 0, 0))],
        out_shape=[jax.ShapeDtypeStruct((bn, l, B_Z), BF16), jax.ShapeDtypeStruct((bn, H_B, DK_B, DV_B), F32)],
        scratch_shapes=[pltpu.VMEM((bn, H_B, DK_B, DV_B), F32)],
        compiler_params=_cp("arbitrary"),
        name="gdn_scan",
    )(r3(u), r3(w), r3(qg), r3(kd), r3(aqk), gl, r3(z), gnorm.reshape(1, DV_B))
    return o.reshape(t, B_Z), s_new


def _conv_step(x, hist_ref, cw_ref, new_ref, ch):
    out = hist_ref[:, 0:ch] * cw_ref[0:1, :]
    for i in range(1, CONV_W - 1):
        out = out + hist_ref[:, i * ch:(i + 1) * ch] * cw_ref[i:i + 1, :]
    out = out + x * cw_ref[CONV_W - 1:CONV_W, :]
    new_ref[:, 0:(CONV_W - 2) * ch] = hist_ref[:, ch:(CONV_W - 1) * ch]
    new_ref[:, (CONV_W - 2) * ch:] = x
    return out


def _gdn_decode_pre_body(qkv_ref, hist_ref, sm_ref, cw_ref, arow_ref, dtrow_ref,
                         new_ref, w_ref, qg_ref, k_ref, u_ref, qk_ref, gl_ref):
    conv = _silu(_conv_step(qkv_ref[...], hist_ref, cw_ref, new_ref, B_QKV))
    nq = H_B * DK_B
    sm = sm_ref[...]
    beta_all = jax.nn.sigmoid(sm)
    g_all = -jnp.exp(arow_ref[...]) * _softplus(sm + dtrow_ref[...])
    for h in range(H_B):
        hs = slice(h * DK_B, (h + 1) * DK_B)
        qh = _l2norm(conv[:, hs]) * (DK_B ** -0.5)
        kh = _l2norm(conv[:, nq + h * DK_B:nq + (h + 1) * DK_B])
        vh = conv[:, 2 * nq + h * DV_B:2 * nq + (h + 1) * DV_B]
        beta = beta_all[:, h:h + 1]
        eg = jnp.exp(g_all[:, H_B + h:H_B + h + 1])
        w_ref[:, hs] = kh * beta * eg
        qg_ref[:, hs] = qh * eg
        k_ref[:, hs] = kh
        u_ref[:, hs] = vh * beta
        qk_ref[:, hs] = jnp.broadcast_to(jnp.sum(qh * kh, axis=-1, keepdims=True), qh.shape)
        gl_ref[:, hs] = jnp.broadcast_to(eg, qh.shape)


def _gdn_decode_state_body(s_ref, w_ref, qg_ref, k_ref, u_ref, qk_ref, gl_ref, z_ref, gn_ref,
                           s_out_ref, o_ref):
    nb = s_ref.shape[0]
    row = lax.broadcasted_iota(jnp.int32, (nb, 1), 0)
    wq = jnp.concatenate([w_ref[...], qg_ref[...]], axis=0).astype(BF16)
    ws = jnp.zeros((nb, DV_B), F32)
    qs = jnp.zeros((nb, DV_B), F32)
    for bb in range(nb):
        res = _dot(wq, s_ref[bb, 0].astype(BF16))
        ws = jnp.where(row == bb, res[:nb], ws)
        qs = jnp.where(row == bb, res[nb:], qs)
    v_new = u_ref[...] - ws
    o = qs + qk_ref[...] * v_new
    o_ref[...] = _rms(o, gn_ref[...]) * _silu(z_ref[...])
    k = k_ref[...]
    v16 = v_new.astype(BF16)
    gl = gl_ref[...]
    for bb in range(nb):
        k_one = jnp.where(row == bb, k, 0.0).astype(BF16)
        s_out_ref[bb, 0] = s_ref[bb, 0] * gl[bb:bb + 1, :] + _dot_tn(k_one, v16)


def _gdn_decode(qkvb, hist, z, small, conv_w, a_log, dt_bias, gnorm, s_all, e):
    bn = qkvb.shape[0]
    pad_row = lambda v: jnp.zeros((1, LANES), F32).at[0, H_B:2 * H_B].set(v)
    wide = jax.ShapeDtypeStruct((bn, H_B * DK_B), F32)
    new_conv, w, qg, k, u, qk, gl = pl.pallas_call(
        _gdn_decode_pre_body,
        out_shape=[jax.ShapeDtypeStruct(hist.shape, F32)] + [wide] * 6,
        compiler_params=pltpu.CompilerParams(vmem_limit_bytes=VMEM_LIMIT_BYTES),
        name="gdn_decode_pre",
    )(qkvb, hist, small, conv_w, pad_row(a_log), pad_row(dt_bias))
    nb = SUBLANES
    vec = pl.BlockSpec((nb, DK_B), lambda i, h: (i, h))
    st = pl.BlockSpec((nb, 1, DK_B, DV_B), lambda i, h: (i, h, 0, 0))
    st_in = pl.BlockSpec((None, nb, 1, DK_B, DV_B), lambda i, h: (e, i, h, 0, 0))
    s_new, o = pl.pallas_call(
        _gdn_decode_state_body,
        grid=(bn // nb, H_B),
        in_specs=[st_in, vec, vec, vec, vec, vec, vec, vec, _whole((1, DV_B))],
        out_specs=[st, vec],
        out_shape=[jax.ShapeDtypeStruct(s_all.shape[1:], F32), wide],
        compiler_params=_cp("parallel", "parallel"),
        name="gdn_decode_state",
    )(s_all, w, qg, k, u, qk, gl, z, gnorm.reshape(1, DV_B))
    return o, new_conv, s_new


def _head_expander(width):
    e = np.zeros((LANES, H_C * width), np.float32)
    for h in range(H_C):
        e[h, h * width:(h + 1) * width] = 1.0
    return jnp.asarray(np.tile(e, (3, 1)), BF16)


def _group_rms(y, g):
    gs = D_INNER // G_C
    parts = [_rms(y[:, i * gs:(i + 1) * gs], g[:, i * gs:(i + 1) * gs]) for i in range(G_C)]
    return jnp.concatenate(parts, axis=1)


def _ssd_prompt_body(z_ref, xbc_ref, sm_ref, smt_ref, cw_ref, cb_ref, dtrow_ref, arow_ref, dtcol_ref,
                     acol_ref, dx_ref, gn_ref, e_ref, o_ref, s_out_ref, xbuf_ref, s_ref):
    c = pl.program_id(1)
    rows = xbc_ref.shape[0]

    @pl.when(c == 0)
    def _():
        xbuf_ref[0:SUBLANES, :] = jnp.zeros((SUBLANES, xbuf_ref.shape[1]), F32)
        s_ref[...] = jnp.zeros_like(s_ref)

    xbc = _silu(_conv_chunk(xbc_ref, xbuf_ref, cw_ref, rows) + cb_ref[...])
    xs = xbc[:, :D_INNER]
    bm = xbc[:, D_INNER:D_INNER + G_C * N_C]
    cm = xbc[:, D_INNER + G_C * N_C:]
    e = e_ref[...]
    lower = _tri(rows, "lower")
    dt = _softplus(sm_ref[...] + dtrow_ref[...])
    acum = _dot_hi(lower.astype(F32), dt * -jnp.exp(arow_ref[...]))
    a_t = _softplus(smt_ref[...] + dtcol_ref[...]) * -jnp.exp(acol_ref[...])
    acum_t = _dot_hi(a_t, _tri(rows, "upper").astype(F32))
    xdt = xs * _expand(dt, e)
    xdte = (xdt * _expand(jnp.exp(acum[rows - 1:rows, :] - acum), e)).astype(BF16)
    scale_y = _expand(jnp.exp(acum), e)
    chunk_decay = scale_y[rows - 1:rows, :]
    xdt16 = xdt.astype(BF16)
    neg_upper = jnp.where(lower, 0.0, -jnp.inf)
    gw = HPG * P_C
    ys = []
    for g in range(G_C):
        bg = bm[:, g * N_C:(g + 1) * N_C]
        cg16 = cm[:, g * N_C:(g + 1) * N_C].astype(BF16)
        cb = _dot_nt(cg16, bg.astype(BF16))
        yg = []
        for hh in range(HPG):
            h = g * HPG + hh
            lmat = jnp.exp(acum[:, h:h + 1] - acum_t[h:h + 1, :] + neg_upper)
            yg.append(_dot((cb * lmat).astype(BF16), xdt16[:, h * P_C:(h + 1) * P_C]))
        gs = slice(g * gw, (g + 1) * gw)
        sg = s_ref[:, gs]
        y_off = _dot(cg16, sg.astype(BF16)) * scale_y[:, gs]
        ys.append(jnp.concatenate(yg, axis=1) + y_off)
        s_ref[:, gs] = sg * chunk_decay[:, gs] + _dot(bg.T.astype(BF16), xdte[:, gs])
    y = jnp.concatenate(ys, axis=1) + dx_ref[...] * xs
    y = y * _silu(z_ref[...])
    o_ref[...] = _group_rms(y, gn_ref[...]).astype(BF16)

    @pl.when(c == pl.num_programs(1) - 1)
    def _():
        s_out_ref[0] = s_ref[...].T.reshape(H_C, P_C, N_C)


def _ssd_small_params(dt_bias, a_log, ns):
    row = lambda v: jnp.zeros((1, LANES), F32).at[0, :H_C].set(v)
    col = lambda v: jnp.zeros((ns, 1), F32).at[:H_C, 0].set(v)
    return row(dt_bias), row(a_log), col(dt_bias), col(a_log)


def _ssd_prompt(z, xbc, small, small_t, conv_w, conv_b, dt_bias, a_log, d_skip, gnorm, bn, l):
    rows = SSD_CHUNK
    nc = l // rows
    t = bn * l
    ns = small_t.shape[0]
    dtrow, arow, dtcol, acol = _ssd_small_params(dt_bias, a_log, ns)
    tok = lambda n: pl.BlockSpec((rows, n), lambda b, c: (b * nc + c, 0))
    return pl.pallas_call(
        _ssd_prompt_body,
        grid=(bn, nc),
        in_specs=[
            tok(D_INNER), tok(SSD_CONV_CH), tok(LANES),
            pl.BlockSpec((ns, rows), lambda b, c: (0, b * nc + c)),
            _whole((CONV_W, SSD_CONV_CH)), _whole((1, SSD_CONV_CH)),
            _whole((1, LANES)), _whole((1, LANES)), _whole((ns, 1)), _whole((ns, 1)),
            _whole((1, D_INNER)), _whole((1, D_INNER)), _whole((3 * LANES, D_INNER)),
        ],
        out_specs=[tok(D_INNER), pl.BlockSpec((1, H_C, P_C, N_C), lambda b, c: (b, 0, 0, 0))],
        out_shape=[jax.ShapeDtypeStruct((t, D_INNER), BF16), jax.ShapeDtypeStruct((bn, H_C, P_C, N_C), F32)],
        scratch_shapes=[pltpu.VMEM((rows + SUBLANES, SSD_CONV_CH), F32), pltpu.VMEM((N_C, D_INNER), F32)],
        compiler_params=_cp("arbitrary", "arbitrary"),
        name="ssd_prompt",
    )(z, xbc, small, small_t, conv_w, conv_b.reshape(1, SSD_CONV_CH), dtrow, arow, dtcol, acol,
      jnp.repeat(d_skip, P_C).reshape(1, D_INNER), gnorm.reshape(1, D_INNER), _head_expander(P_C))


def _ssd_decode_pre_body(xbc_ref, hist_ref, sm_ref, cw_ref, cb_ref, dtrow_ref, arow_ref, e_ref, en_ref,
                         new_ref, xs_ref, xdt_ref, b_ref, c_ref, dax_ref, dan_ref):
    xbc = _silu(_conv_step(xbc_ref[...], hist_ref, cw_ref, new_ref, SSD_CONV_CH) + cb_ref[...])
    xs = xbc[:, :D_INNER]
    dt = _softplus(sm_ref[...] + dtrow_ref[...])
    a = dt * -jnp.exp(arow_ref[...])
    xs_ref[...] = xs
    xdt_ref[...] = xs * _expand(dt, e_ref[...])
    b_ref[...] = xbc[:, D_INNER:D_INNER + G_C * N_C]
    c_ref[...] = xbc[:, D_INNER + G_C * N_C:]
    dax_ref[...] = jnp.exp(_expand(a, e_ref[...]))
    dan_ref[...] = jnp.exp(_expand(a, en_ref[...]))


def _ssd_decode_state_body(s_ref, xdt_ref, b_ref, c_ref, dan_ref, s_out_ref, yoff_ref):
    nb = s_ref.shape[0]
    gw = HPG * P_C
    row = lax.broadcasted_iota(jnp.int32, (nb, 1), 0)
    xdt = xdt_ref[...]
    b16 = b_ref[...].astype(BF16)
    c16 = c_ref[...].astype(BF16)
    dan = dan_ref[...]
    yoff = jnp.zeros((nb, gw), F32)
    for bb in range(nb):
        s = s_ref[bb].reshape(gw, N_C)
        res = _dot_nt(c16, s.astype(BF16))
        yoff = jnp.where(row == bb, res, yoff)
        x_one = jnp.where(row == bb, xdt, 0.0).astype(BF16)
        upd = _dot_tn(x_one, b16)
        for hh in range(HPG):
            rs = slice(hh * P_C, (hh + 1) * P_C)
            s_out_ref[bb, hh] = s[rs] * dan[bb:bb + 1, hh * N_C:(hh + 1) * N_C] + upd[rs]
    yoff_ref[...] = yoff


def _ssd_decode_post_body(yoff_ref, dax_ref, xdt_ref, xs_ref, b_ref, c_ref, z_ref, dx_ref, gn_ref, o_ref):
    gw = HPG * P_C
    bc = b_ref[...] * c_ref[...]
    cbx = [jnp.broadcast_to(jnp.sum(bc[:, g * N_C:(g + 1) * N_C], axis=-1, keepdims=True), (bc.shape[0], gw))
           for g in range(G_C)]
    y = yoff_ref[...] * dax_ref[...] + jnp.concatenate(cbx, axis=1) * xdt_ref[...]
    y = y + dx_ref[...] * xs_ref[...]
    y = y * _silu(z_ref[...])
    o_ref[...] = _group_rms(y, gn_ref[...])


def _ssd_decode(z, xbc, hist, small, conv_w, conv_b, dt_bias, a_log, d_skip, gnorm, s_all, e):
    bn = xbc.shape[0]
    dtrow, arow, _, _ = _ssd_small_params(dt_bias, a_log, H_C)
    wide = jax.ShapeDtypeStruct((bn, D_INNER), F32)
    grp = jax.ShapeDtypeStruct((bn, G_C * N_C), F32)
    plain = pltpu.CompilerParams(vmem_limit_bytes=VMEM_LIMIT_BYTES)
    new_conv, xs, xdt, bm, cm, dax, dan = pl.pallas_call(
        _ssd_decode_pre_body,
        out_shape=[jax.ShapeDtypeStruct(hist.shape, F32), wide, wide, grp, grp, wide,
                   jax.ShapeDtypeStruct((bn, H_C * N_C), F32)],
        compiler_params=plain,
        name="ssd_decode_pre",
    )(xbc, hist, small, conv_w, conv_b.reshape(1, SSD_CONV_CH), dtrow, arow,
      _head_expander(P_C), _head_expander(N_C))
    nb = SUBLANES
    gw = HPG * P_C
    st = pl.BlockSpec((nb, HPG, P_C, N_C), lambda i, g: (i, g, 0, 0))
    st_in = pl.BlockSpec((None, nb, HPG, P_C, N_C), lambda i, g: (e, i, g, 0, 0))
    s_new, yoff = pl.pallas_call(
        _ssd_decode_state_body,
        grid=(bn // nb, G_C),
        in_specs=[st_in, pl.BlockSpec((nb, gw), lambda i, g: (i, g)), pl.BlockSpec((nb, N_C), lambda i, g: (i, g)),
                  pl.BlockSpec((nb, N_C), lambda i, g: (i, g)), pl.BlockSpec((nb, HPG * N_C), lambda i, g: (i, g))],
        out_specs=[st, pl.BlockSpec((nb, gw), lambda i, g: (i, g))],
        out_shape=[jax.ShapeDtypeStruct(s_all.shape[1:], F32), wide],
        compiler_params=_cp("parallel", "parallel"),
        name="ssd_decode_state",
    )(s_all, xdt, bm, cm, dan)
    mix = pl.pallas_call(
        _ssd_decode_post_body,
        out_shape=wide,
        compiler_params=plain,
        name="ssd_decode_post",
    )(yoff, dax, xdt, xs, bm, cm, z, jnp.repeat(d_skip, P_C).reshape(1, D_INNER), gnorm.reshape(1, D_INNER))
    return mix, new_conv, s_new


def _narrow_weights(w_small):
    n = w_small.shape[1]
    return jnp.zeros((D_MODEL, LANES), BF16).at[:, :n].set(w_small.astype(BF16))


def _even_q_pad(qa):
    bn = qa.shape[0]
    q = qa.reshape(bn, KV_A, G_A, HD_A)
    out = jnp.zeros((bn, KV_A, G_A, KV_A, HD_A), F32)
    for kv in range(KV_A):
        out = out.at[:, kv, :, kv, :].set(q[:, kv])
    return out.reshape(bn, H_A, KV_A * HD_A)


def _even_o_unpad(o8):
    bn = o8.shape[0]
    o = o8.reshape(bn, KV_A, G_A, KV_A, HD_A)
    return jnp.concatenate([o[:, kv, :, kv, :].reshape(bn, G_A * HD_A) for kv in range(KV_A)], axis=1)


def _trunk(x3, states, wts):
    (rel_bias, norm_ff1, norm_mix, norm_ff2, norm_final,
     ff1, ff2, even_w_in, even_w_out, swa_sinks, gdn_conv_w, gdn_A_log, gdn_dt_bias, gdn_norm,
     ssd_w_in, ssd_w_out, ssd_conv_w, ssd_conv_b, ssd_dt_bias, ssd_A_log, ssd_D, ssd_norm) = wts
    bn, l, d = x3.shape
    t = bn * l
    x = x3.reshape(t, d)
    decode = states is not None
    ks, vs, gconv, gssm, sconv, sssm = [], [], [], [], [], []
    depth = norm_ff1.shape[0]
    width = KV_A * HD_A
    for layer in range(depth):
        x = _ffn(x, norm_ff1[layer], *ff1[layer])
        e = layer // 2
        if layer % 2 == 0:
            w_main, ws = even_w_in[e]
            qa, kv, qkvb, z, small, small_t = _inproj(x, norm_mix[layer], w_main, ws, 2 * SUBLANES,
                                                      (A_Q, 2 * width, B_QKV, B_Z))
            if decode:
                ck = states[0][e].reshape(bn, WINDOW, width)
                cv = states[1][e].reshape(bn, WINDOW, width)
                o8 = _swa_decode(_even_q_pad(qa), kv, ck, cv, rel_bias, swa_sinks[e])
                o_a = _even_o_unpad(o8)
                hist = states[2][e].reshape(bn, (CONV_W - 1) * B_QKV)
                o_b, new_conv, s_new = _gdn_decode(qkvb, hist, z, small, gdn_conv_w[e], gdn_A_log[e],
                                                   gdn_dt_bias[e], gdn_norm[e], states[3], e)
                new_k, new_v = kv[:, :width], kv[:, width:]
                new_conv = new_conv.reshape(bn, CONV_W - 1, B_QKV)
            else:
                o_a = _swa_prompt(qa, kv, rel_bias, swa_sinks[e], bn, l)
                o_b, s_new = _gdn_prompt(qkvb, z, small, small_t, gdn_conv_w[e], gdn_A_log[e],
                                         gdn_dt_bias[e], gdn_norm[e], bn, l)
                kv3 = kv.reshape(bn, l, 2 * width)
                new_k = kv3[:, l - WINDOW:, :width]
                new_v = kv3[:, l - WINDOW:, width:]
                new_conv = qkvb.reshape(bn, l, B_QKV)[:, l - (CONV_W - 1):]
            ks.append(new_k)
            vs.append(new_v)
            gconv.append(new_conv)
            gssm.append(s_new)
            mixes = [(o_a, even_w_out[e][:A_Q]), (o_b, even_w_out[e][A_Q:])]
        else:
            w_main, ws = ssd_w_in[e]
            z, xbc, small, small_t = _inproj(x, norm_mix[layer], w_main, ws, H_C, (D_INNER, SSD_CONV_CH))
            if decode:
                hist = states[4][e].reshape(bn, (CONV_W - 1) * SSD_CONV_CH)
                mix, new_conv, s_new = _ssd_decode(z, xbc, hist, small, ssd_conv_w[e], ssd_conv_b[e],
                                                   ssd_dt_bias[e], ssd_A_log[e], ssd_D[e], ssd_norm[e],
                                                   states[5], e)
                new_conv = new_conv.reshape(bn, CONV_W - 1, SSD_CONV_CH)
            else:
                mix, s_new = _ssd_prompt(z, xbc, small, small_t, ssd_conv_w[e], ssd_conv_b[e], ssd_dt_bias[e],
                                         ssd_A_log[e], ssd_D[e], ssd_norm[e], bn, l)
                new_conv = xbc.reshape(bn, l, SSD_CONV_CH)[:, l - (CONV_W - 1):]
            sconv.append(new_conv)
            sssm.append(s_new)
            mixes = [(mix, ssd_w_out[e])]
        x = _ffn(x, norm_ff2[layer], *ff2[layer], mixes=mixes,
                 g_final=norm_final if layer == depth - 1 else None)
    n_even = len(ks)
    if decode:
        win = lambda c: c.reshape(n_even, bn, WINDOW, width)
        ks = _cache_shift(win(states[0]), jnp.stack(ks))
        vs = _cache_shift(win(states[1]), jnp.stack(vs))
    else:
        ks, vs = jnp.stack(ks), jnp.stack(vs)
    kv5 = (n_even, bn, WINDOW, KV_A, HD_A)
    return (x.reshape(bn, l, d), ks.reshape(kv5), vs.reshape(kv5), jnp.stack(gconv), jnp.stack(gssm),
            jnp.stack(sconv), jnp.stack(sssm))


def kernel(x_prompt, x_sample, cache_swa_k, cache_swa_v, state_gdn_conv, state_gdn_ssm, state_ssd_conv, state_ssd_ssm, rel_bias, norm_ff1, norm_mix, norm_ff2, norm_final, ff1_gate, ff1_up, ff1_down, ff2_gate, ff2_up, ff2_down, even_w_in, even_w_out, swa_sinks, gdn_conv_w, gdn_A_log, gdn_dt_bias, gdn_norm, ssd_w_in, ssd_w_out, ssd_conv_w, ssd_conv_b, ssd_dt_bias, ssd_A_log, ssd_D, ssd_norm):
    depth = norm_ff1.shape[0]
    ff1 = [(ff1_gate[i].astype(BF16), ff1_up[i].astype(BF16), ff1_down[i].astype(BF16)) for i in range(depth)]
    ff2 = [(ff2_gate[i].astype(BF16), ff2_up[i].astype(BF16), ff2_down[i].astype(BF16)) for i in range(depth)]
    n_even_main = A_Q + 2 * KV_A * HD_A + B_QKV + B_Z
    even_in = [(even_w_in[e][:, :n_even_main].astype(BF16), _narrow_weights(even_w_in[e][:, n_even_main:]))
               for e in range(even_w_in.shape[0])]
    n_odd_main = D_INNER + SSD_CONV_CH
    odd_in = [(ssd_w_in[e][:, :n_odd_main].astype(BF16), _narrow_weights(ssd_w_in[e][:, n_odd_main:]))
              for e in range(ssd_w_in.shape[0])]
    wts = (rel_bias, norm_ff1, norm_mix, norm_ff2, norm_final, ff1, ff2,
           even_in, even_w_out.astype(BF16), swa_sinks, gdn_conv_w, gdn_A_log, gdn_dt_bias, gdn_norm,
           odd_in, ssd_w_out.astype(BF16), ssd_conv_w, ssd_conv_b, ssd_dt_bias, ssd_A_log, ssd_D, ssd_norm)
    y_p, p_k, p_v, p_gconv, p_gssm, p_sconv, p_sssm = _trunk(x_prompt, None, wts)
    states = (cache_swa_k, cache_swa_v, state_gdn_conv, state_gdn_ssm, state_ssd_conv, state_ssd_ssm)
    y_s, s_k, s_v, s_gconv, s_gssm, s_sconv, s_sssm = _trunk(x_sample, states, wts)
    return (y_p, y_s, p_k, p_v, p_gconv, p_gssm, p_sconv, p_sssm,
            s_k, s_v, s_gconv, s_gssm, s_sconv, s_sssm)
```

```python
import functools
import math

import numpy as np
import jax
import jax.numpy as jnp
from jax import lax
from jax.experimental import pallas as pl
from jax.experimental.pallas import tpu as pltpu

F32 = jnp.float32
BF16 = jnp.bfloat16
HI = lax.Precision.HIGHEST

EPS = 1e-6
NEG_INF = -1e30
D_MODEL = 1024
WINDOW = 128
BLOCK = 128
H_A, KV_A, G_A, HD_A = 8, 2, 4, 64
N_BUCKETS, MAX_DIST = 32, 128
H_B, DK_B, DV_B = 4, 128, 128
CONV_W = 4
GDN_CHUNK = 64
D_INNER = 2048
P_C, H_C, N_C, G_C = 64, 32, 128, 4
HPG = H_C // G_C
SSD_CHUNK = 128
A_Q = H_A * HD_A
B_QKV = 3 * H_B * DK_B
B_Z = H_B * DV_B
SSD_CONV_CH = D_INNER + 2 * G_C * N_C
LANES = 128
SUBLANES = 8
VMEM_LIMIT_BYTES = 56 * 1024 * 1024


def _cp(*sem):
    return pltpu.CompilerParams(dimension_semantics=sem, vmem_limit_bytes=VMEM_LIMIT_BYTES)


def _whole(shape):
    nd = len(shape)
    return pl.BlockSpec(shape, lambda *_: (0,) * nd, pipeline_mode=pl.Buffered(1))


def _rms(x, g):
    return x * lax.rsqrt(jnp.mean(x * x, axis=-1, keepdims=True) + EPS) * g


def _silu(x):
    return x * jax.nn.sigmoid(x)


def _softplus(x):
    return jnp.maximum(x, 0.0) + jnp.log1p(jnp.exp(-jnp.abs(x)))


def _dot(a, b):
    return jnp.dot(a, b, preferred_element_type=F32)


def _dot_nt(a, b):
    return lax.dot_general(a, b, (((1,), (1,)), ((), ())), preferred_element_type=F32)


def _dot_tn(a, b):
    return lax.dot_general(a, b, (((0,), (0,)), ((), ())), preferred_element_type=F32)


def _dot_hi(a, b):
    return jnp.dot(a, b, precision=HI, preferred_element_type=F32)


def _expand(x, e3):
    hi = x.astype(BF16)
    r = x - hi.astype(F32)
    mid = r.astype(BF16)
    lo = (r - mid.astype(F32)).astype(BF16)
    return _dot(jnp.concatenate([hi, mid, lo], axis=1), e3)


def _tri(n, kind):
    r = lax.broadcasted_iota(jnp.int32, (n, n), 0)
    c = lax.broadcasted_iota(jnp.int32, (n, n), 1)
    return {"lower": r >= c, "strict_lower": r > c, "upper": r <= c}[kind]


def _ffn_body(*refs, n_mix, final):
    refs = list(refs)
    x = refs.pop(0)[...]
    for _ in range(n_mix):
        m_ref, wo_ref = refs.pop(0), refs.pop(0)
        x = x + _dot(m_ref[...].astype(BF16), wo_ref[...])
    g_ref, wg_ref, wu_ref, wd_ref = refs[:4]
    o_ref = refs[-1]
    hn = _rms(x, g_ref[...]).astype(BF16)
    act = (_silu(_dot(hn, wg_ref[...])) * _dot(hn, wu_ref[...])).astype(BF16)
    y = x + 0.5 * _dot(act, wd_ref[...])
    if final:
        y = _rms(y, refs[4][...])
    o_ref[...] = y


def _ffn(x, g, wg, wu, wd, mixes=(), g_final=None):
    t, d = x.shape
    tm = min(t, 512)
    final = g_final is not None
    row = pl.BlockSpec((tm, d), lambda i: (i, 0))
    in_specs, args = [row], [x]
    for m, w in mixes:
        in_specs += [pl.BlockSpec((tm, m.shape[1]), lambda i: (i, 0)), _whole(w.shape)]
        args += [m, w]
    in_specs += [_whole((1, d)), _whole(wg.shape), _whole(wu.shape), _whole(wd.shape)]
    args += [g.reshape(1, d), wg, wu, wd]
    if final:
        in_specs.append(_whole((1, d)))
        args.append(g_final.reshape(1, d))
    return pl.pallas_call(
        functools.partial(_ffn_body, n_mix=len(mixes), final=final),
        grid=(t // tm,),
        in_specs=in_specs,
        out_specs=row,
        out_shape=jax.ShapeDtypeStruct((t, d), F32),
        compiler_params=_cp("parallel"),
        name="ffn",
    )(*args)


def _inproj_body(x_ref, g_ref, w_ref, ws_ref, *outs, splits):
    hn = _rms(x_ref[...], g_ref[...]).astype(BF16)
    off = 0
    for o_ref, n in zip(outs[:-2], splits):
        o_ref[...] = _dot(hn, w_ref[:, off:off + n])
        off += n
    small = _dot(hn, ws_ref[...])
    outs[-2][...] = small
    outs[-1][...] = small.T[:outs[-1].shape[0]]


def _inproj(x, g, w, ws, ns, splits):
    t, d = x.shape
    tm = min(t, 512)
    row = pl.BlockSpec((tm, d), lambda i: (i, 0))
    out_specs = [pl.BlockSpec((tm, n), lambda i: (i, 0)) for n in splits]
    out_specs += [pl.BlockSpec((tm, LANES), lambda i: (i, 0)), pl.BlockSpec((ns, tm), lambda i: (0, i))]
    out_shape = [jax.ShapeDtypeStruct((t, n), F32) for n in splits]
    out_shape += [jax.ShapeDtypeStruct((t, LANES), F32), jax.ShapeDtypeStruct((ns, t), F32)]
    return pl.pallas_call(
        functools.partial(_inproj_body, splits=splits),
        grid=(t // tm,),
        in_specs=[row, _whole((1, d)), _whole(w.shape), _whole(ws.shape)],
        out_specs=out_specs,
        out_shape=out_shape,
        compiler_params=_cp("parallel"),
        name="inproj",
    )(x, g.reshape(1, d), w, ws)


def _t5_bucket_np(dist):
    max_exact = N_BUCKETS // 2
    df = np.maximum(dist, max_exact).astype(np.float32)
    large = max_exact + (np.log(df / np.float32(max_exact)) / np.float32(math.log(MAX_DIST / max_exact))
                         * np.float32(N_BUCKETS - max_exact)).astype(np.int32)
    return np.where(dist < max_exact, dist, np.minimum(large, N_BUCKETS - 1)).astype(np.int32)


def _band_bucket_ids(n_q, n_k, offset):
    d = offset + np.arange(n_q)[:, None] - np.arange(n_k)[None, :]
    valid = (d >= 0) & (d <= WINDOW)
    return np.where(valid, _t5_bucket_np(np.clip(d, 0, WINDOW)), -1).astype(np.int32)


def _bias_from_buckets(bid, rb_ref, h):
    acc = jnp.full(bid.shape, NEG_INF, F32)
    for bk in range(N_BUCKETS):
        acc = jnp.where(bid == bk, rb_ref[bk, h], acc)
    return acc


def _swa_prompt_body(bid_ref, rb_ref, sk_ref, q_ref, kvp_ref, kvc_ref, o_ref, bias_ref):
    first_step = (pl.program_id(0) == 0) & (pl.program_id(1) == 0)

    @pl.when(first_step)
    def _():
        bid = bid_ref[...]
        col = lax.broadcasted_iota(jnp.int32, (BLOCK, 2 * BLOCK), 1)
        for h in range(H_A):
            bias = _bias_from_buckets(bid, rb_ref, h)
            bias_ref[h] = bias
            bias_ref[H_A + h] = jnp.where(col < BLOCK, NEG_INF, bias)

    first_block = jnp.where(pl.program_id(1) == 0, H_A, 0)
    kvp = kvp_ref[...]
    kvc = kvc_ref[...]
    scale = HD_A ** -0.5
    outs = []
    for kv in range(KV_A):
        ks = slice(kv * HD_A, (kv + 1) * HD_A)
        vs = slice(KV_A * HD_A + kv * HD_A, KV_A * HD_A + (kv + 1) * HD_A)
        k = jnp.concatenate([kvp[:, ks], kvc[:, ks]], axis=0).astype(BF16)
        v = jnp.concatenate([kvp[:, vs], kvc[:, vs]], axis=0).astype(BF16)
        for g in range(G_A):
            h = kv * G_A + g
            q = (q_ref[:, h * HD_A:(h + 1) * HD_A] * scale).astype(BF16)
            s = _dot_nt(q, k) + bias_ref[first_block + h]
            sk = sk_ref[h]
            m = jnp.maximum(jnp.max(s, axis=-1, keepdims=True), sk)
            p = jnp.exp(s - m)
            denom = jnp.sum(p, axis=-1, keepdims=True) + jnp.exp(sk - m)
            outs.append(_dot(p.astype(BF16), v) / denom)
    o_ref[...] = jnp.concatenate(outs, axis=1).astype(BF16)


def _swa_prompt(qa, kv, rel_bias, sinks, bn, l):
    nb = l // BLOCK
    t = bn * l
    bid = jnp.asarray(_band_bucket_ids(BLOCK, 2 * BLOCK, BLOCK))
    kv_blk = (BLOCK, 2 * KV_A * HD_A)
    return pl.pallas_call(
        _swa_prompt_body,
        grid=(bn, nb),
        in_specs=[
            _whole((BLOCK, 2 * BLOCK)),
            pl.BlockSpec(memory_space=pltpu.SMEM),
            pl.BlockSpec(memory_space=pltpu.SMEM),
            pl.BlockSpec((BLOCK, A_Q), lambda b, i: (b * nb + i, 0)),
            pl.BlockSpec(kv_blk, lambda b, i: (jnp.maximum(b * nb + i - 1, 0), 0)),
            pl.BlockSpec(kv_blk, lambda b, i: (b * nb + i, 0)),
        ],
        out_specs=pl.BlockSpec((BLOCK, A_Q), lambda b, i: (b * nb + i, 0)),
        out_shape=jax.ShapeDtypeStruct((t, A_Q), BF16),
        scratch_shapes=[pltpu.VMEM((2 * H_A, BLOCK, 2 * BLOCK), F32)],
        compiler_params=_cp("arbitrary", "arbitrary"),
        name="swa_prompt",
    )(bid, rel_bias, sinks, qa, kv, kv)


def _swa_decode_body(bid_ref, rb_ref, sk_ref, q_ref, kvn_ref, ck_ref, cv_ref, o_ref):
    bid = bid_ref[...]
    row = lax.broadcasted_iota(jnp.int32, (H_A, 1), 0)
    bias = jnp.zeros((H_A, bid.shape[1]), F32)
    sk = jnp.zeros((H_A, 1), F32)
    for h in range(H_A):
        bias = jnp.where(row == h, _bias_from_buckets(bid, rb_ref, h), bias)
        sk = jnp.where(row == h, sk_ref[h], sk)
    bias_c = bias[:, :WINDOW]
    bias_n = bias[:, WINDOW:WINDOW + 1]
    scale = HD_A ** -0.5
    q = q_ref[...]
    kvn = kvn_ref[...]
    width = KV_A * HD_A
    k_new = kvn[:, None, :width]
    v_new = kvn[:, None, width:]
    s = lax.dot_general(q.astype(BF16), ck_ref[...].astype(BF16), (((2,), (2,)), ((0,), (0,))),
                        preferred_element_type=F32) * scale + bias_c[None]
    s_n = jnp.sum(q * k_new, axis=-1, keepdims=True) * scale + bias_n[None]
    m = jnp.maximum(jnp.maximum(jnp.max(s, axis=-1, keepdims=True), s_n), sk[None])
    p = jnp.exp(s - m)
    p_n = jnp.exp(s_n - m)
    denom = jnp.sum(p, axis=-1, keepdims=True) + p_n + jnp.exp(sk[None] - m)
    o = lax.dot_general((p / denom).astype(BF16), cv_ref[...].astype(BF16), (((2,), (1,)), ((0,), (0,))),
                        preferred_element_type=F32)
    o_ref[...] = o + (p_n / denom) * v_new


def _swa_decode(q_pad, kv_new, cache_k, cache_v, rel_bias, sinks):
    bn = q_pad.shape[0]
    bs = 32
    width = KV_A * HD_A
    ids = _band_bucket_ids(1, WINDOW + 1, WINDOW)
    bid = np.full((1, WINDOW + LANES), -1, np.int32)
    bid[:, :WINDOW + 1] = ids
    return pl.pallas_call(
        _swa_decode_body,
        grid=(bn // bs,),
        in_specs=[
            _whole((1, WINDOW + LANES)),
            pl.BlockSpec(memory_space=pltpu.SMEM),
            pl.BlockSpec(memory_space=pltpu.SMEM),
            pl.BlockSpec((bs, H_A, width), lambda i: (i, 0, 0)),
            pl.BlockSpec((bs, 2 * width), lambda i: (i, 0)),
            pl.BlockSpec((bs, WINDOW, width), lambda i: (i, 0, 0)),
            pl.BlockSpec((bs, WINDOW, width), lambda i: (i, 0, 0)),
        ],
        out_specs=pl.BlockSpec((bs, H_A, width), lambda i: (i, 0, 0)),
        out_shape=jax.ShapeDtypeStruct((bn, H_A, width), F32),
        compiler_params=_cp("parallel"),
        name="swa_decode",
    )(jnp.asarray(bid), rel_bias, sinks, q_pad, kv_new, cache_k, cache_v)


def _cache_shift_body(c_ref, n_ref, o_ref):
    o_ref[:, 0:WINDOW - 1, :] = c_ref[:, 1:WINDOW, :]
    o_ref[:, WINDOW - 1:WINDOW, :] = n_ref[...][:, None, :]


def _cache_shift(cache_all, new_rows):
    n, bn, win, width = cache_all.shape
    bs = 32
    blk = pl.BlockSpec((None, bs, win, width), lambda e, i: (e, i, 0, 0))
    return pl.pallas_call(
        _cache_shift_body,
        grid=(n, bn // bs),
        in_specs=[blk, pl.BlockSpec((None, bs, width), lambda e, i: (e, i, 0))],
        out_specs=blk,
        out_shape=jax.ShapeDtypeStruct(cache_all.shape, F32),
        compiler_params=_cp("parallel", "parallel"),
        name="cache_shift",
    )(cache_all, new_rows)


def _conv_rows(x_ref, xbuf_ref, cw_ref, rows):
    xbuf_ref[SUBLANES:SUBLANES + rows, :] = x_ref[...]
    x = xbuf_ref[...]
    out = x * cw_ref[0:1, :]
    for i in range(1, CONV_W):
        out = pltpu.roll(out, 1, axis=0) + x * cw_ref[i:i + 1, :]
    return out[SUBLANES:]


def _conv_chunk(x_ref, xbuf_ref, cw_ref, rows):
    out = _conv_rows(x_ref, xbuf_ref, cw_ref, rows)
    xbuf_ref[0:SUBLANES, :] = xbuf_ref[rows:rows + SUBLANES, :]
    return out


def _l2norm(x):
    return x * lax.rsqrt(jnp.sum(x * x, axis=-1, keepdims=True) + EPS)


def _chunk_masks(n, cc):
    r = lax.broadcasted_iota(jnp.int32, (n, n), 0)
    c = lax.broadcasted_iota(jnp.int32, (n, n), 1)
    shift = int(math.log2(cc))
    same = lax.shift_right_logical(r, shift) == lax.shift_right_logical(c, shift)
    return same, same & (r >= c), same & (r > c), same & (r <= c)


def _dot_bf16x3(a, b):
    a_hi = a.astype(BF16)
    a_lo = (a - a_hi.astype(F32)).astype(BF16)
    b_hi = b.astype(BF16)
    b_lo = (b - b_hi.astype(F32)).astype(BF16)
    return _dot(a_hi, b_hi) + _dot(a_hi, b_lo) + _dot(a_lo, b_hi)


def _gdn_prep_body(qkv_ref, halo_ref, sm_ref, smt_ref, cw_ref, acol_ref, dtcol_ref,
                   u_ref, w_ref, qg_ref, kd_ref, aqk_ref, gl_ref, xbuf_ref):
    rows = qkv_ref.shape[0]
    cc = GDN_CHUNK
    xbuf_ref[0:SUBLANES, :] = jnp.where(pl.program_id(1) == 0, 0.0, halo_ref[...])
    conv = _silu(_conv_rows(qkv_ref, xbuf_ref, cw_ref, rows))
    nq = H_B * DK_B
    beta_all = jax.nn.sigmoid(sm_ref[...])
    gt_all = -jnp.exp(acol_ref[...]) * _softplus(smt_ref[...] + dtcol_ref[...])
    pad = jnp.zeros((LANES - gt_all.shape[0], rows), F32)
    g_all = jnp.concatenate([gt_all, pad], axis=0).T
    same, lower, strict, upper = _chunk_masks(rows, cc)
    gc = _dot_hi(lower.astype(F32), g_all)
    gct = _dot_hi(gt_all, upper.astype(F32))
    gsum = _dot_hi(same.astype(F32), g_all)
    neg_outside = jnp.where(lower, 0.0, -jnp.inf)
    neg_strict = jnp.where(strict, -1.0, 0.0)
    heads = range(H_B)
    qh, kh16, kb, decay, egc = [], [], [], [], []
    for h in heads:
        q = _l2norm(conv[:, h * DK_B:(h + 1) * DK_B]) * (DK_B ** -0.5)
        k = _l2norm(conv[:, nq + h * DK_B:nq + (h + 1) * DK_B])
        v = conv[:, 2 * nq + h * DV_B:2 * nq + (h + 1) * DV_B]
        beta = beta_all[:, h:h + 1]
        gcol = gc[:, H_B + h:H_B + h + 1]
        grow = gct[H_B + h:H_B + h + 1, :]
        decay.append(jnp.exp(gcol - grow + neg_outside))
        eg = jnp.exp(gcol)
        qg_ref[:, h * DK_B:(h + 1) * DK_B] = (q * eg).astype(BF16)
        kd_ref[:, h * DK_B:(h + 1) * DK_B] = (k * jnp.exp(gsum[:, H_B + h:H_B + h + 1] - gcol)).astype(BF16)
        qh.append(q.astype(BF16))
        kh16.append(k.astype(BF16))
        kb.append(k * beta)
        egc.append((v * beta, eg))
    a1 = [_dot_nt(kb[h].astype(BF16), kh16[h]) * decay[h] * neg_strict for h in heads]
    for h in heads:
        aqk = _dot_nt(qh[h], kh16[h]) * decay[h]
        blocks = [aqk[i * cc:(i + 1) * cc, i * cc:(i + 1) * cc] for i in range(rows // cc)]
        aqk_ref[:, h * cc:(h + 1) * cc] = jnp.concatenate(blocks, axis=0).astype(BF16)
    sq = lambda xs: [_dot(x.astype(BF16), x.astype(BF16)) for x in xs]
    pair = lambda xs, ys: [x + y + _dot(x.astype(BF16), y.astype(BF16)) for x, y in zip(xs, ys)]
    a2 = sq(a1)
    a4 = sq(a2)
    x1 = pair(a1, a2)
    a8 = sq(a4)
    a16 = sq(a8)
    x2 = pair(a4, a8)
    a32 = sq(a16)
    x12 = pair(x1, x2)
    x3 = pair(a16, a32)
    r = pair(x12, x3)
    for h in heads:
        vb, eg = egc[h]
        rhs = jnp.concatenate([vb, kb[h] * eg], axis=1)
        sol = rhs + _dot_bf16x3(r[h], rhs)
        u_ref[:, h * DV_B:(h + 1) * DV_B] = sol[:, :DV_B]
        w_ref[:, h * DK_B:(h + 1) * DK_B] = sol[:, DV_B:].astype(BF16)
    for i in range(rows // cc):
        gl = [jnp.broadcast_to(jnp.exp(gsum[i * cc:i * cc + 1, H_B + h:H_B + h + 1]), (SUBLANES, DV_B))
              for h in heads]
        gl_ref[0, i] = jnp.concatenate(gl, axis=1)


def _gdn_scan_body(u_ref, w_ref, qg_ref, kd_ref, aqk_ref, gl_ref, z_ref, gn_ref, o_ref, s_out_ref, s_ref):
    c = pl.program_id(0)
    bn = u_ref.shape[0]
    cc = GDN_CHUNK

    @pl.when(c == 0)
    def _():
        s_ref[...] = jnp.zeros_like(s_ref)

    gn = gn_ref[...]
    group = 2
    for b0 in range(0, bn, group):
        combos = [(b, h) for b in range(b0, b0 + group) for h in range(H_B)]
        res = {}
        for b, h in combos:
            hs = slice(h * DK_B, (h + 1) * DK_B)
            wq = jnp.concatenate([w_ref[b, :, hs], qg_ref[b, :, hs]], axis=0)
            res[b, h] = _dot(wq, s_ref[b, h].astype(BF16))
        v16 = {}
        for b, h in combos:
            hs = slice(h * DV_B, (h + 1) * DV_B)
            v16[b, h] = (u_ref[b, :, hs] - res[b, h][:cc]).astype(BF16)
        for b, h in combos:
            hs = slice(h * DV_B, (h + 1) * DV_B)
            o = res[b, h][cc:] + _dot(aqk_ref[b, :, h * cc:(h + 1) * cc], v16[b, h])
            upd = _dot_tn(kd_ref[b, :, hs], v16[b, h])
            s_ref[b, h] = s_ref[b, h] * gl_ref[b, 0, 0:1, hs] + upd
            o_ref[b, :, hs] = (_rms(o, gn) * _silu(z_ref[b, :, hs])).astype(BF16)

    @pl.when(c == pl.num_programs(0) - 1)
    def _():
        s_out_ref[...] = s_ref[...]


def _gdn_prompt(qkvb, z, small, small_t, conv_w, a_log, dt_bias, gnorm, bn, l):
    rows = 4 * GDN_CHUNK
    cc = GDN_CHUNK
    nb = l // rows
    nc = l // cc
    t = bn * l
    ns = small_t.shape[0]
    nq = H_B * DK_B
    pad_col = lambda v: jnp.zeros((ns, 1), F32).at[H_B:2 * H_B, 0].set(v)
    tok = lambda n: pl.BlockSpec((rows, n), lambda b, i: (b * nb + i, 0))
    halo = pl.BlockSpec((SUBLANES, B_QKV), lambda b, i: (jnp.maximum((b * nb + i) * (rows // SUBLANES) - 1, 0), 0))
    u, w, qg, kd, aqk, gl = pl.pallas_call(
        _gdn_prep_body,
        grid=(bn, nb),
        in_specs=[
            tok(B_QKV), halo, tok(LANES),
            pl.BlockSpec((ns, rows), lambda b, i: (0, b * nb + i)),
            _whole((CONV_W, B_QKV)), _whole((ns, 1)), _whole((ns, 1)),
        ],
        out_specs=[tok(nq), tok(nq), tok(nq), tok(nq), tok(H_B * cc),
                   pl.BlockSpec((1, rows // cc, SUBLANES, nq), lambda b, i: (b, i, 0, 0))],
        out_shape=[jax.ShapeDtypeStruct((t, nq), F32)] + [jax.ShapeDtypeStruct((t, nq), BF16)] * 3
        + [jax.ShapeDtypeStruct((t, H_B * cc), BF16), jax.ShapeDtypeStruct((bn, nc, SUBLANES, nq), F32)],
        scratch_shapes=[pltpu.VMEM((rows + SUBLANES, B_QKV), F32)],
        compiler_params=_cp("parallel", "parallel"),
        name="gdn_prep",
    )(qkvb, qkvb, small, small_t, conv_w, pad_col(a_log), pad_col(dt_bias))
    seq = lambda n: pl.BlockSpec((bn, cc, n), lambda c: (0, c, 0))
    r3 = lambda a: a.reshape(bn, l, a.shape[1])
    o, s_new = pl.pallas_call(
        _gdn_scan_body,
        grid=(nc,),
        in_specs=[seq(nq), seq(nq), seq(nq), seq(nq), seq(H_B * cc),
                  pl.BlockSpec((bn, 1, SUBLANES, nq), lambda c: (0, c, 0, 0)), seq(B_Z), _whole((1, DV_B))],
        out_specs=[seq(B_Z), pl.BlockSpec((bn, H_B, DK_B, DV_B), lambda c: (0, 0, 0, 0))],
        out_shape=[jax.ShapeDtypeStruct((bn, l, B_Z), BF16), jax.ShapeDtypeStruct((bn, H_B, DK_B, DV_B), F32)],
        scratch_shapes=[pltpu.VMEM((bn, H_B, DK_B, DV_B), F32)],
        compiler_params=_cp("arbitrary"),
        name="gdn_scan",
    )(r3(u), r3(w), r3(qg), r3(kd), r3(aqk), gl, r3(z), gnorm.reshape(1, DV_B))
    return o.reshape(t, B_Z), s_new


def _conv_step(x, hist_ref, cw_ref, new_ref, ch):
    out = hist_ref[:, 0:ch] * cw_ref[0:1, :]
    for i in range(1, CONV_W - 1):
        out = out + hist_ref[:, i * ch:(i + 1) * ch] * cw_ref[i:i + 1, :]
    out = out + x * cw_ref[CONV_W - 1:CONV_W, :]
    new_ref[:, 0:(CONV_W - 2) * ch] = hist_ref[:, ch:(CONV_W - 1) * ch]
    new_ref[:, (CONV_W - 2) * ch:] = x
    return out


def _gdn_decode_pre_body(qkv_ref, hist_ref, sm_ref, cw_ref, arow_ref, dtrow_ref,
                         new_ref, w_ref, qg_ref, k_ref, u_ref, qk_ref, gl_ref):
    conv = _silu(_conv_step(qkv_ref[...], hist_ref, cw_ref, new_ref, B_QKV))
    nq = H_B * DK_B
    sm = sm_ref[...]
    beta_all = jax.nn.sigmoid(sm)
    g_all = -jnp.exp(arow_ref[...]) * _softplus(sm + dtrow_ref[...])
    for h in range(H_B):
        hs = slice(h * DK_B, (h + 1) * DK_B)
        qh = _l2norm(conv[:, hs]) * (DK_B ** -0.5)
        kh = _l2norm(conv[:, nq + h * DK_B:nq + (h + 1) * DK_B])
        vh = conv[:, 2 * nq + h * DV_B:2 * nq + (h + 1) * DV_B]
        beta = beta_all[:, h:h + 1]
        eg = jnp.exp(g_all[:, H_B + h:H_B + h + 1])
        w_ref[:, hs] = kh * beta * eg
        qg_ref[:, hs] = qh * eg
        k_ref[:, hs] = kh
        u_ref[:, hs] = vh * beta
        qk_ref[:, hs] = jnp.broadcast_to(jnp.sum(qh * kh, axis=-1, keepdims=True), qh.shape)
        gl_ref[:, hs] = jnp.broadcast_to(eg, qh.shape)


def _gdn_decode_state_body(s_ref, w_ref, qg_ref, k_ref, u_ref, qk_ref, gl_ref, z_ref, gn_ref, done_ref,
                           s_out_ref, o_ref):
    del done_ref
    nb = s_ref.shape[0]
    row = lax.broadcasted_iota(jnp.int32, (nb, 1), 0)
    wq = jnp.concatenate([w_ref[...], qg_ref[...]], axis=0).astype(BF16)
    ws = jnp.zeros((nb, DV_B), F32)
    qs = jnp.zeros((nb, DV_B), F32)
    for bb in range(nb):
        res = _dot(wq, s_ref[bb, 0].astype(BF16))
        ws = jnp.where(row == bb, res[:nb], ws)
        qs = jnp.where(row == bb, res[nb:], qs)
    v_new = u_ref[...] - ws
    o = qs + qk_ref[...] * v_new
    o_ref[...] = _rms(o, gn_ref[...]) * _silu(z_ref[...])
    k = k_ref[...]
    v16 = v_new.astype(BF16)
    gl = gl_ref[...]
    for bb in range(nb):
        k_one = jnp.where(row == bb, k, 0.0).astype(BF16)
        s_out_ref[bb, 0] = s_ref[bb, 0] * gl[bb:bb + 1, :] + _dot_tn(k_one, v16)


def _gdn_decode(qkvb, hist, z, small, conv_w, a_log, dt_bias, gnorm, s_all, s_done, e):
    bn = qkvb.shape[0]
    pad_row = lambda v: jnp.zeros((1, LANES), F32).at[0, H_B:2 * H_B].set(v)
    wide = jax.ShapeDtypeStruct((bn, H_B * DK_B), F32)
    new_conv, w, qg, k, u, qk, gl = pl.pallas_call(
        _gdn_decode_pre_body,
        out_shape=[jax.ShapeDtypeStruct(hist.shape, F32)] + [wide] * 6,
        compiler_params=pltpu.CompilerParams(vmem_limit_bytes=VMEM_LIMIT_BYTES),
        name="gdn_decode_pre",
    )(qkvb, hist, small, conv_w, pad_row(a_log), pad_row(dt_bias))
    nb = SUBLANES
    vec = pl.BlockSpec((nb, DK_B), lambda i, h: (i, h))
    st = pl.BlockSpec((None, nb, 1, DK_B, DV_B), lambda i, h: (e, i, h, 0, 0))
    s_done, o = pl.pallas_call(
        _gdn_decode_state_body,
        grid=(bn // nb, H_B),
        in_specs=[st, vec, vec, vec, vec, vec, vec, vec, _whole((1, DV_B)), pl.BlockSpec(memory_space=pl.ANY)],
        out_specs=[st, vec],
        out_shape=[jax.ShapeDtypeStruct(s_all.shape, F32), wide],
        input_output_aliases={9: 0},
        compiler_params=_cp("parallel", "parallel"),
        name="gdn_decode_state",
    )(s_all, w, qg, k, u, qk, gl, z, gnorm.reshape(1, DV_B), s_done)
    return o, new_conv, s_done


def _head_expander(width):
    e = np.zeros((LANES, H_C * width), np.float32)
    for h in range(H_C):
        e[h, h * width:(h + 1) * width] = 1.0
    return jnp.asarray(np.tile(e, (3, 1)), BF16)


def _group_rms(y, g):
    gs = D_INNER // G_C
    parts = [_rms(y[:, i * gs:(i + 1) * gs], g[:, i * gs:(i + 1) * gs]) for i in range(G_C)]
    return jnp.concatenate(parts, axis=1)


def _ssd_prompt_body(z_ref, xbc_ref, sm_ref, smt_ref, cw_ref, cb_ref, dtrow_ref, arow_ref, dtcol_ref,
                     acol_ref, dx_ref, gn_ref, e_ref, o_ref, s_out_ref, xbuf_ref, s_ref):
    c = pl.program_id(1)
    rows = xbc_ref.shape[0]

    @pl.when(c == 0)
    def _():
        xbuf_ref[0:SUBLANES, :] = jnp.zeros((SUBLANES, xbuf_ref.shape[1]), F32)
        s_ref[...] = jnp.zeros_like(s_ref)

    xbc = _silu(_conv_chunk(xbc_ref, xbuf_ref, cw_ref, rows) + cb_ref[...])
    xs = xbc[:, :D_INNER]
    bm = xbc[:, D_INNER:D_INNER + G_C * N_C]
    cm = xbc[:, D_INNER + G_C * N_C:]
    e = e_ref[...]
    lower = _tri(rows, "lower")
    dt = _softplus(sm_ref[...] + dtrow_ref[...])
    acum = _dot_hi(lower.astype(F32), dt * -jnp.exp(arow_ref[...]))
    a_t = _softplus(smt_ref[...] + dtcol_ref[...]) * -jnp.exp(acol_ref[...])
    acum_t = _dot_hi(a_t, _tri(rows, "upper").astype(F32))
    xdt = xs * _expand(dt, e)
    xdte = (xdt * _expand(jnp.exp(acum[rows - 1:rows, :] - acum), e)).astype(BF16)
    scale_y = _expand(jnp.exp(acum), e)
    chunk_decay = scale_y[rows - 1:rows, :]
    xdt16 = xdt.astype(BF16)
    neg_upper = jnp.where(lower, 0.0, -jnp.inf)
    gw = HPG * P_C
    ys = []
    for g in range(G_C):
        bg = bm[:, g * N_C:(g + 1) * N_C]
        cg16 = cm[:, g * N_C:(g + 1) * N_C].astype(BF16)
        cb = _dot_nt(cg16, bg.astype(BF16))
        yg = []
        for hh in range(HPG):
            h = g * HPG + hh
            lmat = jnp.exp(acum[:, h:h + 1] - acum_t[h:h + 1, :] + neg_upper)
            yg.append(_dot((cb * lmat).astype(BF16), xdt16[:, h * P_C:(h + 1) * P_C]))
        gs = slice(g * gw, (g + 1) * gw)
        sg = s_ref[:, gs]
        y_off = _dot(cg16, sg.astype(BF16)) * scale_y[:, gs]
        ys.append(jnp.concatenate(yg, axis=1) + y_off)
        s_ref[:, gs] = sg * chunk_decay[:, gs] + _dot(bg.T.astype(BF16), xdte[:, gs])
    y = jnp.concatenate(ys, axis=1) + dx_ref[...] * xs
    y = y * _silu(z_ref[...])
    o_ref[...] = _group_rms(y, gn_ref[...]).astype(BF16)

    @pl.when(c == pl.num_programs(1) - 1)
    def _():
        s_out_ref[0] = s_ref[...].T.reshape(H_C, P_C, N_C)


def _ssd_small_params(dt_bias, a_log, ns):
    row = lambda v: jnp.zeros((1, LANES), F32).at[0, :H_C].set(v)
    col = lambda v: jnp.zeros((ns, 1), F32).at[:H_C, 0].set(v)
    return row(dt_bias), row(a_log), col(dt_bias), col(a_log)


def _ssd_prompt(z, xbc, small, small_t, conv_w, conv_b, dt_bias, a_log, d_skip, gnorm, bn, l):
    rows = SSD_CHUNK
    nc = l // rows
    t = bn * l
    ns = small_t.shape[0]
    dtrow, arow, dtcol, acol = _ssd_small_params(dt_bias, a_log, ns)
    tok = lambda n: pl.BlockSpec((rows, n), lambda b, c: (b * nc + c, 0))
    return pl.pallas_call(
        _ssd_prompt_body,
        grid=(bn, nc),
        in_specs=[
            tok(D_INNER), tok(SSD_CONV_CH), tok(LANES),
            pl.BlockSpec((ns, rows), lambda b, c: (0, b * nc + c)),
            _whole((CONV_W, SSD_CONV_CH)), _whole((1, SSD_CONV_CH)),
            _whole((1, LANES)), _whole((1, LANES)), _whole((ns, 1)), _whole((ns, 1)),
            _whole((1, D_INNER)), _whole((1, D_INNER)), _whole((3 * LANES, D_INNER)),
        ],
        out_specs=[tok(D_INNER), pl.BlockSpec((1, H_C, P_C, N_C), lambda b, c: (b, 0, 0, 0))],
        out_shape=[jax.ShapeDtypeStruct((t, D_INNER), BF16), jax.ShapeDtypeStruct((bn, H_C, P_C, N_C), F32)],
        scratch_shapes=[pltpu.VMEM((rows + SUBLANES, SSD_CONV_CH), F32), pltpu.VMEM((N_C, D_INNER), F32)],
        compiler_params=_cp("arbitrary", "arbitrary"),
        name="ssd_prompt",
    )(z, xbc, small, small_t, conv_w, conv_b.reshape(1, SSD_CONV_CH), dtrow, arow, dtcol, acol,
      jnp.repeat(d_skip, P_C).reshape(1, D_INNER), gnorm.reshape(1, D_INNER), _head_expander(P_C))


def _ssd_decode_pre_body(xbc_ref, hist_ref, sm_ref, cw_ref, cb_ref, dtrow_ref, arow_ref, e_ref, en_ref,
                         new_ref, xs_ref, xdt_ref, b_ref, c_ref, dax_ref, dan_ref):
    xbc = _silu(_conv_step(xbc_ref[...], hist_ref, cw_ref, new_ref, SSD_CONV_CH) + cb_ref[...])
    xs = xbc[:, :D_INNER]
    dt = _softplus(sm_ref[...] + dtrow_ref[...])
    a = dt * -jnp.exp(arow_ref[...])
    xs_ref[...] = xs
    xdt_ref[...] = xs * _expand(dt, e_ref[...])
    b_ref[...] = xbc[:, D_INNER:D_INNER + G_C * N_C]
    c_ref[...] = xbc[:, D_INNER + G_C * N_C:]
    dax_ref[...] = jnp.exp(_expand(a, e_ref[...]))
    dan_ref[...] = jnp.exp(_expand(a, en_ref[...]))


def _ssd_decode_state_body(s_ref, xdt_ref, b_ref, c_ref, dan_ref, done_ref, s_out_ref, yoff_ref):
    del done_ref
    nb = s_ref.shape[0]
    gw = HPG * P_C
    row = lax.broadcasted_iota(jnp.int32, (nb, 1), 0)
    xdt = xdt_ref[...]
    b16 = b_ref[...].astype(BF16)
    c16 = c_ref[...].astype(BF16)
    dan = dan_ref[...]
    yoff = jnp.zeros((nb, gw), F32)
    for bb in range(nb):
        s = s_ref[bb].reshape(gw, N_C)
        res = _dot_nt(c16, s.astype(BF16))
        yoff = jnp.where(row == bb, res, yoff)
        x_one = jnp.where(row == bb, xdt, 0.0).astype(BF16)
        upd = _dot_tn(x_one, b16)
        for hh in range(HPG):
            rs = slice(hh * P_C, (hh + 1) * P_C)
            s_out_ref[bb, hh] = s[rs] * dan[bb:bb + 1, hh * N_C:(hh + 1) * N_C] + upd[rs]
    yoff_ref[...] = yoff


def _ssd_decode_post_body(yoff_ref, dax_ref, xdt_ref, xs_ref, b_ref, c_ref, z_ref, dx_ref, gn_ref, o_ref):
    gw = HPG * P_C
    bc = b_ref[...] * c_ref[...]
    cbx = [jnp.broadcast_to(jnp.sum(bc[:, g * N_C:(g + 1) * N_C], axis=-1, keepdims=True), (bc.shape[0], gw))
           for g in range(G_C)]
    y = yoff_ref[...] * dax_ref[...] + jnp.concatenate(cbx, axis=1) * xdt_ref[...]
    y = y + dx_ref[...] * xs_ref[...]
    y = y * _silu(z_ref[...])
    o_ref[...] = _group_rms(y, gn_ref[...])


def _ssd_decode(z, xbc, hist, small, conv_w, conv_b, dt_bias, a_log, d_skip, gnorm, s_all, s_done, e):
    bn = xbc.shape[0]
    dtrow, arow, _, _ = _ssd_small_params(dt_bias, a_log, H_C)
    wide = jax.ShapeDtypeStruct((bn, D_INNER), F32)
    grp = jax.ShapeDtypeStruct((bn, G_C * N_C), F32)
    plain = pltpu.CompilerParams(vmem_limit_bytes=VMEM_LIMIT_BYTES)
    new_conv, xs, xdt, bm, cm, dax, dan = pl.pallas_call(
        _ssd_decode_pre_body,
        out_shape=[jax.ShapeDtypeStruct(hist.shape, F32), wide, wide, grp, grp, wide,
                   jax.ShapeDtypeStruct((bn, H_C * N_C), F32)],
        compiler_params=plain,
        name="ssd_decode_pre",
    )(xbc, hist, small, conv_w, conv_b.reshape(1, SSD_CONV_CH), dtrow, arow,
      _head_expander(P_C), _head_expander(N_C))
    nb = SUBLANES
    gw = HPG * P_C
    st = pl.BlockSpec((None, nb, HPG, P_C, N_C), lambda i, g: (e, i, g, 0, 0))
    s_done, yoff = pl.pallas_call(
        _ssd_decode_state_body,
        grid=(bn // nb, G_C),
        in_specs=[st, pl.BlockSpec((nb, gw), lambda i, g: (i, g)), pl.BlockSpec((nb, N_C), lambda i, g: (i, g)),
                  pl.BlockSpec((nb, N_C), lambda i, g: (i, g)), pl.BlockSpec((nb, HPG * N_C), lambda i, g: (i, g)),
                  pl.BlockSpec(memory_space=pl.ANY)],
        out_specs=[st, pl.BlockSpec((nb, gw), lambda i, g: (i, g))],
        out_shape=[jax.ShapeDtypeStruct(s_all.shape, F32), wide],
        input_output_aliases={5: 0},
        compiler_params=_cp("parallel", "parallel"),
        name="ssd_decode_state",
    )(s_all, xdt, bm, cm, dan, s_done)
    mix = pl.pallas_call(
        _ssd_decode_post_body,
        out_shape=wide,
        compiler_params=plain,
        name="ssd_decode_post",
    )(yoff, dax, xdt, xs, bm, cm, z, jnp.repeat(d_skip, P_C).reshape(1, D_INNER), gnorm.reshape(1, D_INNER))
    return mix, new_conv, s_done


def _narrow_weights(w_small):
    n = w_small.shape[1]
    return jnp.zeros((D_MODEL, LANES), BF16).at[:, :n].set(w_small.astype(BF16))


def _even_q_pad(qa):
    bn = qa.shape[0]
    q = qa.reshape(bn, KV_A, G_A, HD_A)
    out = jnp.zeros((bn, KV_A, G_A, KV_A, HD_A), F32)
    for kv in range(KV_A):
        out = out.at[:, kv, :, kv, :].set(q[:, kv])
    return out.reshape(bn, H_A, KV_A * HD_A)


def _even_o_unpad(o8):
    bn = o8.shape[0]
    o = o8.reshape(bn, KV_A, G_A, KV_A, HD_A)
    return jnp.concatenate([o[:, kv, :, kv, :].reshape(bn, G_A * HD_A) for kv in range(KV_A)], axis=1)


def _trunk(x3, states, wts):
    (rel_bias, norm_ff1, norm_mix, norm_ff2, norm_final,
     ff1, ff2, even_w_in, even_w_out, swa_sinks, gdn_conv_w, gdn_A_log, gdn_dt_bias, gdn_norm,
     ssd_w_in, ssd_w_out, ssd_conv_w, ssd_conv_b, ssd_dt_bias, ssd_A_log, ssd_D, ssd_norm) = wts
    bn, l, d = x3.shape
    t = bn * l
    x = x3.reshape(t, d)
    decode = states is not None
    ks, vs, gconv, gssm, sconv, sssm = [], [], [], [], [], []
    if decode:
        gssm_done, sssm_done = jnp.zeros_like(states[3]), jnp.zeros_like(states[5])
    depth = norm_ff1.shape[0]
    width = KV_A * HD_A
    for layer in range(depth):
        x = _ffn(x, norm_ff1[layer], *ff1[layer])
        e = layer // 2
        if layer % 2 == 0:
            w_main, ws = even_w_in[e]
            qa, kv, qkvb, z, small, small_t = _inproj(x, norm_mix[layer], w_main, ws, 2 * SUBLANES,
                                                      (A_Q, 2 * width, B_QKV, B_Z))
            if decode:
                ck = states[0][e].reshape(bn, WINDOW, width)
                cv = states[1][e].reshape(bn, WINDOW, width)
                o8 = _swa_decode(_even_q_pad(qa), kv, ck, cv, rel_bias, swa_sinks[e])
                o_a = _even_o_unpad(o8)
                hist = states[2][e].reshape(bn, (CONV_W - 1) * B_QKV)
                o_b, new_conv, gssm_done = _gdn_decode(qkvb, hist, z, small, gdn_conv_w[e], gdn_A_log[e],
                                                       gdn_dt_bias[e], gdn_norm[e], states[3], gssm_done, e)
                new_k, new_v = kv[:, :width], kv[:, width:]
                new_conv = new_conv.reshape(bn, CONV_W - 1, B_QKV)
            else:
                o_a = _swa_prompt(qa, kv, rel_bias, swa_sinks[e], bn, l)
                o_b, s_new = _gdn_prompt(qkvb, z, small, small_t, gdn_conv_w[e], gdn_A_log[e],
                                         gdn_dt_bias[e], gdn_norm[e], bn, l)
                kv3 = kv.reshape(bn, l, 2 * width)
                new_k = kv3[:, l - WINDOW:, :width]
                new_v = kv3[:, l - WINDOW:, width:]
                new_conv = qkvb.reshape(bn, l, B_QKV)[:, l - (CONV_W - 1):]
                gssm.append(s_new)
            ks.append(new_k)
            vs.append(new_v)
            gconv.append(new_conv)
            mixes = [(o_a, even_w_out[e][:A_Q]), (o_b, even_w_out[e][A_Q:])]
        else:
            w_main, ws = ssd_w_in[e]
            z, xbc, small, small_t = _inproj(x, norm_mix[layer], w_main, ws, H_C, (D_INNER, SSD_CONV_CH))
            if decode:
                hist = states[4][e].reshape(bn, (CONV_W - 1) * SSD_CONV_CH)
                mix, new_conv, sssm_done = _ssd_decode(z, xbc, hist, small, ssd_conv_w[e], ssd_conv_b[e],
                                                       ssd_dt_bias[e], ssd_A_log[e], ssd_D[e], ssd_norm[e],
                                                       states[5], sssm_done, e)
                new_conv = new_conv.reshape(bn, CONV_W - 1, SSD_CONV_CH)
            else:
                mix, s_new = _ssd_prompt(z, xbc, small, small_t, ssd_conv_w[e], ssd_conv_b[e], ssd_dt_bias[e],
                                         ssd_A_log[e], ssd_D[e], ssd_norm[e], bn, l)
                new_conv = xbc.reshape(bn, l, SSD_CONV_CH)[:, l - (CONV_W - 1):]
                sssm.append(s_new)
            sconv.append(new_conv)
            mixes = [(mix, ssd_w_out[e])]
        x = _ffn(x, norm_ff2[layer], *ff2[layer], mixes=mixes,
                 g_final=norm_final if layer == depth - 1 else None)
    n_even = len(ks)
    if decode:
        win = lambda c: c.reshape(n_even, bn, WINDOW, width)
        ks = _cache_shift(win(states[0]), jnp.stack(ks))
        vs = _cache_shift(win(states[1]), jnp.stack(vs))
        gssm, sssm = gssm_done, sssm_done
    else:
        ks, vs, gssm, sssm = jnp.stack(ks), jnp.stack(vs), jnp.stack(gssm), jnp.stack(sssm)
    kv5 = (n_even, bn, WINDOW, KV_A, HD_A)
    return (x.reshape(bn, l, d), ks.reshape(kv5), vs.reshape(kv5), jnp.stack(gconv), gssm,
            jnp.stack(sconv), sssm)


def kernel(x_prompt, x_sample, cache_swa_k, cache_swa_v, state_gdn_conv, state_gdn_ssm, state_ssd_conv, state_ssd_ssm, rel_bias, norm_ff1, norm_mix, norm_ff2, norm_final, ff1_gate, ff1_up, ff1_down, ff2_gate, ff2_up, ff2_down, even_w_in, even_w_out, swa_sinks, gdn_conv_w, gdn_A_log, gdn_dt_bias, gdn_norm, ssd_w_in, ssd_w_out, ssd_conv_w, ssd_conv_b, ssd_dt_bias, ssd_A_log, ssd_D, ssd_norm):
    depth = norm_ff1.shape[0]
    ff1 = [(ff1_gate[i].astype(BF16), ff1_up[i].astype(BF16), ff1_down[i].astype(BF16)) for i in range(depth)]
    ff2 = [(ff2_gate[i].astype(BF16), ff2_up[i].astype(BF16), ff2_down[i].astype(BF16)) for i in range(depth)]
    n_even_main = A_Q + 2 * KV_A * HD_A + B_QKV + B_Z
    even_in = [(even_w_in[e][:, :n_even_main].astype(BF16), _narrow_weights(even_w_in[e][:, n_even_main:]))
               for e in range(even_w_in.shape[0])]
    n_odd_main = D_INNER + SSD_CONV_CH
    odd_in = [(ssd_w_in[e][:, :n_odd_main].astype(BF16), _narrow_weights(ssd_w_in[e][:, n_odd_main:]))
              for e in range(ssd_w_in.shape[0])]
    wts = (rel_bias, norm_ff1, norm_mix, norm_ff2, norm_final, ff1, ff2,
           even_in, even_w_out.astype(BF16), swa_sinks, gdn_conv_w, gdn_A_log, gdn_dt_bias, gdn_norm,
           odd_in, ssd_w_out.astype(BF16), ssd_conv_w, ssd_conv_b, ssd_dt_bias, ssd_A_log, ssd_D, ssd_norm)
    y_p, p_k, p_v, p_gconv, p_gssm, p_sconv, p_sssm = _trunk(x_prompt, None, wts)
    states = (cache_swa_k, cache_swa_v, state_gdn_conv, state_gdn_ssm, state_ssd_conv, state_ssd_ssm)
    y_s, s_k, s_v, s_gconv, s_gssm, s_sconv, s_sssm = _trunk(x_sample, states, wts)
    return (y_p, y_s, p_k, p_v, p_gconv, p_gssm, p_sconv, p_sssm,
            s_k, s_v, s_gconv, s_gssm, s_sconv, s_sssm)
```

```python
import functools
import math

import numpy as np
import jax
import jax.numpy as jnp
from jax import lax
from jax.experimental import pallas as pl
from jax.experimental.pallas import tpu as pltpu

F32 = jnp.float32
BF16 = jnp.bfloat16
HI = lax.Precision.HIGHEST

EPS = 1e-6
NEG_INF = -1e30
D_MODEL = 1024
WINDOW = 128
BLOCK = 128
H_A, KV_A, G_A, HD_A = 8, 2, 4, 64
N_BUCKETS, MAX_DIST = 32, 128
H_B, DK_B, DV_B = 4, 128, 128
CONV_W = 4
GDN_CHUNK = 64
D_INNER = 2048
P_C, H_C, N_C, G_C = 64, 32, 128, 4
HPG = H_C // G_C
SSD_CHUNK = 128
A_Q = H_A * HD_A
B_QKV = 3 * H_B * DK_B
B_Z = H_B * DV_B
SSD_CONV_CH = D_INNER + 2 * G_C * N_C
LANES = 128
SUBLANES = 8
VMEM_LIMIT_BYTES = 56 * 1024 * 1024


def _cp(*sem):
    return pltpu.CompilerParams(dimension_semantics=sem, vmem_limit_bytes=VMEM_LIMIT_BYTES)


def _whole(shape):
    nd = len(shape)
    return pl.BlockSpec(shape, lambda *_: (0,) * nd, pipeline_mode=pl.Buffered(1))


def _rms(x, g):
    return x * lax.rsqrt(jnp.mean(x * x, axis=-1, keepdims=True) + EPS) * g


def _silu(x):
    return x * jax.nn.sigmoid(x)


def _softplus(x):
    return jnp.maximum(x, 0.0) + jnp.log1p(jnp.exp(-jnp.abs(x)))


def _dot(a, b):
    return jnp.dot(a, b, preferred_element_type=F32)


def _dot_nt(a, b):
    return lax.dot_general(a, b, (((1,), (1,)), ((), ())), preferred_element_type=F32)


def _dot_tn(a, b):
    return lax.dot_general(a, b, (((0,), (0,)), ((), ())), preferred_element_type=F32)


def _dot_hi(a, b):
    return jnp.dot(a, b, precision=HI, preferred_element_type=F32)


def _expand(x, e3):
    hi = x.astype(BF16)
    r = x - hi.astype(F32)
    mid = r.astype(BF16)
    lo = (r - mid.astype(F32)).astype(BF16)
    return _dot(jnp.concatenate([hi, mid, lo], axis=1), e3)


def _tri(n, kind):
    r = lax.broadcasted_iota(jnp.int32, (n, n), 0)
    c = lax.broadcasted_iota(jnp.int32, (n, n), 1)
    return {"lower": r >= c, "strict_lower": r > c, "upper": r <= c}[kind]


def _ffn_body(*refs, n_mix, final):
    refs = list(refs)
    x = refs.pop(0)[...]
    for _ in range(n_mix):
        m_ref, wo_ref = refs.pop(0), refs.pop(0)
        x = x + _dot(m_ref[...].astype(BF16), wo_ref[...])
    g_ref, wg_ref, wu_ref, wd_ref = refs[:4]
    o_ref = refs[-1]
    hn = _rms(x, g_ref[...]).astype(BF16)
    act = (_silu(_dot(hn, wg_ref[...])) * _dot(hn, wu_ref[...])).astype(BF16)
    y = x + 0.5 * _dot(act, wd_ref[...])
    if final:
        y = _rms(y, refs[4][...])
    o_ref[...] = y


def _layer_slab(w_all, layer, rows=None, row_block=0):
    shape = (rows or w_all.shape[1], w_all.shape[2])
    return pl.BlockSpec((None,) + shape, lambda i: (layer, row_block, 0), pipeline_mode=pl.Buffered(1))


def _ffn(x, g, ffw, layer, mixes=(), g_final=None):
    t, d = x.shape
    tm = min(t, 512)
    final = g_final is not None
    row = pl.BlockSpec((tm, d), lambda i: (i, 0))
    in_specs, args = [row], [x]
    for m, w_all, e, row_block in mixes:
        in_specs += [pl.BlockSpec((tm, m.shape[1]), lambda i: (i, 0)), _layer_slab(w_all, e, m.shape[1], row_block)]
        args += [m, w_all]
    in_specs += [_whole((1, d))] + [_layer_slab(w, layer) for w in ffw]
    args += [g.reshape(1, d), *ffw]
    if final:
        in_specs.append(_whole((1, d)))
        args.append(g_final.reshape(1, d))
    return pl.pallas_call(
        functools.partial(_ffn_body, n_mix=len(mixes), final=final),
        grid=(t // tm,),
        in_specs=in_specs,
        out_specs=row,
        out_shape=jax.ShapeDtypeStruct((t, d), F32),
        compiler_params=_cp("parallel"),
        name="ffn",
    )(*args)


def _inproj_body(x_ref, g_ref, w_ref, ws_ref, *outs, splits):
    hn = _rms(x_ref[...], g_ref[...]).astype(BF16)
    off = 0
    for o_ref, n in zip(outs[:-2], splits):
        o_ref[...] = _dot(hn, w_ref[:, off:off + n])
        off += n
    small = _dot(hn, ws_ref[...])
    outs[-2][...] = small
    outs[-1][...] = small.T[:outs[-1].shape[0]]


def _inproj(x, g, w_all, e, ws, ns, splits):
    t, d = x.shape
    w_spec = _layer_slab(w_all, e)
    tm = min(t, 512)
    row = pl.BlockSpec((tm, d), lambda i: (i, 0))
    out_specs = [pl.BlockSpec((tm, n), lambda i: (i, 0)) for n in splits]
    out_specs += [pl.BlockSpec((tm, LANES), lambda i: (i, 0)), pl.BlockSpec((ns, tm), lambda i: (0, i))]
    out_shape = [jax.ShapeDtypeStruct((t, n), F32) for n in splits]
    out_shape += [jax.ShapeDtypeStruct((t, LANES), F32), jax.ShapeDtypeStruct((ns, t), F32)]
    return pl.pallas_call(
        functools.partial(_inproj_body, splits=splits),
        grid=(t // tm,),
        in_specs=[row, _whole((1, d)), w_spec, _whole(ws.shape)],
        out_specs=out_specs,
        out_shape=out_shape,
        compiler_params=_cp("parallel"),
        name="inproj",
    )(x, g.reshape(1, d), w_all, ws)


def _t5_bucket_np(dist):
    max_exact = N_BUCKETS // 2
    df = np.maximum(dist, max_exact).astype(np.float32)
    large = max_exact + (np.log(df / np.float32(max_exact)) / np.float32(math.log(MAX_DIST / max_exact))
                         * np.float32(N_BUCKETS - max_exact)).astype(np.int32)
    return np.where(dist < max_exact, dist, np.minimum(large, N_BUCKETS - 1)).astype(np.int32)


def _band_bucket_ids(n_q, n_k, offset):
    d = offset + np.arange(n_q)[:, None] - np.arange(n_k)[None, :]
    valid = (d >= 0) & (d <= WINDOW)
    return np.where(valid, _t5_bucket_np(np.clip(d, 0, WINDOW)), -1).astype(np.int32)


def _bias_from_buckets(bid, rb_ref, h):
    acc = jnp.full(bid.shape, NEG_INF, F32)
    for bk in range(N_BUCKETS):
        acc = jnp.where(bid == bk, rb_ref[bk, h], acc)
    return acc


def _swa_prompt_body(bid_ref, rb_ref, sk_ref, q_ref, kvp_ref, kvc_ref, o_ref, bias_ref):
    first_step = (pl.program_id(0) == 0) & (pl.program_id(1) == 0)

    @pl.when(first_step)
    def _():
        bid = bid_ref[...]
        col = lax.broadcasted_iota(jnp.int32, (BLOCK, 2 * BLOCK), 1)
        for h in range(H_A):
            bias = _bias_from_buckets(bid, rb_ref, h)
            bias_ref[h] = bias
            bias_ref[H_A + h] = jnp.where(col < BLOCK, NEG_INF, bias)

    first_block = jnp.where(pl.program_id(1) == 0, H_A, 0)
    kvp = kvp_ref[...]
    kvc = kvc_ref[...]
    scale = HD_A ** -0.5
    outs = []
    for kv in range(KV_A):
        ks = slice(kv * HD_A, (kv + 1) * HD_A)
        vs = slice(KV_A * HD_A + kv * HD_A, KV_A * HD_A + (kv + 1) * HD_A)
        k = jnp.concatenate([kvp[:, ks], kvc[:, ks]], axis=0).astype(BF16)
        v = jnp.concatenate([kvp[:, vs], kvc[:, vs]], axis=0).astype(BF16)
        for g in range(G_A):
            h = kv * G_A + g
            q = (q_ref[:, h * HD_A:(h + 1) * HD_A] * scale).astype(BF16)
            s = _dot_nt(q, k) + bias_ref[first_block + h]
            sk = sk_ref[h]
            m = jnp.maximum(jnp.max(s, axis=-1, keepdims=True), sk)
            p = jnp.exp(s - m)
            denom = jnp.sum(p, axis=-1, keepdims=True) + jnp.exp(sk - m)
            outs.append(_dot(p.astype(BF16), v) / denom)
    o_ref[...] = jnp.concatenate(outs, axis=1).astype(BF16)


def _swa_prompt(qa, kv, rel_bias, sinks, bn, l):
    nb = l // BLOCK
    t = bn * l
    bid = jnp.asarray(_band_bucket_ids(BLOCK, 2 * BLOCK, BLOCK))
    kv_blk = (BLOCK, 2 * KV_A * HD_A)
    return pl.pallas_call(
        _swa_prompt_body,
        grid=(bn, nb),
        in_specs=[
            _whole((BLOCK, 2 * BLOCK)),
            pl.BlockSpec(memory_space=pltpu.SMEM),
            pl.BlockSpec(memory_space=pltpu.SMEM),
            pl.BlockSpec((BLOCK, A_Q), lambda b, i: (b * nb + i, 0)),
            pl.BlockSpec(kv_blk, lambda b, i: (jnp.maximum(b * nb + i - 1, 0), 0)),
            pl.BlockSpec(kv_blk, lambda b, i: (b * nb + i, 0)),
        ],
        out_specs=pl.BlockSpec((BLOCK, A_Q), lambda b, i: (b * nb + i, 0)),
        out_shape=jax.ShapeDtypeStruct((t, A_Q), BF16),
        scratch_shapes=[pltpu.VMEM((2 * H_A, BLOCK, 2 * BLOCK), F32)],
        compiler_params=_cp("arbitrary", "arbitrary"),
        name="swa_prompt",
    )(bid, rel_bias, sinks, qa, kv, kv)


def _swa_decode_body(bid_ref, rb_ref, sk_ref, q_ref, kvn_ref, ckt_ref, cvt_ref, o_ref):
    bid = bid_ref[...]
    row = lax.broadcasted_iota(jnp.int32, (H_A, 1), 0)
    bias = jnp.zeros((H_A, bid.shape[1]), F32)
    sk = jnp.zeros((H_A, 1), F32)
    for h in range(H_A):
        bias = jnp.where(row == h, _bias_from_buckets(bid, rb_ref, h), bias)
        sk = jnp.where(row == h, sk_ref[h], sk)
    bias_c = bias[:, :WINDOW]
    bias_n = bias[:, WINDOW:WINDOW + 1]
    scale = HD_A ** -0.5
    q = q_ref[...]
    kvn = kvn_ref[...]
    width = KV_A * HD_A
    k_new = kvn[:, None, :width]
    v_new = kvn[:, None, width:]
    s = lax.dot_general(q.astype(BF16), ckt_ref[...].astype(BF16), (((2,), (1,)), ((0,), (0,))),
                        preferred_element_type=F32) * scale + bias_c[None]
    s_n = jnp.sum(q * k_new, axis=-1, keepdims=True) * scale + bias_n[None]
    m = jnp.maximum(jnp.maximum(jnp.max(s, axis=-1, keepdims=True), s_n), sk[None])
    p = jnp.exp(s - m)
    p_n = jnp.exp(s_n - m)
    denom = jnp.sum(p, axis=-1, keepdims=True) + p_n + jnp.exp(sk[None] - m)
    o = lax.dot_general((p / denom).astype(BF16), cvt_ref[...].astype(BF16), (((2,), (2,)), ((0,), (0,))),
                        preferred_element_type=F32)
    o_ref[...] = o + (p_n / denom) * v_new


def _swa_decode(q_pad, kv_new, cache_kt, cache_vt, e, rel_bias, sinks):
    bn = q_pad.shape[0]
    bs = 32
    width = KV_A * HD_A
    ids = _band_bucket_ids(1, WINDOW + 1, WINDOW)
    bid = np.full((1, WINDOW + LANES), -1, np.int32)
    bid[:, :WINDOW + 1] = ids
    return pl.pallas_call(
        _swa_decode_body,
        grid=(bn // bs,),
        in_specs=[
            _whole((1, WINDOW + LANES)),
            pl.BlockSpec(memory_space=pltpu.SMEM),
            pl.BlockSpec(memory_space=pltpu.SMEM),
            pl.BlockSpec((bs, H_A, width), lambda i: (i, 0, 0)),
            pl.BlockSpec((bs, 2 * width), lambda i: (i, 0)),
            pl.BlockSpec((None, bs, width, WINDOW), lambda i: (e, i, 0, 0)),
            pl.BlockSpec((None, bs, width, WINDOW), lambda i: (e, i, 0, 0)),
        ],
        out_specs=pl.BlockSpec((bs, H_A, width), lambda i: (i, 0, 0)),
        out_shape=jax.ShapeDtypeStruct((bn, H_A, width), F32),
        compiler_params=_cp("parallel"),
        name="swa_decode",
    )(jnp.asarray(bid), rel_bias, sinks, q_pad, kv_new, cache_kt, cache_vt)


def _cache_shift_body(c_ref, nt_ref, o_ref):
    o_ref[...] = pltpu.roll(c_ref[...], WINDOW - 1, axis=2)
    nt = nt_ref[...]
    for b in range(c_ref.shape[0]):
        o_ref[b, :, WINDOW - 1:WINDOW] = nt[:, b:b + 1]


def _cache_shift(cache_t, new_t):
    n, bn, width, win = cache_t.shape
    bs = new_t.shape[3]
    blk = pl.BlockSpec((None, bs, width, win), lambda e, i: (e, i, 0, 0))
    return pl.pallas_call(
        _cache_shift_body,
        grid=(n, bn // bs),
        in_specs=[blk, pl.BlockSpec((None, None, width, bs), lambda e, i: (e, i, 0, 0))],
        out_specs=blk,
        out_shape=jax.ShapeDtypeStruct(cache_t.shape, F32),
        compiler_params=_cp("parallel", "parallel"),
        name="cache_shift",
    )(cache_t, new_t)


def _conv_rows(x_ref, xbuf_ref, cw_ref, rows):
    xbuf_ref[SUBLANES:SUBLANES + rows, :] = x_ref[...]
    x = xbuf_ref[...]
    out = x * cw_ref[0:1, :]
    for i in range(1, CONV_W):
        out = pltpu.roll(out, 1, axis=0) + x * cw_ref[i:i + 1, :]
    return out[SUBLANES:]


def _conv_chunk(x_ref, xbuf_ref, cw_ref, rows):
    out = _conv_rows(x_ref, xbuf_ref, cw_ref, rows)
    xbuf_ref[0:SUBLANES, :] = xbuf_ref[rows:rows + SUBLANES, :]
    return out


def _l2norm(x):
    return x * lax.rsqrt(jnp.sum(x * x, axis=-1, keepdims=True) + EPS)


def _chunk_masks(n, cc):
    r = lax.broadcasted_iota(jnp.int32, (n, n), 0)
    c = lax.broadcasted_iota(jnp.int32, (n, n), 1)
    shift = int(math.log2(cc))
    same = lax.shift_right_logical(r, shift) == lax.shift_right_logical(c, shift)
    return same, same & (r >= c), same & (r > c), same & (r <= c)


def _dot_bf16x3(a, b):
    a_hi = a.astype(BF16)
    a_lo = (a - a_hi.astype(F32)).astype(BF16)
    b_hi = b.astype(BF16)
    b_lo = (b - b_hi.astype(F32)).astype(BF16)
    return _dot(a_hi, b_hi) + _dot(a_hi, b_lo) + _dot(a_lo, b_hi)


def _gdn_prep_body(qkv_ref, halo_ref, sm_ref, smt_ref, cw_ref, acol_ref, dtcol_ref,
                   u_ref, w_ref, qg_ref, kd_ref, aqk_ref, gl_ref, xbuf_ref):
    rows = qkv_ref.shape[0]
    cc = GDN_CHUNK
    xbuf_ref[0:SUBLANES, :] = jnp.where(pl.program_id(1) == 0, 0.0, halo_ref[...])
    conv = _silu(_conv_rows(qkv_ref, xbuf_ref, cw_ref, rows))
    nq = H_B * DK_B
    beta_all = jax.nn.sigmoid(sm_ref[...])
    gt_all = -jnp.exp(acol_ref[...]) * _softplus(smt_ref[...] + dtcol_ref[...])
    pad = jnp.zeros((LANES - gt_all.shape[0], rows), F32)
    g_all = jnp.concatenate([gt_all, pad], axis=0).T
    same, lower, strict, upper = _chunk_masks(rows, cc)
    gc = _dot_hi(lower.astype(F32), g_all)
    gct = _dot_hi(gt_all, upper.astype(F32))
    gsum = _dot_hi(same.astype(F32), g_all)
    neg_outside = jnp.where(lower, 0.0, -jnp.inf)
    neg_strict = jnp.where(strict, -1.0, 0.0)
    heads = range(H_B)
    qh, kh16, kb, decay, egc = [], [], [], [], []
    for h in heads:
        q = _l2norm(conv[:, h * DK_B:(h + 1) * DK_B]) * (DK_B ** -0.5)
        k = _l2norm(conv[:, nq + h * DK_B:nq + (h + 1) * DK_B])
        v = conv[:, 2 * nq + h * DV_B:2 * nq + (h + 1) * DV_B]
        beta = beta_all[:, h:h + 1]
        gcol = gc[:, H_B + h:H_B + h + 1]
        grow = gct[H_B + h:H_B + h + 1, :]
        decay.append(jnp.exp(gcol - grow + neg_outside))
        eg = jnp.exp(gcol)
        qg_ref[:, h * DK_B:(h + 1) * DK_B] = (q * eg).astype(BF16)
        kd_ref[:, h * DK_B:(h + 1) * DK_B] = (k * jnp.exp(gsum[:, H_B + h:H_B + h + 1] - gcol)).astype(BF16)
        qh.append(q.astype(BF16))
        kh16.append(k.astype(BF16))
        kb.append(k * beta)
        egc.append((v * beta, eg))
    a1 = [_dot_nt(kb[h].astype(BF16), kh16[h]) * decay[h] * neg_strict for h in heads]
    for h in heads:
        aqk = _dot_nt(qh[h], kh16[h]) * decay[h]
        blocks = [aqk[i * cc:(i + 1) * cc, i * cc:(i + 1) * cc] for i in range(rows // cc)]
        aqk_ref[:, h * cc:(h + 1) * cc] = jnp.concatenate(blocks, axis=0).astype(BF16)
    sq = lambda xs: [_dot(x.astype(BF16), x.astype(BF16)) for x in xs]
    pair = lambda xs, ys: [x + y + _dot(x.astype(BF16), y.astype(BF16)) for x, y in zip(xs, ys)]
    a2 = sq(a1)
    a4 = sq(a2)
    x1 = pair(a1, a2)
    a8 = sq(a4)
    a16 = sq(a8)
    x2 = pair(a4, a8)
    a32 = sq(a16)
    x12 = pair(x1, x2)
    x3 = pair(a16, a32)
    r = pair(x12, x3)
    for h in heads:
        vb, eg = egc[h]
        rhs = jnp.concatenate([vb, kb[h] * eg], axis=1)
        sol = rhs + _dot_bf16x3(r[h], rhs)
        u_ref[:, h * DV_B:(h + 1) * DV_B] = sol[:, :DV_B]
        w_ref[:, h * DK_B:(h + 1) * DK_B] = sol[:, DV_B:].astype(BF16)
    for i in range(rows // cc):
        gl = [jnp.broadcast_to(jnp.exp(gsum[i * cc:i * cc + 1, H_B + h:H_B + h + 1]), (SUBLANES, DV_B))
              for h in heads]
        gl_ref[0, i] = jnp.concatenate(gl, axis=1)


def _gdn_scan_body(u_ref, w_ref, qg_ref, kd_ref, aqk_ref, gl_ref, z_ref, gn_ref, o_ref, s_out_ref, s_ref):
    c = pl.program_id(0)
    bn = u_ref.shape[0]
    cc = GDN_CHUNK

    @pl.when(c == 0)
    def _():
        s_ref[...] = jnp.zeros_like(s_ref)

    gn = gn_ref[...]
    group = 2
    for b0 in range(0, bn, group):
        combos = [(b, h) for b in range(b0, b0 + group) for h in range(H_B)]
        res = {}
        for b, h in combos:
            hs = slice(h * DK_B, (h + 1) * DK_B)
            wq = jnp.concatenate([w_ref[b, :, hs], qg_ref[b, :, hs]], axis=0)
            res[b, h] = _dot(wq, s_ref[b, h].astype(BF16))
        v16 = {}
        for b, h in combos:
            hs = slice(h * DV_B, (h + 1) * DV_B)
            v16[b, h] = (u_ref[b, :, hs] - res[b, h][:cc]).astype(BF16)
        for b, h in combos:
            hs = slice(h * DV_B, (h + 1) * DV_B)
            o = res[b, h][cc:] + _dot(aqk_ref[b, :, h * cc:(h + 1) * cc], v16[b, h])
            upd = _dot_tn(kd_ref[b, :, hs], v16[b, h])
            s_ref[b, h] = s_ref[b, h] * gl_ref[b, 0, 0:1, hs] + upd
            o_ref[b, :, hs] = (_rms(o, gn) * _silu(z_ref[b, :, hs])).astype(BF16)

    @pl.when(c == pl.num_programs(0) - 1)
    def _():
        s_out_ref[...] = s_ref[...]


def _gdn_prompt(qkvb, z, small, small_t, conv_w, a_log, dt_bias, gnorm, bn, l):
    rows = 4 * GDN_CHUNK
    cc = GDN_CHUNK
    nb = l // rows
    nc = l // cc
    t = bn * l
    ns = small_t.shape[0]
    nq = H_B * DK_B
    pad_col = lambda v: jnp.zeros((ns, 1), F32).at[H_B:2 * H_B, 0].set(v)
    tok = lambda n: pl.BlockSpec((rows, n), lambda b, i: (b * nb + i, 0))
    halo = pl.BlockSpec((SUBLANES, B_QKV), lambda b, i: (jnp.maximum((b * nb + i) * (rows // SUBLANES) - 1, 0), 0))
    u, w, qg, kd, aqk, gl = pl.pallas_call(
        _gdn_prep_body,
        grid=(bn, nb),
        in_specs=[
            tok(B_QKV), halo, tok(LANES),
            pl.BlockSpec((ns, rows), lambda b, i: (0, b * nb + i)),
            _whole((CONV_W, B_QKV)), _whole((ns, 1)), _whole((ns, 1)),
        ],
        out_specs=[tok(nq), tok(nq), tok(nq), tok(nq), tok(H_B * cc),
                   pl.BlockSpec((1, rows // cc, SUBLANES, nq), lambda b, i: (b, i, 0, 0))],
        out_shape=[jax.ShapeDtypeStruct((t, nq), F32)] + [jax.ShapeDtypeStruct((t, nq), BF16)] * 3
        + [jax.ShapeDtypeStruct((t, H_B * cc), BF16), jax.ShapeDtypeStruct((bn, nc, SUBLANES, nq), F32)],
        scratch_shapes=[pltpu.VMEM((rows + SUBLANES, B_QKV), F32)],
        compiler_params=_cp("parallel", "parallel"),
        name="gdn_prep",
    )(qkvb, qkvb, small, small_t, conv_w, pad_col(a_log), pad_col(dt_bias))
    seq = lambda n: pl.BlockSpec((bn, cc, n), lambda c: (0, c, 0))
    r3 = lambda a: a.reshape(bn, l, a.shape[1])
    o, s_new = pl.pallas_call(
        _gdn_scan_body,
        grid=(nc,),
        in_specs=[seq(nq), seq(nq), seq(nq), seq(nq), seq(H_B * cc),
                  pl.BlockSpec((bn, 1, SUBLANES, nq), lambda c: (0, c, 0, 0)), seq(B_Z), _whole((1, DV_B))],
        out_specs=[seq(B_Z), pl.BlockSpec((bn, H_B, DK_B, DV_B), lambda c: (0, 0, 0, 0))],
        out_shape=[jax.ShapeDtypeStruct((bn, l, B_Z), BF16), jax.ShapeDtypeStruct((bn, H_B, DK_B, DV_B), F32)],
        scratch_shapes=[pltpu.VMEM((bn, H_B, DK_B, DV_B), F32)],
        compiler_params=_cp("arbitrary"),
        name="gdn_scan",
    )(r3(u), r3(w), r3(qg), r3(kd), r3(aqk), gl, r3(z), gnorm.reshape(1, DV_B))
    return o.reshape(t, B_Z), s_new


def _conv_step(x, hist_ref, cw_ref, new_ref):
    out = hist_ref[0] * cw_ref[0:1, :]
    for i in range(1, CONV_W - 1):
        out = out + hist_ref[i] * cw_ref[i:i + 1, :]
    out = out + x * cw_ref[CONV_W - 1:CONV_W, :]
    for i in range(CONV_W - 2):
        new_ref[i] = hist_ref[i + 1]
    new_ref[CONV_W - 2] = x
    return out


def _gdn_decode_pre_body(qkv_ref, hist_ref, sm_ref, cw_ref, arow_ref, dtrow_ref,
                         new_ref, w_ref, qg_ref, k_ref, u_ref, qk_ref, gl_ref):
    conv = _silu(_conv_step(qkv_ref[...], hist_ref, cw_ref, new_ref))
    nq = H_B * DK_B
    sm = sm_ref[...]
    beta_all = jax.nn.sigmoid(sm)
    g_all = -jnp.exp(arow_ref[...]) * _softplus(sm + dtrow_ref[...])
    for h in range(H_B):
        hs = slice(h * DK_B, (h + 1) * DK_B)
        qh = _l2norm(conv[:, hs]) * (DK_B ** -0.5)
        kh = _l2norm(conv[:, nq + h * DK_B:nq + (h + 1) * DK_B])
        vh = conv[:, 2 * nq + h * DV_B:2 * nq + (h + 1) * DV_B]
        beta = beta_all[:, h:h + 1]
        eg = jnp.exp(g_all[:, H_B + h:H_B + h + 1])
        w_ref[:, hs] = kh * beta * eg
        qg_ref[:, hs] = qh * eg
        k_ref[:, hs] = kh
        u_ref[:, hs] = vh * beta
        qk_ref[:, hs] = jnp.broadcast_to(jnp.sum(qh * kh, axis=-1, keepdims=True), qh.shape)
        gl_ref[:, hs] = jnp.broadcast_to(eg, qh.shape)


def _gdn_decode_state_body(s_ref, w_ref, qg_ref, k_ref, u_ref, qk_ref, gl_ref, z_ref, gn_ref, done_ref,
                           s_out_ref, o_ref):
    del done_ref
    nb = s_ref.shape[0]
    row = lax.broadcasted_iota(jnp.int32, (nb, 1), 0)
    for h in range(H_B):
        hs = slice(h * DK_B, (h + 1) * DK_B)
        wq = jnp.concatenate([w_ref[:, hs], qg_ref[:, hs]], axis=0).astype(BF16)
        ws = jnp.zeros((nb, DV_B), F32)
        qs = jnp.zeros((nb, DV_B), F32)
        for bb in range(nb):
            res = _dot(wq, s_ref[bb, h].astype(BF16))
            ws = jnp.where(row == bb, res[:nb], ws)
            qs = jnp.where(row == bb, res[nb:], qs)
        v_new = u_ref[:, hs] - ws
        o = qs + qk_ref[:, hs] * v_new
        o_ref[:, hs] = _rms(o, gn_ref[...]) * _silu(z_ref[:, hs])
        k = k_ref[:, hs]
        v16 = v_new.astype(BF16)
        gl = gl_ref[:, hs]
        for bb in range(nb):
            k_one = jnp.where(row == bb, k, 0.0).astype(BF16)
            s_out_ref[bb, h] = s_ref[bb, h] * gl[bb:bb + 1, :] + _dot_tn(k_one, v16)


def _gdn_decode(qkvb, hist, z, small, conv_w, a_log, dt_bias, gnorm, s_all, s_done, e):
    bn = qkvb.shape[0]
    pad_row = lambda v: jnp.zeros((1, LANES), F32).at[0, H_B:2 * H_B].set(v)
    wide = jax.ShapeDtypeStruct((bn, H_B * DK_B), F32)
    new_conv, w, qg, k, u, qk, gl = pl.pallas_call(
        _gdn_decode_pre_body,
        out_shape=[jax.ShapeDtypeStruct(hist.shape, F32)] + [wide] * 6,
        compiler_params=pltpu.CompilerParams(vmem_limit_bytes=VMEM_LIMIT_BYTES),
        name="gdn_decode_pre",
    )(qkvb, hist, small, conv_w, pad_row(a_log), pad_row(dt_bias))
    nb = SUBLANES
    vec = pl.BlockSpec((nb, H_B * DK_B), lambda i: (i, 0))
    st = pl.BlockSpec((None, nb, H_B, DK_B, DV_B), lambda i: (e, i, 0, 0, 0))
    s_done, o = pl.pallas_call(
        _gdn_decode_state_body,
        grid=(bn // nb,),
        in_specs=[st, vec, vec, vec, vec, vec, vec, vec, _whole((1, DV_B)), pl.BlockSpec(memory_space=pl.ANY)],
        out_specs=[st, vec],
        out_shape=[jax.ShapeDtypeStruct(s_all.shape, F32), wide],
        input_output_aliases={9: 0},
        compiler_params=_cp("parallel"),
        name="gdn_decode_state",
    )(s_all, w, qg, k, u, qk, gl, z, gnorm.reshape(1, DV_B), s_done)
    return o, new_conv, s_done


def _head_expander(width):
    e = np.zeros((LANES, H_C * width), np.float32)
    for h in range(H_C):
        e[h, h * width:(h + 1) * width] = 1.0
    return jnp.asarray(np.tile(e, (3, 1)), BF16)


def _group_rms(y, g):
    gs = D_INNER // G_C
    parts = [_rms(y[:, i * gs:(i + 1) * gs], g[:, i * gs:(i + 1) * gs]) for i in range(G_C)]
    return jnp.concatenate(parts, axis=1)


def _ssd_prompt_body(z_ref, xbc_ref, sm_ref, smt_ref, cw_ref, cb_ref, dtrow_ref, arow_ref, dtcol_ref,
                     acol_ref, dx_ref, gn_ref, e_ref, o_ref, s_out_ref, xbuf_ref, s_ref):
    c = pl.program_id(1)
    rows = xbc_ref.shape[0]

    @pl.when(c == 0)
    def _():
        xbuf_ref[0:SUBLANES, :] = jnp.zeros((SUBLANES, xbuf_ref.shape[1]), F32)
        s_ref[...] = jnp.zeros_like(s_ref)

    xbc = _silu(_conv_chunk(xbc_ref, xbuf_ref, cw_ref, rows) + cb_ref[...])
    xs = xbc[:, :D_INNER]
    bm = xbc[:, D_INNER:D_INNER + G_C * N_C]
    cm = xbc[:, D_INNER + G_C * N_C:]
    e = e_ref[...]
    lower = _tri(rows, "lower")
    dt = _softplus(sm_ref[...] + dtrow_ref[...])
    acum = _dot_hi(lower.astype(F32), dt * -jnp.exp(arow_ref[...]))
    a_t = _softplus(smt_ref[...] + dtcol_ref[...]) * -jnp.exp(acol_ref[...])
    acum_t = _dot_hi(a_t, _tri(rows, "upper").astype(F32))
    xdt = xs * _expand(dt, e)
    xdte = (xdt * _expand(jnp.exp(acum[rows - 1:rows, :] - acum), e)).astype(BF16)
    scale_y = _expand(jnp.exp(acum), e)
    chunk_decay = scale_y[rows - 1:rows, :]
    xdt16 = xdt.astype(BF16)
    neg_upper = jnp.where(lower, 0.0, -jnp.inf)
    gw = HPG * P_C
    ys = []
    for g in range(G_C):
        bg = bm[:, g * N_C:(g + 1) * N_C]
        cg16 = cm[:, g * N_C:(g + 1) * N_C].astype(BF16)
        cb = _dot_nt(cg16, bg.astype(BF16))
        yg = []
        for hh in range(HPG):
            h = g * HPG + hh
            lmat = jnp.exp(acum[:, h:h + 1] - acum_t[h:h + 1, :] + neg_upper)
            yg.append(_dot((cb * lmat).astype(BF16), xdt16[:, h * P_C:(h + 1) * P_C]))
        gs = slice(g * gw, (g + 1) * gw)
        sg = s_ref[:, gs]
        y_off = _dot(cg16, sg.astype(BF16)) * scale_y[:, gs]
        ys.append(jnp.concatenate(yg, axis=1) + y_off)
        s_ref[:, gs] = sg * chunk_decay[:, gs] + _dot(bg.T.astype(BF16), xdte[:, gs])
    y = jnp.concatenate(ys, axis=1) + dx_ref[...] * xs
    y = y * _silu(z_ref[...])
    o_ref[...] = _group_rms(y, gn_ref[...]).astype(BF16)

    @pl.when(c == pl.num_programs(1) - 1)
    def _():
        s_out_ref[0] = s_ref[...].T.reshape(H_C, P_C, N_C)


def _ssd_small_params(dt_bias, a_log, ns):
    row = lambda v: jnp.zeros((1, LANES), F32).at[0, :H_C].set(v)
    col = lambda v: jnp.zeros((ns, 1), F32).at[:H_C, 0].set(v)
    return row(dt_bias), row(a_log), col(dt_bias), col(a_log)


def _ssd_prompt(z, xbc, small, small_t, conv_w, conv_b, dt_bias, a_log, d_skip, gnorm, bn, l):
    rows = SSD_CHUNK
    nc = l // rows
    t = bn * l
    ns = small_t.shape[0]
    dtrow, arow, dtcol, acol = _ssd_small_params(dt_bias, a_log, ns)
    tok = lambda n: pl.BlockSpec((rows, n), lambda b, c: (b * nc + c, 0))
    return pl.pallas_call(
        _ssd_prompt_body,
        grid=(bn, nc),
        in_specs=[
            tok(D_INNER), tok(SSD_CONV_CH), tok(LANES),
            pl.BlockSpec((ns, rows), lambda b, c: (0, b * nc + c)),
            _whole((CONV_W, SSD_CONV_CH)), _whole((1, SSD_CONV_CH)),
            _whole((1, LANES)), _whole((1, LANES)), _whole((ns, 1)), _whole((ns, 1)),
            _whole((1, D_INNER)), _whole((1, D_INNER)), _whole((3 * LANES, D_INNER)),
        ],
        out_specs=[tok(D_INNER), pl.BlockSpec((1, H_C, P_C, N_C), lambda b, c: (b, 0, 0, 0))],
        out_shape=[jax.ShapeDtypeStruct((t, D_INNER), BF16), jax.ShapeDtypeStruct((bn, H_C, P_C, N_C), F32)],
        scratch_shapes=[pltpu.VMEM((rows + SUBLANES, SSD_CONV_CH), F32), pltpu.VMEM((N_C, D_INNER), F32)],
        compiler_params=_cp("arbitrary", "arbitrary"),
        name="ssd_prompt",
    )(z, xbc, small, small_t, conv_w, conv_b.reshape(1, SSD_CONV_CH), dtrow, arow, dtcol, acol,
      jnp.repeat(d_skip, P_C).reshape(1, D_INNER), gnorm.reshape(1, D_INNER), _head_expander(P_C))


def _ssd_decode_pre_body(xbc_ref, hist_ref, sm_ref, cw_ref, cb_ref, dtrow_ref, arow_ref, e_ref, en_ref,
                         new_ref, xs_ref, xdt_ref, b_ref, c_ref, dax_ref, dan_ref):
    xbc = _silu(_conv_step(xbc_ref[...], hist_ref, cw_ref, new_ref) + cb_ref[...])
    xs = xbc[:, :D_INNER]
    dt = _softplus(sm_ref[...] + dtrow_ref[...])
    a = dt * -jnp.exp(arow_ref[...])
    xs_ref[...] = xs
    xdt_ref[...] = xs * _expand(dt, e_ref[...])
    b_ref[...] = xbc[:, D_INNER:D_INNER + G_C * N_C]
    c_ref[...] = xbc[:, D_INNER + G_C * N_C:]
    dax_ref[...] = jnp.exp(_expand(a, e_ref[...]))
    dan_ref[...] = jnp.exp(_expand(a, en_ref[...]))


def _ssd_decode_state_body(s_ref, xdt_ref, b_ref, c_ref, dan_ref, done_ref, s_out_ref, yoff_ref):
    del done_ref
    gw = HPG * P_C
    sub = SUBLANES
    row = lax.broadcasted_iota(jnp.int32, (sub, 1), 0)
    for r0 in range(0, s_ref.shape[0], sub):
        rows = slice(r0, r0 + sub)
        xdt = xdt_ref[rows, :]
        b16 = b_ref[rows, :].astype(BF16)
        c16 = c_ref[rows, :].astype(BF16)
        dan = dan_ref[rows, :]
        yoff = jnp.zeros((sub, gw), F32)
        for bb in range(sub):
            s = s_ref[r0 + bb].reshape(gw, N_C)
            res = _dot_nt(c16, s.astype(BF16))
            yoff = jnp.where(row == bb, res, yoff)
            x_one = jnp.where(row == bb, xdt, 0.0).astype(BF16)
            upd = _dot_tn(x_one, b16)
            for hh in range(HPG):
                rs = slice(hh * P_C, (hh + 1) * P_C)
                s_out_ref[r0 + bb, hh] = s[rs] * dan[bb:bb + 1, hh * N_C:(hh + 1) * N_C] + upd[rs]
        yoff_ref[rows, :] = yoff


def _ssd_decode_post_body(yoff_ref, dax_ref, xdt_ref, xs_ref, b_ref, c_ref, z_ref, dx_ref, gn_ref, o_ref):
    gw = HPG * P_C
    bc = b_ref[...] * c_ref[...]
    cbx = [jnp.broadcast_to(jnp.sum(bc[:, g * N_C:(g + 1) * N_C], axis=-1, keepdims=True), (bc.shape[0], gw))
           for g in range(G_C)]
    y = yoff_ref[...] * dax_ref[...] + jnp.concatenate(cbx, axis=1) * xdt_ref[...]
    y = y + dx_ref[...] * xs_ref[...]
    y = y * _silu(z_ref[...])
    o_ref[...] = _group_rms(y, gn_ref[...])


def _ssd_decode(z, xbc, hist, small, conv_w, conv_b, dt_bias, a_log, d_skip, gnorm, s_all, s_done, e):
    bn = xbc.shape[0]
    dtrow, arow, _, _ = _ssd_small_params(dt_bias, a_log, H_C)
    wide = jax.ShapeDtypeStruct((bn, D_INNER), F32)
    grp = jax.ShapeDtypeStruct((bn, G_C * N_C), F32)
    plain = pltpu.CompilerParams(vmem_limit_bytes=VMEM_LIMIT_BYTES)
    new_conv, xs, xdt, bm, cm, dax, dan = pl.pallas_call(
        _ssd_decode_pre_body,
        out_shape=[jax.ShapeDtypeStruct(hist.shape, F32), wide, wide, grp, grp, wide,
                   jax.ShapeDtypeStruct((bn, H_C * N_C), F32)],
        compiler_params=plain,
        name="ssd_decode_pre",
    )(xbc, hist, small, conv_w, conv_b.reshape(1, SSD_CONV_CH), dtrow, arow,
      _head_expander(P_C), _head_expander(N_C))
    nb = 2 * SUBLANES
    gw = HPG * P_C
    st = pl.BlockSpec((None, nb, HPG, P_C, N_C), lambda i, g: (e, i, g, 0, 0))
    s_done, yoff = pl.pallas_call(
        _ssd_decode_state_body,
        grid=(bn // nb, G_C),
        in_specs=[st, pl.BlockSpec((nb, gw), lambda i, g: (i, g)), pl.BlockSpec((nb, N_C), lambda i, g: (i, g)),
                  pl.BlockSpec((nb, N_C), lambda i, g: (i, g)), pl.BlockSpec((nb, HPG * N_C), lambda i, g: (i, g)),
                  pl.BlockSpec(memory_space=pl.ANY)],
        out_specs=[st, pl.BlockSpec((nb, gw), lambda i, g: (i, g))],
        out_shape=[jax.ShapeDtypeStruct(s_all.shape, F32), wide],
        input_output_aliases={5: 0},
        compiler_params=_cp("parallel", "parallel"),
        name="ssd_decode_state",
    )(s_all, xdt, bm, cm, dan, s_done)
    mix = pl.pallas_call(
        _ssd_decode_post_body,
        out_shape=wide,
        compiler_params=plain,
        name="ssd_decode_post",
    )(yoff, dax, xdt, xs, bm, cm, z, jnp.repeat(d_skip, P_C).reshape(1, D_INNER), gnorm.reshape(1, D_INNER))
    return mix, new_conv, s_done


def _narrow_weights(w_small):
    n = w_small.shape[1]
    return jnp.zeros((D_MODEL, LANES), BF16).at[:, :n].set(w_small.astype(BF16))


def _even_q_pad(qa):
    bn = qa.shape[0]
    q = qa.reshape(bn, KV_A, G_A, HD_A)
    out = jnp.zeros((bn, KV_A, G_A, KV_A, HD_A), F32)
    for kv in range(KV_A):
        out = out.at[:, kv, :, kv, :].set(q[:, kv])
    return out.reshape(bn, H_A, KV_A * HD_A)


def _even_o_unpad(o8):
    bn = o8.shape[0]
    o = o8.reshape(bn, KV_A, G_A, KV_A, HD_A)
    return jnp.concatenate([o[:, kv, :, kv, :].reshape(bn, G_A * HD_A) for kv in range(KV_A)], axis=1)


def _trunk(x3, states, wts):
    (rel_bias, norm_ff1, norm_mix, norm_ff2, norm_final,
     ff1, ff2, even_w_in, even_w_out, swa_sinks, gdn_conv_w, gdn_A_log, gdn_dt_bias, gdn_norm,
     ssd_w_in, ssd_w_out, ssd_conv_w, ssd_conv_b, ssd_dt_bias, ssd_A_log, ssd_D, ssd_norm) = wts
    bn, l, d = x3.shape
    t = bn * l
    x = x3.reshape(t, d)
    decode = states is not None
    ks, vs, gconv, gssm, sconv, sssm = [], [], [], [], [], []
    depth = norm_ff1.shape[0]
    width = KV_A * HD_A
    if decode:
        gssm_done, sssm_done = jnp.zeros_like(states[3]), jnp.zeros_like(states[5])
        n_even = states[0].shape[0]
        cache_t = lambda c: c.transpose(0, 1, 3, 4, 2).reshape(n_even, bn, width, WINDOW)
        ckt, cvt = cache_t(states[0]), cache_t(states[1])
        ghist, shist = states[2].transpose(0, 2, 1, 3), states[4].transpose(0, 2, 1, 3)
    for layer in range(depth):
        x = _ffn(x, norm_ff1[layer], ff1, layer)
        e = layer // 2
        if layer % 2 == 0:
            w_all, ws = even_w_in
            qa, kv, qkvb, z, small, small_t = _inproj(x, norm_mix[layer], w_all, e, ws[e], 2 * SUBLANES,
                                                      (A_Q, 2 * width, B_QKV, B_Z))
            if decode:
                o8 = _swa_decode(_even_q_pad(qa), kv, ckt, cvt, e, rel_bias, swa_sinks[e])
                o_a = _even_o_unpad(o8)
                o_b, new_conv, gssm_done = _gdn_decode(qkvb, ghist[e], z, small, gdn_conv_w[e], gdn_A_log[e],
                                                       gdn_dt_bias[e], gdn_norm[e], states[3], gssm_done, e)
                new_k, new_v = kv[:, :width], kv[:, width:]
                new_conv = new_conv.transpose(1, 0, 2)
            else:
                o_a = _swa_prompt(qa, kv, rel_bias, swa_sinks[e], bn, l)
                o_b, s_new = _gdn_prompt(qkvb, z, small, small_t, gdn_conv_w[e], gdn_A_log[e],
                                         gdn_dt_bias[e], gdn_norm[e], bn, l)
                kv3 = kv.reshape(bn, l, 2 * width)
                new_k = kv3[:, l - WINDOW:, :width]
                new_v = kv3[:, l - WINDOW:, width:]
                new_conv = qkvb.reshape(bn, l, B_QKV)[:, l - (CONV_W - 1):]
                gssm.append(s_new)
            ks.append(new_k)
            vs.append(new_v)
            gconv.append(new_conv)
            mixes = [(o_a, even_w_out, e, 0), (o_b, even_w_out, e, 1)]
        else:
            w_all, ws = ssd_w_in
            z, xbc, small, small_t = _inproj(x, norm_mix[layer], w_all, e, ws[e], H_C, (D_INNER, SSD_CONV_CH))
            if decode:
                mix, new_conv, sssm_done = _ssd_decode(z, xbc, shist[e], small, ssd_conv_w[e], ssd_conv_b[e],
                                                       ssd_dt_bias[e], ssd_A_log[e], ssd_D[e], ssd_norm[e],
                                                       states[5], sssm_done, e)
                new_conv = new_conv.transpose(1, 0, 2)
            else:
                mix, s_new = _ssd_prompt(z, xbc, small, small_t, ssd_conv_w[e], ssd_conv_b[e], ssd_dt_bias[e],
                                         ssd_A_log[e], ssd_D[e], ssd_norm[e], bn, l)
                new_conv = xbc.reshape(bn, l, SSD_CONV_CH)[:, l - (CONV_W - 1):]
                sssm.append(s_new)
            sconv.append(new_conv)
            mixes = [(mix, ssd_w_out, e, 0)]
        x = _ffn(x, norm_ff2[layer], ff2, layer, mixes=mixes,
                 g_final=norm_final if layer == depth - 1 else None)
    if decode:
        bs = 32
        new_t = lambda rows: jnp.stack(rows).reshape(n_even, bn // bs, bs, width).transpose(0, 1, 3, 2)
        back = lambda c: c.reshape(n_even, bn, KV_A, HD_A, WINDOW).transpose(0, 1, 4, 2, 3)
        ks = back(_cache_shift(ckt, new_t(ks)))
        vs = back(_cache_shift(cvt, new_t(vs)))
        gssm, sssm = gssm_done, sssm_done
    else:
        kv5 = (len(ks), bn, WINDOW, KV_A, HD_A)
        ks, vs = jnp.stack(ks).reshape(kv5), jnp.stack(vs).reshape(kv5)
        gssm, sssm = jnp.stack(gssm), jnp.stack(sssm)
    return (x.reshape(bn, l, d), ks, vs, jnp.stack(gconv), gssm, jnp.stack(sconv), sssm)


def kernel(x_prompt, x_sample, cache_swa_k, cache_swa_v, state_gdn_conv, state_gdn_ssm, state_ssd_conv, state_ssd_ssm, rel_bias, norm_ff1, norm_mix, norm_ff2, norm_final, ff1_gate, ff1_up, ff1_down, ff2_gate, ff2_up, ff2_down, even_w_in, even_w_out, swa_sinks, gdn_conv_w, gdn_A_log, gdn_dt_bias, gdn_norm, ssd_w_in, ssd_w_out, ssd_conv_w, ssd_conv_b, ssd_dt_bias, ssd_A_log, ssd_D, ssd_norm):
    depth = norm_ff1.shape[0]
    ff1 = (ff1_gate.astype(BF16), ff1_up.astype(BF16), ff1_down.astype(BF16))
    ff2 = (ff2_gate.astype(BF16), ff2_up.astype(BF16), ff2_down.astype(BF16))
    n_even_main = A_Q + 2 * KV_A * HD_A + B_QKV + B_Z
    even_in = (even_w_in.astype(BF16),
               [_narrow_weights(even_w_in[e][:, n_even_main:]) for e in range(even_w_in.shape[0])])
    n_odd_main = D_INNER + SSD_CONV_CH
    odd_in = (ssd_w_in.astype(BF16),
              [_narrow_weights(ssd_w_in[e][:, n_odd_main:]) for e in range(ssd_w_in.shape[0])])
    wts = (rel_bias, norm_ff1, norm_mix, norm_ff2, norm_final, ff1, ff2,
           even_in, even_w_out.astype(BF16), swa_sinks, gdn_conv_w, gdn_A_log, gdn_dt_bias, gdn_norm,
           odd_in, ssd_w_out.astype(BF16), ssd_conv_w, ssd_conv_b, ssd_dt_bias, ssd_A_log, ssd_D, ssd_norm)
    y_p, p_k, p_v, p_gconv, p_gssm, p_sconv, p_sssm = _trunk(x_prompt, None, wts)
    states = (cache_swa_k, cache_swa_v, state_gdn_conv, state_gdn_ssm, state_ssd_conv, state_ssd_ssm)
    y_s, s_k, s_v, s_gconv, s_gssm, s_sconv, s_sssm = _trunk(x_sample, states, wts)
    return (y_p, y_s, p_k, p_v, p_gconv, p_gssm, p_sconv, p_sssm,
            s_k, s_v, s_gconv, s_gssm, s_sconv, s_sssm)
```

```python
import functools
import math

import numpy as np
import jax
import jax.numpy as jnp
from jax import lax
from jax.experimental import pallas as pl
from jax.experimental.pallas import tpu as pltpu

F32 = jnp.float32
BF16 = jnp.bfloat16
HI = lax.Precision.HIGHEST

EPS = 1e-6
NEG_INF = -1e30
LOG2E = math.log2(math.e)
D_MODEL = 1024
WINDOW = 128
BLOCK = 128
H_A, KV_A, G_A, HD_A = 8, 2, 4, 64
N_BUCKETS, MAX_DIST = 32, 128
H_B, DK_B, DV_B = 4, 128, 128
CONV_W = 4
GDN_CHUNK = 64
D_INNER = 2048
P_C, H_C, N_C, G_C = 64, 32, 128, 4
HPG = H_C // G_C
SSD_CHUNK = 128
A_Q = H_A * HD_A
B_QKV = 3 * H_B * DK_B
B_Z = H_B * DV_B
SSD_CONV_CH = D_INNER + 2 * G_C * N_C
LANES = 128
SUBLANES = 8
VMEM_LIMIT_BYTES = 56 * 1024 * 1024


def _cp(*sem):
    return pltpu.CompilerParams(dimension_semantics=sem, vmem_limit_bytes=VMEM_LIMIT_BYTES)


def _whole(shape):
    nd = len(shape)
    return pl.BlockSpec(shape, lambda *_: (0,) * nd, pipeline_mode=pl.Buffered(1))


def _rms(x, g):
    return x * lax.rsqrt(jnp.mean(x * x, axis=-1, keepdims=True) + EPS) * g


def _silu(x):
    return x * jax.nn.sigmoid(x)


def _softplus(x):
    return jnp.maximum(x, 0.0) + jnp.log1p(jnp.exp(-jnp.abs(x)))


def _dot(a, b):
    return jnp.dot(a, b, preferred_element_type=F32)


def _dot_nt(a, b):
    return lax.dot_general(a, b, (((1,), (1,)), ((), ())), preferred_element_type=F32)


def _dot_tn(a, b):
    return lax.dot_general(a, b, (((0,), (0,)), ((), ())), preferred_element_type=F32)


def _dot_hi(a, b):
    return jnp.dot(a, b, precision=HI, preferred_element_type=F32)


def _expand(x, e3):
    hi = x.astype(BF16)
    r = x - hi.astype(F32)
    mid = r.astype(BF16)
    lo = (r - mid.astype(F32)).astype(BF16)
    return _dot(jnp.concatenate([hi, mid, lo], axis=1), e3)


def _tri(n, kind):
    r = lax.broadcasted_iota(jnp.int32, (n, n), 0)
    c = lax.broadcasted_iota(jnp.int32, (n, n), 1)
    return {"lower": r >= c, "strict_lower": r > c, "upper": r <= c}[kind]


def _ffn_body(*refs, n_mix, final):
    refs = list(refs)
    x = refs.pop(0)[...]
    for _ in range(n_mix):
        m_ref, wo_ref = refs.pop(0), refs.pop(0)
        x = x + _dot(m_ref[...].astype(BF16), wo_ref[...])
    g_ref, wg_ref, wu_ref, wd_ref = refs[:4]
    o_ref = refs[-1]
    hn = _rms(x, g_ref[...]).astype(BF16)
    act = (_silu(_dot(hn, wg_ref[...])) * _dot(hn, wu_ref[...])).astype(BF16)
    y = x + 0.5 * _dot(act, wd_ref[...])
    if final:
        y = _rms(y, refs[4][...])
    o_ref[...] = y


def _layer_slab(w_all, layer, rows=None, row_block=0):
    shape = (rows or w_all.shape[1], w_all.shape[2])
    return pl.BlockSpec((None,) + shape, lambda i: (layer, row_block, 0), pipeline_mode=pl.Buffered(1))


def _ffn(x, g, ffw, layer, mixes=(), g_final=None):
    t, d = x.shape
    tm = min(t, 512)
    final = g_final is not None
    row = pl.BlockSpec((tm, d), lambda i: (i, 0))
    in_specs, args = [row], [x]
    for m, w_all, e, row_block in mixes:
        in_specs += [pl.BlockSpec((tm, m.shape[1]), lambda i: (i, 0)), _layer_slab(w_all, e, m.shape[1], row_block)]
        args += [m, w_all]
    in_specs += [_whole((1, d))] + [_layer_slab(w, layer) for w in ffw]
    args += [g.reshape(1, d), *ffw]
    if final:
        in_specs.append(_whole((1, d)))
        args.append(g_final.reshape(1, d))
    return pl.pallas_call(
        functools.partial(_ffn_body, n_mix=len(mixes), final=final),
        grid=(t // tm,),
        in_specs=in_specs,
        out_specs=row,
        out_shape=jax.ShapeDtypeStruct((t, d), F32),
        compiler_params=_cp("parallel"),
        name="ffn",
    )(*args)


def _inproj_body(x_ref, g_ref, w_ref, ws_ref, *outs, splits):
    hn = _rms(x_ref[...], g_ref[...]).astype(BF16)
    off = 0
    for o_ref, n in zip(outs[:-2], splits):
        o_ref[...] = _dot(hn, w_ref[:, off:off + n])
        off += n
    small = _dot(hn, ws_ref[...])
    outs[-2][...] = small
    outs[-1][...] = small.T[:outs[-1].shape[0]]


def _inproj(x, g, w_all, e, ws, ns, splits):
    t, d = x.shape
    w_spec = _layer_slab(w_all, e)
    tm = min(t, 512)
    row = pl.BlockSpec((tm, d), lambda i: (i, 0))
    out_specs = [pl.BlockSpec((tm, n), lambda i: (i, 0)) for n in splits]
    out_specs += [pl.BlockSpec((tm, LANES), lambda i: (i, 0)), pl.BlockSpec((ns, tm), lambda i: (0, i))]
    out_shape = [jax.ShapeDtypeStruct((t, n), F32) for n in splits]
    out_shape += [jax.ShapeDtypeStruct((t, LANES), F32), jax.ShapeDtypeStruct((ns, t), F32)]
    return pl.pallas_call(
        functools.partial(_inproj_body, splits=splits),
        grid=(t // tm,),
        in_specs=[row, _whole((1, d)), w_spec, _whole(ws.shape)],
        out_specs=out_specs,
        out_shape=out_shape,
        compiler_params=_cp("parallel"),
        name="inproj",
    )(x, g.reshape(1, d), w_all, ws)


def _t5_bucket_np(dist):
    max_exact = N_BUCKETS // 2
    df = np.maximum(dist, max_exact).astype(np.float32)
    large = max_exact + (np.log(df / np.float32(max_exact)) / np.float32(math.log(MAX_DIST / max_exact))
                         * np.float32(N_BUCKETS - max_exact)).astype(np.int32)
    return np.where(dist < max_exact, dist, np.minimum(large, N_BUCKETS - 1)).astype(np.int32)


def _band_bucket_ids(n_q, n_k, offset):
    d = offset + np.arange(n_q)[:, None] - np.arange(n_k)[None, :]
    valid = (d >= 0) & (d <= WINDOW)
    return np.where(valid, _t5_bucket_np(np.clip(d, 0, WINDOW)), -1).astype(np.int32)


def _bias_from_buckets(bid, rb_ref, h):
    acc = jnp.full(bid.shape, NEG_INF, F32)
    for bk in range(N_BUCKETS):
        acc = jnp.where(bid == bk, rb_ref[bk, h], acc)
    return acc


def _swa_prompt_body(bid_ref, rb_ref, sk_ref, q_ref, kvp_ref, kvc_ref, o_ref, bias_ref):
    first_step = (pl.program_id(0) == 0) & (pl.program_id(1) == 0)

    @pl.when(first_step)
    def _():
        bid = bid_ref[...]
        col = lax.broadcasted_iota(jnp.int32, (BLOCK, 2 * BLOCK), 1)
        for h in range(H_A):
            bias = _bias_from_buckets(bid, rb_ref, h)
            bias_ref[h] = bias
            bias_ref[H_A + h] = jnp.where(col < BLOCK, NEG_INF, bias)

    first_block = jnp.where(pl.program_id(1) == 0, H_A, 0)
    kvp = kvp_ref[...]
    kvc = kvc_ref[...]
    scale = HD_A ** -0.5
    k, v = [], []
    for kv in range(KV_A):
        ks = slice(kv * HD_A, (kv + 1) * HD_A)
        vs = slice(KV_A * HD_A + kv * HD_A, KV_A * HD_A + (kv + 1) * HD_A)
        k.append(jnp.concatenate([kvp[:, ks], kvc[:, ks]], axis=0).astype(BF16))
        v.append(jnp.concatenate([kvp[:, vs], kvc[:, vs]], axis=0).astype(BF16))
    heads = range(H_A)
    s = [_dot_nt((q_ref[:, h * HD_A:(h + 1) * HD_A] * scale).astype(BF16), k[h // G_A])
         + bias_ref[first_block + h] for h in heads]
    m = [jnp.maximum(jnp.max(s[h], axis=-1, keepdims=True), sk_ref[h]) for h in heads]
    p = [jnp.exp(s[h] - m[h]) for h in heads]
    denom = [jnp.sum(p[h], axis=-1, keepdims=True) + jnp.exp(sk_ref[h] - m[h]) for h in heads]
    outs = [_dot(p[h].astype(BF16), v[h // G_A]) / denom[h] for h in heads]
    o_ref[...] = jnp.concatenate(outs, axis=1).astype(BF16)


def _swa_prompt(qa, kv, rel_bias, sinks, bn, l):
    nb = l // BLOCK
    t = bn * l
    bid = jnp.asarray(_band_bucket_ids(BLOCK, 2 * BLOCK, BLOCK))
    kv_blk = (BLOCK, 2 * KV_A * HD_A)
    return pl.pallas_call(
        _swa_prompt_body,
        grid=(bn, nb),
        in_specs=[
            _whole((BLOCK, 2 * BLOCK)),
            pl.BlockSpec(memory_space=pltpu.SMEM),
            pl.BlockSpec(memory_space=pltpu.SMEM),
            pl.BlockSpec((BLOCK, A_Q), lambda b, i: (b * nb + i, 0)),
            pl.BlockSpec(kv_blk, lambda b, i: (jnp.maximum(b * nb + i - 1, 0), 0)),
            pl.BlockSpec(kv_blk, lambda b, i: (b * nb + i, 0)),
        ],
        out_specs=pl.BlockSpec((BLOCK, A_Q), lambda b, i: (b * nb + i, 0)),
        out_shape=jax.ShapeDtypeStruct((t, A_Q), BF16),
        scratch_shapes=[pltpu.VMEM((2 * H_A, BLOCK, 2 * BLOCK), F32)],
        compiler_params=_cp("arbitrary", "arbitrary"),
        name="swa_prompt",
    )(bid, rel_bias, sinks, qa, kv, kv)


def _swa_decode_body(bid_ref, rb_ref, sk_ref, q_ref, kvn_ref, ckt_ref, cvt_ref, o_ref):
    bid = bid_ref[...]
    row = lax.broadcasted_iota(jnp.int32, (H_A, 1), 0)
    bias = jnp.zeros((H_A, bid.shape[1]), F32)
    sk = jnp.zeros((H_A, 1), F32)
    for h in range(H_A):
        bias = jnp.where(row == h, _bias_from_buckets(bid, rb_ref, h), bias)
        sk = jnp.where(row == h, sk_ref[h], sk)
    bias_c = bias[:, :WINDOW]
    bias_n = bias[:, WINDOW:WINDOW + 1]
    scale = HD_A ** -0.5
    q = q_ref[...]
    kvn = kvn_ref[...]
    width = KV_A * HD_A
    k_new = kvn[:, None, :width]
    v_new = kvn[:, None, width:]
    s = lax.dot_general(q.astype(BF16), ckt_ref[...].astype(BF16), (((2,), (1,)), ((0,), (0,))),
                        preferred_element_type=F32) * scale + bias_c[None]
    s_n = jnp.sum(q * k_new, axis=-1, keepdims=True) * scale + bias_n[None]
    m = jnp.maximum(jnp.maximum(jnp.max(s, axis=-1, keepdims=True), s_n), sk[None])
    p = jnp.exp(s - m)
    p_n = jnp.exp(s_n - m)
    denom = jnp.sum(p, axis=-1, keepdims=True) + p_n + jnp.exp(sk[None] - m)
    o = lax.dot_general((p / denom).astype(BF16), cvt_ref[...].astype(BF16), (((2,), (2,)), ((0,), (0,))),
                        preferred_element_type=F32)
    o_ref[...] = o + (p_n / denom) * v_new


def _swa_decode(q_pad, kv_new, cache_kt, cache_vt, e, rel_bias, sinks):
    bn = q_pad.shape[0]
    bs = 32
    width = KV_A * HD_A
    ids = _band_bucket_ids(1, WINDOW + 1, WINDOW)
    bid = np.full((1, WINDOW + LANES), -1, np.int32)
    bid[:, :WINDOW + 1] = ids
    return pl.pallas_call(
        _swa_decode_body,
        grid=(bn // bs,),
        in_specs=[
            _whole((1, WINDOW + LANES)),
            pl.BlockSpec(memory_space=pltpu.SMEM),
            pl.BlockSpec(memory_space=pltpu.SMEM),
            pl.BlockSpec((bs, H_A, width), lambda i: (i, 0, 0)),
            pl.BlockSpec((bs, 2 * width), lambda i: (i, 0)),
            pl.BlockSpec((None, bs, width, WINDOW), lambda i: (e, i, 0, 0)),
            pl.BlockSpec((None, bs, width, WINDOW), lambda i: (e, i, 0, 0)),
        ],
        out_specs=pl.BlockSpec((bs, H_A, width), lambda i: (i, 0, 0)),
        out_shape=jax.ShapeDtypeStruct((bn, H_A, width), F32),
        compiler_params=_cp("parallel"),
        name="swa_decode",
    )(jnp.asarray(bid), rel_bias, sinks, q_pad, kv_new, cache_kt, cache_vt)


def _cache_shift_body(c_ref, nt_ref, o_ref):
    o_ref[...] = pltpu.roll(c_ref[...], WINDOW - 1, axis=2)
    nt = nt_ref[...]
    for b in range(c_ref.shape[0]):
        o_ref[b, :, WINDOW - 1:WINDOW] = nt[:, b:b + 1]


def _cache_shift(cache_t, new_t):
    n, bn, width, win = cache_t.shape
    bs = new_t.shape[3]
    blk = pl.BlockSpec((None, bs, width, win), lambda e, i: (e, i, 0, 0))
    return pl.pallas_call(
        _cache_shift_body,
        grid=(n, bn // bs),
        in_specs=[blk, pl.BlockSpec((None, None, width, bs), lambda e, i: (e, i, 0, 0))],
        out_specs=blk,
        out_shape=jax.ShapeDtypeStruct(cache_t.shape, F32),
        compiler_params=_cp("parallel", "parallel"),
        name="cache_shift",
    )(cache_t, new_t)


def _conv_rows(x_ref, xbuf_ref, cw_ref, rows):
    xbuf_ref[SUBLANES:SUBLANES + rows, :] = x_ref[...]
    x = xbuf_ref[...]
    out = x * cw_ref[0:1, :]
    for i in range(1, CONV_W):
        out = pltpu.roll(out, 1, axis=0) + x * cw_ref[i:i + 1, :]
    return out[SUBLANES:]


def _conv_chunk(x_ref, xbuf_ref, cw_ref, rows):
    out = _conv_rows(x_ref, xbuf_ref, cw_ref, rows)
    xbuf_ref[0:SUBLANES, :] = xbuf_ref[rows:rows + SUBLANES, :]
    return out


def _l2norm(x):
    return x * lax.rsqrt(jnp.sum(x * x, axis=-1, keepdims=True) + EPS)


def _chunk_masks(n, cc):
    r = lax.broadcasted_iota(jnp.int32, (n, n), 0)
    c = lax.broadcasted_iota(jnp.int32, (n, n), 1)
    shift = int(math.log2(cc))
    same = lax.shift_right_logical(r, shift) == lax.shift_right_logical(c, shift)
    return same, same & (r >= c), same & (r > c), same & (r <= c)


def _dot_bf16x3(a, b):
    a_hi = a.astype(BF16)
    a_lo = (a - a_hi.astype(F32)).astype(BF16)
    b_hi = b.astype(BF16)
    b_lo = (b - b_hi.astype(F32)).astype(BF16)
    return _dot(a_hi, b_hi) + _dot(a_hi, b_lo) + _dot(a_lo, b_hi)


def _gdn_prep_body(qkv_ref, halo_ref, sm_ref, smt_ref, cw_ref, acol_ref, dtcol_ref,
                   u_ref, w_ref, qg_ref, kd_ref, aqk_ref, gl_ref, xbuf_ref):
    rows = qkv_ref.shape[0]
    cc = GDN_CHUNK
    xbuf_ref[0:SUBLANES, :] = jnp.where(pl.program_id(1) == 0, 0.0, halo_ref[...])
    conv = _silu(_conv_rows(qkv_ref, xbuf_ref, cw_ref, rows))
    nq = H_B * DK_B
    beta_all = jax.nn.sigmoid(sm_ref[...])
    gt_all = -jnp.exp(acol_ref[...]) * _softplus(smt_ref[...] + dtcol_ref[...])
    pad = jnp.zeros((LANES - gt_all.shape[0], rows), F32)
    g_all = jnp.concatenate([gt_all, pad], axis=0).T
    same, lower, strict, upper = _chunk_masks(rows, cc)
    gc = _dot_hi(lower.astype(F32), g_all)
    gct = _dot_hi(gt_all, upper.astype(F32))
    gsum = _dot_hi(same.astype(F32), g_all)
    neg_outside = jnp.where(lower, 0.0, -jnp.inf)
    neg_strict = jnp.where(strict, -1.0, 0.0)
    gc2, gct2 = gc * LOG2E, gct * LOG2E
    heads = range(H_B)
    qh, kh16, kb, decay, egc = [], [], [], [], []
    for h in heads:
        q = _l2norm(conv[:, h * DK_B:(h + 1) * DK_B]) * (DK_B ** -0.5)
        k = _l2norm(conv[:, nq + h * DK_B:nq + (h + 1) * DK_B])
        v = conv[:, 2 * nq + h * DV_B:2 * nq + (h + 1) * DV_B]
        beta = beta_all[:, h:h + 1]
        gcol = gc[:, H_B + h:H_B + h + 1]
        decay.append(jnp.exp2(gc2[:, H_B + h:H_B + h + 1] - gct2[H_B + h:H_B + h + 1, :] + neg_outside))
        eg = jnp.exp(gcol)
        qg_ref[:, h * DK_B:(h + 1) * DK_B] = (q * eg).astype(BF16)
        kd_ref[:, h * DK_B:(h + 1) * DK_B] = (k * jnp.exp(gsum[:, H_B + h:H_B + h + 1] - gcol)).astype(BF16)
        qh.append(q.astype(BF16))
        kh16.append(k.astype(BF16))
        kb.append(k * beta)
        egc.append((v * beta, eg))
    a1 = [_dot_nt(kb[h].astype(BF16), kh16[h]) * decay[h] * neg_strict for h in heads]
    for h in heads:
        aqk = _dot_nt(qh[h], kh16[h]) * decay[h]
        blocks = [aqk[i * cc:(i + 1) * cc, i * cc:(i + 1) * cc] for i in range(rows // cc)]
        aqk_ref[:, h * cc:(h + 1) * cc] = jnp.concatenate(blocks, axis=0).astype(BF16)
    sq = lambda xs: [_dot(x.astype(BF16), x.astype(BF16)) for x in xs]
    pair = lambda xs, ys: [x + y + _dot(x.astype(BF16), y.astype(BF16)) for x, y in zip(xs, ys)]
    a2 = sq(a1)
    a4 = sq(a2)
    x1 = pair(a1, a2)
    a8 = sq(a4)
    a16 = sq(a8)
    x2 = pair(a4, a8)
    a32 = sq(a16)
    x12 = pair(x1, x2)
    x3 = pair(a16, a32)
    r = pair(x12, x3)
    for h in heads:
        vb, eg = egc[h]
        rhs = jnp.concatenate([vb, kb[h] * eg], axis=1)
        sol = rhs + _dot_bf16x3(r[h], rhs)
        u_ref[:, h * DV_B:(h + 1) * DV_B] = sol[:, :DV_B]
        w_ref[:, h * DK_B:(h + 1) * DK_B] = sol[:, DV_B:].astype(BF16)
    for i in range(rows // cc):
        gl = [jnp.broadcast_to(jnp.exp(gsum[i * cc:i * cc + 1, H_B + h:H_B + h + 1]), (SUBLANES, DV_B))
              for h in heads]
        gl_ref[0, i] = jnp.concatenate(gl, axis=1)


def _gdn_scan_body(u_ref, w_ref, qg_ref, kd_ref, aqk_ref, gl_ref, z_ref, gn_ref, o_ref, s_out_ref, s_ref):
    c = pl.program_id(0)
    bn = u_ref.shape[0]
    cc = GDN_CHUNK

    @pl.when(c == 0)
    def _():
        s_ref[...] = jnp.zeros_like(s_ref)

    gn = gn_ref[...]
    group = 2
    for b0 in range(0, bn, group):
        combos = [(b, h) for b in range(b0, b0 + group) for h in range(H_B)]
        res = {}
        for b, h in combos:
            hs = slice(h * DK_B, (h + 1) * DK_B)
            wq = jnp.concatenate([w_ref[b, :, hs], qg_ref[b, :, hs]], axis=0)
            res[b, h] = _dot(wq, s_ref[b, h].astype(BF16))
        v16 = {}
        for b, h in combos:
            hs = slice(h * DV_B, (h + 1) * DV_B)
            v16[b, h] = (u_ref[b, :, hs] - res[b, h][:cc]).astype(BF16)
        for b, h in combos:
            hs = slice(h * DV_B, (h + 1) * DV_B)
            o = res[b, h][cc:] + _dot(aqk_ref[b, :, h * cc:(h + 1) * cc], v16[b, h])
            upd = _dot_tn(kd_ref[b, :, hs], v16[b, h])
            s_ref[b, h] = s_ref[b, h] * gl_ref[b, 0, 0:1, hs] + upd
            o_ref[b, :, hs] = (_rms(o, gn) * _silu(z_ref[b, :, hs])).astype(BF16)

    @pl.when(c == pl.num_programs(0) - 1)
    def _():
        s_out_ref[...] = s_ref[...]


def _gdn_prompt(qkvb, z, small, small_t, conv_w, a_log, dt_bias, gnorm, bn, l):
    rows = 4 * GDN_CHUNK
    cc = GDN_CHUNK
    nb = l // rows
    nc = l // cc
    t = bn * l
    ns = small_t.shape[0]
    nq = H_B * DK_B
    pad_col = lambda v: jnp.zeros((ns, 1), F32).at[H_B:2 * H_B, 0].set(v)
    tok = lambda n: pl.BlockSpec((rows, n), lambda b, i: (b * nb + i, 0))
    halo = pl.BlockSpec((SUBLANES, B_QKV), lambda b, i: (jnp.maximum((b * nb + i) * (rows // SUBLANES) - 1, 0), 0))
    u, w, qg, kd, aqk, gl = pl.pallas_call(
        _gdn_prep_body,
        grid=(bn, nb),
        in_specs=[
            tok(B_QKV), halo, tok(LANES),
            pl.BlockSpec((ns, rows), lambda b, i: (0, b * nb + i)),
            _whole((CONV_W, B_QKV)), _whole((ns, 1)), _whole((ns, 1)),
        ],
        out_specs=[tok(nq), tok(nq), tok(nq), tok(nq), tok(H_B * cc),
                   pl.BlockSpec((1, rows // cc, SUBLANES, nq), lambda b, i: (b, i, 0, 0))],
        out_shape=[jax.ShapeDtypeStruct((t, nq), F32)] + [jax.ShapeDtypeStruct((t, nq), BF16)] * 3
        + [jax.ShapeDtypeStruct((t, H_B * cc), BF16), jax.ShapeDtypeStruct((bn, nc, SUBLANES, nq), F32)],
        scratch_shapes=[pltpu.VMEM((rows + SUBLANES, B_QKV), F32)],
        compiler_params=_cp("parallel", "parallel"),
        name="gdn_prep",
    )(qkvb, qkvb, small, small_t, conv_w, pad_col(a_log), pad_col(dt_bias))
    seq = lambda n: pl.BlockSpec((bn, cc, n), lambda c: (0, c, 0))
    r3 = lambda a: a.reshape(bn, l, a.shape[1])
    o, s_new = pl.pallas_call(
        _gdn_scan_body,
        grid=(nc,),
        in_specs=[seq(nq), seq(nq), seq(nq), seq(nq), seq(H_B * cc),
                  pl.BlockSpec((bn, 1, SUBLANES, nq), lambda c: (0, c, 0, 0)), seq(B_Z), _whole((1, DV_B))],
        out_specs=[seq(B_Z), pl.BlockSpec((bn, H_B, DK_B, DV_B), lambda c: (0, 0, 0, 0))],
        out_shape=[jax.ShapeDtypeStruct((bn, l, B_Z), BF16), jax.ShapeDtypeStruct((bn, H_B, DK_B, DV_B), F32)],
        scratch_shapes=[pltpu.VMEM((bn, H_B, DK_B, DV_B), F32)],
        compiler_params=_cp("arbitrary"),
        name="gdn_scan",
    )(r3(u), r3(w), r3(qg), r3(kd), r3(aqk), gl, r3(z), gnorm.reshape(1, DV_B))
    return o.reshape(t, B_Z), s_new


def _conv_step(x, hist_ref, cw_ref, new_ref):
    out = hist_ref[0] * cw_ref[0:1, :]
    for i in range(1, CONV_W - 1):
        out = out + hist_ref[i] * cw_ref[i:i + 1, :]
    out = out + x * cw_ref[CONV_W - 1:CONV_W, :]
    for i in range(CONV_W - 2):
        new_ref[i] = hist_ref[i + 1]
    new_ref[CONV_W - 2] = x
    return out


def _gdn_decode_pre_body(qkv_ref, hist_ref, sm_ref, cw_ref, arow_ref, dtrow_ref,
                         new_ref, w_ref, qg_ref, k_ref, u_ref, qk_ref, gl_ref):
    conv = _silu(_conv_step(qkv_ref[...], hist_ref, cw_ref, new_ref))
    nq = H_B * DK_B
    sm = sm_ref[...]
    beta_all = jax.nn.sigmoid(sm)
    g_all = -jnp.exp(arow_ref[...]) * _softplus(sm + dtrow_ref[...])
    for h in range(H_B):
        hs = slice(h * DK_B, (h + 1) * DK_B)
        qh = _l2norm(conv[:, hs]) * (DK_B ** -0.5)
        kh = _l2norm(conv[:, nq + h * DK_B:nq + (h + 1) * DK_B])
        vh = conv[:, 2 * nq + h * DV_B:2 * nq + (h + 1) * DV_B]
        beta = beta_all[:, h:h + 1]
        eg = jnp.exp(g_all[:, H_B + h:H_B + h + 1])
        w_ref[:, hs] = kh * beta * eg
        qg_ref[:, hs] = qh * eg
        k_ref[:, hs] = kh
        u_ref[:, hs] = vh * beta
        qk_ref[:, hs] = jnp.broadcast_to(jnp.sum(qh * kh, axis=-1, keepdims=True), qh.shape)
        gl_ref[:, hs] = jnp.broadcast_to(eg, qh.shape)


def _gdn_decode_state_body(s_ref, w_ref, qg_ref, k_ref, u_ref, qk_ref, gl_ref, z_ref, gn_ref, done_ref,
                           s_out_ref, o_ref):
    del done_ref
    nb = s_ref.shape[0]
    row = lax.broadcasted_iota(jnp.int32, (nb, 1), 0)
    for h in range(H_B):
        hs = slice(h * DK_B, (h + 1) * DK_B)
        wq = jnp.concatenate([w_ref[:, hs], qg_ref[:, hs]], axis=0).astype(BF16)
        ws = jnp.zeros((nb, DV_B), F32)
        qs = jnp.zeros((nb, DV_B), F32)
        for bb in range(nb):
            res = _dot(wq, s_ref[bb, h].astype(BF16))
            ws = jnp.where(row == bb, res[:nb], ws)
            qs = jnp.where(row == bb, res[nb:], qs)
        v_new = u_ref[:, hs] - ws
        o = qs + qk_ref[:, hs] * v_new
        o_ref[:, hs] = _rms(o, gn_ref[...]) * _silu(z_ref[:, hs])
        k = k_ref[:, hs]
        v16 = v_new.astype(BF16)
        gl = gl_ref[:, hs]
        for bb in range(nb):
            k_one = jnp.where(row == bb, k, 0.0).astype(BF16)
            s_out_ref[bb, h] = s_ref[bb, h] * gl[bb:bb + 1, :] + _dot_tn(k_one, v16)


def _gdn_decode(qkvb, hist, z, small, conv_w, a_log, dt_bias, gnorm, s_all, s_done, e):
    bn = qkvb.shape[0]
    pad_row = lambda v: jnp.zeros((1, LANES), F32).at[0, H_B:2 * H_B].set(v)
    wide = jax.ShapeDtypeStruct((bn, H_B * DK_B), F32)
    new_conv, w, qg, k, u, qk, gl = pl.pallas_call(
        _gdn_decode_pre_body,
        out_shape=[jax.ShapeDtypeStruct(hist.shape, F32)] + [wide] * 6,
        compiler_params=pltpu.CompilerParams(vmem_limit_bytes=VMEM_LIMIT_BYTES),
        name="gdn_decode_pre",
    )(qkvb, hist, small, conv_w, pad_row(a_log), pad_row(dt_bias))
    nb = SUBLANES
    vec = pl.BlockSpec((nb, H_B * DK_B), lambda i: (i, 0))
    st = pl.BlockSpec((None, nb, H_B, DK_B, DV_B), lambda i: (e, i, 0, 0, 0))
    s_done, o = pl.pallas_call(
        _gdn_decode_state_body,
        grid=(bn // nb,),
        in_specs=[st, vec, vec, vec, vec, vec, vec, vec, _whole((1, DV_B)), pl.BlockSpec(memory_space=pl.ANY)],
        out_specs=[st, vec],
        out_shape=[jax.ShapeDtypeStruct(s_all.shape, F32), wide],
        input_output_aliases={9: 0},
        compiler_params=_cp("parallel"),
        name="gdn_decode_state",
    )(s_all, w, qg, k, u, qk, gl, z, gnorm.reshape(1, DV_B), s_done)
    return o, new_conv, s_done


def _head_expander(width):
    e = np.zeros((LANES, H_C * width), np.float32)
    for h in range(H_C):
        e[h, h * width:(h + 1) * width] = 1.0
    return jnp.asarray(np.tile(e, (3, 1)), BF16)


def _group_rms(y, g):
    gs = D_INNER // G_C
    parts = [_rms(y[:, i * gs:(i + 1) * gs], g[:, i * gs:(i + 1) * gs]) for i in range(G_C)]
    return jnp.concatenate(parts, axis=1)


def _ssd_prompt_body(z_ref, xbc_ref, sm_ref, smt_ref, cw_ref, cb_ref, dtrow_ref, arow_ref, dtcol_ref,
                     acol_ref, dx_ref, gn_ref, e_ref, o_ref, s_out_ref, xbuf_ref, s_ref):
    c = pl.program_id(1)
    rows = xbc_ref.shape[0]

    @pl.when(c == 0)
    def _():
        xbuf_ref[0:SUBLANES, :] = jnp.zeros((SUBLANES, xbuf_ref.shape[1]), F32)
        s_ref[...] = jnp.zeros_like(s_ref)

    xbc = _silu(_conv_chunk(xbc_ref, xbuf_ref, cw_ref, rows) + cb_ref[...])
    xs = xbc[:, :D_INNER]
    bm = xbc[:, D_INNER:D_INNER + G_C * N_C]
    cm = xbc[:, D_INNER + G_C * N_C:]
    e = e_ref[...]
    lower = _tri(rows, "lower")
    dt = _softplus(sm_ref[...] + dtrow_ref[...])
    acum = _dot_hi(lower.astype(F32), dt * -jnp.exp(arow_ref[...]))
    a_t = _softplus(smt_ref[...] + dtcol_ref[...]) * -jnp.exp(acol_ref[...])
    acum_t = _dot_hi(a_t, _tri(rows, "upper").astype(F32))
    xdt = xs * _expand(dt, e)
    xdte = (xdt * _expand(jnp.exp(acum[rows - 1:rows, :] - acum), e)).astype(BF16)
    scale_y = _expand(jnp.exp(acum), e)
    chunk_decay = scale_y[rows - 1:rows, :]
    xdt16 = xdt.astype(BF16)
    neg_upper = jnp.where(lower, 0.0, -jnp.inf)
    acum2, acum2_t = acum * LOG2E, acum_t * LOG2E
    gw = HPG * P_C
    ys = []
    for g in range(G_C):
        bg = bm[:, g * N_C:(g + 1) * N_C]
        cg16 = cm[:, g * N_C:(g + 1) * N_C].astype(BF16)
        cb = _dot_nt(cg16, bg.astype(BF16))
        yg = []
        for hh in range(HPG):
            h = g * HPG + hh
            lmat = jnp.exp2(acum2[:, h:h + 1] - acum2_t[h:h + 1, :] + neg_upper)
            yg.append(_dot((cb * lmat).astype(BF16), xdt16[:, h * P_C:(h + 1) * P_C]))
        gs = slice(g * gw, (g + 1) * gw)
        sg = s_ref[:, gs]
        y_off = _dot(cg16, sg.astype(BF16)) * scale_y[:, gs]
        ys.append(jnp.concatenate(yg, axis=1) + y_off)
        s_ref[:, gs] = sg * chunk_decay[:, gs] + _dot(bg.T.astype(BF16), xdte[:, gs])
    y = jnp.concatenate(ys, axis=1) + dx_ref[...] * xs
    y = y * _silu(z_ref[...])
    o_ref[...] = _group_rms(y, gn_ref[...]).astype(BF16)

    @pl.when(c == pl.num_programs(1) - 1)
    def _():
        s_out_ref[0] = s_ref[...].T.reshape(H_C, P_C, N_C)


def _ssd_small_params(dt_bias, a_log, ns):
    row = lambda v: jnp.zeros((1, LANES), F32).at[0, :H_C].set(v)
    col = lambda v: jnp.zeros((ns, 1), F32).at[:H_C, 0].set(v)
    return row(dt_bias), row(a_log), col(dt_bias), col(a_log)


def _ssd_prompt(z, xbc, small, small_t, conv_w, conv_b, dt_bias, a_log, d_skip, gnorm, bn, l):
    rows = SSD_CHUNK
    nc = l // rows
    t = bn * l
    ns = small_t.shape[0]
    dtrow, arow, dtcol, acol = _ssd_small_params(dt_bias, a_log, ns)
    tok = lambda n: pl.BlockSpec((rows, n), lambda b, c: (b * nc + c, 0))
    return pl.pallas_call(
        _ssd_prompt_body,
        grid=(bn, nc),
        in_specs=[
            tok(D_INNER), tok(SSD_CONV_CH), tok(LANES),
            pl.BlockSpec((ns, rows), lambda b, c: (0, b * nc + c)),
            _whole((CONV_W, SSD_CONV_CH)), _whole((1, SSD_CONV_CH)),
            _whole((1, LANES)), _whole((1, LANES)), _whole((ns, 1)), _whole((ns, 1)),
            _whole((1, D_INNER)), _whole((1, D_INNER)), _whole((3 * LANES, D_INNER)),
        ],
        out_specs=[tok(D_INNER), pl.BlockSpec((1, H_C, P_C, N_C), lambda b, c: (b, 0, 0, 0))],
        out_shape=[jax.ShapeDtypeStruct((t, D_INNER), BF16), jax.ShapeDtypeStruct((bn, H_C, P_C, N_C), F32)],
        scratch_shapes=[pltpu.VMEM((rows + SUBLANES, SSD_CONV_CH), F32), pltpu.VMEM((N_C, D_INNER), F32)],
        compiler_params=_cp("arbitrary", "arbitrary"),
        name="ssd_prompt",
    )(z, xbc, small, small_t, conv_w, conv_b.reshape(1, SSD_CONV_CH), dtrow, arow, dtcol, acol,
      jnp.repeat(d_skip, P_C).reshape(1, D_INNER), gnorm.reshape(1, D_INNER), _head_expander(P_C))


def _ssd_decode_pre_body(xbc_ref, hist_ref, sm_ref, cw_ref, cb_ref, dtrow_ref, arow_ref, e_ref, en_ref,
                         new_ref, xs_ref, xdt_ref, b_ref, c_ref, dax_ref, dan_ref):
    xbc = _silu(_conv_step(xbc_ref[...], hist_ref, cw_ref, new_ref) + cb_ref[...])
    xs = xbc[:, :D_INNER]
    dt = _softplus(sm_ref[...] + dtrow_ref[...])
    a = dt * -jnp.exp(arow_ref[...])
    xs_ref[...] = xs
    xdt_ref[...] = xs * _expand(dt, e_ref[...])
    b_ref[...] = xbc[:, D_INNER:D_INNER + G_C * N_C]
    c_ref[...] = xbc[:, D_INNER + G_C * N_C:]
    dax_ref[...] = jnp.exp(_expand(a, e_ref[...]))
    dan_ref[...] = jnp.exp(_expand(a, en_ref[...]))


def _ssd_decode_state_body(s_ref, xdt_ref, b_ref, c_ref, dan_ref, done_ref, s_out_ref, yoff_ref):
    del done_ref
    gw = HPG * P_C
    sub = SUBLANES
    row = lax.broadcasted_iota(jnp.int32, (sub, 1), 0)
    for r0 in range(0, s_ref.shape[0], sub):
        rows = slice(r0, r0 + sub)
        xdt = xdt_ref[rows, :]
        b16 = b_ref[rows, :].astype(BF16)
        c16 = c_ref[rows, :].astype(BF16)
        dan = dan_ref[rows, :]
        yoff = jnp.zeros((sub, gw), F32)
        for bb in range(sub):
            s = s_ref[r0 + bb].reshape(gw, N_C)
            res = _dot_nt(c16, s.astype(BF16))
            yoff = jnp.where(row == bb, res, yoff)
            x_one = jnp.where(row == bb, xdt, 0.0).astype(BF16)
            upd = _dot_tn(x_one, b16)
            for hh in range(HPG):
                rs = slice(hh * P_C, (hh + 1) * P_C)
                s_out_ref[r0 + bb, hh] = s[rs] * dan[bb:bb + 1, hh * N_C:(hh + 1) * N_C] + upd[rs]
        yoff_ref[rows, :] = yoff


def _ssd_decode_post_body(yoff_ref, dax_ref, xdt_ref, xs_ref, b_ref, c_ref, z_ref, dx_ref, gn_ref, o_ref):
    gw = HPG * P_C
    bc = b_ref[...] * c_ref[...]
    cbx = [jnp.broadcast_to(jnp.sum(bc[:, g * N_C:(g + 1) * N_C], axis=-1, keepdims=True), (bc.shape[0], gw))
           for g in range(G_C)]
    y = yoff_ref[...] * dax_ref[...] + jnp.concatenate(cbx, axis=1) * xdt_ref[...]
    y = y + dx_ref[...] * xs_ref[...]
    y = y * _silu(z_ref[...])
    o_ref[...] = _group_rms(y, gn_ref[...])


def _ssd_decode(z, xbc, hist, small, conv_w, conv_b, dt_bias, a_log, d_skip, gnorm, s_all, s_done, e):
    bn = xbc.shape[0]
    dtrow, arow, _, _ = _ssd_small_params(dt_bias, a_log, H_C)
    wide = jax.ShapeDtypeStruct((bn, D_INNER), F32)
    grp = jax.ShapeDtypeStruct((bn, G_C * N_C), F32)
    plain = pltpu.CompilerParams(vmem_limit_bytes=VMEM_LIMIT_BYTES)
    new_conv, xs, xdt, bm, cm, dax, dan = pl.pallas_call(
        _ssd_decode_pre_body,
        out_shape=[jax.ShapeDtypeStruct(hist.shape, F32), wide, wide, grp, grp, wide,
                   jax.ShapeDtypeStruct((bn, H_C * N_C), F32)],
        compiler_params=plain,
        name="ssd_decode_pre",
    )(xbc, hist, small, conv_w, conv_b.reshape(1, SSD_CONV_CH), dtrow, arow,
      _head_expander(P_C), _head_expander(N_C))
    nb = 2 * SUBLANES
    gw = HPG * P_C
    st = pl.BlockSpec((None, nb, HPG, P_C, N_C), lambda i, g: (e, i, g, 0, 0))
    s_done, yoff = pl.pallas_call(
        _ssd_decode_state_body,
        grid=(bn // nb, G_C),
        in_specs=[st, pl.BlockSpec((nb, gw), lambda i, g: (i, g)), pl.BlockSpec((nb, N_C), lambda i, g: (i, g)),
                  pl.BlockSpec((nb, N_C), lambda i, g: (i, g)), pl.BlockSpec((nb, HPG * N_C), lambda i, g: (i, g)),
                  pl.BlockSpec(memory_space=pl.ANY)],
        out_specs=[st, pl.BlockSpec((nb, gw), lambda i, g: (i, g))],
        out_shape=[jax.ShapeDtypeStruct(s_all.shape, F32), wide],
        input_output_aliases={5: 0},
        compiler_params=_cp("parallel", "parallel"),
        name="ssd_decode_state",
    )(s_all, xdt, bm, cm, dan, s_done)
    mix = pl.pallas_call(
        _ssd_decode_post_body,
        out_shape=wide,
        compiler_params=plain,
        name="ssd_decode_post",
    )(yoff, dax, xdt, xs, bm, cm, z, jnp.repeat(d_skip, P_C).reshape(1, D_INNER), gnorm.reshape(1, D_INNER))
    return mix, new_conv, s_done


def _narrow_weights(w_small):
    n = w_small.shape[1]
    return jnp.zeros((D_MODEL, LANES), BF16).at[:, :n].set(w_small.astype(BF16))


def _even_q_pad(qa):
    bn = qa.shape[0]
    q = qa.reshape(bn, KV_A, G_A, HD_A)
    out = jnp.zeros((bn, KV_A, G_A, KV_A, HD_A), F32)
    for kv in range(KV_A):
        out = out.at[:, kv, :, kv, :].set(q[:, kv])
    return out.reshape(bn, H_A, KV_A * HD_A)


def _even_o_unpad(o8):
    bn = o8.shape[0]
    o = o8.reshape(bn, KV_A, G_A, KV_A, HD_A)
    return jnp.concatenate([o[:, kv, :, kv, :].reshape(bn, G_A * HD_A) for kv in range(KV_A)], axis=1)


def _trunk(x3, states, wts):
    (rel_bias, norm_ff1, norm_mix, norm_ff2, norm_final,
     ff1, ff2, even_w_in, even_w_out, swa_sinks, gdn_conv_w, gdn_A_log, gdn_dt_bias, gdn_norm,
     ssd_w_in, ssd_w_out, ssd_conv_w, ssd_conv_b, ssd_dt_bias, ssd_A_log, ssd_D, ssd_norm) = wts
    bn, l, d = x3.shape
    t = bn * l
    x = x3.reshape(t, d)
    decode = states is not None
    ks, vs, gconv, gssm, sconv, sssm = [], [], [], [], [], []
    depth = norm_ff1.shape[0]
    width = KV_A * HD_A
    if decode:
        gssm_done, sssm_done = jnp.zeros_like(states[3]), jnp.zeros_like(states[5])
        n_even = states[0].shape[0]
        cache_t = lambda c: c.transpose(0, 1, 3, 4, 2).reshape(n_even, bn, width, WINDOW)
        ckt, cvt = cache_t(states[0]), cache_t(states[1])
        ghist, shist = states[2].transpose(0, 2, 1, 3), states[4].transpose(0, 2, 1, 3)
    for layer in range(depth):
        x = _ffn(x, norm_ff1[layer], ff1, layer)
        e = layer // 2
        if layer % 2 == 0:
            w_all, ws = even_w_in
            qa, kv, qkvb, z, small, small_t = _inproj(x, norm_mix[layer], w_all, e, ws[e], 2 * SUBLANES,
                                                      (A_Q, 2 * width, B_QKV, B_Z))
            if decode:
                o8 = _swa_decode(_even_q_pad(qa), kv, ckt, cvt, e, rel_bias, swa_sinks[e])
                o_a = _even_o_unpad(o8)
                o_b, new_conv, gssm_done = _gdn_decode(qkvb, ghist[e], z, small, gdn_conv_w[e], gdn_A_log[e],
                                                       gdn_dt_bias[e], gdn_norm[e], states[3], gssm_done, e)
                new_k, new_v = kv[:, :width], kv[:, width:]
                new_conv = new_conv.transpose(1, 0, 2)
            else:
                o_a = _swa_prompt(qa, kv, rel_bias, swa_sinks[e], bn, l)
                o_b, s_new = _gdn_prompt(qkvb, z, small, small_t, gdn_conv_w[e], gdn_A_log[e],
                                         gdn_dt_bias[e], gdn_norm[e], bn, l)
                kv3 = kv.reshape(bn, l, 2 * width)
                new_k = kv3[:, l - WINDOW:, :width]
                new_v = kv3[:, l - WINDOW:, width:]
                new_conv = qkvb.reshape(bn, l, B_QKV)[:, l - (CONV_W - 1):]
                gssm.append(s_new)
            ks.append(new_k)
            vs.append(new_v)
            gconv.append(new_conv)
            mixes = [(o_a, even_w_out, e, 0), (o_b, even_w_out, e, 1)]
        else:
            w_all, ws = ssd_w_in
            z, xbc, small, small_t = _inproj(x, norm_mix[layer], w_all, e, ws[e], H_C, (D_INNER, SSD_CONV_CH))
            if decode:
                mix, new_conv, sssm_done = _ssd_decode(z, xbc, shist[e], small, ssd_conv_w[e], ssd_conv_b[e],
                                                       ssd_dt_bias[e], ssd_A_log[e], ssd_D[e], ssd_norm[e],
                                                       states[5], sssm_done, e)
                new_conv = new_conv.transpose(1, 0, 2)
            else:
                mix, s_new = _ssd_prompt(z, xbc, small, small_t, ssd_conv_w[e], ssd_conv_b[e], ssd_dt_bias[e],
                                         ssd_A_log[e], ssd_D[e], ssd_norm[e], bn, l)
                new_conv = xbc.reshape(bn, l, SSD_CONV_CH)[:, l - (CONV_W - 1):]
                sssm.append(s_new)
            sconv.append(new_conv)
            mixes = [(mix, ssd_w_out, e, 0)]
        x = _ffn(x, norm_ff2[layer], ff2, layer, mixes=mixes,
                 g_final=norm_final if layer == depth - 1 else None)
    if decode:
        bs = 32
        new_t = lambda rows: jnp.stack(rows).reshape(n_even, bn // bs, bs, width).transpose(0, 1, 3, 2)
        back = lambda c: c.reshape(n_even, bn, KV_A, HD_A, WINDOW).transpose(0, 1, 4, 2, 3)
        ks = back(_cache_shift(ckt, new_t(ks)))
        vs = back(_cache_shift(cvt, new_t(vs)))
        gssm, sssm = gssm_done, sssm_done
    else:
        kv5 = (len(ks), bn, WINDOW, KV_A, HD_A)
        ks, vs = jnp.stack(ks).reshape(kv5), jnp.stack(vs).reshape(kv5)
        gssm, sssm = jnp.stack(gssm), jnp.stack(sssm)
    return (x.reshape(bn, l, d), ks, vs, jnp.stack(gconv), gssm, jnp.stack(sconv), sssm)


def kernel(x_prompt, x_sample, cache_swa_k, cache_swa_v, state_gdn_conv, state_gdn_ssm, state_ssd_conv, state_ssd_ssm, rel_bias, norm_ff1, norm_mix, norm_ff2, norm_final, ff1_gate, ff1_up, ff1_down, ff2_gate, ff2_up, ff2_down, even_w_in, even_w_out, swa_sinks, gdn_conv_w, gdn_A_log, gdn_dt_bias, gdn_norm, ssd_w_in, ssd_w_out, ssd_conv_w, ssd_conv_b, ssd_dt_bias, ssd_A_log, ssd_D, ssd_norm):
    depth = norm_ff1.shape[0]
    ff1 = (ff1_gate.astype(BF16), ff1_up.astype(BF16), ff1_down.astype(BF16))
    ff2 = (ff2_gate.astype(BF16), ff2_up.astype(BF16), ff2_down.astype(BF16))
    n_even_main = A_Q + 2 * KV_A * HD_A + B_QKV + B_Z
    even_in = (even_w_in.astype(BF16),
               [_narrow_weights(even_w_in[e][:, n_even_main:]) for e in range(even_w_in.shape[0])])
    n_odd_main = D_INNER + SSD_CONV_CH
    odd_in = (ssd_w_in.astype(BF16),
              [_narrow_weights(ssd_w_in[e][:, n_odd_main:]) for e in range(ssd_w_in.shape[0])])
    wts = (rel_bias, norm_ff1, norm_mix, norm_ff2, norm_final, ff1, ff2,
           even_in, even_w_out.astype(BF16), swa_sinks, gdn_conv_w, gdn_A_log, gdn_dt_bias, gdn_norm,
           odd_in, ssd_w_out.astype(BF16), ssd_conv_w, ssd_conv_b, ssd_dt_bias, ssd_A_log, ssd_D, ssd_norm)
    y_p, p_k, p_v, p_gconv, p_gssm, p_sconv, p_sssm = _trunk(x_prompt, None, wts)
    states = (cache_swa_k, cache_swa_v, state_gdn_conv, state_gdn_ssm, state_ssd_conv, state_ssd_ssm)
    y_s, s_k, s_v, s_gconv, s_gssm, s_sconv, s_sssm = _trunk(x_sample, states, wts)
    return (y_p, y_s, p_k, p_v, p_gconv, p_gssm, p_sconv, p_sssm,
            s_k, s_v, s_gconv, s_gssm, s_sconv, s_sssm)
```

```python
import functools
import math

import numpy as np
import jax
import jax.numpy as jnp
from jax import lax
from jax.experimental import pallas as pl
from jax.experimental.pallas import tpu as pltpu

F32 = jnp.float32
BF16 = jnp.bfloat16
HI = lax.Precision.HIGHEST

EPS = 1e-6
NEG_INF = -1e30
LOG2E = math.log2(math.e)
D_MODEL = 1024
WINDOW = 128
BLOCK = 128
H_A, KV_A, G_A, HD_A = 8, 2, 4, 64
N_BUCKETS, MAX_DIST = 32, 128
H_B, DK_B, DV_B = 4, 128, 128
CONV_W = 4
GDN_CHUNK = 64
D_INNER = 2048
P_C, H_C, N_C, G_C = 64, 32, 128, 4
HPG = H_C // G_C
SSD_CHUNK = 128
A_Q = H_A * HD_A
B_QKV = 3 * H_B * DK_B
B_Z = H_B * DV_B
SSD_CONV_CH = D_INNER + 2 * G_C * N_C
LANES = 128
SUBLANES = 8
VMEM_LIMIT_BYTES = 56 * 1024 * 1024
ROW_TILE = 512
SWA_QUERY_BLOCKS = 4
GDN_PREP_CHUNKS = 4
GDN_SCAN_GROUP = 2
DECODE_SEQ_BLOCK = 32
SSD_STATE_SEQS = 2 * SUBLANES


def _cp(*sem):
    return pltpu.CompilerParams(dimension_semantics=sem, vmem_limit_bytes=VMEM_LIMIT_BYTES)


def _whole(shape):
    nd = len(shape)
    return pl.BlockSpec(shape, lambda *_: (0,) * nd, pipeline_mode=pl.Buffered(1))


def _rms(x, g):
    return x * lax.rsqrt(jnp.mean(x * x, axis=-1, keepdims=True) + EPS) * g


def _silu(x):
    return x * jax.nn.sigmoid(x)


def _softplus(x):
    return jnp.maximum(x, 0.0) + jnp.log1p(jnp.exp(-jnp.abs(x)))


def _dot(a, b):
    return jnp.dot(a, b, preferred_element_type=F32)


def _dot_nt(a, b):
    return lax.dot_general(a, b, (((1,), (1,)), ((), ())), preferred_element_type=F32)


def _dot_tn(a, b):
    return lax.dot_general(a, b, (((0,), (0,)), ((), ())), preferred_element_type=F32)


def _dot_hi(a, b):
    return jnp.dot(a, b, precision=HI, preferred_element_type=F32)


def _expand(x, e3):
    hi = x.astype(BF16)
    r = x - hi.astype(F32)
    mid = r.astype(BF16)
    lo = (r - mid.astype(F32)).astype(BF16)
    return _dot(jnp.concatenate([hi, mid, lo], axis=1), e3)


def _tri(n, kind):
    r = lax.broadcasted_iota(jnp.int32, (n, n), 0)
    c = lax.broadcasted_iota(jnp.int32, (n, n), 1)
    return {"lower": r >= c, "strict_lower": r > c, "upper": r <= c}[kind]


def _ffn_body(*refs, n_mix, final):
    refs = list(refs)
    x = refs.pop(0)[...]
    for _ in range(n_mix):
        m_ref, wo_ref = refs.pop(0), refs.pop(0)
        x = x + _dot(m_ref[...].astype(BF16), wo_ref[...])
    g_ref, wg_ref, wu_ref, wd_ref = refs[:4]
    o_ref = refs[-1]
    hn = _rms(x, g_ref[...]).astype(BF16)
    act = (_silu(_dot(hn, wg_ref[...])) * _dot(hn, wu_ref[...])).astype(BF16)
    y = x + 0.5 * _dot(act, wd_ref[...])
    if final:
        y = _rms(y, refs[4][...])
    o_ref[...] = y


def _layer_slab(w_all, layer, rows=None, row_block=0):
    shape = (rows or w_all.shape[1], w_all.shape[2])
    return pl.BlockSpec((None,) + shape, lambda i: (layer, row_block, 0), pipeline_mode=pl.Buffered(1))


def _ffn(x, g, ffw, layer, mixes=(), g_final=None):
    t, d = x.shape
    tm = min(t, ROW_TILE)
    final = g_final is not None
    row = pl.BlockSpec((tm, d), lambda i: (i, 0))
    in_specs, args = [row], [x]
    for m, w_all, e, row_block in mixes:
        in_specs += [pl.BlockSpec((tm, m.shape[1]), lambda i: (i, 0)), _layer_slab(w_all, e, m.shape[1], row_block)]
        args += [m, w_all]
    in_specs += [_whole((1, d))] + [_layer_slab(w, layer) for w in ffw]
    args += [g.reshape(1, d), *ffw]
    if final:
        in_specs.append(_whole((1, d)))
        args.append(g_final.reshape(1, d))
    return pl.pallas_call(
        functools.partial(_ffn_body, n_mix=len(mixes), final=final),
        grid=(t // tm,),
        in_specs=in_specs,
        out_specs=row,
        out_shape=jax.ShapeDtypeStruct((t, d), F32),
        compiler_params=_cp("parallel"),
        name="ffn",
    )(*args)


def _inproj_body(x_ref, g_ref, w_ref, ws_ref, *outs, splits):
    hn = _rms(x_ref[...], g_ref[...]).astype(BF16)
    off = 0
    for o_ref, n in zip(outs[:-2], splits):
        o_ref[...] = _dot(hn, w_ref[:, off:off + n])
        off += n
    small = _dot(hn, ws_ref[...])
    outs[-2][...] = small
    outs[-1][...] = small.T[:outs[-1].shape[0]]


def _inproj(x, g, w_all, e, ws, ns, splits):
    t, d = x.shape
    w_spec = _layer_slab(w_all, e)
    tm = min(t, ROW_TILE)
    row = pl.BlockSpec((tm, d), lambda i: (i, 0))
    out_specs = [pl.BlockSpec((tm, n), lambda i: (i, 0)) for n in splits]
    out_specs += [pl.BlockSpec((tm, LANES), lambda i: (i, 0)), pl.BlockSpec((ns, tm), lambda i: (0, i))]
    out_shape = [jax.ShapeDtypeStruct((t, n), F32) for n in splits]
    out_shape += [jax.ShapeDtypeStruct((t, LANES), F32), jax.ShapeDtypeStruct((ns, t), F32)]
    return pl.pallas_call(
        functools.partial(_inproj_body, splits=splits),
        grid=(t // tm,),
        in_specs=[row, _whole((1, d)), w_spec, _whole(ws.shape)],
        out_specs=out_specs,
        out_shape=out_shape,
        compiler_params=_cp("parallel"),
        name="inproj",
    )(x, g.reshape(1, d), w_all, ws)


def _t5_bucket_np(dist):
    max_exact = N_BUCKETS // 2
    df = np.maximum(dist, max_exact).astype(np.float32)
    large = max_exact + (np.log(df / np.float32(max_exact)) / np.float32(math.log(MAX_DIST / max_exact))
                         * np.float32(N_BUCKETS - max_exact)).astype(np.int32)
    return np.where(dist < max_exact, dist, np.minimum(large, N_BUCKETS - 1)).astype(np.int32)


def _band_bucket_ids(n_q, n_k, offset):
    d = offset + np.arange(n_q)[:, None] - np.arange(n_k)[None, :]
    valid = (d >= 0) & (d <= WINDOW)
    return np.where(valid, _t5_bucket_np(np.clip(d, 0, WINDOW)), -1).astype(np.int32)


def _bias_from_buckets(bid, rb_ref, h):
    acc = jnp.full(bid.shape, NEG_INF, F32)
    for bk in range(N_BUCKETS):
        acc = jnp.where(bid == bk, rb_ref[bk, h], acc)
    return acc


def _swa_prompt_body(bid_ref, rb_ref, sk_ref, q_ref, kvp_ref, kvc_ref, o_ref, bias_ref):
    first_step = (pl.program_id(0) == 0) & (pl.program_id(1) == 0)

    @pl.when(first_step)
    def _():
        bid = bid_ref[...]
        col = lax.broadcasted_iota(jnp.int32, (BLOCK, 2 * BLOCK), 1)
        for h in range(H_A):
            bias = _bias_from_buckets(bid, rb_ref, h)
            bias_ref[h] = bias
            bias_ref[H_A + h] = jnp.where(col < BLOCK, NEG_INF, bias)

    kvp = kvp_ref[...]
    kvc = kvc_ref[...]
    scale = HD_A ** -0.5
    heads = range(H_A)
    for j in range(q_ref.shape[0] // BLOCK):
        rows = slice(j * BLOCK, (j + 1) * BLOCK)
        prev = kvp if j == 0 else kvc[(j - 1) * BLOCK:j * BLOCK]
        first_block = jnp.where(pl.program_id(1) == 0, H_A, 0) if j == 0 else 0
        k, v = [], []
        for kv in range(KV_A):
            ks = slice(kv * HD_A, (kv + 1) * HD_A)
            vs = slice(KV_A * HD_A + kv * HD_A, KV_A * HD_A + (kv + 1) * HD_A)
            k.append(jnp.concatenate([prev[:, ks], kvc[rows, ks]], axis=0).astype(BF16))
            v.append(jnp.concatenate([prev[:, vs], kvc[rows, vs]], axis=0).astype(BF16))
        s = [_dot_nt((q_ref[rows, h * HD_A:(h + 1) * HD_A] * scale).astype(BF16), k[h // G_A])
             + bias_ref[first_block + h] for h in heads]
        m = [jnp.maximum(jnp.max(s[h], axis=-1, keepdims=True), sk_ref[h]) for h in heads]
        p = [jnp.exp(s[h] - m[h]) for h in heads]
        denom = [jnp.sum(p[h], axis=-1, keepdims=True) + jnp.exp(sk_ref[h] - m[h]) for h in heads]
        outs = [_dot(p[h].astype(BF16), v[h // G_A]) / denom[h] for h in heads]
        o_ref[rows, :] = jnp.concatenate(outs, axis=1).astype(BF16)


def _swa_prompt(qa, kv, rel_bias, sinks, bn, l):
    qb = SWA_QUERY_BLOCKS
    nb = l // (qb * BLOCK)
    t = bn * l
    bid = jnp.asarray(_band_bucket_ids(BLOCK, 2 * BLOCK, BLOCK))
    width = 2 * KV_A * HD_A
    return pl.pallas_call(
        _swa_prompt_body,
        grid=(bn, nb),
        in_specs=[
            _whole((BLOCK, 2 * BLOCK)),
            pl.BlockSpec(memory_space=pltpu.SMEM),
            pl.BlockSpec(memory_space=pltpu.SMEM),
            pl.BlockSpec((qb * BLOCK, A_Q), lambda b, i: (b * nb + i, 0)),
            pl.BlockSpec((BLOCK, width), lambda b, i: (jnp.maximum((b * nb + i) * qb - 1, 0), 0)),
            pl.BlockSpec((qb * BLOCK, width), lambda b, i: (b * nb + i, 0)),
        ],
        out_specs=pl.BlockSpec((qb * BLOCK, A_Q), lambda b, i: (b * nb + i, 0)),
        out_shape=jax.ShapeDtypeStruct((t, A_Q), BF16),
        scratch_shapes=[pltpu.VMEM((2 * H_A, BLOCK, 2 * BLOCK), F32)],
        compiler_params=_cp("arbitrary", "arbitrary"),
        name="swa_prompt",
    )(bid, rel_bias, sinks, qa, kv, kv)


def _swa_decode_body(bid_ref, rb_ref, sk_ref, q_ref, kvn_ref, ckt_ref, cvt_ref, o_ref):
    bid = bid_ref[...]
    row = lax.broadcasted_iota(jnp.int32, (H_A, 1), 0)
    bias = jnp.zeros((H_A, bid.shape[1]), F32)
    sk = jnp.zeros((H_A, 1), F32)
    for h in range(H_A):
        bias = jnp.where(row == h, _bias_from_buckets(bid, rb_ref, h), bias)
        sk = jnp.where(row == h, sk_ref[h], sk)
    bias_c = bias[:, :WINDOW]
    bias_n = bias[:, WINDOW:WINDOW + 1]
    scale = HD_A ** -0.5
    q = q_ref[...]
    kvn = kvn_ref[...]
    width = KV_A * HD_A
    k_new = kvn[:, None, :width]
    v_new = kvn[:, None, width:]
    s = lax.dot_general(q.astype(BF16), ckt_ref[...].astype(BF16), (((2,), (1,)), ((0,), (0,))),
                        preferred_element_type=F32) * scale + bias_c[None]
    s_n = jnp.sum(q * k_new, axis=-1, keepdims=True) * scale + bias_n[None]
    m = jnp.maximum(jnp.maximum(jnp.max(s, axis=-1, keepdims=True), s_n), sk[None])
    p = jnp.exp(s - m)
    p_n = jnp.exp(s_n - m)
    denom = jnp.sum(p, axis=-1, keepdims=True) + p_n + jnp.exp(sk[None] - m)
    o = lax.dot_general((p / denom).astype(BF16), cvt_ref[...].astype(BF16), (((2,), (2,)), ((0,), (0,))),
                        preferred_element_type=F32)
    o_ref[...] = o + (p_n / denom) * v_new


def _swa_decode(q_pad, kv_new, cache_kt, cache_vt, e, rel_bias, sinks):
    bn = q_pad.shape[0]
    bs = DECODE_SEQ_BLOCK
    width = KV_A * HD_A
    ids = _band_bucket_ids(1, WINDOW + 1, WINDOW)
    bid = np.full((1, WINDOW + LANES), -1, np.int32)
    bid[:, :WINDOW + 1] = ids
    return pl.pallas_call(
        _swa_decode_body,
        grid=(bn // bs,),
        in_specs=[
            _whole((1, WINDOW + LANES)),
            pl.BlockSpec(memory_space=pltpu.SMEM),
            pl.BlockSpec(memory_space=pltpu.SMEM),
            pl.BlockSpec((bs, H_A, width), lambda i: (i, 0, 0)),
            pl.BlockSpec((bs, 2 * width), lambda i: (i, 0)),
            pl.BlockSpec((None, bs, width, WINDOW), lambda i: (e, i, 0, 0)),
            pl.BlockSpec((None, bs, width, WINDOW), lambda i: (e, i, 0, 0)),
        ],
        out_specs=pl.BlockSpec((bs, H_A, width), lambda i: (i, 0, 0)),
        out_shape=jax.ShapeDtypeStruct((bn, H_A, width), F32),
        compiler_params=_cp("parallel"),
        name="swa_decode",
    )(jnp.asarray(bid), rel_bias, sinks, q_pad, kv_new, cache_kt, cache_vt)


def _cache_shift_body(c_ref, nt_ref, o_ref):
    o_ref[...] = pltpu.roll(c_ref[...], WINDOW - 1, axis=2)
    nt = nt_ref[...]
    for b in range(c_ref.shape[0]):
        o_ref[b, :, WINDOW - 1:WINDOW] = nt[:, b:b + 1]


def _cache_shift(cache_t, new_t):
    n, bn, width, win = cache_t.shape
    bs = new_t.shape[3]
    blk = pl.BlockSpec((None, bs, width, win), lambda e, i: (e, i, 0, 0))
    return pl.pallas_call(
        _cache_shift_body,
        grid=(n, bn // bs),
        in_specs=[blk, pl.BlockSpec((None, None, width, bs), lambda e, i: (e, i, 0, 0))],
        out_specs=blk,
        out_shape=jax.ShapeDtypeStruct(cache_t.shape, F32),
        compiler_params=_cp("parallel", "parallel"),
        name="cache_shift",
    )(cache_t, new_t)


def _conv_rows(x_ref, xbuf_ref, cw_ref, rows):
    xbuf_ref[SUBLANES:SUBLANES + rows, :] = x_ref[...]
    x = xbuf_ref[...]
    out = x * cw_ref[0:1, :]
    for i in range(1, CONV_W):
        out = pltpu.roll(out, 1, axis=0) + x * cw_ref[i:i + 1, :]
    return out[SUBLANES:]


def _conv_chunk(x_ref, xbuf_ref, cw_ref, rows):
    out = _conv_rows(x_ref, xbuf_ref, cw_ref, rows)
    xbuf_ref[0:SUBLANES, :] = xbuf_ref[rows:rows + SUBLANES, :]
    return out


def _l2norm(x):
    return x * lax.rsqrt(jnp.sum(x * x, axis=-1, keepdims=True) + EPS)


def _chunk_masks(n, cc):
    r = lax.broadcasted_iota(jnp.int32, (n, n), 0)
    c = lax.broadcasted_iota(jnp.int32, (n, n), 1)
    shift = int(math.log2(cc))
    same = lax.shift_right_logical(r, shift) == lax.shift_right_logical(c, shift)
    return same, same & (r >= c), same & (r <= c)


def _dot_bf16x3(a, b):
    a_hi = a.astype(BF16)
    a_lo = (a - a_hi.astype(F32)).astype(BF16)
    b_hi = b.astype(BF16)
    b_lo = (b - b_hi.astype(F32)).astype(BF16)
    return _dot(a_hi, b_hi) + _dot(a_hi, b_lo) + _dot(a_lo, b_hi)


def _solve_masks(n, cc, base=SUBLANES):
    r = np.arange(n)[:, None]
    c = np.arange(n)[None, :]
    masks = [-(((r // base) == (c // base)) & (r > c)).astype(np.float32)]
    s = base
    while s < cc:
        sibling = ((r // (2 * s)) == (c // (2 * s))) & ((r // s) % 2 == 1) & ((c // s) % 2 == 0)
        masks.append(sibling.astype(np.float32))
        s *= 2
    return np.stack(masks)


def _gdn_prep_body(qkv_ref, halo_ref, sm_ref, smt_ref, cw_ref, acol_ref, dtcol_ref, tri_ref,
                   u_ref, w_ref, qg_ref, kd_ref, aqk_ref, gl_ref, xbuf_ref):
    rows = qkv_ref.shape[0]
    cc = GDN_CHUNK
    xbuf_ref[0:SUBLANES, :] = jnp.where(pl.program_id(1) == 0, 0.0, halo_ref[...])
    conv = _silu(_conv_rows(qkv_ref, xbuf_ref, cw_ref, rows))
    nq = H_B * DK_B
    beta_all = jax.nn.sigmoid(sm_ref[...])
    gt_all = -jnp.exp(acol_ref[...]) * _softplus(smt_ref[...] + dtcol_ref[...])
    pad = jnp.zeros((LANES - gt_all.shape[0], rows), F32)
    g_all = jnp.concatenate([gt_all, pad], axis=0).T
    same, lower, upper = _chunk_masks(rows, cc)
    gc = _dot_hi(lower.astype(F32), g_all)
    gct = _dot_hi(gt_all, upper.astype(F32))
    gsum = _dot_hi(same.astype(F32), g_all)
    neg_outside = jnp.where(lower, 0.0, -jnp.inf)
    gc2, gct2 = gc * LOG2E, gct * LOG2E
    heads = range(H_B)
    qh, kh16, kb, decay, egc = [], [], [], [], []
    for h in heads:
        q = _l2norm(conv[:, h * DK_B:(h + 1) * DK_B]) * (DK_B ** -0.5)
        k = _l2norm(conv[:, nq + h * DK_B:nq + (h + 1) * DK_B])
        v = conv[:, 2 * nq + h * DV_B:2 * nq + (h + 1) * DV_B]
        beta = beta_all[:, h:h + 1]
        gcol = gc[:, H_B + h:H_B + h + 1]
        decay.append(jnp.exp2(gc2[:, H_B + h:H_B + h + 1] - gct2[H_B + h:H_B + h + 1, :] + neg_outside))
        eg = jnp.exp(gcol)
        qg_ref[:, h * DK_B:(h + 1) * DK_B] = (q * eg).astype(BF16)
        kd_ref[:, h * DK_B:(h + 1) * DK_B] = (k * jnp.exp(gsum[:, H_B + h:H_B + h + 1] - gcol)).astype(BF16)
        qh.append(q.astype(BF16))
        kh16.append(k.astype(BF16))
        kb.append(k * beta)
        egc.append((v * beta, eg))
    m = [_dot_nt(kb[h].astype(BF16), kh16[h]) * decay[h] for h in heads]
    for h in heads:
        aqk = _dot_nt(qh[h], kh16[h]) * decay[h]
        blocks = [aqk[i * cc:(i + 1) * cc, i * cc:(i + 1) * cc] for i in range(rows // cc)]
        aqk_ref[:, h * cc:(h + 1) * cc] = jnp.concatenate(blocks, axis=0).astype(BF16)
    mm = lambda xs, ys: [_dot(x.astype(BF16), y.astype(BF16)) for x, y in zip(xs, ys)]
    a = [m[h] * tri_ref[0] for h in heads]
    a2 = mm(a, a)
    a3 = mm(a, a2)
    a4 = mm(a2, a2)
    x1 = [a[h] + a2[h] + a3[h] for h in heads]
    x1a4 = mm(x1, a4)
    r = [x1[h] + a4[h] + x1a4[h] for h in heads]
    for level in range(1, tri_ref.shape[0]):
        c = [m[h] * tri_ref[level] for h in heads]
        rc = mm(r, c)
        y = [c[h] + rc[h] for h in heads]
        yr = mm(y, r)
        r = [r[h] - y[h] - yr[h] for h in heads]
    for h in heads:
        vb, eg = egc[h]
        rhs = jnp.concatenate([vb, kb[h] * eg], axis=1)
        sol = rhs + _dot_bf16x3(r[h], rhs)
        u_ref[:, h * DV_B:(h + 1) * DV_B] = sol[:, :DV_B]
        w_ref[:, h * DK_B:(h + 1) * DK_B] = sol[:, DV_B:].astype(BF16)
    for i in range(rows // cc):
        gl = [jnp.broadcast_to(jnp.exp(gsum[i * cc:i * cc + 1, H_B + h:H_B + h + 1]), (SUBLANES, DV_B))
              for h in heads]
        gl_ref[0, i] = jnp.concatenate(gl, axis=1)


def _gdn_scan_body(u_ref, w_ref, qg_ref, kd_ref, aqk_ref, gl_ref, z_ref, gn_ref, o_ref, s_out_ref, s_ref):
    c = pl.program_id(0)
    bn = u_ref.shape[0]
    cc = GDN_CHUNK

    @pl.when(c == 0)
    def _():
        s_ref[...] = jnp.zeros_like(s_ref)

    gn = gn_ref[...]
    group = GDN_SCAN_GROUP
    for b0 in range(0, bn, group):
        combos = [(b, h) for b in range(b0, min(b0 + group, bn)) for h in range(H_B)]
        res = {}
        for b, h in combos:
            hs = slice(h * DK_B, (h + 1) * DK_B)
            wq = jnp.concatenate([w_ref[b, :, hs], qg_ref[b, :, hs]], axis=0)
            res[b, h] = _dot(wq, s_ref[b, h].astype(BF16))
        v16 = {}
        for b, h in combos:
            hs = slice(h * DV_B, (h + 1) * DV_B)
            v16[b, h] = (u_ref[b, :, hs] - res[b, h][:cc]).astype(BF16)
        for b, h in combos:
            hs = slice(h * DV_B, (h + 1) * DV_B)
            o = res[b, h][cc:] + _dot(aqk_ref[b, :, h * cc:(h + 1) * cc], v16[b, h])
            upd = _dot_tn(kd_ref[b, :, hs], v16[b, h])
            s_ref[b, h] = s_ref[b, h] * gl_ref[b, 0, 0:1, hs] + upd
            o_ref[b, :, hs] = (_rms(o, gn) * _silu(z_ref[b, :, hs])).astype(BF16)

    @pl.when(c == pl.num_programs(0) - 1)
    def _():
        s_out_ref[...] = s_ref[...]


def _gdn_prompt(qkvb, z, small, small_t, conv_w, a_log, dt_bias, gnorm, bn, l):
    rows = GDN_PREP_CHUNKS * GDN_CHUNK
    cc = GDN_CHUNK
    nb = l // rows
    nc = l // cc
    t = bn * l
    ns = small_t.shape[0]
    nq = H_B * DK_B
    pad_col = lambda v: jnp.zeros((ns, 1), F32).at[H_B:2 * H_B, 0].set(v)
    tri = jnp.asarray(_solve_masks(rows, cc))
    tok = lambda n: pl.BlockSpec((rows, n), lambda b, i: (b * nb + i, 0))
    halo = pl.BlockSpec((SUBLANES, B_QKV), lambda b, i: (jnp.maximum((b * nb + i) * (rows // SUBLANES) - 1, 0), 0))
    u, w, qg, kd, aqk, gl = pl.pallas_call(
        _gdn_prep_body,
        grid=(bn, nb),
        in_specs=[
            tok(B_QKV), halo, tok(LANES),
            pl.BlockSpec((ns, rows), lambda b, i: (0, b * nb + i)),
            _whole((CONV_W, B_QKV)), _whole((ns, 1)), _whole((ns, 1)), _whole(tri.shape),
        ],
        out_specs=[tok(nq), tok(nq), tok(nq), tok(nq), tok(H_B * cc),
                   pl.BlockSpec((1, rows // cc, SUBLANES, nq), lambda b, i: (b, i, 0, 0))],
        out_shape=[jax.ShapeDtypeStruct((t, nq), F32)] + [jax.ShapeDtypeStruct((t, nq), BF16)] * 3
        + [jax.ShapeDtypeStruct((t, H_B * cc), BF16), jax.ShapeDtypeStruct((bn, nc, SUBLANES, nq), F32)],
        scratch_shapes=[pltpu.VMEM((rows + SUBLANES, B_QKV), F32)],
        compiler_params=_cp("parallel", "parallel"),
        name="gdn_prep",
    )(qkvb, qkvb, small, small_t, conv_w, pad_col(a_log), pad_col(dt_bias), tri)
    seq = lambda n: pl.BlockSpec((bn, cc, n), lambda c: (0, c, 0))
    r3 = lambda a: a.reshape(bn, l, a.shape[1])
    o, s_new = pl.pallas_call(
        _gdn_scan_body,
        grid=(nc,),
        in_specs=[seq(nq), seq(nq), seq(nq), seq(nq), seq(H_B * cc),
                  pl.BlockSpec((bn, 1, SUBLANES, nq), lambda c: (0, c, 0, 0)), seq(B_Z), _whole((1, DV_B))],
        out_specs=[seq(B_Z), pl.BlockSpec((bn, H_B, DK_B, DV_B), lambda c: (0, 0, 0, 0))],
        out_shape=[jax.ShapeDtypeStruct((bn, l, B_Z), BF16), jax.ShapeDtypeStruct((bn, H_B, DK_B, DV_B), F32)],
        scratch_shapes=[pltpu.VMEM((bn, H_B, DK_B, DV_B), F32)],
        compiler_params=_cp("arbitrary"),
        name="gdn_scan",
    )(r3(u), r3(w), r3(qg), r3(kd), r3(aqk), gl, r3(z), gnorm.reshape(1, DV_B))
    return o.reshape(t, B_Z), s_new


def _conv_step(x, hist_ref, cw_ref, new_ref):
    out = hist_ref[0] * cw_ref[0:1, :]
    for i in range(1, CONV_W - 1):
        out = out + hist_ref[i] * cw_ref[i:i + 1, :]
    out = out + x * cw_ref[CONV_W - 1:CONV_W, :]
    for i in range(CONV_W - 2):
        new_ref[i] = hist_ref[i + 1]
    new_ref[CONV_W - 2] = x
    return out


def _gdn_decode_pre_body(qkv_ref, hist_ref, sm_ref, cw_ref, arow_ref, dtrow_ref,
                         new_ref, w_ref, qg_ref, k_ref, u_ref, qk_ref, gl_ref):
    conv = _silu(_conv_step(qkv_ref[...], hist_ref, cw_ref, new_ref))
    nq = H_B * DK_B
    sm = sm_ref[...]
    beta_all = jax.nn.sigmoid(sm)
    g_all = -jnp.exp(arow_ref[...]) * _softplus(sm + dtrow_ref[...])
    for h in range(H_B):
        hs = slice(h * DK_B, (h + 1) * DK_B)
        qh = _l2norm(conv[:, hs]) * (DK_B ** -0.5)
        kh = _l2norm(conv[:, nq + h * DK_B:nq + (h + 1) * DK_B])
        vh = conv[:, 2 * nq + h * DV_B:2 * nq + (h + 1) * DV_B]
        beta = beta_all[:, h:h + 1]
        eg = jnp.exp(g_all[:, H_B + h:H_B + h + 1])
        w_ref[:, hs] = kh * beta * eg
        qg_ref[:, hs] = qh * eg
        k_ref[:, hs] = kh
        u_ref[:, hs] = vh * beta
        qk_ref[:, hs] = jnp.broadcast_to(jnp.sum(qh * kh, axis=-1, keepdims=True), qh.shape)
        gl_ref[:, hs] = jnp.broadcast_to(eg, qh.shape)


def _gdn_decode_state_body(s_ref, w_ref, qg_ref, k_ref, u_ref, qk_ref, gl_ref, z_ref, gn_ref, done_ref,
                           s_out_ref, o_ref):
    del done_ref
    nb = s_ref.shape[0]
    row = lax.broadcasted_iota(jnp.int32, (nb, 1), 0)
    for h in range(H_B):
        hs = slice(h * DK_B, (h + 1) * DK_B)
        wq = jnp.concatenate([w_ref[:, hs], qg_ref[:, hs]], axis=0).astype(BF16)
        ws = jnp.zeros((nb, DV_B), F32)
        qs = jnp.zeros((nb, DV_B), F32)
        for bb in range(nb):
            res = _dot(wq, s_ref[bb, h].astype(BF16))
            ws = jnp.where(row == bb, res[:nb], ws)
            qs = jnp.where(row == bb, res[nb:], qs)
        v_new = u_ref[:, hs] - ws
        o = qs + qk_ref[:, hs] * v_new
        o_ref[:, hs] = _rms(o, gn_ref[...]) * _silu(z_ref[:, hs])
        k = k_ref[:, hs]
        v16 = v_new.astype(BF16)
        gl = gl_ref[:, hs]
        for bb in range(nb):
            k_one = jnp.where(row == bb, k, 0.0).astype(BF16)
            s_out_ref[bb, h] = s_ref[bb, h] * gl[bb:bb + 1, :] + _dot_tn(k_one, v16)


def _gdn_decode(qkvb, hist, z, small, conv_w, a_log, dt_bias, gnorm, s_all, s_done, e):
    bn = qkvb.shape[0]
    pad_row = lambda v: jnp.zeros((1, LANES), F32).at[0, H_B:2 * H_B].set(v)
    wide = jax.ShapeDtypeStruct((bn, H_B * DK_B), F32)
    new_conv, w, qg, k, u, qk, gl = pl.pallas_call(
        _gdn_decode_pre_body,
        out_shape=[jax.ShapeDtypeStruct(hist.shape, F32)] + [wide] * 6,
        compiler_params=pltpu.CompilerParams(vmem_limit_bytes=VMEM_LIMIT_BYTES),
        name="gdn_decode_pre",
    )(qkvb, hist, small, conv_w, pad_row(a_log), pad_row(dt_bias))
    nb = SUBLANES
    vec = pl.BlockSpec((nb, H_B * DK_B), lambda i: (i, 0))
    st = pl.BlockSpec((None, nb, H_B, DK_B, DV_B), lambda i: (e, i, 0, 0, 0))
    s_done, o = pl.pallas_call(
        _gdn_decode_state_body,
        grid=(bn // nb,),
        in_specs=[st, vec, vec, vec, vec, vec, vec, vec, _whole((1, DV_B)), pl.BlockSpec(memory_space=pl.ANY)],
        out_specs=[st, vec],
        out_shape=[jax.ShapeDtypeStruct(s_all.shape, F32), wide],
        input_output_aliases={9: 0},
        compiler_params=_cp("parallel"),
        name="gdn_decode_state",
    )(s_all, w, qg, k, u, qk, gl, z, gnorm.reshape(1, DV_B), s_done)
    return o, new_conv, s_done


def _head_expander(width):
    e = np.zeros((LANES, H_C * width), np.float32)
    for h in range(H_C):
        e[h, h * width:(h + 1) * width] = 1.0
    return jnp.asarray(np.tile(e, (3, 1)), BF16)


def _group_rms(y, g):
    gs = D_INNER // G_C
    parts = [_rms(y[:, i * gs:(i + 1) * gs], g[:, i * gs:(i + 1) * gs]) for i in range(G_C)]
    return jnp.concatenate(parts, axis=1)


def _ssd_prompt_body(z_ref, xbc_ref, sm_ref, smt_ref, cw_ref, cb_ref, dtrow_ref, arow_ref, dtcol_ref,
                     acol_ref, dx_ref, gn_ref, e_ref, o_ref, s_out_ref, xbuf_ref, s_ref):
    c = pl.program_id(1)
    rows = xbc_ref.shape[0]

    @pl.when(c == 0)
    def _():
        xbuf_ref[0:SUBLANES, :] = jnp.zeros((SUBLANES, xbuf_ref.shape[1]), F32)
        s_ref[...] = jnp.zeros_like(s_ref)

    xbc = _silu(_conv_chunk(xbc_ref, xbuf_ref, cw_ref, rows) + cb_ref[...])
    xs = xbc[:, :D_INNER]
    bm = xbc[:, D_INNER:D_INNER + G_C * N_C]
    cm = xbc[:, D_INNER + G_C * N_C:]
    e = e_ref[...]
    lower = _tri(rows, "lower")
    dt = _softplus(sm_ref[...] + dtrow_ref[...])
    acum = _dot_hi(lower.astype(F32), dt * -jnp.exp(arow_ref[...]))
    a_t = _softplus(smt_ref[...] + dtcol_ref[...]) * -jnp.exp(acol_ref[...])
    acum_t = _dot_hi(a_t, _tri(rows, "upper").astype(F32))
    xdt = xs * _expand(dt, e)
    xdte = (xdt * _expand(jnp.exp(acum[rows - 1:rows, :] - acum), e)).astype(BF16)
    scale_y = _expand(jnp.exp(acum), e)
    chunk_decay = scale_y[rows - 1:rows, :]
    xdt16 = xdt.astype(BF16)
    neg_upper = jnp.where(lower, 0.0, -jnp.inf)
    acum2, acum2_t = acum * LOG2E, acum_t * LOG2E
    gw = HPG * P_C
    ys = []
    for g in range(G_C):
        bg = bm[:, g * N_C:(g + 1) * N_C]
        cg16 = cm[:, g * N_C:(g + 1) * N_C].astype(BF16)
        cb = _dot_nt(cg16, bg.astype(BF16))
        yg = []
        for hh in range(HPG):
            h = g * HPG + hh
            lmat = jnp.exp2(acum2[:, h:h + 1] - acum2_t[h:h + 1, :] + neg_upper)
            yg.append(_dot((cb * lmat).astype(BF16), xdt16[:, h * P_C:(h + 1) * P_C]))
        gs = slice(g * gw, (g + 1) * gw)
        sg = s_ref[:, gs]
        y_off = _dot(cg16, sg.astype(BF16)) * scale_y[:, gs]
        ys.append(jnp.concatenate(yg, axis=1) + y_off)
        s_ref[:, gs] = sg * chunk_decay[:, gs] + _dot(bg.T.astype(BF16), xdte[:, gs])
    y = jnp.concatenate(ys, axis=1) + dx_ref[...] * xs
    y = y * _silu(z_ref[...])
    o_ref[...] = _group_rms(y, gn_ref[...]).astype(BF16)

    @pl.when(c == pl.num_programs(1) - 1)
    def _():
        s_out_ref[0] = s_ref[...].T.reshape(H_C, P_C, N_C)


def _ssd_small_params(dt_bias, a_log, ns):
    row = lambda v: jnp.zeros((1, LANES), F32).at[0, :H_C].set(v)
    col = lambda v: jnp.zeros((ns, 1), F32).at[:H_C, 0].set(v)
    return row(dt_bias), row(a_log), col(dt_bias), col(a_log)


def _ssd_prompt(z, xbc, small, small_t, conv_w, conv_b, dt_bias, a_log, d_skip, gnorm, bn, l):
    rows = SSD_CHUNK
    nc = l // rows
    t = bn * l
    ns = small_t.shape[0]
    dtrow, arow, dtcol, acol = _ssd_small_params(dt_bias, a_log, ns)
    tok = lambda n: pl.BlockSpec((rows, n), lambda b, c: (b * nc + c, 0))
    return pl.pallas_call(
        _ssd_prompt_body,
        grid=(bn, nc),
        in_specs=[
            tok(D_INNER), tok(SSD_CONV_CH), tok(LANES),
            pl.BlockSpec((ns, rows), lambda b, c: (0, b * nc + c)),
            _whole((CONV_W, SSD_CONV_CH)), _whole((1, SSD_CONV_CH)),
            _whole((1, LANES)), _whole((1, LANES)), _whole((ns, 1)), _whole((ns, 1)),
            _whole((1, D_INNER)), _whole((1, D_INNER)), _whole((3 * LANES, D_INNER)),
        ],
        out_specs=[tok(D_INNER), pl.BlockSpec((1, H_C, P_C, N_C), lambda b, c: (b, 0, 0, 0))],
        out_shape=[jax.ShapeDtypeStruct((t, D_INNER), BF16), jax.ShapeDtypeStruct((bn, H_C, P_C, N_C), F32)],
        scratch_shapes=[pltpu.VMEM((rows + SUBLANES, SSD_CONV_CH), F32), pltpu.VMEM((N_C, D_INNER), F32)],
        compiler_params=_cp("arbitrary", "arbitrary"),
        name="ssd_prompt",
    )(z, xbc, small, small_t, conv_w, conv_b.reshape(1, SSD_CONV_CH), dtrow, arow, dtcol, acol,
      jnp.repeat(d_skip, P_C).reshape(1, D_INNER), gnorm.reshape(1, D_INNER), _head_expander(P_C))


def _ssd_decode_pre_body(xbc_ref, hist_ref, sm_ref, cw_ref, cb_ref, dtrow_ref, arow_ref, e_ref, en_ref,
                         new_ref, xs_ref, xdt_ref, b_ref, c_ref, dax_ref, dan_ref):
    xbc = _silu(_conv_step(xbc_ref[...], hist_ref, cw_ref, new_ref) + cb_ref[...])
    xs = xbc[:, :D_INNER]
    dt = _softplus(sm_ref[...] + dtrow_ref[...])
    a = dt * -jnp.exp(arow_ref[...])
    xs_ref[...] = xs
    xdt_ref[...] = xs * _expand(dt, e_ref[...])
    b_ref[...] = xbc[:, D_INNER:D_INNER + G_C * N_C]
    c_ref[...] = xbc[:, D_INNER + G_C * N_C:]
    dax_ref[...] = jnp.exp(_expand(a, e_ref[...]))
    dan_ref[...] = jnp.exp(_expand(a, en_ref[...]))


def _ssd_decode_state_body(s_ref, xdt_ref, b_ref, c_ref, dan_ref, done_ref, s_out_ref, yoff_ref):
    del done_ref
    gw = HPG * P_C
    sub = SUBLANES
    row = lax.broadcasted_iota(jnp.int32, (sub, 1), 0)
    for r0 in range(0, s_ref.shape[0], sub):
        rows = slice(r0, r0 + sub)
        xdt = xdt_ref[rows, :]
        b16 = b_ref[rows, :].astype(BF16)
        c16 = c_ref[rows, :].astype(BF16)
        dan = dan_ref[rows, :]
        yoff = jnp.zeros((sub, gw), F32)
        for bb in range(sub):
            s = s_ref[r0 + bb].reshape(gw, N_C)
            res = _dot_nt(c16, s.astype(BF16))
            yoff = jnp.where(row == bb, res, yoff)
            x_one = jnp.where(row == bb, xdt, 0.0).astype(BF16)
            upd = _dot_tn(x_one, b16)
            for hh in range(HPG):
                rs = slice(hh * P_C, (hh + 1) * P_C)
                s_out_ref[r0 + bb, hh] = s[rs] * dan[bb:bb + 1, hh * N_C:(hh + 1) * N_C] + upd[rs]
        yoff_ref[rows, :] = yoff


def _ssd_decode_post_body(yoff_ref, dax_ref, xdt_ref, xs_ref, b_ref, c_ref, z_ref, dx_ref, gn_ref, o_ref):
    gw = HPG * P_C
    bc = b_ref[...] * c_ref[...]
    cbx = [jnp.broadcast_to(jnp.sum(bc[:, g * N_C:(g + 1) * N_C], axis=-1, keepdims=True), (bc.shape[0], gw))
           for g in range(G_C)]
    y = yoff_ref[...] * dax_ref[...] + jnp.concatenate(cbx, axis=1) * xdt_ref[...]
    y = y + dx_ref[...] * xs_ref[...]
    y = y * _silu(z_ref[...])
    o_ref[...] = _group_rms(y, gn_ref[...])


def _ssd_decode(z, xbc, hist, small, conv_w, conv_b, dt_bias, a_log, d_skip, gnorm, s_all, s_done, e):
    bn = xbc.shape[0]
    dtrow, arow, _, _ = _ssd_small_params(dt_bias, a_log, H_C)
    wide = jax.ShapeDtypeStruct((bn, D_INNER), F32)
    grp = jax.ShapeDtypeStruct((bn, G_C * N_C), F32)
    plain = pltpu.CompilerParams(vmem_limit_bytes=VMEM_LIMIT_BYTES)
    new_conv, xs, xdt, bm, cm, dax, dan = pl.pallas_call(
        _ssd_decode_pre_body,
        out_shape=[jax.ShapeDtypeStruct(hist.shape, F32), wide, wide, grp, grp, wide,
                   jax.ShapeDtypeStruct((bn, H_C * N_C), F32)],
        compiler_params=plain,
        name="ssd_decode_pre",
    )(xbc, hist, small, conv_w, conv_b.reshape(1, SSD_CONV_CH), dtrow, arow,
      _head_expander(P_C), _head_expander(N_C))
    nb = SSD_STATE_SEQS
    gw = HPG * P_C
    st = pl.BlockSpec((None, nb, HPG, P_C, N_C), lambda i, g: (e, i, g, 0, 0))
    s_done, yoff = pl.pallas_call(
        _ssd_decode_state_body,
        grid=(bn // nb, G_C),
        in_specs=[st, pl.BlockSpec((nb, gw), lambda i, g: (i, g)), pl.BlockSpec((nb, N_C), lambda i, g: (i, g)),
                  pl.BlockSpec((nb, N_C), lambda i, g: (i, g)), pl.BlockSpec((nb, HPG * N_C), lambda i, g: (i, g)),
                  pl.BlockSpec(memory_space=pl.ANY)],
        out_specs=[st, pl.BlockSpec((nb, gw), lambda i, g: (i, g))],
        out_shape=[jax.ShapeDtypeStruct(s_all.shape, F32), wide],
        input_output_aliases={5: 0},
        compiler_params=_cp("parallel", "parallel"),
        name="ssd_decode_state",
    )(s_all, xdt, bm, cm, dan, s_done)
    mix = pl.pallas_call(
        _ssd_decode_post_body,
        out_shape=wide,
        compiler_params=plain,
        name="ssd_decode_post",
    )(yoff, dax, xdt, xs, bm, cm, z, jnp.repeat(d_skip, P_C).reshape(1, D_INNER), gnorm.reshape(1, D_INNER))
    return mix, new_conv, s_done


def _narrow_weights(w_small):
    n = w_small.shape[1]
    return jnp.zeros((D_MODEL, LANES), BF16).at[:, :n].set(w_small.astype(BF16))


def _even_q_pad(qa):
    bn = qa.shape[0]
    q = qa.reshape(bn, KV_A, G_A, HD_A)
    out = jnp.zeros((bn, KV_A, G_A, KV_A, HD_A), F32)
    for kv in range(KV_A):
        out = out.at[:, kv, :, kv, :].set(q[:, kv])
    return out.reshape(bn, H_A, KV_A * HD_A)


def _even_o_unpad(o8):
    bn = o8.shape[0]
    o = o8.reshape(bn, KV_A, G_A, KV_A, HD_A)
    return jnp.concatenate([o[:, kv, :, kv, :].reshape(bn, G_A * HD_A) for kv in range(KV_A)], axis=1)


def _trunk(x3, states, wts):
    (rel_bias, norm_ff1, norm_mix, norm_ff2, norm_final,
     ff1, ff2, even_w_in, even_w_out, swa_sinks, gdn_conv_w, gdn_A_log, gdn_dt_bias, gdn_norm,
     ssd_w_in, ssd_w_out, ssd_conv_w, ssd_conv_b, ssd_dt_bias, ssd_A_log, ssd_D, ssd_norm) = wts
    bn, l, d = x3.shape
    t = bn * l
    x = x3.reshape(t, d)
    decode = states is not None
    ks, vs, gconv, gssm, sconv, sssm = [], [], [], [], [], []
    depth = norm_ff1.shape[0]
    width = KV_A * HD_A
    if decode:
        gssm_done, sssm_done = jnp.zeros_like(states[3]), jnp.zeros_like(states[5])
        n_even = states[0].shape[0]
        cache_t = lambda c: c.transpose(0, 1, 3, 4, 2).reshape(n_even, bn, width, WINDOW)
        ckt, cvt = cache_t(states[0]), cache_t(states[1])
        ghist, shist = states[2].transpose(0, 2, 1, 3), states[4].transpose(0, 2, 1, 3)
    for layer in range(depth):
        x = _ffn(x, norm_ff1[layer], ff1, layer)
        e = layer // 2
        if layer % 2 == 0:
            w_all, ws = even_w_in
            qa, kv, qkvb, z, small, small_t = _inproj(x, norm_mix[layer], w_all, e, ws[e], 2 * SUBLANES,
                                                      (A_Q, 2 * width, B_QKV, B_Z))
            if decode:
                o8 = _swa_decode(_even_q_pad(qa), kv, ckt, cvt, e, rel_bias, swa_sinks[e])
                o_a = _even_o_unpad(o8)
                o_b, new_conv, gssm_done = _gdn_decode(qkvb, ghist[e], z, small, gdn_conv_w[e], gdn_A_log[e],
                                                       gdn_dt_bias[e], gdn_norm[e], states[3], gssm_done, e)
                new_k, new_v = kv[:, :width], kv[:, width:]
                new_conv = new_conv.transpose(1, 0, 2)
            else:
                o_a = _swa_prompt(qa, kv, rel_bias, swa_sinks[e], bn, l)
                o_b, s_new = _gdn_prompt(qkvb, z, small, small_t, gdn_conv_w[e], gdn_A_log[e],
                                         gdn_dt_bias[e], gdn_norm[e], bn, l)
                kv3 = kv.reshape(bn, l, 2 * width)
                new_k = kv3[:, l - WINDOW:, :width]
                new_v = kv3[:, l - WINDOW:, width:]
                new_conv = qkvb.reshape(bn, l, B_QKV)[:, l - (CONV_W - 1):]
                gssm.append(s_new)
            ks.append(new_k)
            vs.append(new_v)
            gconv.append(new_conv)
            mixes = [(o_a, even_w_out, e, 0), (o_b, even_w_out, e, 1)]
        else:
            w_all, ws = ssd_w_in
            z, xbc, small, small_t = _inproj(x, norm_mix[layer], w_all, e, ws[e], H_C, (D_INNER, SSD_CONV_CH))
            if decode:
                mix, new_conv, sssm_done = _ssd_decode(z, xbc, shist[e], small, ssd_conv_w[e], ssd_conv_b[e],
                                                       ssd_dt_bias[e], ssd_A_log[e], ssd_D[e], ssd_norm[e],
                                                       states[5], sssm_done, e)
                new_conv = new_conv.transpose(1, 0, 2)
            else:
                mix, s_new = _ssd_prompt(z, xbc, small, small_t, ssd_conv_w[e], ssd_conv_b[e], ssd_dt_bias[e],
                                         ssd_A_log[e], ssd_D[e], ssd_norm[e], bn, l)
                new_conv = xbc.reshape(bn, l, SSD_CONV_CH)[:, l - (CONV_W - 1):]
                sssm.append(s_new)
            sconv.append(new_conv)
            mixes = [(mix, ssd_w_out, e, 0)]
        x = _ffn(x, norm_ff2[layer], ff2, layer, mixes=mixes,
                 g_final=norm_final if layer == depth - 1 else None)
    if decode:
        bs = DECODE_SEQ_BLOCK
        new_t = lambda rows: jnp.stack(rows).reshape(n_even, bn // bs, bs, width).transpose(0, 1, 3, 2)
        back = lambda c: c.reshape(n_even, bn, KV_A, HD_A, WINDOW).transpose(0, 1, 4, 2, 3)
        ks = back(_cache_shift(ckt, new_t(ks)))
        vs = back(_cache_shift(cvt, new_t(vs)))
        gssm, sssm = gssm_done, sssm_done
    else:
        kv5 = (len(ks), bn, WINDOW, KV_A, HD_A)
        ks, vs = jnp.stack(ks).reshape(kv5), jnp.stack(vs).reshape(kv5)
        gssm, sssm = jnp.stack(gssm), jnp.stack(sssm)
    return (x.reshape(bn, l, d), ks, vs, jnp.stack(gconv), gssm, jnp.stack(sconv), sssm)


def kernel(x_prompt, x_sample, cache_swa_k, cache_swa_v, state_gdn_conv, state_gdn_ssm, state_ssd_conv, state_ssd_ssm, rel_bias, norm_ff1, norm_mix, norm_ff2, norm_final, ff1_gate, ff1_up, ff1_down, ff2_gate, ff2_up, ff2_down, even_w_in, even_w_out, swa_sinks, gdn_conv_w, gdn_A_log, gdn_dt_bias, gdn_norm, ssd_w_in, ssd_w_out, ssd_conv_w, ssd_conv_b, ssd_dt_bias, ssd_A_log, ssd_D, ssd_norm):
    depth = norm_ff1.shape[0]
    ff1 = (ff1_gate.astype(BF16), ff1_up.astype(BF16), ff1_down.astype(BF16))
    ff2 = (ff2_gate.astype(BF16), ff2_up.astype(BF16), ff2_down.astype(BF16))
    n_even_main = A_Q + 2 * KV_A * HD_A + B_QKV + B_Z
    even_in = (even_w_in.astype(BF16),
               [_narrow_weights(even_w_in[e][:, n_even_main:]) for e in range(even_w_in.shape[0])])
    n_odd_main = D_INNER + SSD_CONV_CH
    odd_in = (ssd_w_in.astype(BF16),
              [_narrow_weights(ssd_w_in[e][:, n_odd_main:]) for e in range(ssd_w_in.shape[0])])
    wts = (rel_bias, norm_ff1, norm_mix, norm_ff2, norm_final, ff1, ff2,
           even_in, even_w_out.astype(BF16), swa_sinks, gdn_conv_w, gdn_A_log, gdn_dt_bias, gdn_norm,
           odd_in, ssd_w_out.astype(BF16), ssd_conv_w, ssd_conv_b, ssd_dt_bias, ssd_A_log, ssd_D, ssd_norm)
    y_p, p_k, p_v, p_gconv, p_gssm, p_sconv, p_sssm = _trunk(x_prompt, None, wts)
    states = (cache_swa_k, cache_swa_v, state_gdn_conv, state_gdn_ssm, state_ssd_conv, state_ssd_ssm)
    y_s, s_k, s_v, s_gconv, s_gssm, s_sconv, s_sssm = _trunk(x_sample, states, wts)
    return (y_p, y_s, p_k, p_v, p_gconv, p_gssm, p_sconv, p_sssm,
            s_k, s_v, s_gconv, s_gssm, s_sconv, s_sssm)
```

```python
import functools
import math

import numpy as np
import jax
import jax.numpy as jnp
from jax import lax
from jax.experimental import pallas as pl
from jax.experimental.pallas import tpu as pltpu

F32 = jnp.float32
BF16 = jnp.bfloat16
HI = lax.Precision.HIGHEST

EPS = 1e-6
NEG_INF = -1e30
LOG2E = math.log2(math.e)
D_MODEL = 1024
WINDOW = 128
BLOCK = 128
H_A, KV_A, G_A, HD_A = 8, 2, 4, 64
N_BUCKETS, MAX_DIST = 32, 128
H_B, DK_B, DV_B = 4, 128, 128
CONV_W = 4
GDN_CHUNK = 64
D_INNER = 2048
P_C, H_C, N_C, G_C = 64, 32, 128, 4
HPG = H_C // G_C
SSD_CHUNK = 128
A_Q = H_A * HD_A
B_QKV = 3 * H_B * DK_B
B_Z = H_B * DV_B
SSD_CONV_CH = D_INNER + 2 * G_C * N_C
LANES = 128
SUBLANES = 8
VMEM_LIMIT_BYTES = 56 * 1024 * 1024
ROW_TILE = 512
SWA_QUERY_BLOCKS = 4
GDN_PREP_CHUNKS = 4
GDN_SCAN_GROUP = 2
DECODE_SEQ_BLOCK = 32
SSD_STATE_SEQS = 4 * SUBLANES
GDN_STATE_SEQS = 2 * SUBLANES


def _cp(*sem):
    return pltpu.CompilerParams(dimension_semantics=sem, vmem_limit_bytes=VMEM_LIMIT_BYTES)


def _whole(shape):
    nd = len(shape)
    return pl.BlockSpec(shape, lambda *_: (0,) * nd, pipeline_mode=pl.Buffered(1))


def _rms(x, g):
    return x * lax.rsqrt(jnp.mean(x * x, axis=-1, keepdims=True) + EPS) * g


def _silu(x):
    return x * jax.nn.sigmoid(x)


def _softplus(x):
    return jnp.maximum(x, 0.0) + jnp.log1p(jnp.exp(-jnp.abs(x)))


def _dot(a, b):
    return jnp.dot(a, b, preferred_element_type=F32)


def _dot_nt(a, b):
    return lax.dot_general(a, b, (((1,), (1,)), ((), ())), preferred_element_type=F32)


def _dot_tn(a, b):
    return lax.dot_general(a, b, (((0,), (0,)), ((), ())), preferred_element_type=F32)


def _dot_hi(a, b):
    return jnp.dot(a, b, precision=HI, preferred_element_type=F32)


def _expand(x, e3):
    hi = x.astype(BF16)
    r = x - hi.astype(F32)
    mid = r.astype(BF16)
    lo = (r - mid.astype(F32)).astype(BF16)
    return _dot(jnp.concatenate([hi, mid, lo], axis=1), e3)


def _tri(n, kind):
    r = lax.broadcasted_iota(jnp.int32, (n, n), 0)
    c = lax.broadcasted_iota(jnp.int32, (n, n), 1)
    return {"lower": r >= c, "strict_lower": r > c, "upper": r <= c}[kind]


def _ffn_body(*refs, n_mix, final):
    refs = list(refs)
    x = refs.pop(0)[...]
    for _ in range(n_mix):
        m_ref, wo_ref = refs.pop(0), refs.pop(0)
        x = x + _dot(m_ref[...].astype(BF16), wo_ref[...])
    g_ref, wg_ref, wu_ref, wd_ref = refs[:4]
    o_ref = refs[-1]
    hn = _rms(x, g_ref[...]).astype(BF16)
    act = (_silu(_dot(hn, wg_ref[...])) * _dot(hn, wu_ref[...])).astype(BF16)
    y = x + 0.5 * _dot(act, wd_ref[...])
    if final:
        y = _rms(y, refs[4][...])
    o_ref[...] = y


def _layer_slab(w_all, layer, rows=None, row_block=0):
    shape = (rows or w_all.shape[1], w_all.shape[2])
    return pl.BlockSpec((None,) + shape, lambda i: (layer, row_block, 0), pipeline_mode=pl.Buffered(1))


def _ffn(x, g, ffw, layer, mixes=(), g_final=None):
    t, d = x.shape
    tm = min(t, ROW_TILE)
    final = g_final is not None
    row = pl.BlockSpec((tm, d), lambda i: (i, 0))
    in_specs, args = [row], [x]
    for m, w_all, e, row_block in mixes:
        in_specs += [pl.BlockSpec((tm, m.shape[1]), lambda i: (i, 0)), _layer_slab(w_all, e, m.shape[1], row_block)]
        args += [m, w_all]
    in_specs += [_whole((1, d))] + [_layer_slab(w, layer) for w in ffw]
    args += [g.reshape(1, d), *ffw]
    if final:
        in_specs.append(_whole((1, d)))
        args.append(g_final.reshape(1, d))
    return pl.pallas_call(
        functools.partial(_ffn_body, n_mix=len(mixes), final=final),
        grid=(t // tm,),
        in_specs=in_specs,
        out_specs=row,
        out_shape=jax.ShapeDtypeStruct((t, d), F32),
        compiler_params=_cp("parallel"),
        name="ffn",
    )(*args)


def _inproj_body(x_ref, g_ref, w_ref, ws_ref, *outs, splits):
    hn = _rms(x_ref[...], g_ref[...]).astype(BF16)
    off = 0
    for o_ref, n in zip(outs[:-2], splits):
        o_ref[...] = _dot(hn, w_ref[:, off:off + n])
        off += n
    small = _dot(hn, ws_ref[...])
    outs[-2][...] = small
    outs[-1][...] = small.T[:outs[-1].shape[0]]


def _inproj(x, g, w_all, e, ws, ns, splits):
    t, d = x.shape
    w_spec = _layer_slab(w_all, e)
    tm = min(t, ROW_TILE)
    row = pl.BlockSpec((tm, d), lambda i: (i, 0))
    out_specs = [pl.BlockSpec((tm, n), lambda i: (i, 0)) for n in splits]
    out_specs += [pl.BlockSpec((tm, LANES), lambda i: (i, 0)), pl.BlockSpec((ns, tm), lambda i: (0, i))]
    out_shape = [jax.ShapeDtypeStruct((t, n), F32) for n in splits]
    out_shape += [jax.ShapeDtypeStruct((t, LANES), F32), jax.ShapeDtypeStruct((ns, t), F32)]
    return pl.pallas_call(
        functools.partial(_inproj_body, splits=splits),
        grid=(t // tm,),
        in_specs=[row, _whole((1, d)), w_spec, _whole(ws.shape)],
        out_specs=out_specs,
        out_shape=out_shape,
        compiler_params=_cp("parallel"),
        name="inproj",
    )(x, g.reshape(1, d), w_all, ws)


def _t5_bucket_np(dist):
    max_exact = N_BUCKETS // 2
    df = np.maximum(dist, max_exact).astype(np.float32)
    large = max_exact + (np.log(df / np.float32(max_exact)) / np.float32(math.log(MAX_DIST / max_exact))
                         * np.float32(N_BUCKETS - max_exact)).astype(np.int32)
    return np.where(dist < max_exact, dist, np.minimum(large, N_BUCKETS - 1)).astype(np.int32)


def _band_bucket_ids(n_q, n_k, offset):
    d = offset + np.arange(n_q)[:, None] - np.arange(n_k)[None, :]
    valid = (d >= 0) & (d <= WINDOW)
    return np.where(valid, _t5_bucket_np(np.clip(d, 0, WINDOW)), -1).astype(np.int32)


def _bias_from_buckets(bid, rb_ref, h):
    acc = jnp.full(bid.shape, NEG_INF, F32)
    for bk in range(N_BUCKETS):
        acc = jnp.where(bid == bk, rb_ref[bk, h], acc)
    return acc


def _swa_prompt_body(bid_ref, rb_ref, sk_ref, q_ref, kvp_ref, kvc_ref, o_ref, bias_ref):
    first_step = (pl.program_id(0) == 0) & (pl.program_id(1) == 0)

    @pl.when(first_step)
    def _():
        bid = bid_ref[...]
        col = lax.broadcasted_iota(jnp.int32, (BLOCK, 2 * BLOCK), 1)
        for h in range(H_A):
            bias = _bias_from_buckets(bid, rb_ref, h)
            bias_ref[h] = bias
            bias_ref[H_A + h] = jnp.where(col < BLOCK, NEG_INF, bias)

    kvp = kvp_ref[...]
    kvc = kvc_ref[...]
    scale = HD_A ** -0.5
    heads = range(H_A)
    for j in range(q_ref.shape[0] // BLOCK):
        rows = slice(j * BLOCK, (j + 1) * BLOCK)
        prev = kvp if j == 0 else kvc[(j - 1) * BLOCK:j * BLOCK]
        first_block = jnp.where(pl.program_id(1) == 0, H_A, 0) if j == 0 else 0
        k, v = [], []
        for kv in range(KV_A):
            ks = slice(kv * HD_A, (kv + 1) * HD_A)
            vs = slice(KV_A * HD_A + kv * HD_A, KV_A * HD_A + (kv + 1) * HD_A)
            k.append(jnp.concatenate([prev[:, ks], kvc[rows, ks]], axis=0).astype(BF16))
            v.append(jnp.concatenate([prev[:, vs], kvc[rows, vs]], axis=0).astype(BF16))
        s = [_dot_nt((q_ref[rows, h * HD_A:(h + 1) * HD_A] * scale).astype(BF16), k[h // G_A])
             + bias_ref[first_block + h] for h in heads]
        m = [jnp.maximum(jnp.max(s[h], axis=-1, keepdims=True), sk_ref[h]) for h in heads]
        p = [jnp.exp(s[h] - m[h]) for h in heads]
        denom = [jnp.sum(p[h], axis=-1, keepdims=True) + jnp.exp(sk_ref[h] - m[h]) for h in heads]
        outs = [_dot(p[h].astype(BF16), v[h // G_A]) / denom[h] for h in heads]
        o_ref[rows, :] = jnp.concatenate(outs, axis=1).astype(BF16)


def _swa_prompt(qa, kv, rel_bias, sinks, bn, l):
    qb = SWA_QUERY_BLOCKS
    nb = l // (qb * BLOCK)
    t = bn * l
    bid = jnp.asarray(_band_bucket_ids(BLOCK, 2 * BLOCK, BLOCK))
    width = 2 * KV_A * HD_A
    return pl.pallas_call(
        _swa_prompt_body,
        grid=(bn, nb),
        in_specs=[
            _whole((BLOCK, 2 * BLOCK)),
            pl.BlockSpec(memory_space=pltpu.SMEM),
            pl.BlockSpec(memory_space=pltpu.SMEM),
            pl.BlockSpec((qb * BLOCK, A_Q), lambda b, i: (b * nb + i, 0)),
            pl.BlockSpec((BLOCK, width), lambda b, i: (jnp.maximum((b * nb + i) * qb - 1, 0), 0)),
            pl.BlockSpec((qb * BLOCK, width), lambda b, i: (b * nb + i, 0)),
        ],
        out_specs=pl.BlockSpec((qb * BLOCK, A_Q), lambda b, i: (b * nb + i, 0)),
        out_shape=jax.ShapeDtypeStruct((t, A_Q), BF16),
        scratch_shapes=[pltpu.VMEM((2 * H_A, BLOCK, 2 * BLOCK), F32)],
        compiler_params=_cp("arbitrary", "arbitrary"),
        name="swa_prompt",
    )(bid, rel_bias, sinks, qa, kv, kv)


def _swa_decode_body(bid_ref, rb_ref, sk_ref, q_ref, kvn_ref, ckt_ref, cvt_ref, o_ref):
    bid = bid_ref[...]
    row = lax.broadcasted_iota(jnp.int32, (H_A, 1), 0)
    bias = jnp.zeros((H_A, bid.shape[1]), F32)
    sk = jnp.zeros((H_A, 1), F32)
    for h in range(H_A):
        bias = jnp.where(row == h, _bias_from_buckets(bid, rb_ref, h), bias)
        sk = jnp.where(row == h, sk_ref[h], sk)
    bias_c = bias[:, :WINDOW]
    bias_n = bias[:, WINDOW:WINDOW + 1]
    scale = HD_A ** -0.5
    q = q_ref[...]
    kvn = kvn_ref[...]
    width = KV_A * HD_A
    k_new = kvn[:, None, :width]
    v_new = kvn[:, None, width:]
    s = lax.dot_general(q.astype(BF16), ckt_ref[...].astype(BF16), (((2,), (1,)), ((0,), (0,))),
                        preferred_element_type=F32) * scale + bias_c[None]
    s_n = jnp.sum(q * k_new, axis=-1, keepdims=True) * scale + bias_n[None]
    m = jnp.maximum(jnp.maximum(jnp.max(s, axis=-1, keepdims=True), s_n), sk[None])
    p = jnp.exp(s - m)
    p_n = jnp.exp(s_n - m)
    denom = jnp.sum(p, axis=-1, keepdims=True) + p_n + jnp.exp(sk[None] - m)
    o = lax.dot_general((p / denom).astype(BF16), cvt_ref[...].astype(BF16), (((2,), (2,)), ((0,), (0,))),
                        preferred_element_type=F32)
    o_ref[...] = o + (p_n / denom) * v_new


def _swa_decode(q_pad, kv_new, cache_kt, cache_vt, e, rel_bias, sinks):
    bn = q_pad.shape[0]
    bs = DECODE_SEQ_BLOCK
    width = KV_A * HD_A
    ids = _band_bucket_ids(1, WINDOW + 1, WINDOW)
    bid = np.full((1, WINDOW + LANES), -1, np.int32)
    bid[:, :WINDOW + 1] = ids
    return pl.pallas_call(
        _swa_decode_body,
        grid=(bn // bs,),
        in_specs=[
            _whole((1, WINDOW + LANES)),
            pl.BlockSpec(memory_space=pltpu.SMEM),
            pl.BlockSpec(memory_space=pltpu.SMEM),
            pl.BlockSpec((bs, H_A, width), lambda i: (i, 0, 0)),
            pl.BlockSpec((bs, 2 * width), lambda i: (i, 0)),
            pl.BlockSpec((None, bs, width, WINDOW), lambda i: (e, i, 0, 0)),
            pl.BlockSpec((None, bs, width, WINDOW), lambda i: (e, i, 0, 0)),
        ],
        out_specs=pl.BlockSpec((bs, H_A, width), lambda i: (i, 0, 0)),
        out_shape=jax.ShapeDtypeStruct((bn, H_A, width), F32),
        compiler_params=_cp("parallel"),
        name="swa_decode",
    )(jnp.asarray(bid), rel_bias, sinks, q_pad, kv_new, cache_kt, cache_vt)


def _cache_shift_body(c_ref, nt_ref, o_ref):
    o_ref[...] = pltpu.roll(c_ref[...], WINDOW - 1, axis=2)
    nt = nt_ref[...]
    for b in range(c_ref.shape[0]):
        o_ref[b, :, WINDOW - 1:WINDOW] = nt[:, b:b + 1]


def _cache_shift(cache_t, new_t):
    n, bn, width, win = cache_t.shape
    bs = new_t.shape[3]
    blk = pl.BlockSpec((None, bs, width, win), lambda e, i: (e, i, 0, 0))
    return pl.pallas_call(
        _cache_shift_body,
        grid=(n, bn // bs),
        in_specs=[blk, pl.BlockSpec((None, None, width, bs), lambda e, i: (e, i, 0, 0))],
        out_specs=blk,
        out_shape=jax.ShapeDtypeStruct(cache_t.shape, F32),
        compiler_params=_cp("parallel", "parallel"),
        name="cache_shift",
    )(cache_t, new_t)


def _conv_rows(x_ref, xbuf_ref, cw_ref, rows):
    xbuf_ref[SUBLANES:SUBLANES + rows, :] = x_ref[...]
    x = xbuf_ref[...]
    out = x * cw_ref[0:1, :]
    for i in range(1, CONV_W):
        out = pltpu.roll(out, 1, axis=0) + x * cw_ref[i:i + 1, :]
    return out[SUBLANES:]


def _conv_chunk(x_ref, xbuf_ref, cw_ref, rows):
    out = _conv_rows(x_ref, xbuf_ref, cw_ref, rows)
    xbuf_ref[0:SUBLANES, :] = xbuf_ref[rows:rows + SUBLANES, :]
    return out


def _l2norm(x):
    return x * lax.rsqrt(jnp.sum(x * x, axis=-1, keepdims=True) + EPS)


def _chunk_masks(n, cc):
    r = lax.broadcasted_iota(jnp.int32, (n, n), 0)
    c = lax.broadcasted_iota(jnp.int32, (n, n), 1)
    shift = int(math.log2(cc))
    same = lax.shift_right_logical(r, shift) == lax.shift_right_logical(c, shift)
    return same, same & (r >= c), same & (r <= c)


def _dot_bf16x3(a, b):
    a_hi = a.astype(BF16)
    a_lo = (a - a_hi.astype(F32)).astype(BF16)
    b_hi = b.astype(BF16)
    b_lo = (b - b_hi.astype(F32)).astype(BF16)
    return _dot(a_hi, b_hi) + _dot(a_hi, b_lo) + _dot(a_lo, b_hi)


def _solve_masks(n, cc, base=SUBLANES):
    r = np.arange(n)[:, None]
    c = np.arange(n)[None, :]
    masks = [-(((r // base) == (c // base)) & (r > c)).astype(np.float32)]
    s = base
    while s < cc:
        sibling = ((r // (2 * s)) == (c // (2 * s))) & ((r // s) % 2 == 1) & ((c // s) % 2 == 0)
        masks.append(sibling.astype(np.float32))
        s *= 2
    return np.stack(masks)


def _gdn_prep_body(qkv_ref, halo_ref, sm_ref, smt_ref, cw_ref, acol_ref, dtcol_ref, tri_ref,
                   u_ref, w_ref, qg_ref, kd_ref, aqk_ref, gl_ref, xbuf_ref):
    rows = qkv_ref.shape[0]
    cc = GDN_CHUNK
    xbuf_ref[0:SUBLANES, :] = jnp.where(pl.program_id(1) == 0, 0.0, halo_ref[...])
    conv = _silu(_conv_rows(qkv_ref, xbuf_ref, cw_ref, rows))
    nq = H_B * DK_B
    beta_all = jax.nn.sigmoid(sm_ref[...])
    gt_all = -jnp.exp(acol_ref[...]) * _softplus(smt_ref[...] + dtcol_ref[...])
    pad = jnp.zeros((LANES - gt_all.shape[0], rows), F32)
    g_all = jnp.concatenate([gt_all, pad], axis=0).T
    same, lower, upper = _chunk_masks(rows, cc)
    gc = _dot_hi(lower.astype(F32), g_all)
    gct = _dot_hi(gt_all, upper.astype(F32))
    gsum = _dot_hi(same.astype(F32), g_all)
    neg_outside = jnp.where(lower, 0.0, -jnp.inf)
    gc2, gct2 = gc * LOG2E, gct * LOG2E
    heads = range(H_B)
    qh, kh16, kb, decay, egc = [], [], [], [], []
    for h in heads:
        q = _l2norm(conv[:, h * DK_B:(h + 1) * DK_B]) * (DK_B ** -0.5)
        k = _l2norm(conv[:, nq + h * DK_B:nq + (h + 1) * DK_B])
        v = conv[:, 2 * nq + h * DV_B:2 * nq + (h + 1) * DV_B]
        beta = beta_all[:, h:h + 1]
        gcol = gc[:, H_B + h:H_B + h + 1]
        decay.append(jnp.exp2(gc2[:, H_B + h:H_B + h + 1] - gct2[H_B + h:H_B + h + 1, :] + neg_outside))
        eg = jnp.exp(gcol)
        qg_ref[:, h * DK_B:(h + 1) * DK_B] = (q * eg).astype(BF16)
        kd_ref[:, h * DK_B:(h + 1) * DK_B] = (k * jnp.exp(gsum[:, H_B + h:H_B + h + 1] - gcol)).astype(BF16)
        qh.append(q.astype(BF16))
        kh16.append(k.astype(BF16))
        kb.append(k * beta)
        egc.append((v * beta, eg))
    m = [_dot_nt(kb[h].astype(BF16), kh16[h]) * decay[h] for h in heads]
    for h in heads:
        aqk = _dot_nt(qh[h], kh16[h]) * decay[h]
        blocks = [aqk[i * cc:(i + 1) * cc, i * cc:(i + 1) * cc] for i in range(rows // cc)]
        aqk_ref[:, h * cc:(h + 1) * cc] = jnp.concatenate(blocks, axis=0).astype(BF16)
    mm = lambda xs, ys: [_dot(x.astype(BF16), y.astype(BF16)) for x, y in zip(xs, ys)]
    a = [m[h] * tri_ref[0] for h in heads]
    a2 = mm(a, a)
    a3 = mm(a, a2)
    a4 = mm(a2, a2)
    x1 = [a[h] + a2[h] + a3[h] for h in heads]
    x1a4 = mm(x1, a4)
    r = [x1[h] + a4[h] + x1a4[h] for h in heads]
    for level in range(1, tri_ref.shape[0]):
        c = [m[h] * tri_ref[level] for h in heads]
        rc = mm(r, c)
        y = [c[h] + rc[h] for h in heads]
        yr = mm(y, r)
        r = [r[h] - y[h] - yr[h] for h in heads]
    for h in heads:
        vb, eg = egc[h]
        rhs = jnp.concatenate([vb, kb[h] * eg], axis=1)
        sol = rhs + _dot_bf16x3(r[h], rhs)
        u_ref[:, h * DV_B:(h + 1) * DV_B] = sol[:, :DV_B]
        w_ref[:, h * DK_B:(h + 1) * DK_B] = sol[:, DV_B:].astype(BF16)
    for i in range(rows // cc):
        gl = [jnp.broadcast_to(jnp.exp(gsum[i * cc:i * cc + 1, H_B + h:H_B + h + 1]), (SUBLANES, DV_B))
              for h in heads]
        gl_ref[0, i] = jnp.concatenate(gl, axis=1)


def _gdn_scan_body(u_ref, w_ref, qg_ref, kd_ref, aqk_ref, gl_ref, z_ref, gn_ref, o_ref, s_out_ref, s_ref):
    c = pl.program_id(0)
    bn = u_ref.shape[0]
    cc = GDN_CHUNK

    @pl.when(c == 0)
    def _():
        s_ref[...] = jnp.zeros_like(s_ref)

    gn = gn_ref[...]
    group = GDN_SCAN_GROUP
    for b0 in range(0, bn, group):
        combos = [(b, h) for b in range(b0, min(b0 + group, bn)) for h in range(H_B)]
        res = {}
        for b, h in combos:
            hs = slice(h * DK_B, (h + 1) * DK_B)
            wq = jnp.concatenate([w_ref[b, :, hs], qg_ref[b, :, hs]], axis=0)
            res[b, h] = _dot(wq, s_ref[b, h].astype(BF16))
        v16 = {}
        for b, h in combos:
            hs = slice(h * DV_B, (h + 1) * DV_B)
            v16[b, h] = (u_ref[b, :, hs] - res[b, h][:cc]).astype(BF16)
        for b, h in combos:
            hs = slice(h * DV_B, (h + 1) * DV_B)
            o = res[b, h][cc:] + _dot(aqk_ref[b, :, h * cc:(h + 1) * cc], v16[b, h])
            upd = _dot_tn(kd_ref[b, :, hs], v16[b, h])
            s_ref[b, h] = s_ref[b, h] * gl_ref[b, 0, 0:1, hs] + upd
            o_ref[b, :, hs] = (_rms(o, gn) * _silu(z_ref[b, :, hs])).astype(BF16)

    @pl.when(c == pl.num_programs(0) - 1)
    def _():
        s_out_ref[...] = s_ref[...]


def _gdn_prompt(qkvb, z, small, small_t, conv_w, a_log, dt_bias, gnorm, bn, l):
    rows = GDN_PREP_CHUNKS * GDN_CHUNK
    cc = GDN_CHUNK
    nb = l // rows
    nc = l // cc
    t = bn * l
    ns = small_t.shape[0]
    nq = H_B * DK_B
    pad_col = lambda v: jnp.zeros((ns, 1), F32).at[H_B:2 * H_B, 0].set(v)
    tri = jnp.asarray(_solve_masks(rows, cc))
    tok = lambda n: pl.BlockSpec((rows, n), lambda b, i: (b * nb + i, 0))
    halo = pl.BlockSpec((SUBLANES, B_QKV), lambda b, i: (jnp.maximum((b * nb + i) * (rows // SUBLANES) - 1, 0), 0))
    u, w, qg, kd, aqk, gl = pl.pallas_call(
        _gdn_prep_body,
        grid=(bn, nb),
        in_specs=[
            tok(B_QKV), halo, tok(LANES),
            pl.BlockSpec((ns, rows), lambda b, i: (0, b * nb + i)),
            _whole((CONV_W, B_QKV)), _whole((ns, 1)), _whole((ns, 1)), _whole(tri.shape),
        ],
        out_specs=[tok(nq), tok(nq), tok(nq), tok(nq), tok(H_B * cc),
                   pl.BlockSpec((1, rows // cc, SUBLANES, nq), lambda b, i: (b, i, 0, 0))],
        out_shape=[jax.ShapeDtypeStruct((t, nq), F32)] + [jax.ShapeDtypeStruct((t, nq), BF16)] * 3
        + [jax.ShapeDtypeStruct((t, H_B * cc), BF16), jax.ShapeDtypeStruct((bn, nc, SUBLANES, nq), F32)],
        scratch_shapes=[pltpu.VMEM((rows + SUBLANES, B_QKV), F32)],
        compiler_params=_cp("parallel", "parallel"),
        name="gdn_prep",
    )(qkvb, qkvb, small, small_t, conv_w, pad_col(a_log), pad_col(dt_bias), tri)
    seq = lambda n: pl.BlockSpec((bn, cc, n), lambda c: (0, c, 0))
    r3 = lambda a: a.reshape(bn, l, a.shape[1])
    o, s_new = pl.pallas_call(
        _gdn_scan_body,
        grid=(nc,),
        in_specs=[seq(nq), seq(nq), seq(nq), seq(nq), seq(H_B * cc),
                  pl.BlockSpec((bn, 1, SUBLANES, nq), lambda c: (0, c, 0, 0)), seq(B_Z), _whole((1, DV_B))],
        out_specs=[seq(B_Z), pl.BlockSpec((bn, H_B, DK_B, DV_B), lambda c: (0, 0, 0, 0))],
        out_shape=[jax.ShapeDtypeStruct((bn, l, B_Z), BF16), jax.ShapeDtypeStruct((bn, H_B, DK_B, DV_B), F32)],
        scratch_shapes=[pltpu.VMEM((bn, H_B, DK_B, DV_B), F32)],
        compiler_params=_cp("arbitrary"),
        name="gdn_scan",
    )(r3(u), r3(w), r3(qg), r3(kd), r3(aqk), gl, r3(z), gnorm.reshape(1, DV_B))
    return o.reshape(t, B_Z), s_new


def _conv_step(x, hist_ref, cw_ref, new_ref):
    out = hist_ref[0] * cw_ref[0:1, :]
    for i in range(1, CONV_W - 1):
        out = out + hist_ref[i] * cw_ref[i:i + 1, :]
    out = out + x * cw_ref[CONV_W - 1:CONV_W, :]
    for i in range(CONV_W - 2):
        new_ref[i] = hist_ref[i + 1]
    new_ref[CONV_W - 2] = x
    return out


def _gdn_decode_pre_body(qkv_ref, hist_ref, sm_ref, cw_ref, arow_ref, dtrow_ref,
                         new_ref, w_ref, qg_ref, k_ref, u_ref, qk_ref, gl_ref):
    conv = _silu(_conv_step(qkv_ref[...], hist_ref, cw_ref, new_ref))
    nq = H_B * DK_B
    sm = sm_ref[...]
    beta_all = jax.nn.sigmoid(sm)
    g_all = -jnp.exp(arow_ref[...]) * _softplus(sm + dtrow_ref[...])
    for h in range(H_B):
        hs = slice(h * DK_B, (h + 1) * DK_B)
        qh = _l2norm(conv[:, hs]) * (DK_B ** -0.5)
        kh = _l2norm(conv[:, nq + h * DK_B:nq + (h + 1) * DK_B])
        vh = conv[:, 2 * nq + h * DV_B:2 * nq + (h + 1) * DV_B]
        beta = beta_all[:, h:h + 1]
        eg = jnp.exp(g_all[:, H_B + h:H_B + h + 1])
        w_ref[:, hs] = kh * beta * eg
        qg_ref[:, hs] = qh * eg
        k_ref[:, hs] = kh
        u_ref[:, hs] = vh * beta
        qk_ref[:, hs] = jnp.broadcast_to(jnp.sum(qh * kh, axis=-1, keepdims=True), qh.shape)
        gl_ref[:, hs] = jnp.broadcast_to(eg, qh.shape)


def _gdn_decode_state_body(s_ref, w_ref, qg_ref, k_ref, u_ref, qk_ref, gl_ref, z_ref, gn_ref, done_ref,
                           s_out_ref, o_ref):
    del done_ref
    sub = SUBLANES
    row = lax.broadcasted_iota(jnp.int32, (sub, 1), 0)
    for r0 in range(0, s_ref.shape[0], sub):
        rows = slice(r0, r0 + sub)
        for h in range(H_B):
            hs = slice(h * DK_B, (h + 1) * DK_B)
            wq = jnp.concatenate([w_ref[rows, hs], qg_ref[rows, hs]], axis=0).astype(BF16)
            ws = jnp.zeros((sub, DV_B), F32)
            qs = jnp.zeros((sub, DV_B), F32)
            for bb in range(sub):
                res = _dot(wq, s_ref[r0 + bb, h].astype(BF16))
                ws = jnp.where(row == bb, res[:sub], ws)
                qs = jnp.where(row == bb, res[sub:], qs)
            v_new = u_ref[rows, hs] - ws
            o = qs + qk_ref[rows, hs] * v_new
            o_ref[rows, hs] = _rms(o, gn_ref[...]) * _silu(z_ref[rows, hs])
            k = k_ref[rows, hs]
            v16 = v_new.astype(BF16)
            gl = gl_ref[rows, hs]
            for bb in range(sub):
                k_one = jnp.where(row == bb, k, 0.0).astype(BF16)
                s_out_ref[r0 + bb, h] = s_ref[r0 + bb, h] * gl[bb:bb + 1, :] + _dot_tn(k_one, v16)


def _gdn_decode(qkvb, hist, z, small, conv_w, a_log, dt_bias, gnorm, s_all, s_done, e):
    bn = qkvb.shape[0]
    pad_row = lambda v: jnp.zeros((1, LANES), F32).at[0, H_B:2 * H_B].set(v)
    wide = jax.ShapeDtypeStruct((bn, H_B * DK_B), F32)
    new_conv, w, qg, k, u, qk, gl = pl.pallas_call(
        _gdn_decode_pre_body,
        out_shape=[jax.ShapeDtypeStruct(hist.shape, F32)] + [wide] * 6,
        compiler_params=pltpu.CompilerParams(vmem_limit_bytes=VMEM_LIMIT_BYTES),
        name="gdn_decode_pre",
    )(qkvb, hist, small, conv_w, pad_row(a_log), pad_row(dt_bias))
    nb = GDN_STATE_SEQS
    vec = pl.BlockSpec((nb, H_B * DK_B), lambda i: (i, 0))
    st = pl.BlockSpec((None, nb, H_B, DK_B, DV_B), lambda i: (e, i, 0, 0, 0))
    s_done, o = pl.pallas_call(
        _gdn_decode_state_body,
        grid=(bn // nb,),
        in_specs=[st, vec, vec, vec, vec, vec, vec, vec, _whole((1, DV_B)), pl.BlockSpec(memory_space=pl.ANY)],
        out_specs=[st, vec],
        out_shape=[jax.ShapeDtypeStruct(s_all.shape, F32), wide],
        input_output_aliases={9: 0},
        compiler_params=_cp("parallel"),
        name="gdn_decode_state",
    )(s_all, w, qg, k, u, qk, gl, z, gnorm.reshape(1, DV_B), s_done)
    return o, new_conv, s_done


def _head_expander(width):
    e = np.zeros((LANES, H_C * width), np.float32)
    for h in range(H_C):
        e[h, h * width:(h + 1) * width] = 1.0
    return jnp.asarray(np.tile(e, (3, 1)), BF16)


def _group_rms(y, g):
    gs = D_INNER // G_C
    parts = [_rms(y[:, i * gs:(i + 1) * gs], g[:, i * gs:(i + 1) * gs]) for i in range(G_C)]
    return jnp.concatenate(parts, axis=1)


def _ssd_prompt_body(z_ref, xbc_ref, sm_ref, smt_ref, cw_ref, cb_ref, dtrow_ref, arow_ref, dtcol_ref,
                     acol_ref, dx_ref, gn_ref, e_ref, o_ref, s_out_ref, xbuf_ref, s_ref):
    c = pl.program_id(1)
    rows = xbc_ref.shape[0]

    @pl.when(c == 0)
    def _():
        xbuf_ref[0:SUBLANES, :] = jnp.zeros((SUBLANES, xbuf_ref.shape[1]), F32)
        s_ref[...] = jnp.zeros_like(s_ref)

    xbc = _silu(_conv_chunk(xbc_ref, xbuf_ref, cw_ref, rows) + cb_ref[...])
    xs = xbc[:, :D_INNER]
    bm = xbc[:, D_INNER:D_INNER + G_C * N_C]
    cm = xbc[:, D_INNER + G_C * N_C:]
    e = e_ref[...]
    lower = _tri(rows, "lower")
    dt = _softplus(sm_ref[...] + dtrow_ref[...])
    acum = _dot_hi(lower.astype(F32), dt * -jnp.exp(arow_ref[...]))
    a_t = _softplus(smt_ref[...] + dtcol_ref[...]) * -jnp.exp(acol_ref[...])
    acum_t = _dot_hi(a_t, _tri(rows, "upper").astype(F32))
    xdt = xs * _expand(dt, e)
    xdte = (xdt * _expand(jnp.exp(acum[rows - 1:rows, :] - acum), e)).astype(BF16)
    scale_y = _expand(jnp.exp(acum), e)
    chunk_decay = scale_y[rows - 1:rows, :]
    xdt16 = xdt.astype(BF16)
    neg_upper = jnp.where(lower, 0.0, -jnp.inf)
    acum2, acum2_t = acum * LOG2E, acum_t * LOG2E
    gw = HPG * P_C
    ys = []
    for g in range(G_C):
        bg = bm[:, g * N_C:(g + 1) * N_C]
        cg16 = cm[:, g * N_C:(g + 1) * N_C].astype(BF16)
        cb = _dot_nt(cg16, bg.astype(BF16))
        yg = []
        for hh in range(HPG):
            h = g * HPG + hh
            lmat = jnp.exp2(acum2[:, h:h + 1] - acum2_t[h:h + 1, :] + neg_upper)
            yg.append(_dot((cb * lmat).astype(BF16), xdt16[:, h * P_C:(h + 1) * P_C]))
        gs = slice(g * gw, (g + 1) * gw)
        sg = s_ref[:, gs]
        y_off = _dot(cg16, sg.astype(BF16)) * scale_y[:, gs]
        ys.append(jnp.concatenate(yg, axis=1) + y_off)
        s_ref[:, gs] = sg * chunk_decay[:, gs] + _dot(bg.T.astype(BF16), xdte[:, gs])
    y = jnp.concatenate(ys, axis=1) + dx_ref[...] * xs
    y = y * _silu(z_ref[...])
    o_ref[...] = _group_rms(y, gn_ref[...]).astype(BF16)

    @pl.when(c == pl.num_programs(1) - 1)
    def _():
        s_out_ref[0] = s_ref[...].T.reshape(H_C, P_C, N_C)


def _ssd_small_params(dt_bias, a_log, ns):
    row = lambda v: jnp.zeros((1, LANES), F32).at[0, :H_C].set(v)
    col = lambda v: jnp.zeros((ns, 1), F32).at[:H_C, 0].set(v)
    return row(dt_bias), row(a_log), col(dt_bias), col(a_log)


def _ssd_prompt(z, xbc, small, small_t, conv_w, conv_b, dt_bias, a_log, d_skip, gnorm, bn, l):
    rows = SSD_CHUNK
    nc = l // rows
    t = bn * l
    ns = small_t.shape[0]
    dtrow, arow, dtcol, acol = _ssd_small_params(dt_bias, a_log, ns)
    tok = lambda n: pl.BlockSpec((rows, n), lambda b, c: (b * nc + c, 0))
    return pl.pallas_call(
        _ssd_prompt_body,
        grid=(bn, nc),
        in_specs=[
            tok(D_INNER), tok(SSD_CONV_CH), tok(LANES),
            pl.BlockSpec((ns, rows), lambda b, c: (0, b * nc + c)),
            _whole((CONV_W, SSD_CONV_CH)), _whole((1, SSD_CONV_CH)),
            _whole((1, LANES)), _whole((1, LANES)), _whole((ns, 1)), _whole((ns, 1)),
            _whole((1, D_INNER)), _whole((1, D_INNER)), _whole((3 * LANES, D_INNER)),
        ],
        out_specs=[tok(D_INNER), pl.BlockSpec((1, H_C, P_C, N_C), lambda b, c: (b, 0, 0, 0))],
        out_shape=[jax.ShapeDtypeStruct((t, D_INNER), BF16), jax.ShapeDtypeStruct((bn, H_C, P_C, N_C), F32)],
        scratch_shapes=[pltpu.VMEM((rows + SUBLANES, SSD_CONV_CH), F32), pltpu.VMEM((N_C, D_INNER), F32)],
        compiler_params=_cp("arbitrary", "arbitrary"),
        name="ssd_prompt",
    )(z, xbc, small, small_t, conv_w, conv_b.reshape(1, SSD_CONV_CH), dtrow, arow, dtcol, acol,
      jnp.repeat(d_skip, P_C).reshape(1, D_INNER), gnorm.reshape(1, D_INNER), _head_expander(P_C))


def _ssd_decode_pre_body(xbc_ref, hist_ref, sm_ref, cw_ref, cb_ref, dtrow_ref, arow_ref, e_ref, en_ref,
                         new_ref, xs_ref, xdt_ref, b_ref, c_ref, dax_ref, dan_ref):
    xbc = _silu(_conv_step(xbc_ref[...], hist_ref, cw_ref, new_ref) + cb_ref[...])
    xs = xbc[:, :D_INNER]
    dt = _softplus(sm_ref[...] + dtrow_ref[...])
    a = dt * -jnp.exp(arow_ref[...])
    xs_ref[...] = xs
    xdt_ref[...] = xs * _expand(dt, e_ref[...])
    b_ref[...] = xbc[:, D_INNER:D_INNER + G_C * N_C]
    c_ref[...] = xbc[:, D_INNER + G_C * N_C:]
    dax_ref[...] = jnp.exp(_expand(a, e_ref[...]))
    dan_ref[...] = jnp.exp(_expand(a, en_ref[...]))


def _ssd_decode_state_body(s_ref, xdt_ref, b_ref, c_ref, dan_ref, done_ref, s_out_ref, yoff_ref):
    del done_ref
    gw = HPG * P_C
    sub = SUBLANES
    row = lax.broadcasted_iota(jnp.int32, (sub, 1), 0)
    for r0 in range(0, s_ref.shape[0], sub):
        rows = slice(r0, r0 + sub)
        xdt = xdt_ref[rows, :]
        b16 = b_ref[rows, :].astype(BF16)
        c16 = c_ref[rows, :].astype(BF16)
        dan = dan_ref[rows, :]
        yoff = jnp.zeros((sub, gw), F32)
        for bb in range(sub):
            s = s_ref[r0 + bb].reshape(gw, N_C)
            res = _dot_nt(c16, s.astype(BF16))
            yoff = jnp.where(row == bb, res, yoff)
            x_one = jnp.where(row == bb, xdt, 0.0).astype(BF16)
            upd = _dot_tn(x_one, b16)
            for hh in range(HPG):
                rs = slice(hh * P_C, (hh + 1) * P_C)
                s_out_ref[r0 + bb, hh] = s[rs] * dan[bb:bb + 1, hh * N_C:(hh + 1) * N_C] + upd[rs]
        yoff_ref[rows, :] = yoff


def _ssd_decode_post_body(yoff_ref, dax_ref, xdt_ref, xs_ref, b_ref, c_ref, z_ref, dx_ref, gn_ref, o_ref):
    gw = HPG * P_C
    bc = b_ref[...] * c_ref[...]
    cbx = [jnp.broadcast_to(jnp.sum(bc[:, g * N_C:(g + 1) * N_C], axis=-1, keepdims=True), (bc.shape[0], gw))
           for g in range(G_C)]
    y = yoff_ref[...] * dax_ref[...] + jnp.concatenate(cbx, axis=1) * xdt_ref[...]
    y = y + dx_ref[...] * xs_ref[...]
    y = y * _silu(z_ref[...])
    o_ref[...] = _group_rms(y, gn_ref[...])


def _ssd_decode(z, xbc, hist, small, conv_w, conv_b, dt_bias, a_log, d_skip, gnorm, s_all, s_done, e):
    bn = xbc.shape[0]
    dtrow, arow, _, _ = _ssd_small_params(dt_bias, a_log, H_C)
    wide = jax.ShapeDtypeStruct((bn, D_INNER), F32)
    grp = jax.ShapeDtypeStruct((bn, G_C * N_C), F32)
    plain = pltpu.CompilerParams(vmem_limit_bytes=VMEM_LIMIT_BYTES)
    new_conv, xs, xdt, bm, cm, dax, dan = pl.pallas_call(
        _ssd_decode_pre_body,
        out_shape=[jax.ShapeDtypeStruct(hist.shape, F32), wide, wide, grp, grp, wide,
                   jax.ShapeDtypeStruct((bn, H_C * N_C), F32)],
        compiler_params=plain,
        name="ssd_decode_pre",
    )(xbc, hist, small, conv_w, conv_b.reshape(1, SSD_CONV_CH), dtrow, arow,
      _head_expander(P_C), _head_expander(N_C))
    nb = SSD_STATE_SEQS
    gw = HPG * P_C
    st = pl.BlockSpec((None, nb, HPG, P_C, N_C), lambda i, g: (e, i, g, 0, 0))
    s_done, yoff = pl.pallas_call(
        _ssd_decode_state_body,
        grid=(bn // nb, G_C),
        in_specs=[st, pl.BlockSpec((nb, gw), lambda i, g: (i, g)), pl.BlockSpec((nb, N_C), lambda i, g: (i, g)),
                  pl.BlockSpec((nb, N_C), lambda i, g: (i, g)), pl.BlockSpec((nb, HPG * N_C), lambda i, g: (i, g)),
                  pl.BlockSpec(memory_space=pl.ANY)],
        out_specs=[st, pl.BlockSpec((nb, gw), lambda i, g: (i, g))],
        out_shape=[jax.ShapeDtypeStruct(s_all.shape, F32), wide],
        input_output_aliases={5: 0},
        compiler_params=_cp("parallel", "parallel"),
        name="ssd_decode_state",
    )(s_all, xdt, bm, cm, dan, s_done)
    mix = pl.pallas_call(
        _ssd_decode_post_body,
        out_shape=wide,
        compiler_params=plain,
        name="ssd_decode_post",
    )(yoff, dax, xdt, xs, bm, cm, z, jnp.repeat(d_skip, P_C).reshape(1, D_INNER), gnorm.reshape(1, D_INNER))
    return mix, new_conv, s_done


def _narrow_weights(w_small):
    n = w_small.shape[1]
    return jnp.zeros((D_MODEL, LANES), BF16).at[:, :n].set(w_small.astype(BF16))


def _even_q_pad(qa):
    bn = qa.shape[0]
    q = qa.reshape(bn, KV_A, G_A, HD_A)
    out = jnp.zeros((bn, KV_A, G_A, KV_A, HD_A), F32)
    for kv in range(KV_A):
        out = out.at[:, kv, :, kv, :].set(q[:, kv])
    return out.reshape(bn, H_A, KV_A * HD_A)


def _even_o_unpad(o8):
    bn = o8.shape[0]
    o = o8.reshape(bn, KV_A, G_A, KV_A, HD_A)
    return jnp.concatenate([o[:, kv, :, kv, :].reshape(bn, G_A * HD_A) for kv in range(KV_A)], axis=1)


def _trunk(x3, states, wts):
    (rel_bias, norm_ff1, norm_mix, norm_ff2, norm_final,
     ff1, ff2, even_w_in, even_w_out, swa_sinks, gdn_conv_w, gdn_A_log, gdn_dt_bias, gdn_norm,
     ssd_w_in, ssd_w_out, ssd_conv_w, ssd_conv_b, ssd_dt_bias, ssd_A_log, ssd_D, ssd_norm) = wts
    bn, l, d = x3.shape
    t = bn * l
    x = x3.reshape(t, d)
    decode = states is not None
    ks, vs, gconv, gssm, sconv, sssm = [], [], [], [], [], []
    depth = norm_ff1.shape[0]
    width = KV_A * HD_A
    if decode:
        gssm_done, sssm_done = jnp.zeros_like(states[3]), jnp.zeros_like(states[5])
        n_even = states[0].shape[0]
        cache_t = lambda c: c.transpose(0, 1, 3, 4, 2).reshape(n_even, bn, width, WINDOW)
        ckt, cvt = cache_t(states[0]), cache_t(states[1])
        ghist, shist = states[2].transpose(0, 2, 1, 3), states[4].transpose(0, 2, 1, 3)
    for layer in range(depth):
        x = _ffn(x, norm_ff1[layer], ff1, layer)
        e = layer // 2
        if layer % 2 == 0:
            w_all, ws = even_w_in
            qa, kv, qkvb, z, small, small_t = _inproj(x, norm_mix[layer], w_all, e, ws[e], 2 * SUBLANES,
                                                      (A_Q, 2 * width, B_QKV, B_Z))
            if decode:
                o8 = _swa_decode(_even_q_pad(qa), kv, ckt, cvt, e, rel_bias, swa_sinks[e])
                o_a = _even_o_unpad(o8)
                o_b, new_conv, gssm_done = _gdn_decode(qkvb, ghist[e], z, small, gdn_conv_w[e], gdn_A_log[e],
                                                       gdn_dt_bias[e], gdn_norm[e], states[3], gssm_done, e)
                new_k, new_v = kv[:, :width], kv[:, width:]
                new_conv = new_conv.transpose(1, 0, 2)
            else:
                o_a = _swa_prompt(qa, kv, rel_bias, swa_sinks[e], bn, l)
                o_b, s_new = _gdn_prompt(qkvb, z, small, small_t, gdn_conv_w[e], gdn_A_log[e],
                                         gdn_dt_bias[e], gdn_norm[e], bn, l)
                kv3 = kv.reshape(bn, l, 2 * width)
                new_k = kv3[:, l - WINDOW:, :width]
                new_v = kv3[:, l - WINDOW:, width:]
                new_conv = qkvb.reshape(bn, l, B_QKV)[:, l - (CONV_W - 1):]
                gssm.append(s_new)
            ks.append(new_k)
            vs.append(new_v)
            gconv.append(new_conv)
            mixes = [(o_a, even_w_out, e, 0), (o_b, even_w_out, e, 1)]
        else:
            w_all, ws = ssd_w_in
            z, xbc, small, small_t = _inproj(x, norm_mix[layer], w_all, e, ws[e], H_C, (D_INNER, SSD_CONV_CH))
            if decode:
                mix, new_conv, sssm_done = _ssd_decode(z, xbc, shist[e], small, ssd_conv_w[e], ssd_conv_b[e],
                                                       ssd_dt_bias[e], ssd_A_log[e], ssd_D[e], ssd_norm[e],
                                                       states[5], sssm_done, e)
                new_conv = new_conv.transpose(1, 0, 2)
            else:
                mix, s_new = _ssd_prompt(z, xbc, small, small_t, ssd_conv_w[e], ssd_conv_b[e], ssd_dt_bias[e],
                                         ssd_A_log[e], ssd_D[e], ssd_norm[e], bn, l)
                new_conv = xbc.reshape(bn, l, SSD_CONV_CH)[:, l - (CONV_W - 1):]
                sssm.append(s_new)
            sconv.append(new_conv)
            mixes = [(mix, ssd_w_out, e, 0)]
        x = _ffn(x, norm_ff2[layer], ff2, layer, mixes=mixes,
                 g_final=norm_final if layer == depth - 1 else None)
    if decode:
        bs = DECODE_SEQ_BLOCK
        new_t = lambda rows: jnp.stack(rows).reshape(n_even, bn // bs, bs, width).transpose(0, 1, 3, 2)
        back = lambda c: c.reshape(n_even, bn, KV_A, HD_A, WINDOW).transpose(0, 1, 4, 2, 3)
        ks = back(_cache_shift(ckt, new_t(ks)))
        vs = back(_cache_shift(cvt, new_t(vs)))
        gssm, sssm = gssm_done, sssm_done
    else:
        kv5 = (len(ks), bn, WINDOW, KV_A, HD_A)
        ks, vs = jnp.stack(ks).reshape(kv5), jnp.stack(vs).reshape(kv5)
        gssm, sssm = jnp.stack(gssm), jnp.stack(sssm)
    return (x.reshape(bn, l, d), ks, vs, jnp.stack(gconv), gssm, jnp.stack(sconv), sssm)


def kernel(x_prompt, x_sample, cache_swa_k, cache_swa_v, state_gdn_conv, state_gdn_ssm, state_ssd_conv, state_ssd_ssm, rel_bias, norm_ff1, norm_mix, norm_ff2, norm_final, ff1_gate, ff1_up, ff1_down, ff2_gate, ff2_up, ff2_down, even_w_in, even_w_out, swa_sinks, gdn_conv_w, gdn_A_log, gdn_dt_bias, gdn_norm, ssd_w_in, ssd_w_out, ssd_conv_w, ssd_conv_b, ssd_dt_bias, ssd_A_log, ssd_D, ssd_norm):
    depth = norm_ff1.shape[0]
    ff1 = (ff1_gate.astype(BF16), ff1_up.astype(BF16), ff1_down.astype(BF16))
    ff2 = (ff2_gate.astype(BF16), ff2_up.astype(BF16), ff2_down.astype(BF16))
    n_even_main = A_Q + 2 * KV_A * HD_A + B_QKV + B_Z
    even_in = (even_w_in.astype(BF16),
               [_narrow_weights(even_w_in[e][:, n_even_main:]) for e in range(even_w_in.shape[0])])
    n_odd_main = D_INNER + SSD_CONV_CH
    odd_in = (ssd_w_in.astype(BF16),
              [_narrow_weights(ssd_w_in[e][:, n_odd_main:]) for e in range(ssd_w_in.shape[0])])
    wts = (rel_bias, norm_ff1, norm_mix, norm_ff2, norm_final, ff1, ff2,
           even_in, even_w_out.astype(BF16), swa_sinks, gdn_conv_w, gdn_A_log, gdn_dt_bias, gdn_norm,
           odd_in, ssd_w_out.astype(BF16), ssd_conv_w, ssd_conv_b, ssd_dt_bias, ssd_A_log, ssd_D, ssd_norm)
    y_p, p_k, p_v, p_gconv, p_gssm, p_sconv, p_sssm = _trunk(x_prompt, None, wts)
    states = (cache_swa_k, cache_swa_v, state_gdn_conv, state_gdn_ssm, state_ssd_conv, state_ssd_ssm)
    y_s, s_k, s_v, s_gconv, s_gssm, s_sconv, s_sssm = _trunk(x_sample, states, wts)
    return (y_p, y_s, p_k, p_v, p_gconv, p_gssm, p_sconv, p_sssm,
            s_k, s_v, s_gconv, s_gssm, s_sconv, s_sssm)
```

```python
import functools
import math

import numpy as np
import jax
import jax.numpy as jnp
from jax import lax
from jax.experimental import pallas as pl
from jax.experimental.pallas import tpu as pltpu

F32 = jnp.float32
BF16 = jnp.bfloat16
HI = lax.Precision.HIGHEST

EPS = 1e-6
NEG_INF = -1e30
LOG2E = math.log2(math.e)
D_MODEL = 1024
WINDOW = 128
BLOCK = 128
H_A, KV_A, G_A, HD_A = 8, 2, 4, 64
N_BUCKETS, MAX_DIST = 32, 128
H_B, DK_B, DV_B = 4, 128, 128
CONV_W = 4
GDN_CHUNK = 64
D_INNER = 2048
P_C, H_C, N_C, G_C = 64, 32, 128, 4
HPG = H_C // G_C
SSD_CHUNK = 128
A_Q = H_A * HD_A
B_QKV = 3 * H_B * DK_B
B_Z = H_B * DV_B
SSD_CONV_CH = D_INNER + 2 * G_C * N_C
LANES = 128
SUBLANES = 8
VMEM_LIMIT_BYTES = 56 * 1024 * 1024
ROW_TILE = 512
SWA_QUERY_BLOCKS = 4
GDN_PREP_CHUNKS = 4
GDN_SCAN_GROUP = 2
DECODE_SEQ_BLOCK = 32
SSD_STATE_SEQS = 4 * SUBLANES
GDN_STATE_SEQS = 2 * SUBLANES


def _cp(*sem):
    return pltpu.CompilerParams(dimension_semantics=sem, vmem_limit_bytes=VMEM_LIMIT_BYTES)


def _whole(shape):
    nd = len(shape)
    return pl.BlockSpec(shape, lambda *_: (0,) * nd, pipeline_mode=pl.Buffered(1))


def _rms(x, g):
    return x * lax.rsqrt(jnp.mean(x * x, axis=-1, keepdims=True) + EPS) * g


def _silu(x):
    h = 0.5 * x
    return h * jnp.tanh(h) + h


def _softplus(x):
    return jnp.maximum(x, 0.0) + jnp.log1p(jnp.exp(-jnp.abs(x)))


def _dot(a, b):
    return jnp.dot(a, b, preferred_element_type=F32)


def _dot_nt(a, b):
    return lax.dot_general(a, b, (((1,), (1,)), ((), ())), preferred_element_type=F32)


def _dot_tn(a, b):
    return lax.dot_general(a, b, (((0,), (0,)), ((), ())), preferred_element_type=F32)


def _dot_hi(a, b):
    return jnp.dot(a, b, precision=HI, preferred_element_type=F32)


def _expand(x, e3):
    hi = x.astype(BF16)
    r = x - hi.astype(F32)
    mid = r.astype(BF16)
    lo = (r - mid.astype(F32)).astype(BF16)
    return _dot(jnp.concatenate([hi, mid, lo], axis=1), e3)


def _tri(n, kind):
    r = lax.broadcasted_iota(jnp.int32, (n, n), 0)
    c = lax.broadcasted_iota(jnp.int32, (n, n), 1)
    return {"lower": r >= c, "strict_lower": r > c, "upper": r <= c}[kind]


def _ffn_body(*refs, n_mix, final):
    refs = list(refs)
    x = refs.pop(0)[...]
    for _ in range(n_mix):
        m_ref, wo_ref = refs.pop(0), refs.pop(0)
        x = x + _dot(m_ref[...].astype(BF16), wo_ref[...])
    g_ref, wg_ref, wu_ref, wd_ref = refs[:4]
    o_ref = refs[-1]
    hn = _rms(x, g_ref[...]).astype(BF16)
    act = (_silu(_dot(hn, wg_ref[...])) * _dot(hn, wu_ref[...])).astype(BF16)
    y = x + 0.5 * _dot(act, wd_ref[...])
    if final:
        y = _rms(y, refs[4][...])
    o_ref[...] = y


def _layer_slab(w_all, layer, rows=None, row_block=0):
    shape = (rows or w_all.shape[1], w_all.shape[2])
    return pl.BlockSpec((None,) + shape, lambda i: (layer, row_block, 0), pipeline_mode=pl.Buffered(1))


def _ffn(x, g, ffw, layer, mixes=(), g_final=None):
    t, d = x.shape
    tm = min(t, ROW_TILE)
    final = g_final is not None
    row = pl.BlockSpec((tm, d), lambda i: (i, 0))
    in_specs, args = [row], [x]
    for m, w_all, e, row_block in mixes:
        in_specs += [pl.BlockSpec((tm, m.shape[1]), lambda i: (i, 0)), _layer_slab(w_all, e, m.shape[1], row_block)]
        args += [m, w_all]
    in_specs += [_whole((1, d))] + [_layer_slab(w, layer) for w in ffw]
    args += [g.reshape(1, d), *ffw]
    if final:
        in_specs.append(_whole((1, d)))
        args.append(g_final.reshape(1, d))
    return pl.pallas_call(
        functools.partial(_ffn_body, n_mix=len(mixes), final=final),
        grid=(t // tm,),
        in_specs=in_specs,
        out_specs=row,
        out_shape=jax.ShapeDtypeStruct((t, d), F32),
        compiler_params=_cp("parallel"),
        name="ffn",
    )(*args)


def _inproj_body(x_ref, g_ref, w_ref, ws_ref, *outs, splits):
    hn = _rms(x_ref[...], g_ref[...]).astype(BF16)
    off = 0
    for o_ref, n in zip(outs[:-2], splits):
        o_ref[...] = _dot(hn, w_ref[:, off:off + n])
        off += n
    small = _dot(hn, ws_ref[...])
    outs[-2][...] = small
    outs[-1][...] = small.T[:outs[-1].shape[0]]


def _inproj(x, g, w_all, e, ws, ns, splits):
    t, d = x.shape
    w_spec = _layer_slab(w_all, e)
    tm = min(t, ROW_TILE)
    row = pl.BlockSpec((tm, d), lambda i: (i, 0))
    out_specs = [pl.BlockSpec((tm, n), lambda i: (i, 0)) for n in splits]
    out_specs += [pl.BlockSpec((tm, LANES), lambda i: (i, 0)), pl.BlockSpec((ns, tm), lambda i: (0, i))]
    out_shape = [jax.ShapeDtypeStruct((t, n), F32) for n in splits]
    out_shape += [jax.ShapeDtypeStruct((t, LANES), F32), jax.ShapeDtypeStruct((ns, t), F32)]
    return pl.pallas_call(
        functools.partial(_inproj_body, splits=splits),
        grid=(t // tm,),
        in_specs=[row, _whole((1, d)), w_spec, _whole(ws.shape)],
        out_specs=out_specs,
        out_shape=out_shape,
        compiler_params=_cp("parallel"),
        name="inproj",
    )(x, g.reshape(1, d), w_all, ws)


def _t5_bucket_np(dist):
    max_exact = N_BUCKETS // 2
    df = np.maximum(dist, max_exact).astype(np.float32)
    large = max_exact + (np.log(df / np.float32(max_exact)) / np.float32(math.log(MAX_DIST / max_exact))
                         * np.float32(N_BUCKETS - max_exact)).astype(np.int32)
    return np.where(dist < max_exact, dist, np.minimum(large, N_BUCKETS - 1)).astype(np.int32)


def _band_bucket_ids(n_q, n_k, offset):
    d = offset + np.arange(n_q)[:, None] - np.arange(n_k)[None, :]
    valid = (d >= 0) & (d <= WINDOW)
    return np.where(valid, _t5_bucket_np(np.clip(d, 0, WINDOW)), -1).astype(np.int32)


def _bias_from_buckets(bid, rb_ref, h):
    acc = jnp.full(bid.shape, NEG_INF, F32)
    for bk in range(N_BUCKETS):
        acc = jnp.where(bid == bk, rb_ref[bk, h], acc)
    return acc


def _swa_prompt_body(bid_ref, rb_ref, sk_ref, q_ref, kvp_ref, kvc_ref, o_ref, bias_ref):
    first_step = (pl.program_id(0) == 0) & (pl.program_id(1) == 0)

    @pl.when(first_step)
    def _():
        bid = bid_ref[...]
        col = lax.broadcasted_iota(jnp.int32, (BLOCK, 2 * BLOCK), 1)
        for h in range(H_A):
            bias = _bias_from_buckets(bid, rb_ref, h)
            bias_ref[h] = bias
            bias_ref[H_A + h] = jnp.where(col < BLOCK, NEG_INF, bias)

    kvp = kvp_ref[...]
    kvc = kvc_ref[...]
    scale = HD_A ** -0.5
    heads = range(H_A)
    for j in range(q_ref.shape[0] // BLOCK):
        rows = slice(j * BLOCK, (j + 1) * BLOCK)
        prev = kvp if j == 0 else kvc[(j - 1) * BLOCK:j * BLOCK]
        first_block = jnp.where(pl.program_id(1) == 0, H_A, 0) if j == 0 else 0
        k, v = [], []
        for kv in range(KV_A):
            ks = slice(kv * HD_A, (kv + 1) * HD_A)
            vs = slice(KV_A * HD_A + kv * HD_A, KV_A * HD_A + (kv + 1) * HD_A)
            k.append(jnp.concatenate([prev[:, ks], kvc[rows, ks]], axis=0).astype(BF16))
            v.append(jnp.concatenate([prev[:, vs], kvc[rows, vs]], axis=0).astype(BF16))
        s = [_dot_nt((q_ref[rows, h * HD_A:(h + 1) * HD_A] * scale).astype(BF16), k[h // G_A])
             + bias_ref[first_block + h] for h in heads]
        m = [jnp.maximum(jnp.max(s[h], axis=-1, keepdims=True), sk_ref[h]) for h in heads]
        p = [jnp.exp(s[h] - m[h]) for h in heads]
        denom = [jnp.sum(p[h], axis=-1, keepdims=True) + jnp.exp(sk_ref[h] - m[h]) for h in heads]
        outs = [_dot(p[h].astype(BF16), v[h // G_A]) / denom[h] for h in heads]
        o_ref[rows, :] = jnp.concatenate(outs, axis=1).astype(BF16)


def _swa_prompt(qa, kv, rel_bias, sinks, bn, l):
    qb = SWA_QUERY_BLOCKS
    nb = l // (qb * BLOCK)
    t = bn * l
    bid = jnp.asarray(_band_bucket_ids(BLOCK, 2 * BLOCK, BLOCK))
    width = 2 * KV_A * HD_A
    return pl.pallas_call(
        _swa_prompt_body,
        grid=(bn, nb),
        in_specs=[
            _whole((BLOCK, 2 * BLOCK)),
            pl.BlockSpec(memory_space=pltpu.SMEM),
            pl.BlockSpec(memory_space=pltpu.SMEM),
            pl.BlockSpec((qb * BLOCK, A_Q), lambda b, i: (b * nb + i, 0)),
            pl.BlockSpec((BLOCK, width), lambda b, i: (jnp.maximum((b * nb + i) * qb - 1, 0), 0)),
            pl.BlockSpec((qb * BLOCK, width), lambda b, i: (b * nb + i, 0)),
        ],
        out_specs=pl.BlockSpec((qb * BLOCK, A_Q), lambda b, i: (b * nb + i, 0)),
        out_shape=jax.ShapeDtypeStruct((t, A_Q), BF16),
        scratch_shapes=[pltpu.VMEM((2 * H_A, BLOCK, 2 * BLOCK), F32)],
        compiler_params=_cp("arbitrary", "arbitrary"),
        name="swa_prompt",
    )(bid, rel_bias, sinks, qa, kv, kv)


def _swa_decode_body(bid_ref, rb_ref, sk_ref, q_ref, kvn_ref, ckt_ref, cvt_ref, o_ref):
    bid = bid_ref[...]
    row = lax.broadcasted_iota(jnp.int32, (H_A, 1), 0)
    bias = jnp.zeros((H_A, bid.shape[1]), F32)
    sk = jnp.zeros((H_A, 1), F32)
    for h in range(H_A):
        bias = jnp.where(row == h, _bias_from_buckets(bid, rb_ref, h), bias)
        sk = jnp.where(row == h, sk_ref[h], sk)
    bias_c = bias[:, :WINDOW]
    bias_n = bias[:, WINDOW:WINDOW + 1]
    scale = HD_A ** -0.5
    q = q_ref[...]
    kvn = kvn_ref[...]
    width = KV_A * HD_A
    k_new = kvn[:, None, :width]
    v_new = kvn[:, None, width:]
    s = lax.dot_general(q.astype(BF16), ckt_ref[...].astype(BF16), (((2,), (1,)), ((0,), (0,))),
                        preferred_element_type=F32) * scale + bias_c[None]
    s_n = jnp.sum(q * k_new, axis=-1, keepdims=True) * scale + bias_n[None]
    m = jnp.maximum(jnp.maximum(jnp.max(s, axis=-1, keepdims=True), s_n), sk[None])
    p = jnp.exp(s - m)
    p_n = jnp.exp(s_n - m)
    denom = jnp.sum(p, axis=-1, keepdims=True) + p_n + jnp.exp(sk[None] - m)
    o = lax.dot_general((p / denom).astype(BF16), cvt_ref[...].astype(BF16), (((2,), (2,)), ((0,), (0,))),
                        preferred_element_type=F32)
    o_ref[...] = o + (p_n / denom) * v_new


def _swa_decode(q_pad, kv_new, cache_kt, cache_vt, e, rel_bias, sinks):
    bn = q_pad.shape[0]
    bs = DECODE_SEQ_BLOCK
    width = KV_A * HD_A
    ids = _band_bucket_ids(1, WINDOW + 1, WINDOW)
    bid = np.full((1, WINDOW + LANES), -1, np.int32)
    bid[:, :WINDOW + 1] = ids
    return pl.pallas_call(
        _swa_decode_body,
        grid=(bn // bs,),
        in_specs=[
            _whole((1, WINDOW + LANES)),
            pl.BlockSpec(memory_space=pltpu.SMEM),
            pl.BlockSpec(memory_space=pltpu.SMEM),
            pl.BlockSpec((bs, H_A, width), lambda i: (i, 0, 0)),
            pl.BlockSpec((bs, 2 * width), lambda i: (i, 0)),
            pl.BlockSpec((None, bs, width, WINDOW), lambda i: (e, i, 0, 0)),
            pl.BlockSpec((None, bs, width, WINDOW), lambda i: (e, i, 0, 0)),
        ],
        out_specs=pl.BlockSpec((bs, H_A, width), lambda i: (i, 0, 0)),
        out_shape=jax.ShapeDtypeStruct((bn, H_A, width), F32),
        compiler_params=_cp("parallel"),
        name="swa_decode",
    )(jnp.asarray(bid), rel_bias, sinks, q_pad, kv_new, cache_kt, cache_vt)


def _cache_shift_body(c_ref, nt_ref, o_ref):
    o_ref[...] = pltpu.roll(c_ref[...], WINDOW - 1, axis=2)
    nt = nt_ref[...]
    for b in range(c_ref.shape[0]):
        o_ref[b, :, WINDOW - 1:WINDOW] = nt[:, b:b + 1]


def _cache_shift(cache_t, new_t):
    n, bn, width, win = cache_t.shape
    bs = new_t.shape[3]
    blk = pl.BlockSpec((None, bs, width, win), lambda e, i: (e, i, 0, 0))
    return pl.pallas_call(
        _cache_shift_body,
        grid=(n, bn // bs),
        in_specs=[blk, pl.BlockSpec((None, None, width, bs), lambda e, i: (e, i, 0, 0))],
        out_specs=blk,
        out_shape=jax.ShapeDtypeStruct(cache_t.shape, F32),
        compiler_params=_cp("parallel", "parallel"),
        name="cache_shift",
    )(cache_t, new_t)


def _conv_rows(x_ref, xbuf_ref, cw_ref, rows):
    xbuf_ref[SUBLANES:SUBLANES + rows, :] = x_ref[...]
    x = xbuf_ref[...]
    out = x * cw_ref[0:1, :]
    for i in range(1, CONV_W):
        out = pltpu.roll(out, 1, axis=0) + x * cw_ref[i:i + 1, :]
    return out[SUBLANES:]


def _conv_chunk(x_ref, xbuf_ref, cw_ref, rows):
    out = _conv_rows(x_ref, xbuf_ref, cw_ref, rows)
    xbuf_ref[0:SUBLANES, :] = xbuf_ref[rows:rows + SUBLANES, :]
    return out


def _l2norm(x):
    return x * lax.rsqrt(jnp.sum(x * x, axis=-1, keepdims=True) + EPS)


def _chunk_masks(n, cc):
    r = lax.broadcasted_iota(jnp.int32, (n, n), 0)
    c = lax.broadcasted_iota(jnp.int32, (n, n), 1)
    shift = int(math.log2(cc))
    same = lax.shift_right_logical(r, shift) == lax.shift_right_logical(c, shift)
    return same, same & (r >= c), same & (r <= c)


def _dot_bf16x3(a, b):
    a_hi = a.astype(BF16)
    a_lo = (a - a_hi.astype(F32)).astype(BF16)
    b_hi = b.astype(BF16)
    b_lo = (b - b_hi.astype(F32)).astype(BF16)
    return _dot(a_hi, b_hi) + _dot(a_hi, b_lo) + _dot(a_lo, b_hi)


def _solve_masks(n, cc, base=SUBLANES):
    r = np.arange(n)[:, None]
    c = np.arange(n)[None, :]
    masks = [-(((r // base) == (c // base)) & (r > c)).astype(np.float32)]
    s = base
    while s < cc:
        sibling = ((r // (2 * s)) == (c // (2 * s))) & ((r // s) % 2 == 1) & ((c // s) % 2 == 0)
        masks.append(sibling.astype(np.float32))
        s *= 2
    return np.stack(masks)


def _gdn_prep_body(qkv_ref, halo_ref, sm_ref, smt_ref, cw_ref, acol_ref, dtcol_ref, tri_ref,
                   u_ref, w_ref, qg_ref, kd_ref, aqk_ref, gl_ref, xbuf_ref):
    rows = qkv_ref.shape[0]
    cc = GDN_CHUNK
    xbuf_ref[0:SUBLANES, :] = jnp.where(pl.program_id(1) == 0, 0.0, halo_ref[...])
    conv = _silu(_conv_rows(qkv_ref, xbuf_ref, cw_ref, rows))
    nq = H_B * DK_B
    beta_all = jax.nn.sigmoid(sm_ref[...])
    gt_all = -jnp.exp(acol_ref[...]) * _softplus(smt_ref[...] + dtcol_ref[...])
    pad = jnp.zeros((LANES - gt_all.shape[0], rows), F32)
    g_all = jnp.concatenate([gt_all, pad], axis=0).T
    same, lower, upper = _chunk_masks(rows, cc)
    gc = _dot_hi(lower.astype(F32), g_all)
    gct = _dot_hi(gt_all, upper.astype(F32))
    gsum = _dot_hi(same.astype(F32), g_all)
    neg_outside = jnp.where(lower, 0.0, -jnp.inf)
    gc2, gct2 = gc * LOG2E, gct * LOG2E
    heads = range(H_B)
    qh, kh16, kb, decay, egc = [], [], [], [], []
    for h in heads:
        q = _l2norm(conv[:, h * DK_B:(h + 1) * DK_B]) * (DK_B ** -0.5)
        k = _l2norm(conv[:, nq + h * DK_B:nq + (h + 1) * DK_B])
        v = conv[:, 2 * nq + h * DV_B:2 * nq + (h + 1) * DV_B]
        beta = beta_all[:, h:h + 1]
        gcol = gc[:, H_B + h:H_B + h + 1]
        decay.append(jnp.exp2(gc2[:, H_B + h:H_B + h + 1] - gct2[H_B + h:H_B + h + 1, :] + neg_outside))
        eg = jnp.exp(gcol)
        qg_ref[:, h * DK_B:(h + 1) * DK_B] = (q * eg).astype(BF16)
        kd_ref[:, h * DK_B:(h + 1) * DK_B] = (k * jnp.exp(gsum[:, H_B + h:H_B + h + 1] - gcol)).astype(BF16)
        qh.append(q.astype(BF16))
        kh16.append(k.astype(BF16))
        kb.append(k * beta)
        egc.append((v * beta, eg))
    m = [_dot_nt(kb[h].astype(BF16), kh16[h]) * decay[h] for h in heads]
    for h in heads:
        aqk = _dot_nt(qh[h], kh16[h]) * decay[h]
        blocks = [aqk[i * cc:(i + 1) * cc, i * cc:(i + 1) * cc] for i in range(rows // cc)]
        aqk_ref[:, h * cc:(h + 1) * cc] = jnp.concatenate(blocks, axis=0).astype(BF16)
    mm = lambda xs, ys: [_dot(x.astype(BF16), y.astype(BF16)) for x, y in zip(xs, ys)]
    a = [m[h] * tri_ref[0] for h in heads]
    a2 = mm(a, a)
    a3 = mm(a, a2)
    a4 = mm(a2, a2)
    x1 = [a[h] + a2[h] + a3[h] for h in heads]
    x1a4 = mm(x1, a4)
    r = [x1[h] + a4[h] + x1a4[h] for h in heads]
    for level in range(1, tri_ref.shape[0]):
        c = [m[h] * tri_ref[level] for h in heads]
        rc = mm(r, c)
        y = [c[h] + rc[h] for h in heads]
        yr = mm(y, r)
        r = [r[h] - y[h] - yr[h] for h in heads]
    for h in heads:
        vb, eg = egc[h]
        rhs = jnp.concatenate([vb, kb[h] * eg], axis=1)
        sol = rhs + _dot_bf16x3(r[h], rhs)
        u_ref[:, h * DV_B:(h + 1) * DV_B] = sol[:, :DV_B]
        w_ref[:, h * DK_B:(h + 1) * DK_B] = sol[:, DV_B:].astype(BF16)
    for i in range(rows // cc):
        gl = [jnp.broadcast_to(jnp.exp(gsum[i * cc:i * cc + 1, H_B + h:H_B + h + 1]), (SUBLANES, DV_B))
              for h in heads]
        gl_ref[0, i] = jnp.concatenate(gl, axis=1)


def _gdn_scan_body(u_ref, w_ref, qg_ref, kd_ref, aqk_ref, gl_ref, z_ref, gn_ref, o_ref, s_out_ref, s_ref):
    c = pl.program_id(0)
    bn = u_ref.shape[0]
    cc = GDN_CHUNK

    @pl.when(c == 0)
    def _():
        s_ref[...] = jnp.zeros_like(s_ref)

    gn = gn_ref[...]
    group = GDN_SCAN_GROUP
    for b0 in range(0, bn, group):
        combos = [(b, h) for b in range(b0, min(b0 + group, bn)) for h in range(H_B)]
        res = {}
        for b, h in combos:
            hs = slice(h * DK_B, (h + 1) * DK_B)
            wq = jnp.concatenate([w_ref[b, :, hs], qg_ref[b, :, hs]], axis=0)
            res[b, h] = _dot(wq, s_ref[b, h].astype(BF16))
        v16 = {}
        for b, h in combos:
            hs = slice(h * DV_B, (h + 1) * DV_B)
            v16[b, h] = (u_ref[b, :, hs] - res[b, h][:cc]).astype(BF16)
        for b, h in combos:
            hs = slice(h * DV_B, (h + 1) * DV_B)
            o = res[b, h][cc:] + _dot(aqk_ref[b, :, h * cc:(h + 1) * cc], v16[b, h])
            upd = _dot_tn(kd_ref[b, :, hs], v16[b, h])
            s_ref[b, h] = s_ref[b, h] * gl_ref[b, 0, 0:1, hs] + upd
            o_ref[b, :, hs] = (_rms(o, gn) * _silu(z_ref[b, :, hs])).astype(BF16)

    @pl.when(c == pl.num_programs(0) - 1)
    def _():
        s_out_ref[...] = s_ref[...]


def _gdn_prompt(qkvb, z, small, small_t, conv_w, a_log, dt_bias, gnorm, bn, l):
    rows = GDN_PREP_CHUNKS * GDN_CHUNK
    cc = GDN_CHUNK
    nb = l // rows
    nc = l // cc
    t = bn * l
    ns = small_t.shape[0]
    nq = H_B * DK_B
    pad_col = lambda v: jnp.zeros((ns, 1), F32).at[H_B:2 * H_B, 0].set(v)
    tri = jnp.asarray(_solve_masks(rows, cc))
    tok = lambda n: pl.BlockSpec((rows, n), lambda b, i: (b * nb + i, 0))
    halo = pl.BlockSpec((SUBLANES, B_QKV), lambda b, i: (jnp.maximum((b * nb + i) * (rows // SUBLANES) - 1, 0), 0))
    u, w, qg, kd, aqk, gl = pl.pallas_call(
        _gdn_prep_body,
        grid=(bn, nb),
        in_specs=[
            tok(B_QKV), halo, tok(LANES),
            pl.BlockSpec((ns, rows), lambda b, i: (0, b * nb + i)),
            _whole((CONV_W, B_QKV)), _whole((ns, 1)), _whole((ns, 1)), _whole(tri.shape),
        ],
        out_specs=[tok(nq), tok(nq), tok(nq), tok(nq), tok(H_B * cc),
                   pl.BlockSpec((1, rows // cc, SUBLANES, nq), lambda b, i: (b, i, 0, 0))],
        out_shape=[jax.ShapeDtypeStruct((t, nq), F32)] + [jax.ShapeDtypeStruct((t, nq), BF16)] * 3
        + [jax.ShapeDtypeStruct((t, H_B * cc), BF16), jax.ShapeDtypeStruct((bn, nc, SUBLANES, nq), F32)],
        scratch_shapes=[pltpu.VMEM((rows + SUBLANES, B_QKV), F32)],
        compiler_params=_cp("parallel", "parallel"),
        name="gdn_prep",
    )(qkvb, qkvb, small, small_t, conv_w, pad_col(a_log), pad_col(dt_bias), tri)
    seq = lambda n: pl.BlockSpec((bn, cc, n), lambda c: (0, c, 0))
    r3 = lambda a: a.reshape(bn, l, a.shape[1])
    o, s_new = pl.pallas_call(
        _gdn_scan_body,
        grid=(nc,),
        in_specs=[seq(nq), seq(nq), seq(nq), seq(nq), seq(H_B * cc),
                  pl.BlockSpec((bn, 1, SUBLANES, nq), lambda c: (0, c, 0, 0)), seq(B_Z), _whole((1, DV_B))],
        out_specs=[seq(B_Z), pl.BlockSpec((bn, H_B, DK_B, DV_B), lambda c: (0, 0, 0, 0))],
        out_shape=[jax.ShapeDtypeStruct((bn, l, B_Z), BF16), jax.ShapeDtypeStruct((bn, H_B, DK_B, DV_B), F32)],
        scratch_shapes=[pltpu.VMEM((bn, H_B, DK_B, DV_B), F32)],
        compiler_params=_cp("arbitrary"),
        name="gdn_scan",
    )(r3(u), r3(w), r3(qg), r3(kd), r3(aqk), gl, r3(z), gnorm.reshape(1, DV_B))
    return o.reshape(t, B_Z), s_new


def _conv_step(x, hist_ref, cw_ref, new_ref):
    out = hist_ref[0] * cw_ref[0:1, :]
    for i in range(1, CONV_W - 1):
        out = out + hist_ref[i] * cw_ref[i:i + 1, :]
    out = out + x * cw_ref[CONV_W - 1:CONV_W, :]
    for i in range(CONV_W - 2):
        new_ref[i] = hist_ref[i + 1]
    new_ref[CONV_W - 2] = x
    return out


def _gdn_decode_pre_body(qkv_ref, hist_ref, sm_ref, cw_ref, arow_ref, dtrow_ref,
                         new_ref, w_ref, qg_ref, k_ref, u_ref, qk_ref, gl_ref):
    conv = _silu(_conv_step(qkv_ref[...], hist_ref, cw_ref, new_ref))
    nq = H_B * DK_B
    sm = sm_ref[...]
    beta_all = jax.nn.sigmoid(sm)
    g_all = -jnp.exp(arow_ref[...]) * _softplus(sm + dtrow_ref[...])
    for h in range(H_B):
        hs = slice(h * DK_B, (h + 1) * DK_B)
        qh = _l2norm(conv[:, hs]) * (DK_B ** -0.5)
        kh = _l2norm(conv[:, nq + h * DK_B:nq + (h + 1) * DK_B])
        vh = conv[:, 2 * nq + h * DV_B:2 * nq + (h + 1) * DV_B]
        beta = beta_all[:, h:h + 1]
        eg = jnp.exp(g_all[:, H_B + h:H_B + h + 1])
        w_ref[:, hs] = kh * beta * eg
        qg_ref[:, hs] = qh * eg
        k_ref[:, hs] = kh
        u_ref[:, hs] = vh * beta
        qk_ref[:, hs] = jnp.broadcast_to(jnp.sum(qh * kh, axis=-1, keepdims=True), qh.shape)
        gl_ref[:, hs] = jnp.broadcast_to(eg, qh.shape)


def _gdn_decode_state_body(s_ref, w_ref, qg_ref, k_ref, u_ref, qk_ref, gl_ref, z_ref, gn_ref, done_ref,
                           s_out_ref, o_ref):
    del done_ref
    sub = SUBLANES
    row = lax.broadcasted_iota(jnp.int32, (sub, 1), 0)
    for r0 in range(0, s_ref.shape[0], sub):
        rows = slice(r0, r0 + sub)
        for h in range(H_B):
            hs = slice(h * DK_B, (h + 1) * DK_B)
            wq = jnp.concatenate([w_ref[rows, hs], qg_ref[rows, hs]], axis=0).astype(BF16)
            ws = jnp.zeros((sub, DV_B), F32)
            qs = jnp.zeros((sub, DV_B), F32)
            for bb in range(sub):
                res = _dot(wq, s_ref[r0 + bb, h].astype(BF16))
                ws = jnp.where(row == bb, res[:sub], ws)
                qs = jnp.where(row == bb, res[sub:], qs)
            v_new = u_ref[rows, hs] - ws
            o = qs + qk_ref[rows, hs] * v_new
            o_ref[rows, hs] = _rms(o, gn_ref[...]) * _silu(z_ref[rows, hs])
            k = k_ref[rows, hs]
            v16 = v_new.astype(BF16)
            gl = gl_ref[rows, hs]
            for bb in range(sub):
                k_one = jnp.where(row == bb, k, 0.0).astype(BF16)
                s_out_ref[r0 + bb, h] = s_ref[r0 + bb, h] * gl[bb:bb + 1, :] + _dot_tn(k_one, v16)


def _gdn_decode(qkvb, hist, z, small, conv_w, a_log, dt_bias, gnorm, s_all, s_done, e):
    bn = qkvb.shape[0]
    pad_row = lambda v: jnp.zeros((1, LANES), F32).at[0, H_B:2 * H_B].set(v)
    wide = jax.ShapeDtypeStruct((bn, H_B * DK_B), F32)
    new_conv, w, qg, k, u, qk, gl = pl.pallas_call(
        _gdn_decode_pre_body,
        out_shape=[jax.ShapeDtypeStruct(hist.shape, F32)] + [wide] * 6,
        compiler_params=pltpu.CompilerParams(vmem_limit_bytes=VMEM_LIMIT_BYTES),
        name="gdn_decode_pre",
    )(qkvb, hist, small, conv_w, pad_row(a_log), pad_row(dt_bias))
    nb = GDN_STATE_SEQS
    vec = pl.BlockSpec((nb, H_B * DK_B), lambda i: (i, 0))
    st = pl.BlockSpec((None, nb, H_B, DK_B, DV_B), lambda i: (e, i, 0, 0, 0))
    s_done, o = pl.pallas_call(
        _gdn_decode_state_body,
        grid=(bn // nb,),
        in_specs=[st, vec, vec, vec, vec, vec, vec, vec, _whole((1, DV_B)), pl.BlockSpec(memory_space=pl.ANY)],
        out_specs=[st, vec],
        out_shape=[jax.ShapeDtypeStruct(s_all.shape, F32), wide],
        input_output_aliases={9: 0},
        compiler_params=_cp("parallel"),
        name="gdn_decode_state",
    )(s_all, w, qg, k, u, qk, gl, z, gnorm.reshape(1, DV_B), s_done)
    return o, new_conv, s_done


def _head_expander(width):
    e = np.zeros((LANES, H_C * width), np.float32)
    for h in range(H_C):
        e[h, h * width:(h + 1) * width] = 1.0
    return jnp.asarray(np.tile(e, (3, 1)), BF16)


def _group_rms(y, g):
    gs = D_INNER // G_C
    parts = [_rms(y[:, i * gs:(i + 1) * gs], g[:, i * gs:(i + 1) * gs]) for i in range(G_C)]
    return jnp.concatenate(parts, axis=1)


def _ssd_prompt_body(z_ref, xbc_ref, sm_ref, smt_ref, cw_ref, cb_ref, dtrow_ref, arow_ref, dtcol_ref,
                     acol_ref, dx_ref, gn_ref, e_ref, o_ref, s_out_ref, xbuf_ref, s_ref):
    c = pl.program_id(1)
    rows = xbc_ref.shape[0]

    @pl.when(c == 0)
    def _():
        xbuf_ref[0:SUBLANES, :] = jnp.zeros((SUBLANES, xbuf_ref.shape[1]), F32)
        s_ref[...] = jnp.zeros_like(s_ref)

    xbc = _silu(_conv_chunk(xbc_ref, xbuf_ref, cw_ref, rows) + cb_ref[...])
    xs = xbc[:, :D_INNER]
    bm = xbc[:, D_INNER:D_INNER + G_C * N_C]
    cm = xbc[:, D_INNER + G_C * N_C:]
    e = e_ref[...]
    lower = _tri(rows, "lower")
    dt = _softplus(sm_ref[...] + dtrow_ref[...])
    acum = _dot_hi(lower.astype(F32), dt * -jnp.exp(arow_ref[...]))
    a_t = _softplus(smt_ref[...] + dtcol_ref[...]) * -jnp.exp(acol_ref[...])
    acum_t = _dot_hi(a_t, _tri(rows, "upper").astype(F32))
    xdt = xs * _expand(dt, e)
    xdte = (xdt * _expand(jnp.exp(acum[rows - 1:rows, :] - acum), e)).astype(BF16)
    scale_y = _expand(jnp.exp(acum), e)
    chunk_decay = scale_y[rows - 1:rows, :]
    xdt16 = xdt.astype(BF16)
    neg_upper = jnp.where(lower, 0.0, -jnp.inf)
    acum2, acum2_t = acum * LOG2E, acum_t * LOG2E
    gw = HPG * P_C
    ys = []
    for g in range(G_C):
        bg = bm[:, g * N_C:(g + 1) * N_C]
        cg16 = cm[:, g * N_C:(g + 1) * N_C].astype(BF16)
        cb = _dot_nt(cg16, bg.astype(BF16))
        yg = []
        for hh in range(HPG):
            h = g * HPG + hh
            lmat = jnp.exp2(acum2[:, h:h + 1] - acum2_t[h:h + 1, :] + neg_upper)
            yg.append(_dot((cb * lmat).astype(BF16), xdt16[:, h * P_C:(h + 1) * P_C]))
        gs = slice(g * gw, (g + 1) * gw)
        sg = s_ref[:, gs]
        y_off = _dot(cg16, sg.astype(BF16)) * scale_y[:, gs]
        ys.append(jnp.concatenate(yg, axis=1) + y_off)
        s_ref[:, gs] = sg * chunk_decay[:, gs] + _dot(bg.T.astype(BF16), xdte[:, gs])
    y = jnp.concatenate(ys, axis=1) + dx_ref[...] * xs
    y = y * _silu(z_ref[...])
    o_ref[...] = _group_rms(y, gn_ref[...]).astype(BF16)

    @pl.when(c == pl.num_programs(1) - 1)
    def _():
        s_out_ref[0] = s_ref[...].T.reshape(H_C, P_C, N_C)


def _ssd_small_params(dt_bias, a_log, ns):
    row = lambda v: jnp.zeros((1, LANES), F32).at[0, :H_C].set(v)
    col = lambda v: jnp.zeros((ns, 1), F32).at[:H_C, 0].set(v)
    return row(dt_bias), row(a_log), col(dt_bias), col(a_log)


def _ssd_prompt(z, xbc, small, small_t, conv_w, conv_b, dt_bias, a_log, d_skip, gnorm, bn, l):
    rows = SSD_CHUNK
    nc = l // rows
    t = bn * l
    ns = small_t.shape[0]
    dtrow, arow, dtcol, acol = _ssd_small_params(dt_bias, a_log, ns)
    tok = lambda n: pl.BlockSpec((rows, n), lambda b, c: (b * nc + c, 0))
    return pl.pallas_call(
        _ssd_prompt_body,
        grid=(bn, nc),
        in_specs=[
            tok(D_INNER), tok(SSD_CONV_CH), tok(LANES),
            pl.BlockSpec((ns, rows), lambda b, c: (0, b * nc + c)),
            _whole((CONV_W, SSD_CONV_CH)), _whole((1, SSD_CONV_CH)),
            _whole((1, LANES)), _whole((1, LANES)), _whole((ns, 1)), _whole((ns, 1)),
            _whole((1, D_INNER)), _whole((1, D_INNER)), _whole((3 * LANES, D_INNER)),
        ],
        out_specs=[tok(D_INNER), pl.BlockSpec((1, H_C, P_C, N_C), lambda b, c: (b, 0, 0, 0))],
        out_shape=[jax.ShapeDtypeStruct((t, D_INNER), BF16), jax.ShapeDtypeStruct((bn, H_C, P_C, N_C), F32)],
        scratch_shapes=[pltpu.VMEM((rows + SUBLANES, SSD_CONV_CH), F32), pltpu.VMEM((N_C, D_INNER), F32)],
        compiler_params=_cp("arbitrary", "arbitrary"),
        name="ssd_prompt",
    )(z, xbc, small, small_t, conv_w, conv_b.reshape(1, SSD_CONV_CH), dtrow, arow, dtcol, acol,
      jnp.repeat(d_skip, P_C).reshape(1, D_INNER), gnorm.reshape(1, D_INNER), _head_expander(P_C))


def _ssd_decode_pre_body(xbc_ref, hist_ref, sm_ref, cw_ref, cb_ref, dtrow_ref, arow_ref, e_ref, en_ref,
                         new_ref, xs_ref, xdt_ref, b_ref, c_ref, dax_ref, dan_ref):
    xbc = _silu(_conv_step(xbc_ref[...], hist_ref, cw_ref, new_ref) + cb_ref[...])
    xs = xbc[:, :D_INNER]
    dt = _softplus(sm_ref[...] + dtrow_ref[...])
    a = dt * -jnp.exp(arow_ref[...])
    xs_ref[...] = xs
    xdt_ref[...] = xs * _expand(dt, e_ref[...])
    b_ref[...] = xbc[:, D_INNER:D_INNER + G_C * N_C]
    c_ref[...] = xbc[:, D_INNER + G_C * N_C:]
    dax_ref[...] = jnp.exp(_expand(a, e_ref[...]))
    dan_ref[...] = jnp.exp(_expand(a, en_ref[...]))


def _ssd_decode_state_body(s_ref, xdt_ref, b_ref, c_ref, dan_ref, done_ref, s_out_ref, yoff_ref):
    del done_ref
    gw = HPG * P_C
    sub = SUBLANES
    row = lax.broadcasted_iota(jnp.int32, (sub, 1), 0)
    for r0 in range(0, s_ref.shape[0], sub):
        rows = slice(r0, r0 + sub)
        xdt = xdt_ref[rows, :]
        b16 = b_ref[rows, :].astype(BF16)
        c16 = c_ref[rows, :].astype(BF16)
        dan = dan_ref[rows, :]
        yoff = jnp.zeros((sub, gw), F32)
        for bb in range(sub):
            s = s_ref[r0 + bb].reshape(gw, N_C)
            res = _dot_nt(c16, s.astype(BF16))
            yoff = jnp.where(row == bb, res, yoff)
            x_one = jnp.where(row == bb, xdt, 0.0).astype(BF16)
            upd = _dot_tn(x_one, b16)
            for hh in range(HPG):
                rs = slice(hh * P_C, (hh + 1) * P_C)
                s_out_ref[r0 + bb, hh] = s[rs] * dan[bb:bb + 1, hh * N_C:(hh + 1) * N_C] + upd[rs]
        yoff_ref[rows, :] = yoff


def _ssd_decode_post_body(yoff_ref, dax_ref, xdt_ref, xs_ref, b_ref, c_ref, z_ref, dx_ref, gn_ref, o_ref):
    gw = HPG * P_C
    bc = b_ref[...] * c_ref[...]
    cbx = [jnp.broadcast_to(jnp.sum(bc[:, g * N_C:(g + 1) * N_C], axis=-1, keepdims=True), (bc.shape[0], gw))
           for g in range(G_C)]
    y = yoff_ref[...] * dax_ref[...] + jnp.concatenate(cbx, axis=1) * xdt_ref[...]
    y = y + dx_ref[...] * xs_ref[...]
    y = y * _silu(z_ref[...])
    o_ref[...] = _group_rms(y, gn_ref[...])


def _ssd_decode(z, xbc, hist, small, conv_w, conv_b, dt_bias, a_log, d_skip, gnorm, s_all, s_done, e):
    bn = xbc.shape[0]
    dtrow, arow, _, _ = _ssd_small_params(dt_bias, a_log, H_C)
    wide = jax.ShapeDtypeStruct((bn, D_INNER), F32)
    grp = jax.ShapeDtypeStruct((bn, G_C * N_C), F32)
    plain = pltpu.CompilerParams(vmem_limit_bytes=VMEM_LIMIT_BYTES)
    new_conv, xs, xdt, bm, cm, dax, dan = pl.pallas_call(
        _ssd_decode_pre_body,
        out_shape=[jax.ShapeDtypeStruct(hist.shape, F32), wide, wide, grp, grp, wide,
                   jax.ShapeDtypeStruct((bn, H_C * N_C), F32)],
        compiler_params=plain,
        name="ssd_decode_pre",
    )(xbc, hist, small, conv_w, conv_b.reshape(1, SSD_CONV_CH), dtrow, arow,
      _head_expander(P_C), _head_expander(N_C))
    nb = SSD_STATE_SEQS
    gw = HPG * P_C
    st = pl.BlockSpec((None, nb, HPG, P_C, N_C), lambda i, g: (e, i, g, 0, 0))
    s_done, yoff = pl.pallas_call(
        _ssd_decode_state_body,
        grid=(bn // nb, G_C),
        in_specs=[st, pl.BlockSpec((nb, gw), lambda i, g: (i, g)), pl.BlockSpec((nb, N_C), lambda i, g: (i, g)),
                  pl.BlockSpec((nb, N_C), lambda i, g: (i, g)), pl.BlockSpec((nb, HPG * N_C), lambda i, g: (i, g)),
                  pl.BlockSpec(memory_space=pl.ANY)],
        out_specs=[st, pl.BlockSpec((nb, gw), lambda i, g: (i, g))],
        out_shape=[jax.ShapeDtypeStruct(s_all.shape, F32), wide],
        input_output_aliases={5: 0},
        compiler_params=_cp("parallel", "parallel"),
        name="ssd_decode_state",
    )(s_all, xdt, bm, cm, dan, s_done)
    mix = pl.pallas_call(
        _ssd_decode_post_body,
        out_shape=wide,
        compiler_params=plain,
        name="ssd_decode_post",
    )(yoff, dax, xdt, xs, bm, cm, z, jnp.repeat(d_skip, P_C).reshape(1, D_INNER), gnorm.reshape(1, D_INNER))
    return mix, new_conv, s_done


def _narrow_weights(w_small):
    n = w_small.shape[1]
    return jnp.zeros((D_MODEL, LANES), BF16).at[:, :n].set(w_small.astype(BF16))


def _even_q_pad(qa):
    bn = qa.shape[0]
    q = qa.reshape(bn, KV_A, G_A, HD_A)
    out = jnp.zeros((bn, KV_A, G_A, KV_A, HD_A), F32)
    for kv in range(KV_A):
        out = out.at[:, kv, :, kv, :].set(q[:, kv])
    return out.reshape(bn, H_A, KV_A * HD_A)


def _even_o_unpad(o8):
    bn = o8.shape[0]
    o = o8.reshape(bn, KV_A, G_A, KV_A, HD_A)
    return jnp.concatenate([o[:, kv, :, kv, :].reshape(bn, G_A * HD_A) for kv in range(KV_A)], axis=1)


def _trunk(x3, states, wts):
    (rel_bias, norm_ff1, norm_mix, norm_ff2, norm_final,
     ff1, ff2, even_w_in, even_w_out, swa_sinks, gdn_conv_w, gdn_A_log, gdn_dt_bias, gdn_norm,
     ssd_w_in, ssd_w_out, ssd_conv_w, ssd_conv_b, ssd_dt_bias, ssd_A_log, ssd_D, ssd_norm) = wts
    bn, l, d = x3.shape
    t = bn * l
    x = x3.reshape(t, d)
    decode = states is not None
    ks, vs, gconv, gssm, sconv, sssm = [], [], [], [], [], []
    depth = norm_ff1.shape[0]
    width = KV_A * HD_A
    if decode:
        gssm_done, sssm_done = jnp.zeros_like(states[3]), jnp.zeros_like(states[5])
        n_even = states[0].shape[0]
        cache_t = lambda c: c.transpose(0, 1, 3, 4, 2).reshape(n_even, bn, width, WINDOW)
        ckt, cvt = cache_t(states[0]), cache_t(states[1])
        ghist, shist = states[2].transpose(0, 2, 1, 3), states[4].transpose(0, 2, 1, 3)
    for layer in range(depth):
        x = _ffn(x, norm_ff1[layer], ff1, layer)
        e = layer // 2
        if layer % 2 == 0:
            w_all, ws = even_w_in
            qa, kv, qkvb, z, small, small_t = _inproj(x, norm_mix[layer], w_all, e, ws[e], 2 * SUBLANES,
                                                      (A_Q, 2 * width, B_QKV, B_Z))
            if decode:
                o8 = _swa_decode(_even_q_pad(qa), kv, ckt, cvt, e, rel_bias, swa_sinks[e])
                o_a = _even_o_unpad(o8)
                o_b, new_conv, gssm_done = _gdn_decode(qkvb, ghist[e], z, small, gdn_conv_w[e], gdn_A_log[e],
                                                       gdn_dt_bias[e], gdn_norm[e], states[3], gssm_done, e)
                new_k, new_v = kv[:, :width], kv[:, width:]
                new_conv = new_conv.transpose(1, 0, 2)
            else:
                o_a = _swa_prompt(qa, kv, rel_bias, swa_sinks[e], bn, l)
                o_b, s_new = _gdn_prompt(qkvb, z, small, small_t, gdn_conv_w[e], gdn_A_log[e],
                                         gdn_dt_bias[e], gdn_norm[e], bn, l)
                kv3 = kv.reshape(bn, l, 2 * width)
                new_k = kv3[:, l - WINDOW:, :width]
                new_v = kv3[:, l - WINDOW:, width:]
                new_conv = qkvb.reshape(bn, l, B_QKV)[:, l - (CONV_W - 1):]
                gssm.append(s_new)
            ks.append(new_k)
            vs.append(new_v)
            gconv.append(new_conv)
            mixes = [(o_a, even_w_out, e, 0), (o_b, even_w_out, e, 1)]
        else:
            w_all, ws = ssd_w_in
            z, xbc, small, small_t = _inproj(x, norm_mix[layer], w_all, e, ws[e], H_C, (D_INNER, SSD_CONV_CH))
            if decode:
                mix, new_conv, sssm_done = _ssd_decode(z, xbc, shist[e], small, ssd_conv_w[e], ssd_conv_b[e],
                                                       ssd_dt_bias[e], ssd_A_log[e], ssd_D[e], ssd_norm[e],
                                                       states[5], sssm_done, e)
                new_conv = new_conv.transpose(1, 0, 2)
            else:
                mix, s_new = _ssd_prompt(z, xbc, small, small_t, ssd_conv_w[e], ssd_conv_b[e], ssd_dt_bias[e],
                                         ssd_A_log[e], ssd_D[e], ssd_norm[e], bn, l)
                new_conv = xbc.reshape(bn, l, SSD_CONV_CH)[:, l - (CONV_W - 1):]
                sssm.append(s_new)
            sconv.append(new_conv)
            mixes = [(mix, ssd_w_out, e, 0)]
        x = _ffn(x, norm_ff2[layer], ff2, layer, mixes=mixes,
                 g_final=norm_final if layer == depth - 1 else None)
    if decode:
        bs = DECODE_SEQ_BLOCK
        new_t = lambda rows: jnp.stack(rows).reshape(n_even, bn // bs, bs, width).transpose(0, 1, 3, 2)
        back = lambda c: c.reshape(n_even, bn, KV_A, HD_A, WINDOW).transpose(0, 1, 4, 2, 3)
        ks = back(_cache_shift(ckt, new_t(ks)))
        vs = back(_cache_shift(cvt, new_t(vs)))
        gssm, sssm = gssm_done, sssm_done
    else:
        kv5 = (len(ks), bn, WINDOW, KV_A, HD_A)
        ks, vs = jnp.stack(ks).reshape(kv5), jnp.stack(vs).reshape(kv5)
        gssm, sssm = jnp.stack(gssm), jnp.stack(sssm)
    return (x.reshape(bn, l, d), ks, vs, jnp.stack(gconv), gssm, jnp.stack(sconv), sssm)


def kernel(x_prompt, x_sample, cache_swa_k, cache_swa_v, state_gdn_conv, state_gdn_ssm, state_ssd_conv, state_ssd_ssm, rel_bias, norm_ff1, norm_mix, norm_ff2, norm_final, ff1_gate, ff1_up, ff1_down, ff2_gate, ff2_up, ff2_down, even_w_in, even_w_out, swa_sinks, gdn_conv_w, gdn_A_log, gdn_dt_bias, gdn_norm, ssd_w_in, ssd_w_out, ssd_conv_w, ssd_conv_b, ssd_dt_bias, ssd_A_log, ssd_D, ssd_norm):
    depth = norm_ff1.shape[0]
    ff1 = (ff1_gate.astype(BF16), ff1_up.astype(BF16), ff1_down.astype(BF16))
    ff2 = (ff2_gate.astype(BF16), ff2_up.astype(BF16), ff2_down.astype(BF16))
    n_even_main = A_Q + 2 * KV_A * HD_A + B_QKV + B_Z
    even_in = (even_w_in.astype(BF16),
               [_narrow_weights(even_w_in[e][:, n_even_main:]) for e in range(even_w_in.shape[0])])
    n_odd_main = D_INNER + SSD_CONV_CH
    odd_in = (ssd_w_in.astype(BF16),
              [_narrow_weights(ssd_w_in[e][:, n_odd_main:]) for e in range(ssd_w_in.shape[0])])
    wts = (rel_bias, norm_ff1, norm_mix, norm_ff2, norm_final, ff1, ff2,
           even_in, even_w_out.astype(BF16), swa_sinks, gdn_conv_w, gdn_A_log, gdn_dt_bias, gdn_norm,
           odd_in, ssd_w_out.astype(BF16), ssd_conv_w, ssd_conv_b, ssd_dt_bias, ssd_A_log, ssd_D, ssd_norm)
    y_p, p_k, p_v, p_gconv, p_gssm, p_sconv, p_sssm = _trunk(x_prompt, None, wts)
    states = (cache_swa_k, cache_swa_v, state_gdn_conv, state_gdn_ssm, state_ssd_conv, state_ssd_ssm)
    y_s, s_k, s_v, s_gconv, s_gssm, s_sconv, s_sssm = _trunk(x_sample, states, wts)
    return (y_p, y_s, p_k, p_v, p_gconv, p_gssm, p_sconv, p_sssm,
            s_k, s_v, s_gconv, s_gssm, s_sconv, s_sssm)
```

```python
import functools
import math

import numpy as np
import jax
import jax.numpy as jnp
from jax import lax
from jax.experimental import pallas as pl
from jax.experimental.pallas import tpu as pltpu

F32 = jnp.float32
BF16 = jnp.bfloat16
HI = lax.Precision.HIGHEST

EPS = 1e-6
NEG_INF = -1e30
LOG2E = math.log2(math.e)
D_MODEL = 1024
WINDOW = 128
BLOCK = 128
H_A, KV_A, G_A, HD_A = 8, 2, 4, 64
N_BUCKETS, MAX_DIST = 32, 128
H_B, DK_B, DV_B = 4, 128, 128
CONV_W = 4
GDN_CHUNK = 64
D_INNER = 2048
P_C, H_C, N_C, G_C = 64, 32, 128, 4
HPG = H_C // G_C
SSD_CHUNK = 128
A_Q = H_A * HD_A
B_QKV = 3 * H_B * DK_B
B_Z = H_B * DV_B
SSD_CONV_CH = D_INNER + 2 * G_C * N_C
LANES = 128
SUBLANES = 8
VMEM_LIMIT_BYTES = 56 * 1024 * 1024
ROW_TILE = 512
SWA_QUERY_BLOCKS = 4
GDN_PREP_CHUNKS = 4
GDN_SCAN_GROUP = 2
DECODE_SEQ_BLOCK = 32
SSD_STATE_SEQS = 4 * SUBLANES
GDN_STATE_SEQS = 2 * SUBLANES


def _cp(*sem):
    return pltpu.CompilerParams(dimension_semantics=sem, vmem_limit_bytes=VMEM_LIMIT_BYTES)


def _whole(shape):
    nd = len(shape)
    return pl.BlockSpec(shape, lambda *_: (0,) * nd, pipeline_mode=pl.Buffered(1))


def _rms(x, g):
    return x * lax.rsqrt(jnp.mean(x * x, axis=-1, keepdims=True) + EPS) * g


def _silu(x):
    h = 0.5 * x
    return h * jnp.tanh(h) + h


def _softplus(x):
    return jnp.maximum(x, 0.0) + jnp.log1p(jnp.exp(-jnp.abs(x)))


def _dot(a, b):
    return jnp.dot(a, b, preferred_element_type=F32)


def _dot_nt(a, b):
    return lax.dot_general(a, b, (((1,), (1,)), ((), ())), preferred_element_type=F32)


def _dot_tn(a, b):
    return lax.dot_general(a, b, (((0,), (0,)), ((), ())), preferred_element_type=F32)


def _dot_hi(a, b):
    return jnp.dot(a, b, precision=HI, preferred_element_type=F32)


def _expand(x, e3):
    hi = x.astype(BF16)
    r = x - hi.astype(F32)
    mid = r.astype(BF16)
    lo = (r - mid.astype(F32)).astype(BF16)
    return _dot(jnp.concatenate([hi, mid, lo], axis=1), e3)


def _tri(n, kind):
    r = lax.broadcasted_iota(jnp.int32, (n, n), 0)
    c = lax.broadcasted_iota(jnp.int32, (n, n), 1)
    return {"lower": r >= c, "strict_lower": r > c, "upper": r <= c}[kind]


def _ffn_body(*refs, n_mix, final):
    refs = list(refs)
    x = refs.pop(0)[...]
    for _ in range(n_mix):
        m_ref, wo_ref = refs.pop(0), refs.pop(0)
        x = x + _dot(m_ref[...].astype(BF16), wo_ref[...])
    g_ref, wg_ref, wu_ref, wd_ref = refs[:4]
    o_ref = refs[-1]
    hn = _rms(x, g_ref[...]).astype(BF16)
    act = (_silu(_dot(hn, wg_ref[...])) * _dot(hn, wu_ref[...])).astype(BF16)
    y = x + 0.5 * _dot(act, wd_ref[...])
    if final:
        y = _rms(y, refs[4][...])
    o_ref[...] = y


def _layer_slab(w_all, layer, rows=None, row_block=0):
    shape = (rows or w_all.shape[1], w_all.shape[2])
    return pl.BlockSpec((None,) + shape, lambda i: (layer, row_block, 0), pipeline_mode=pl.Buffered(1))


def _ffn(x, g, ffw, layer, mixes=(), g_final=None):
    t, d = x.shape
    tm = min(t, ROW_TILE)
    final = g_final is not None
    row = pl.BlockSpec((tm, d), lambda i: (i, 0))
    in_specs, args = [row], [x]
    for m, w_all, e, row_block in mixes:
        in_specs += [pl.BlockSpec((tm, m.shape[1]), lambda i: (i, 0)), _layer_slab(w_all, e, m.shape[1], row_block)]
        args += [m, w_all]
    in_specs += [_whole((1, d))] + [_layer_slab(w, layer) for w in ffw]
    args += [g.reshape(1, d), *ffw]
    if final:
        in_specs.append(_whole((1, d)))
        args.append(g_final.reshape(1, d))
    return pl.pallas_call(
        functools.partial(_ffn_body, n_mix=len(mixes), final=final),
        grid=(t // tm,),
        in_specs=in_specs,
        out_specs=row,
        out_shape=jax.ShapeDtypeStruct((t, d), F32),
        compiler_params=_cp("parallel"),
        name="ffn",
    )(*args)


def _inproj_body(x_ref, g_ref, w_ref, ws_ref, *outs, splits):
    hn = _rms(x_ref[...], g_ref[...]).astype(BF16)
    off = 0
    for o_ref, n in zip(outs[:-2], splits):
        o_ref[...] = _dot(hn, w_ref[:, off:off + n])
        off += n
    small = _dot(hn, ws_ref[...])
    outs[-2][...] = small
    outs[-1][...] = small.T[:outs[-1].shape[0]]


def _inproj(x, g, w_all, e, ws, ns, splits):
    t, d = x.shape
    w_spec = _layer_slab(w_all, e)
    tm = min(t, ROW_TILE)
    row = pl.BlockSpec((tm, d), lambda i: (i, 0))
    out_specs = [pl.BlockSpec((tm, n), lambda i: (i, 0)) for n in splits]
    out_specs += [pl.BlockSpec((tm, LANES), lambda i: (i, 0)), pl.BlockSpec((ns, tm), lambda i: (0, i))]
    out_shape = [jax.ShapeDtypeStruct((t, n), F32) for n in splits]
    out_shape += [jax.ShapeDtypeStruct((t, LANES), F32), jax.ShapeDtypeStruct((ns, t), F32)]
    return pl.pallas_call(
        functools.partial(_inproj_body, splits=splits),
        grid=(t // tm,),
        in_specs=[row, _whole((1, d)), w_spec, _whole(ws.shape)],
        out_specs=out_specs,
        out_shape=out_shape,
        compiler_params=_cp("parallel"),
        name="inproj",
    )(x, g.reshape(1, d), w_all, ws)


def _t5_bucket_np(dist):
    max_exact = N_BUCKETS // 2
    df = np.maximum(dist, max_exact).astype(np.float32)
    large = max_exact + (np.log(df / np.float32(max_exact)) / np.float32(math.log(MAX_DIST / max_exact))
                         * np.float32(N_BUCKETS - max_exact)).astype(np.int32)
    return np.where(dist < max_exact, dist, np.minimum(large, N_BUCKETS - 1)).astype(np.int32)


def _band_bucket_ids(n_q, n_k, offset):
    d = offset + np.arange(n_q)[:, None] - np.arange(n_k)[None, :]
    valid = (d >= 0) & (d <= WINDOW)
    return np.where(valid, _t5_bucket_np(np.clip(d, 0, WINDOW)), -1).astype(np.int32)


def _bias_from_buckets(bid, rb_ref, h):
    acc = jnp.full(bid.shape, NEG_INF, F32)
    for bk in range(N_BUCKETS):
        acc = jnp.where(bid == bk, rb_ref[bk, h], acc)
    return acc


def _swa_prompt_body(bid_ref, rb_ref, sk_ref, q_ref, kvp_ref, kvc_ref, o_ref, bias_ref):
    first_step = (pl.program_id(0) == 0) & (pl.program_id(1) == 0)

    @pl.when(first_step)
    def _():
        bid = bid_ref[...]
        col = lax.broadcasted_iota(jnp.int32, (BLOCK, 2 * BLOCK), 1)
        for h in range(H_A):
            bias = _bias_from_buckets(bid, rb_ref, h)
            bias_ref[h] = bias
            bias_ref[H_A + h] = jnp.where(col < BLOCK, NEG_INF, bias)

    kvp = kvp_ref[...]
    kvc = kvc_ref[...]
    scale = HD_A ** -0.5
    heads = range(H_A)
    for j in range(q_ref.shape[0] // BLOCK):
        rows = slice(j * BLOCK, (j + 1) * BLOCK)
        prev = kvp if j == 0 else kvc[(j - 1) * BLOCK:j * BLOCK]
        first_block = jnp.where(pl.program_id(1) == 0, H_A, 0) if j == 0 else 0
        k, v = [], []
        for kv in range(KV_A):
            ks = slice(kv * HD_A, (kv + 1) * HD_A)
            vs = slice(KV_A * HD_A + kv * HD_A, KV_A * HD_A + (kv + 1) * HD_A)
            k.append(jnp.concatenate([prev[:, ks], kvc[rows, ks]], axis=0).astype(BF16))
            v.append(jnp.concatenate([prev[:, vs], kvc[rows, vs]], axis=0).astype(BF16))
        s = [_dot_nt((q_ref[rows, h * HD_A:(h + 1) * HD_A] * scale).astype(BF16), k[h // G_A])
             + bias_ref[first_block + h] for h in heads]
        m = [jnp.maximum(jnp.max(s[h], axis=-1, keepdims=True), sk_ref[h]) for h in heads]
        p = [jnp.exp(s[h] - m[h]) for h in heads]
        denom = [jnp.sum(p[h], axis=-1, keepdims=True) + jnp.exp(sk_ref[h] - m[h]) for h in heads]
        outs = [_dot(p[h].astype(BF16), v[h // G_A]) / denom[h] for h in heads]
        o_ref[rows, :] = jnp.concatenate(outs, axis=1).astype(BF16)


def _swa_prompt(qa, kv, rel_bias, sinks, bn, l):
    qb = SWA_QUERY_BLOCKS
    nb = l // (qb * BLOCK)
    t = bn * l
    bid = jnp.asarray(_band_bucket_ids(BLOCK, 2 * BLOCK, BLOCK))
    width = 2 * KV_A * HD_A
    return pl.pallas_call(
        _swa_prompt_body,
        grid=(bn, nb),
        in_specs=[
            _whole((BLOCK, 2 * BLOCK)),
            pl.BlockSpec(memory_space=pltpu.SMEM),
            pl.BlockSpec(memory_space=pltpu.SMEM),
            pl.BlockSpec((qb * BLOCK, A_Q), lambda b, i: (b * nb + i, 0)),
            pl.BlockSpec((BLOCK, width), lambda b, i: (jnp.maximum((b * nb + i) * qb - 1, 0), 0)),
            pl.BlockSpec((qb * BLOCK, width), lambda b, i: (b * nb + i, 0)),
        ],
        out_specs=pl.BlockSpec((qb * BLOCK, A_Q), lambda b, i: (b * nb + i, 0)),
        out_shape=jax.ShapeDtypeStruct((t, A_Q), BF16),
        scratch_shapes=[pltpu.VMEM((2 * H_A, BLOCK, 2 * BLOCK), F32)],
        compiler_params=_cp("arbitrary", "arbitrary"),
        name="swa_prompt",
    )(bid, rel_bias, sinks, qa, kv, kv)


def _swa_decode_body(bid_ref, rb_ref, sk_ref, q_ref, kvn_ref, ckt_ref, cvt_ref, o_ref):
    bid = bid_ref[...]
    row = lax.broadcasted_iota(jnp.int32, (H_A, 1), 0)
    bias = jnp.zeros((H_A, bid.shape[1]), F32)
    sk = jnp.zeros((H_A, 1), F32)
    for h in range(H_A):
        bias = jnp.where(row == h, _bias_from_buckets(bid, rb_ref, h), bias)
        sk = jnp.where(row == h, sk_ref[h], sk)
    bias_c = bias[:, :WINDOW]
    bias_n = bias[:, WINDOW:WINDOW + 1]
    scale = HD_A ** -0.5
    q = q_ref[...]
    kvn = kvn_ref[...]
    width = KV_A * HD_A
    k_new = kvn[:, None, :width]
    v_new = kvn[:, None, width:]
    s = lax.dot_general(q.astype(BF16), ckt_ref[...].astype(BF16), (((2,), (1,)), ((0,), (0,))),
                        preferred_element_type=F32) * scale + bias_c[None]
    s_n = jnp.sum(q * k_new, axis=-1, keepdims=True) * scale + bias_n[None]
    m = jnp.maximum(jnp.maximum(jnp.max(s, axis=-1, keepdims=True), s_n), sk[None])
    p = jnp.exp(s - m)
    p_n = jnp.exp(s_n - m)
    denom = jnp.sum(p, axis=-1, keepdims=True) + p_n + jnp.exp(sk[None] - m)
    o = lax.dot_general((p / denom).astype(BF16), cvt_ref[...].astype(BF16), (((2,), (2,)), ((0,), (0,))),
                        preferred_element_type=F32)
    o_ref[...] = o + (p_n / denom) * v_new


def _swa_decode(q_pad, kv_new, cache_kt, cache_vt, e, rel_bias, sinks):
    bn = q_pad.shape[0]
    bs = DECODE_SEQ_BLOCK
    width = KV_A * HD_A
    ids = _band_bucket_ids(1, WINDOW + 1, WINDOW)
    bid = np.full((1, WINDOW + LANES), -1, np.int32)
    bid[:, :WINDOW + 1] = ids
    return pl.pallas_call(
        _swa_decode_body,
        grid=(bn // bs,),
        in_specs=[
            _whole((1, WINDOW + LANES)),
            pl.BlockSpec(memory_space=pltpu.SMEM),
            pl.BlockSpec(memory_space=pltpu.SMEM),
            pl.BlockSpec((bs, H_A, width), lambda i: (i, 0, 0)),
            pl.BlockSpec((bs, 2 * width), lambda i: (i, 0)),
            pl.BlockSpec((None, bs, width, WINDOW), lambda i: (e, i, 0, 0)),
            pl.BlockSpec((None, bs, width, WINDOW), lambda i: (e, i, 0, 0)),
        ],
        out_specs=pl.BlockSpec((bs, H_A, width), lambda i: (i, 0, 0)),
        out_shape=jax.ShapeDtypeStruct((bn, H_A, width), F32),
        compiler_params=_cp("parallel"),
        name="swa_decode",
    )(jnp.asarray(bid), rel_bias, sinks, q_pad, kv_new, cache_kt, cache_vt)


def _cache_shift_body(c_ref, nt_ref, o_ref):
    o_ref[...] = pltpu.roll(c_ref[...], WINDOW - 1, axis=2)
    nt = nt_ref[...]
    for b in range(c_ref.shape[0]):
        o_ref[b, :, WINDOW - 1:WINDOW] = nt[:, b:b + 1]


def _cache_shift(cache_t, new_t):
    n, bn, width, win = cache_t.shape
    bs = new_t.shape[3]
    blk = pl.BlockSpec((None, bs, width, win), lambda e, i: (e, i, 0, 0))
    return pl.pallas_call(
        _cache_shift_body,
        grid=(n, bn // bs),
        in_specs=[blk, pl.BlockSpec((None, None, width, bs), lambda e, i: (e, i, 0, 0))],
        out_specs=blk,
        out_shape=jax.ShapeDtypeStruct(cache_t.shape, F32),
        compiler_params=_cp("parallel", "parallel"),
        name="cache_shift",
    )(cache_t, new_t)


def _conv_rows(x_ref, xbuf_ref, cw_ref, rows):
    xbuf_ref[SUBLANES:SUBLANES + rows, :] = x_ref[...]
    x = xbuf_ref[...]
    out = x * cw_ref[0:1, :]
    for i in range(1, CONV_W):
        out = pltpu.roll(out, 1, axis=0) + x * cw_ref[i:i + 1, :]
    return out[SUBLANES:]


def _conv_chunk(x_ref, xbuf_ref, cw_ref, rows):
    out = _conv_rows(x_ref, xbuf_ref, cw_ref, rows)
    xbuf_ref[0:SUBLANES, :] = xbuf_ref[rows:rows + SUBLANES, :]
    return out


def _l2norm(x):
    return x * lax.rsqrt(jnp.sum(x * x, axis=-1, keepdims=True) + EPS)


def _chunk_masks(n, cc):
    r = lax.broadcasted_iota(jnp.int32, (n, n), 0)
    c = lax.broadcasted_iota(jnp.int32, (n, n), 1)
    shift = int(math.log2(cc))
    same = lax.shift_right_logical(r, shift) == lax.shift_right_logical(c, shift)
    return same, same & (r >= c), same & (r <= c)


def _dot_bf16x3(a, b):
    a_hi = a.astype(BF16)
    a_lo = (a - a_hi.astype(F32)).astype(BF16)
    b_hi = b.astype(BF16)
    b_lo = (b - b_hi.astype(F32)).astype(BF16)
    return _dot(a_hi, b_hi) + _dot(a_hi, b_lo) + _dot(a_lo, b_hi)


def _solve_masks(n, cc, base=SUBLANES):
    r = np.arange(n)[:, None]
    c = np.arange(n)[None, :]
    masks = [-(((r // base) == (c // base)) & (r > c)).astype(np.float32)]
    s = base
    while s < cc:
        sibling = ((r // (2 * s)) == (c // (2 * s))) & ((r // s) % 2 == 1) & ((c // s) % 2 == 0)
        masks.append(sibling.astype(np.float32))
        s *= 2
    return np.stack(masks)


def _gdn_prep_body(qkv_ref, halo_ref, sm_ref, smt_ref, cw_ref, acol_ref, dtcol_ref, tri_ref,
                   u_ref, w_ref, qg_ref, kd_ref, aqk_ref, gl_ref, xbuf_ref):
    rows = qkv_ref.shape[0]
    cc = GDN_CHUNK
    xbuf_ref[0:SUBLANES, :] = jnp.where(pl.program_id(1) == 0, 0.0, halo_ref[...])
    conv = _silu(_conv_rows(qkv_ref, xbuf_ref, cw_ref, rows))
    nq = H_B * DK_B
    beta_all = jax.nn.sigmoid(sm_ref[...])
    gt_all = -jnp.exp(acol_ref[...]) * _softplus(smt_ref[...] + dtcol_ref[...])
    pad = jnp.zeros((LANES - gt_all.shape[0], rows), F32)
    g_all = jnp.concatenate([gt_all, pad], axis=0).T
    same, lower, upper = _chunk_masks(rows, cc)
    gc = _dot_hi(lower.astype(F32), g_all)
    gct = _dot_hi(gt_all, upper.astype(F32))
    gsum = _dot_hi(same.astype(F32), g_all)
    neg_outside = jnp.where(lower, 0.0, -jnp.inf)
    gc2, gct2 = gc * LOG2E, gct * LOG2E
    heads = range(H_B)
    qh, kh16, kb, decay, egc = [], [], [], [], []
    for h in heads:
        q = _l2norm(conv[:, h * DK_B:(h + 1) * DK_B]) * (DK_B ** -0.5)
        k = _l2norm(conv[:, nq + h * DK_B:nq + (h + 1) * DK_B])
        v = conv[:, 2 * nq + h * DV_B:2 * nq + (h + 1) * DV_B]
        beta = beta_all[:, h:h + 1]
        gcol = gc[:, H_B + h:H_B + h + 1]
        decay.append(jnp.exp2(gc2[:, H_B + h:H_B + h + 1] - gct2[H_B + h:H_B + h + 1, :] + neg_outside))
        eg = jnp.exp(gcol)
        qg_ref[:, h * DK_B:(h + 1) * DK_B] = (q * eg).astype(BF16)
        kd_ref[:, h * DK_B:(h + 1) * DK_B] = (k * jnp.exp(gsum[:, H_B + h:H_B + h + 1] - gcol)).astype(BF16)
        qh.append(q.astype(BF16))
        kh16.append(k.astype(BF16))
        kb.append(k * beta)
        egc.append((v * beta, eg))
    m = [_dot_nt(kb[h].astype(BF16), kh16[h]) * decay[h] for h in heads]
    for h in heads:
        aqk = _dot_nt(qh[h], kh16[h]) * decay[h]
        blocks = [aqk[i * cc:(i + 1) * cc, i * cc:(i + 1) * cc] for i in range(rows // cc)]
        aqk_ref[:, h * cc:(h + 1) * cc] = jnp.concatenate(blocks, axis=0).astype(BF16)
    mm = lambda xs, ys: [_dot(x.astype(BF16), y.astype(BF16)) for x, y in zip(xs, ys)]
    a = [m[h] * tri_ref[0] for h in heads]
    a2 = mm(a, a)
    a3 = mm(a, a2)
    a4 = mm(a2, a2)
    x1 = [a[h] + a2[h] + a3[h] for h in heads]
    x1a4 = mm(x1, a4)
    r = [x1[h] + a4[h] + x1a4[h] for h in heads]
    for level in range(1, tri_ref.shape[0]):
        c = [m[h] * tri_ref[level] for h in heads]
        rc = mm(r, c)
        y = [c[h] + rc[h] for h in heads]
        yr = mm(y, r)
        r = [r[h] - y[h] - yr[h] for h in heads]
    for h in heads:
        vb, eg = egc[h]
        rhs = jnp.concatenate([vb, kb[h] * eg], axis=1)
        sol = rhs + _dot_bf16x3(r[h], rhs)
        u_ref[:, h * DV_B:(h + 1) * DV_B] = sol[:, :DV_B]
        w_ref[:, h * DK_B:(h + 1) * DK_B] = sol[:, DV_B:].astype(BF16)
    for i in range(rows // cc):
        gl = [jnp.broadcast_to(jnp.exp(gsum[i * cc:i * cc + 1, H_B + h:H_B + h + 1]), (SUBLANES, DV_B))
              for h in heads]
        gl_ref[0, i] = jnp.concatenate(gl, axis=1)


def _gdn_scan_body(u_ref, w_ref, qg_ref, kd_ref, aqk_ref, gl_ref, z_ref, gn_ref, o_ref, s_out_ref, s_ref):
    c = pl.program_id(0)
    bn = u_ref.shape[0]
    cc = GDN_CHUNK

    @pl.when(c == 0)
    def _():
        s_ref[...] = jnp.zeros_like(s_ref)

    gn = gn_ref[...]
    group = GDN_SCAN_GROUP
    for b0 in range(0, bn, group):
        combos = [(b, h) for b in range(b0, min(b0 + group, bn)) for h in range(H_B)]
        res = {}
        for b, h in combos:
            hs = slice(h * DK_B, (h + 1) * DK_B)
            wq = jnp.concatenate([w_ref[b, :, hs], qg_ref[b, :, hs]], axis=0)
            res[b, h] = _dot(wq, s_ref[b, h].astype(BF16))
        v16 = {}
        for b, h in combos:
            hs = slice(h * DV_B, (h + 1) * DV_B)
            v16[b, h] = (u_ref[b, :, hs] - res[b, h][:cc]).astype(BF16)
        for b, h in combos:
            hs = slice(h * DV_B, (h + 1) * DV_B)
            o = res[b, h][cc:] + _dot(aqk_ref[b, :, h * cc:(h + 1) * cc], v16[b, h])
            upd = _dot_tn(kd_ref[b, :, hs], v16[b, h])
            s_ref[b, h] = s_ref[b, h] * gl_ref[b, 0, 0:1, hs] + upd
            o_ref[b, :, hs] = (_rms(o, gn) * _silu(z_ref[b, :, hs])).astype(BF16)

    @pl.when(c == pl.num_programs(0) - 1)
    def _():
        s_out_ref[...] = s_ref[...]


def _gdn_prompt(qkvb, z, small, small_t, conv_w, a_log, dt_bias, gnorm, bn, l):
    rows = GDN_PREP_CHUNKS * GDN_CHUNK
    cc = GDN_CHUNK
    nb = l // rows
    nc = l // cc
    t = bn * l
    ns = small_t.shape[0]
    nq = H_B * DK_B
    pad_col = lambda v: jnp.zeros((ns, 1), F32).at[H_B:2 * H_B, 0].set(v)
    tri = jnp.asarray(_solve_masks(rows, cc))
    tok = lambda n: pl.BlockSpec((rows, n), lambda b, i: (b * nb + i, 0))
    halo = pl.BlockSpec((SUBLANES, B_QKV), lambda b, i: (jnp.maximum((b * nb + i) * (rows // SUBLANES) - 1, 0), 0))
    u, w, qg, kd, aqk, gl = pl.pallas_call(
        _gdn_prep_body,
        grid=(bn, nb),
        in_specs=[
            tok(B_QKV), halo, tok(LANES),
            pl.BlockSpec((ns, rows), lambda b, i: (0, b * nb + i)),
            _whole((CONV_W, B_QKV)), _whole((ns, 1)), _whole((ns, 1)), _whole(tri.shape),
        ],
        out_specs=[tok(nq), tok(nq), tok(nq), tok(nq), tok(H_B * cc),
                   pl.BlockSpec((1, rows // cc, SUBLANES, nq), lambda b, i: (b, i, 0, 0))],
        out_shape=[jax.ShapeDtypeStruct((t, nq), F32)] + [jax.ShapeDtypeStruct((t, nq), BF16)] * 3
        + [jax.ShapeDtypeStruct((t, H_B * cc), BF16), jax.ShapeDtypeStruct((bn, nc, SUBLANES, nq), F32)],
        scratch_shapes=[pltpu.VMEM((rows + SUBLANES, B_QKV), F32)],
        compiler_params=_cp("parallel", "parallel"),
        name="gdn_prep",
    )(qkvb, qkvb, small, small_t, conv_w, pad_col(a_log), pad_col(dt_bias), tri)
    seq = lambda n: pl.BlockSpec((bn, cc, n), lambda c: (0, c, 0))
    r3 = lambda a: a.reshape(bn, l, a.shape[1])
    o, s_new = pl.pallas_call(
        _gdn_scan_body,
        grid=(nc,),
        in_specs=[seq(nq), seq(nq), seq(nq), seq(nq), seq(H_B * cc),
                  pl.BlockSpec((bn, 1, SUBLANES, nq), lambda c: (0, c, 0, 0)), seq(B_Z), _whole((1, DV_B))],
        out_specs=[seq(B_Z), pl.BlockSpec((bn, H_B, DK_B, DV_B), lambda c: (0, 0, 0, 0))],
        out_shape=[jax.ShapeDtypeStruct((bn, l, B_Z), BF16), jax.ShapeDtypeStruct((bn, H_B, DK_B, DV_B), F32)],
        scratch_shapes=[pltpu.VMEM((bn, H_B, DK_B, DV_B), F32)],
        compiler_params=_cp("arbitrary"),
        name="gdn_scan",
    )(r3(u), r3(w), r3(qg), r3(kd), r3(aqk), gl, r3(z), gnorm.reshape(1, DV_B))
    return o.reshape(t, B_Z), s_new


def _conv_step(x, hist_ref, cw_ref, new_ref):
    out = hist_ref[0] * cw_ref[0:1, :]
    for i in range(1, CONV_W - 1):
        out = out + hist_ref[i] * cw_ref[i:i + 1, :]
    out = out + x * cw_ref[CONV_W - 1:CONV_W, :]
    for i in range(CONV_W - 2):
        new_ref[i] = hist_ref[i + 1]
    new_ref[CONV_W - 2] = x
    return out


def _gdn_decode_pre_body(qkv_ref, hist_ref, sm_ref, cw_ref, arow_ref, dtrow_ref,
                         new_ref, w_ref, qg_ref, k_ref, u_ref, qk_ref, gl_ref):
    conv = _silu(_conv_step(qkv_ref[...], hist_ref, cw_ref, new_ref))
    nq = H_B * DK_B
    sm = sm_ref[...]
    beta_all = jax.nn.sigmoid(sm)
    g_all = -jnp.exp(arow_ref[...]) * _softplus(sm + dtrow_ref[...])
    for h in range(H_B):
        hs = slice(h * DK_B, (h + 1) * DK_B)
        qh = _l2norm(conv[:, hs]) * (DK_B ** -0.5)
        kh = _l2norm(conv[:, nq + h * DK_B:nq + (h + 1) * DK_B])
        vh = conv[:, 2 * nq + h * DV_B:2 * nq + (h + 1) * DV_B]
        beta = beta_all[:, h:h + 1]
        eg = jnp.exp(g_all[:, H_B + h:H_B + h + 1])
        w_ref[:, hs] = kh * beta * eg
        qg_ref[:, hs] = qh * eg
        k_ref[:, hs] = kh
        u_ref[:, hs] = vh * beta
        qk_ref[:, hs] = jnp.broadcast_to(jnp.sum(qh * kh, axis=-1, keepdims=True), qh.shape)
        gl_ref[:, hs] = jnp.broadcast_to(eg, qh.shape)


def _first_or_aliased(first, compute, s_out_ref):
    if not first:
        compute()
        return
    slot = pl.program_id(0)
    pl.when(slot == 0)(compute)

    @pl.when(slot != 0)
    def _():
        s_out_ref[...] = jnp.zeros_like(s_out_ref)


def _gdn_decode_state_body(s_ref, w_ref, qg_ref, k_ref, u_ref, qk_ref, gl_ref, z_ref, gn_ref, *rest, first):
    s_out_ref, o_ref = rest[-2:]
    _first_or_aliased(first, functools.partial(
        _gdn_decode_state_compute, s_ref, w_ref, qg_ref, k_ref, u_ref, qk_ref, gl_ref, z_ref, gn_ref,
        s_out_ref, o_ref), s_out_ref)


def _gdn_decode_state_compute(s_ref, w_ref, qg_ref, k_ref, u_ref, qk_ref, gl_ref, z_ref, gn_ref, s_out_ref, o_ref):
    sub = SUBLANES
    row = lax.broadcasted_iota(jnp.int32, (sub, 1), 0)
    for r0 in range(0, s_ref.shape[0], sub):
        rows = slice(r0, r0 + sub)
        for h in range(H_B):
            hs = slice(h * DK_B, (h + 1) * DK_B)
            wq = jnp.concatenate([w_ref[rows, hs], qg_ref[rows, hs]], axis=0).astype(BF16)
            ws = jnp.zeros((sub, DV_B), F32)
            qs = jnp.zeros((sub, DV_B), F32)
            for bb in range(sub):
                res = _dot(wq, s_ref[r0 + bb, h].astype(BF16))
                ws = jnp.where(row == bb, res[:sub], ws)
                qs = jnp.where(row == bb, res[sub:], qs)
            v_new = u_ref[rows, hs] - ws
            o = qs + qk_ref[rows, hs] * v_new
            o_ref[rows, hs] = _rms(o, gn_ref[...]) * _silu(z_ref[rows, hs])
            k = k_ref[rows, hs]
            v16 = v_new.astype(BF16)
            gl = gl_ref[rows, hs]
            for bb in range(sub):
                k_one = jnp.where(row == bb, k, 0.0).astype(BF16)
                s_out_ref[r0 + bb, h] = s_ref[r0 + bb, h] * gl[bb:bb + 1, :] + _dot_tn(k_one, v16)


def _gdn_decode(qkvb, hist, z, small, conv_w, a_log, dt_bias, gnorm, s_all, s_done, e):
    bn = qkvb.shape[0]
    pad_row = lambda v: jnp.zeros((1, LANES), F32).at[0, H_B:2 * H_B].set(v)
    wide = jax.ShapeDtypeStruct((bn, H_B * DK_B), F32)
    new_conv, w, qg, k, u, qk, gl = pl.pallas_call(
        _gdn_decode_pre_body,
        out_shape=[jax.ShapeDtypeStruct(hist.shape, F32)] + [wide] * 6,
        compiler_params=pltpu.CompilerParams(vmem_limit_bytes=VMEM_LIMIT_BYTES),
        name="gdn_decode_pre",
    )(qkvb, hist, small, conv_w, pad_row(a_log), pad_row(dt_bias))
    nb = GDN_STATE_SEQS
    steps = bn // nb
    first = s_done is None
    blk = (None, nb, H_B, DK_B, DV_B)
    args = [s_all, w, qg, k, u, qk, gl, z, gnorm.reshape(1, DV_B)]
    if first:
        assert e == 0
        seq = lambda s, i: jnp.where(s == 0, i, steps - 1)
        vec = pl.BlockSpec((nb, H_B * DK_B), lambda s, i: (seq(s, i), 0))
        in_specs = [pl.BlockSpec(blk, lambda s, i: (0, seq(s, i), 0, 0, 0))] + [vec] * 7
        in_specs.append(pl.BlockSpec((1, DV_B), lambda s, i: (0, 0)))
        out_specs = [pl.BlockSpec(blk, lambda s, i: (s, i, 0, 0, 0)), vec]
        grid, aliases, sem = (s_all.shape[0], steps), {}, ("arbitrary", "arbitrary")
    else:
        vec = pl.BlockSpec((nb, H_B * DK_B), lambda i: (i, 0))
        st = pl.BlockSpec(blk, lambda i: (e, i, 0, 0, 0))
        in_specs = [st] + [vec] * 7 + [_whole((1, DV_B)), pl.BlockSpec(memory_space=pl.ANY)]
        out_specs = [st, vec]
        args.append(s_done)
        grid, aliases, sem = (steps,), {len(args) - 1: 0}, ("parallel",)
    s_done, o = pl.pallas_call(
        functools.partial(_gdn_decode_state_body, first=first),
        grid=grid,
        in_specs=in_specs,
        out_specs=out_specs,
        out_shape=[jax.ShapeDtypeStruct(s_all.shape, F32), wide],
        input_output_aliases=aliases,
        compiler_params=_cp(*sem),
        name="gdn_decode_state",
    )(*args)
    return o, new_conv, s_done


def _head_expander(width):
    e = np.zeros((LANES, H_C * width), np.float32)
    for h in range(H_C):
        e[h, h * width:(h + 1) * width] = 1.0
    return jnp.asarray(np.tile(e, (3, 1)), BF16)


def _group_rms(y, g):
    gs = D_INNER // G_C
    parts = [_rms(y[:, i * gs:(i + 1) * gs], g[:, i * gs:(i + 1) * gs]) for i in range(G_C)]
    return jnp.concatenate(parts, axis=1)


def _ssd_prompt_body(z_ref, xbc_ref, sm_ref, smt_ref, cw_ref, cb_ref, dtrow_ref, arow_ref, dtcol_ref,
                     acol_ref, dx_ref, gn_ref, e_ref, o_ref, s_out_ref, xbuf_ref, s_ref):
    c = pl.program_id(1)
    rows = xbc_ref.shape[0]

    @pl.when(c == 0)
    def _():
        xbuf_ref[0:SUBLANES, :] = jnp.zeros((SUBLANES, xbuf_ref.shape[1]), F32)
        s_ref[...] = jnp.zeros_like(s_ref)

    xbc = _silu(_conv_chunk(xbc_ref, xbuf_ref, cw_ref, rows) + cb_ref[...])
    xs = xbc[:, :D_INNER]
    bm = xbc[:, D_INNER:D_INNER + G_C * N_C]
    cm = xbc[:, D_INNER + G_C * N_C:]
    e = e_ref[...]
    lower = _tri(rows, "lower")
    dt = _softplus(sm_ref[...] + dtrow_ref[...])
    acum = _dot_hi(lower.astype(F32), dt * -jnp.exp(arow_ref[...]))
    a_t = _softplus(smt_ref[...] + dtcol_ref[...]) * -jnp.exp(acol_ref[...])
    acum_t = _dot_hi(a_t, _tri(rows, "upper").astype(F32))
    xdt = xs * _expand(dt, e)
    xdte = (xdt * _expand(jnp.exp(acum[rows - 1:rows, :] - acum), e)).astype(BF16)
    scale_y = _expand(jnp.exp(acum), e)
    chunk_decay = scale_y[rows - 1:rows, :]
    xdt16 = xdt.astype(BF16)
    neg_upper = jnp.where(lower, 0.0, -jnp.inf)
    acum2, acum2_t = acum * LOG2E, acum_t * LOG2E
    gw = HPG * P_C
    ys = []
    for g in range(G_C):
        bg = bm[:, g * N_C:(g + 1) * N_C]
        cg16 = cm[:, g * N_C:(g + 1) * N_C].astype(BF16)
        cb = _dot_nt(cg16, bg.astype(BF16))
        yg = []
        for hh in range(HPG):
            h = g * HPG + hh
            lmat = jnp.exp2(acum2[:, h:h + 1] - acum2_t[h:h + 1, :] + neg_upper)
            yg.append(_dot((cb * lmat).astype(BF16), xdt16[:, h * P_C:(h + 1) * P_C]))
        gs = slice(g * gw, (g + 1) * gw)
        sg = s_ref[:, gs]
        y_off = _dot(cg16, sg.astype(BF16)) * scale_y[:, gs]
        ys.append(jnp.concatenate(yg, axis=1) + y_off)
        s_ref[:, gs] = sg * chunk_decay[:, gs] + _dot(bg.T.astype(BF16), xdte[:, gs])
    y = jnp.concatenate(ys, axis=1) + dx_ref[...] * xs
    y = y * _silu(z_ref[...])
    o_ref[...] = _group_rms(y, gn_ref[...]).astype(BF16)

    @pl.when(c == pl.num_programs(1) - 1)
    def _():
        s_out_ref[0] = s_ref[...].T.reshape(H_C, P_C, N_C)


def _ssd_small_params(dt_bias, a_log, ns):
    row = lambda v: jnp.zeros((1, LANES), F32).at[0, :H_C].set(v)
    col = lambda v: jnp.zeros((ns, 1), F32).at[:H_C, 0].set(v)
    return row(dt_bias), row(a_log), col(dt_bias), col(a_log)


def _ssd_prompt(z, xbc, small, small_t, conv_w, conv_b, dt_bias, a_log, d_skip, gnorm, bn, l):
    rows = SSD_CHUNK
    nc = l // rows
    t = bn * l
    ns = small_t.shape[0]
    dtrow, arow, dtcol, acol = _ssd_small_params(dt_bias, a_log, ns)
    tok = lambda n: pl.BlockSpec((rows, n), lambda b, c: (b * nc + c, 0))
    return pl.pallas_call(
        _ssd_prompt_body,
        grid=(bn, nc),
        in_specs=[
            tok(D_INNER), tok(SSD_CONV_CH), tok(LANES),
            pl.BlockSpec((ns, rows), lambda b, c: (0, b * nc + c)),
            _whole((CONV_W, SSD_CONV_CH)), _whole((1, SSD_CONV_CH)),
            _whole((1, LANES)), _whole((1, LANES)), _whole((ns, 1)), _whole((ns, 1)),
            _whole((1, D_INNER)), _whole((1, D_INNER)), _whole((3 * LANES, D_INNER)),
        ],
        out_specs=[tok(D_INNER), pl.BlockSpec((1, H_C, P_C, N_C), lambda b, c: (b, 0, 0, 0))],
        out_shape=[jax.ShapeDtypeStruct((t, D_INNER), BF16), jax.ShapeDtypeStruct((bn, H_C, P_C, N_C), F32)],
        scratch_shapes=[pltpu.VMEM((rows + SUBLANES, SSD_CONV_CH), F32), pltpu.VMEM((N_C, D_INNER), F32)],
        compiler_params=_cp("arbitrary", "arbitrary"),
        name="ssd_prompt",
    )(z, xbc, small, small_t, conv_w, conv_b.reshape(1, SSD_CONV_CH), dtrow, arow, dtcol, acol,
      jnp.repeat(d_skip, P_C).reshape(1, D_INNER), gnorm.reshape(1, D_INNER), _head_expander(P_C))


def _ssd_decode_pre_body(xbc_ref, hist_ref, sm_ref, cw_ref, cb_ref, dtrow_ref, arow_ref, e_ref, en_ref,
                         new_ref, xs_ref, xdt_ref, b_ref, c_ref, dax_ref, dan_ref):
    xbc = _silu(_conv_step(xbc_ref[...], hist_ref, cw_ref, new_ref) + cb_ref[...])
    xs = xbc[:, :D_INNER]
    dt = _softplus(sm_ref[...] + dtrow_ref[...])
    a = dt * -jnp.exp(arow_ref[...])
    xs_ref[...] = xs
    xdt_ref[...] = xs * _expand(dt, e_ref[...])
    b_ref[...] = xbc[:, D_INNER:D_INNER + G_C * N_C]
    c_ref[...] = xbc[:, D_INNER + G_C * N_C:]
    dax_ref[...] = jnp.exp(_expand(a, e_ref[...]))
    dan_ref[...] = jnp.exp(_expand(a, en_ref[...]))


def _ssd_decode_state_body(s_ref, xdt_ref, b_ref, c_ref, dan_ref, *rest, first):
    s_out_ref, yoff_ref = rest[-2:]
    _first_or_aliased(first, functools.partial(
        _ssd_decode_state_compute, s_ref, xdt_ref, b_ref, c_ref, dan_ref, s_out_ref, yoff_ref), s_out_ref)


def _ssd_decode_state_compute(s_ref, xdt_ref, b_ref, c_ref, dan_ref, s_out_ref, yoff_ref):
    gw = HPG * P_C
    sub = SUBLANES
    row = lax.broadcasted_iota(jnp.int32, (sub, 1), 0)
    for r0 in range(0, s_ref.shape[0], sub):
        rows = slice(r0, r0 + sub)
        xdt = xdt_ref[rows, :]
        b16 = b_ref[rows, :].astype(BF16)
        c16 = c_ref[rows, :].astype(BF16)
        dan = dan_ref[rows, :]
        yoff = jnp.zeros((sub, gw), F32)
        for bb in range(sub):
            s = s_ref[r0 + bb].reshape(gw, N_C)
            res = _dot_nt(c16, s.astype(BF16))
            yoff = jnp.where(row == bb, res, yoff)
            x_one = jnp.where(row == bb, xdt, 0.0).astype(BF16)
            upd = _dot_tn(x_one, b16)
            for hh in range(HPG):
                rs = slice(hh * P_C, (hh + 1) * P_C)
                s_out_ref[r0 + bb, hh] = s[rs] * dan[bb:bb + 1, hh * N_C:(hh + 1) * N_C] + upd[rs]
        yoff_ref[rows, :] = yoff


def _ssd_decode_post_body(yoff_ref, dax_ref, xdt_ref, xs_ref, b_ref, c_ref, z_ref, dx_ref, gn_ref, o_ref):
    gw = HPG * P_C
    bc = b_ref[...] * c_ref[...]
    cbx = [jnp.broadcast_to(jnp.sum(bc[:, g * N_C:(g + 1) * N_C], axis=-1, keepdims=True), (bc.shape[0], gw))
           for g in range(G_C)]
    y = yoff_ref[...] * dax_ref[...] + jnp.concatenate(cbx, axis=1) * xdt_ref[...]
    y = y + dx_ref[...] * xs_ref[...]
    y = y * _silu(z_ref[...])
    o_ref[...] = _group_rms(y, gn_ref[...])


def _ssd_decode(z, xbc, hist, small, conv_w, conv_b, dt_bias, a_log, d_skip, gnorm, s_all, s_done, e):
    bn = xbc.shape[0]
    dtrow, arow, _, _ = _ssd_small_params(dt_bias, a_log, H_C)
    wide = jax.ShapeDtypeStruct((bn, D_INNER), F32)
    grp = jax.ShapeDtypeStruct((bn, G_C * N_C), F32)
    plain = pltpu.CompilerParams(vmem_limit_bytes=VMEM_LIMIT_BYTES)
    new_conv, xs, xdt, bm, cm, dax, dan = pl.pallas_call(
        _ssd_decode_pre_body,
        out_shape=[jax.ShapeDtypeStruct(hist.shape, F32), wide, wide, grp, grp, wide,
                   jax.ShapeDtypeStruct((bn, H_C * N_C), F32)],
        compiler_params=plain,
        name="ssd_decode_pre",
    )(xbc, hist, small, conv_w, conv_b.reshape(1, SSD_CONV_CH), dtrow, arow,
      _head_expander(P_C), _head_expander(N_C))
    nb = SSD_STATE_SEQS
    gw = HPG * P_C
    steps = bn // nb
    first = s_done is None
    blk = (None, nb, HPG, P_C, N_C)
    widths = (gw, N_C, N_C, HPG * N_C)
    args = [s_all, xdt, bm, cm, dan]
    if first:
        assert e == 0
        seq = lambda s, i: jnp.where(s == 0, i, steps - 1)
        grp_of = lambda s, g: jnp.where(s == 0, g, G_C - 1)
        vec = lambda n: pl.BlockSpec((nb, n), lambda s, i, g: (seq(s, i), grp_of(s, g)))
        in_specs = [pl.BlockSpec(blk, lambda s, i, g: (0, seq(s, i), grp_of(s, g), 0, 0))] + [vec(n) for n in widths]
        out_specs = [pl.BlockSpec(blk, lambda s, i, g: (s, i, g, 0, 0)), vec(gw)]
        grid, aliases, sem = (s_all.shape[0], steps, G_C), {}, ("arbitrary",) * 3
    else:
        vec = lambda n: pl.BlockSpec((nb, n), lambda i, g: (i, g))
        st = pl.BlockSpec(blk, lambda i, g: (e, i, g, 0, 0))
        in_specs = [st] + [vec(n) for n in widths] + [pl.BlockSpec(memory_space=pl.ANY)]
        out_specs = [st, vec(gw)]
        args.append(s_done)
        grid, aliases, sem = (steps, G_C), {len(args) - 1: 0}, ("parallel", "parallel")
    s_done, yoff = pl.pallas_call(
        functools.partial(_ssd_decode_state_body, first=first),
        grid=grid,
        in_specs=in_specs,
        out_specs=out_specs,
        out_shape=[jax.ShapeDtypeStruct(s_all.shape, F32), wide],
        input_output_aliases=aliases,
        compiler_params=_cp(*sem),
        name="ssd_decode_state",
    )(*args)
    mix = pl.pallas_call(
        _ssd_decode_post_body,
        out_shape=wide,
        compiler_params=plain,
        name="ssd_decode_post",
    )(yoff, dax, xdt, xs, bm, cm, z, jnp.repeat(d_skip, P_C).reshape(1, D_INNER), gnorm.reshape(1, D_INNER))
    return mix, new_conv, s_done


def _narrow_weights(w_small):
    n = w_small.shape[1]
    return jnp.zeros((D_MODEL, LANES), BF16).at[:, :n].set(w_small.astype(BF16))


def _even_q_pad(qa):
    bn = qa.shape[0]
    q = qa.reshape(bn, KV_A, G_A, HD_A)
    out = jnp.zeros((bn, KV_A, G_A, KV_A, HD_A), F32)
    for kv in range(KV_A):
        out = out.at[:, kv, :, kv, :].set(q[:, kv])
    return out.reshape(bn, H_A, KV_A * HD_A)


def _even_o_unpad(o8):
    bn = o8.shape[0]
    o = o8.reshape(bn, KV_A, G_A, KV_A, HD_A)
    return jnp.concatenate([o[:, kv, :, kv, :].reshape(bn, G_A * HD_A) for kv in range(KV_A)], axis=1)


def _trunk(x3, states, wts):
    (rel_bias, norm_ff1, norm_mix, norm_ff2, norm_final,
     ff1, ff2, even_w_in, even_w_out, swa_sinks, gdn_conv_w, gdn_A_log, gdn_dt_bias, gdn_norm,
     ssd_w_in, ssd_w_out, ssd_conv_w, ssd_conv_b, ssd_dt_bias, ssd_A_log, ssd_D, ssd_norm) = wts
    bn, l, d = x3.shape
    t = bn * l
    x = x3.reshape(t, d)
    decode = states is not None
    ks, vs, gconv, gssm, sconv, sssm = [], [], [], [], [], []
    depth = norm_ff1.shape[0]
    width = KV_A * HD_A
    if decode:
        gssm_done = sssm_done = None
        n_even = states[0].shape[0]
        cache_t = lambda c: c.transpose(0, 1, 3, 4, 2).reshape(n_even, bn, width, WINDOW)
        ckt, cvt = cache_t(states[0]), cache_t(states[1])
        ghist, shist = states[2].transpose(0, 2, 1, 3), states[4].transpose(0, 2, 1, 3)
    for layer in range(depth):
        x = _ffn(x, norm_ff1[layer], ff1, layer)
        e = layer // 2
        if layer % 2 == 0:
            w_all, ws = even_w_in
            qa, kv, qkvb, z, small, small_t = _inproj(x, norm_mix[layer], w_all, e, ws[e], 2 * SUBLANES,
                                                      (A_Q, 2 * width, B_QKV, B_Z))
            if decode:
                o8 = _swa_decode(_even_q_pad(qa), kv, ckt, cvt, e, rel_bias, swa_sinks[e])
                o_a = _even_o_unpad(o8)
                o_b, new_conv, gssm_done = _gdn_decode(qkvb, ghist[e], z, small, gdn_conv_w[e], gdn_A_log[e],
                                                       gdn_dt_bias[e], gdn_norm[e], states[3], gssm_done, e)
                new_k, new_v = kv[:, :width], kv[:, width:]
                new_conv = new_conv.transpose(1, 0, 2)
            else:
                o_a = _swa_prompt(qa, kv, rel_bias, swa_sinks[e], bn, l)
                o_b, s_new = _gdn_prompt(qkvb, z, small, small_t, gdn_conv_w[e], gdn_A_log[e],
                                         gdn_dt_bias[e], gdn_norm[e], bn, l)
                kv3 = kv.reshape(bn, l, 2 * width)
                new_k = kv3[:, l - WINDOW:, :width]
                new_v = kv3[:, l - WINDOW:, width:]
                new_conv = qkvb.reshape(bn, l, B_QKV)[:, l - (CONV_W - 1):]
                gssm.append(s_new)
            ks.append(new_k)
            vs.append(new_v)
            gconv.append(new_conv)
            mixes = [(o_a, even_w_out, e, 0), (o_b, even_w_out, e, 1)]
        else:
            w_all, ws = ssd_w_in
            z, xbc, small, small_t = _inproj(x, norm_mix[layer], w_all, e, ws[e], H_C, (D_INNER, SSD_CONV_CH))
            if decode:
                mix, new_conv, sssm_done = _ssd_decode(z, xbc, shist[e], small, ssd_conv_w[e], ssd_conv_b[e],
                                                       ssd_dt_bias[e], ssd_A_log[e], ssd_D[e], ssd_norm[e],
                                                       states[5], sssm_done, e)
                new_conv = new_conv.transpose(1, 0, 2)
            else:
                mix, s_new = _ssd_prompt(z, xbc, small, small_t, ssd_conv_w[e], ssd_conv_b[e], ssd_dt_bias[e],
                                         ssd_A_log[e], ssd_D[e], ssd_norm[e], bn, l)
                new_conv = xbc.reshape(bn, l, SSD_CONV_CH)[:, l - (CONV_W - 1):]
                sssm.append(s_new)
            sconv.append(new_conv)
            mixes = [(mix, ssd_w_out, e, 0)]
        x = _ffn(x, norm_ff2[layer], ff2, layer, mixes=mixes,
                 g_final=norm_final if layer == depth - 1 else None)
    if decode:
        bs = DECODE_SEQ_BLOCK
        new_t = lambda rows: jnp.stack(rows).reshape(n_even, bn // bs, bs, width).transpose(0, 1, 3, 2)
        back = lambda c: c.reshape(n_even, bn, KV_A, HD_A, WINDOW).transpose(0, 1, 4, 2, 3)
        ks = back(_cache_shift(ckt, new_t(ks)))
        vs = back(_cache_shift(cvt, new_t(vs)))
        gssm, sssm = gssm_done, sssm_done
    else:
        kv5 = (len(ks), bn, WINDOW, KV_A, HD_A)
        ks, vs = jnp.stack(ks).reshape(kv5), jnp.stack(vs).reshape(kv5)
        gssm, sssm = jnp.stack(gssm), jnp.stack(sssm)
    return (x.reshape(bn, l, d), ks, vs, jnp.stack(gconv), gssm, jnp.stack(sconv), sssm)


def kernel(x_prompt, x_sample, cache_swa_k, cache_swa_v, state_gdn_conv, state_gdn_ssm, state_ssd_conv, state_ssd_ssm, rel_bias, norm_ff1, norm_mix, norm_ff2, norm_final, ff1_gate, ff1_up, ff1_down, ff2_gate, ff2_up, ff2_down, even_w_in, even_w_out, swa_sinks, gdn_conv_w, gdn_A_log, gdn_dt_bias, gdn_norm, ssd_w_in, ssd_w_out, ssd_conv_w, ssd_conv_b, ssd_dt_bias, ssd_A_log, ssd_D, ssd_norm):
    depth = norm_ff1.shape[0]
    ff1 = (ff1_gate.astype(BF16), ff1_up.astype(BF16), ff1_down.astype(BF16))
    ff2 = (ff2_gate.astype(BF16), ff2_up.astype(BF16), ff2_down.astype(BF16))
    n_even_main = A_Q + 2 * KV_A * HD_A + B_QKV + B_Z
    even_in = (even_w_in.astype(BF16),
               [_narrow_weights(even_w_in[e][:, n_even_main:]) for e in range(even_w_in.shape[0])])
    n_odd_main = D_INNER + SSD_CONV_CH
    odd_in = (ssd_w_in.astype(BF16),
              [_narrow_weights(ssd_w_in[e][:, n_odd_main:]) for e in range(ssd_w_in.shape[0])])
    wts = (rel_bias, norm_ff1, norm_mix, norm_ff2, norm_final, ff1, ff2,
           even_in, even_w_out.astype(BF16), swa_sinks, gdn_conv_w, gdn_A_log, gdn_dt_bias, gdn_norm,
           odd_in, ssd_w_out.astype(BF16), ssd_conv_w, ssd_conv_b, ssd_dt_bias, ssd_A_log, ssd_D, ssd_norm)
    y_p, p_k, p_v, p_gconv, p_gssm, p_sconv, p_sssm = _trunk(x_prompt, None, wts)
    states = (cache_swa_k, cache_swa_v, state_gdn_conv, state_gdn_ssm, state_ssd_conv, state_ssd_ssm)
    y_s, s_k, s_v, s_gconv, s_gssm, s_sconv, s_sssm = _trunk(x_sample, states, wts)
    return (y_p, y_s, p_k, p_v, p_gconv, p_gssm, p_sconv, p_sssm,
            s_k, s_v, s_gconv, s_gssm, s_sconv, s_sssm)
```

```python
import functools
import math

import numpy as np
import jax
import jax.numpy as jnp
from jax import lax
from jax.experimental import pallas as pl
from jax.experimental.pallas import tpu as pltpu

F32 = jnp.float32
BF16 = jnp.bfloat16
HI = lax.Precision.HIGHEST

EPS = 1e-6
NEG_INF = -1e30
LOG2E = math.log2(math.e)
D_MODEL = 1024
WINDOW = 128
BLOCK = 128
H_A, KV_A, G_A, HD_A = 8, 2, 4, 64
N_BUCKETS, MAX_DIST = 32, 128
H_B, DK_B, DV_B = 4, 128, 128
CONV_W = 4
GDN_CHUNK = 64
D_INNER = 2048
P_C, H_C, N_C, G_C = 64, 32, 128, 4
HPG = H_C // G_C
SSD_CHUNK = 128
A_Q = H_A * HD_A
B_QKV = 3 * H_B * DK_B
B_Z = H_B * DV_B
SSD_CONV_CH = D_INNER + 2 * G_C * N_C
LANES = 128
SUBLANES = 8
VMEM_LIMIT_BYTES = 56 * 1024 * 1024
ROW_TILE = 512
SWA_QUERY_BLOCKS = 4
GDN_PREP_CHUNKS = 4
GDN_SCAN_GROUP = 2
DECODE_SEQ_BLOCK = 64
SSD_STATE_SEQS = 4 * SUBLANES
GDN_STATE_SEQS = 4 * SUBLANES


def _cp(*sem):
    return pltpu.CompilerParams(dimension_semantics=sem, vmem_limit_bytes=VMEM_LIMIT_BYTES)


def _whole(shape):
    nd = len(shape)
    return pl.BlockSpec(shape, lambda *_: (0,) * nd, pipeline_mode=pl.Buffered(1))


def _rms(x, g):
    return x * lax.rsqrt(jnp.mean(x * x, axis=-1, keepdims=True) + EPS) * g


def _silu(x):
    h = 0.5 * x
    return h * jnp.tanh(h) + h


def _softplus(x):
    return jnp.maximum(x, 0.0) + jnp.log1p(jnp.exp(-jnp.abs(x)))


def _dot(a, b):
    return jnp.dot(a, b, preferred_element_type=F32)


def _dot_nt(a, b):
    return lax.dot_general(a, b, (((1,), (1,)), ((), ())), preferred_element_type=F32)


def _dot_tn(a, b):
    return lax.dot_general(a, b, (((0,), (0,)), ((), ())), preferred_element_type=F32)


def _dot_hi(a, b):
    return jnp.dot(a, b, precision=HI, preferred_element_type=F32)


def _expand(x, e3):
    hi = x.astype(BF16)
    r = x - hi.astype(F32)
    mid = r.astype(BF16)
    lo = (r - mid.astype(F32)).astype(BF16)
    return _dot(jnp.concatenate([hi, mid, lo], axis=1), e3)


def _tri(n, kind):
    r = lax.broadcasted_iota(jnp.int32, (n, n), 0)
    c = lax.broadcasted_iota(jnp.int32, (n, n), 1)
    return {"lower": r >= c, "strict_lower": r > c, "upper": r <= c}[kind]


def _ffn_body(*refs, n_mix, final):
    refs = list(refs)
    x = refs.pop(0)[...]
    for _ in range(n_mix):
        m_ref, wo_ref = refs.pop(0), refs.pop(0)
        x = x + _dot(m_ref[...].astype(BF16), wo_ref[...])
    g_ref, wg_ref, wu_ref, wd_ref = refs[:4]
    o_ref = refs[-1]
    hn = _rms(x, g_ref[...]).astype(BF16)
    act = (_silu(_dot(hn, wg_ref[...])) * _dot(hn, wu_ref[...])).astype(BF16)
    y = x + 0.5 * _dot(act, wd_ref[...])
    if final:
        y = _rms(y, refs[4][...])
    o_ref[...] = y


def _layer_slab(w_all, layer, rows=None, row_block=0):
    shape = (rows or w_all.shape[1], w_all.shape[2])
    return pl.BlockSpec((None,) + shape, lambda i: (layer, row_block, 0), pipeline_mode=pl.Buffered(1))


def _ffn(x, g, ffw, layer, mixes=(), g_final=None):
    t, d = x.shape
    tm = min(t, ROW_TILE)
    final = g_final is not None
    row = pl.BlockSpec((tm, d), lambda i: (i, 0))
    in_specs, args = [row], [x]
    for m, w_all, e, row_block in mixes:
        in_specs += [pl.BlockSpec((tm, m.shape[1]), lambda i: (i, 0)), _layer_slab(w_all, e, m.shape[1], row_block)]
        args += [m, w_all]
    in_specs += [_whole((1, d))] + [_layer_slab(w, layer) for w in ffw]
    args += [g.reshape(1, d), *ffw]
    if final:
        in_specs.append(_whole((1, d)))
        args.append(g_final.reshape(1, d))
    return pl.pallas_call(
        functools.partial(_ffn_body, n_mix=len(mixes), final=final),
        grid=(t // tm,),
        in_specs=in_specs,
        out_specs=row,
        out_shape=jax.ShapeDtypeStruct((t, d), F32),
        compiler_params=_cp("parallel"),
        name="ffn",
    )(*args)


def _inproj_body(x_ref, g_ref, w_ref, ws_ref, *outs, splits):
    hn = _rms(x_ref[...], g_ref[...]).astype(BF16)
    off = 0
    for o_ref, n in zip(outs[:-2], splits):
        o_ref[...] = _dot(hn, w_ref[:, off:off + n])
        off += n
    small = _dot(hn, ws_ref[...])
    outs[-2][...] = small
    outs[-1][...] = small.T[:outs[-1].shape[0]]


def _inproj(x, g, w_all, e, ws, ns, splits):
    t, d = x.shape
    w_spec = _layer_slab(w_all, e)
    tm = min(t, ROW_TILE)
    row = pl.BlockSpec((tm, d), lambda i: (i, 0))
    out_specs = [pl.BlockSpec((tm, n), lambda i: (i, 0)) for n in splits]
    out_specs += [pl.BlockSpec((tm, LANES), lambda i: (i, 0)), pl.BlockSpec((ns, tm), lambda i: (0, i))]
    out_shape = [jax.ShapeDtypeStruct((t, n), F32) for n in splits]
    out_shape += [jax.ShapeDtypeStruct((t, LANES), F32), jax.ShapeDtypeStruct((ns, t), F32)]
    return pl.pallas_call(
        functools.partial(_inproj_body, splits=splits),
        grid=(t // tm,),
        in_specs=[row, _whole((1, d)), w_spec, _whole(ws.shape)],
        out_specs=out_specs,
        out_shape=out_shape,
        compiler_params=_cp("parallel"),
        name="inproj",
    )(x, g.reshape(1, d), w_all, ws)


def _t5_bucket_np(dist):
    max_exact = N_BUCKETS // 2
    df = np.maximum(dist, max_exact).astype(np.float32)
    large = max_exact + (np.log(df / np.float32(max_exact)) / np.float32(math.log(MAX_DIST / max_exact))
                         * np.float32(N_BUCKETS - max_exact)).astype(np.int32)
    return np.where(dist < max_exact, dist, np.minimum(large, N_BUCKETS - 1)).astype(np.int32)


def _band_bucket_ids(n_q, n_k, offset):
    d = offset + np.arange(n_q)[:, None] - np.arange(n_k)[None, :]
    valid = (d >= 0) & (d <= WINDOW)
    return np.where(valid, _t5_bucket_np(np.clip(d, 0, WINDOW)), -1).astype(np.int32)


def _bias_from_buckets(bid, rb_ref, h):
    acc = jnp.full(bid.shape, NEG_INF, F32)
    for bk in range(N_BUCKETS):
        acc = jnp.where(bid == bk, rb_ref[bk, h], acc)
    return acc


def _swa_prompt_body(bid_ref, rb_ref, sk_ref, q_ref, kvp_ref, kvc_ref, o_ref, bias_ref):
    first_step = (pl.program_id(0) == 0) & (pl.program_id(1) == 0)

    @pl.when(first_step)
    def _():
        bid = bid_ref[...]
        col = lax.broadcasted_iota(jnp.int32, (BLOCK, 2 * BLOCK), 1)
        for h in range(H_A):
            bias = _bias_from_buckets(bid, rb_ref, h)
            bias_ref[h] = bias
            bias_ref[H_A + h] = jnp.where(col < BLOCK, NEG_INF, bias)

    kvp = kvp_ref[...]
    kvc = kvc_ref[...]
    scale = HD_A ** -0.5
    heads = range(H_A)
    for j in range(q_ref.shape[0] // BLOCK):
        rows = slice(j * BLOCK, (j + 1) * BLOCK)
        prev = kvp if j == 0 else kvc[(j - 1) * BLOCK:j * BLOCK]
        first_block = jnp.where(pl.program_id(1) == 0, H_A, 0) if j == 0 else 0
        k, v = [], []
        for kv in range(KV_A):
            ks = slice(kv * HD_A, (kv + 1) * HD_A)
            vs = slice(KV_A * HD_A + kv * HD_A, KV_A * HD_A + (kv + 1) * HD_A)
            k.append(jnp.concatenate([prev[:, ks], kvc[rows, ks]], axis=0).astype(BF16))
            v.append(jnp.concatenate([prev[:, vs], kvc[rows, vs]], axis=0).astype(BF16))
        s = [_dot_nt((q_ref[rows, h * HD_A:(h + 1) * HD_A] * scale).astype(BF16), k[h // G_A])
             + bias_ref[first_block + h] for h in heads]
        m = [jnp.maximum(jnp.max(s[h], axis=-1, keepdims=True), sk_ref[h]) for h in heads]
        p = [jnp.exp(s[h] - m[h]) for h in heads]
        denom = [jnp.sum(p[h], axis=-1, keepdims=True) + jnp.exp(sk_ref[h] - m[h]) for h in heads]
        outs = [_dot(p[h].astype(BF16), v[h // G_A]) / denom[h] for h in heads]
        o_ref[rows, :] = jnp.concatenate(outs, axis=1).astype(BF16)


def _swa_prompt(qa, kv, rel_bias, sinks, bn, l):
    qb = SWA_QUERY_BLOCKS
    nb = l // (qb * BLOCK)
    t = bn * l
    bid = jnp.asarray(_band_bucket_ids(BLOCK, 2 * BLOCK, BLOCK))
    width = 2 * KV_A * HD_A
    return pl.pallas_call(
        _swa_prompt_body,
        grid=(bn, nb),
        in_specs=[
            _whole((BLOCK, 2 * BLOCK)),
            pl.BlockSpec(memory_space=pltpu.SMEM),
            pl.BlockSpec(memory_space=pltpu.SMEM),
            pl.BlockSpec((qb * BLOCK, A_Q), lambda b, i: (b * nb + i, 0)),
            pl.BlockSpec((BLOCK, width), lambda b, i: (jnp.maximum((b * nb + i) * qb - 1, 0), 0)),
            pl.BlockSpec((qb * BLOCK, width), lambda b, i: (b * nb + i, 0)),
        ],
        out_specs=pl.BlockSpec((qb * BLOCK, A_Q), lambda b, i: (b * nb + i, 0)),
        out_shape=jax.ShapeDtypeStruct((t, A_Q), BF16),
        scratch_shapes=[pltpu.VMEM((2 * H_A, BLOCK, 2 * BLOCK), F32)],
        compiler_params=_cp("arbitrary", "arbitrary"),
        name="swa_prompt",
    )(bid, rel_bias, sinks, qa, kv, kv)


def _swa_decode_body(bid_ref, rb_ref, sk_ref, q_ref, kvn_ref, ckt_ref, cvt_ref, o_ref):
    bid = bid_ref[...]
    row = lax.broadcasted_iota(jnp.int32, (H_A, 1), 0)
    bias = jnp.zeros((H_A, bid.shape[1]), F32)
    sk = jnp.zeros((H_A, 1), F32)
    for h in range(H_A):
        bias = jnp.where(row == h, _bias_from_buckets(bid, rb_ref, h), bias)
        sk = jnp.where(row == h, sk_ref[h], sk)
    bias_c = bias[:, :WINDOW]
    bias_n = bias[:, WINDOW:WINDOW + 1]
    scale = HD_A ** -0.5
    q = q_ref[...]
    kvn = kvn_ref[...]
    width = KV_A * HD_A
    k_new = kvn[:, None, :width]
    v_new = kvn[:, None, width:]
    s = lax.dot_general(q.astype(BF16), ckt_ref[...].astype(BF16), (((2,), (1,)), ((0,), (0,))),
                        preferred_element_type=F32) * scale + bias_c[None]
    s_n = jnp.sum(q * k_new, axis=-1, keepdims=True) * scale + bias_n[None]
    m = jnp.maximum(jnp.maximum(jnp.max(s, axis=-1, keepdims=True), s_n), sk[None])
    p = jnp.exp(s - m)
    p_n = jnp.exp(s_n - m)
    denom = jnp.sum(p, axis=-1, keepdims=True) + p_n + jnp.exp(sk[None] - m)
    o = lax.dot_general((p / denom).astype(BF16), cvt_ref[...].astype(BF16), (((2,), (2,)), ((0,), (0,))),
                        preferred_element_type=F32)
    o_ref[...] = o + (p_n / denom) * v_new


def _swa_decode(q_pad, kv_new, cache_kt, cache_vt, e, rel_bias, sinks):
    bn = q_pad.shape[0]
    bs = DECODE_SEQ_BLOCK
    width = KV_A * HD_A
    ids = _band_bucket_ids(1, WINDOW + 1, WINDOW)
    bid = np.full((1, WINDOW + LANES), -1, np.int32)
    bid[:, :WINDOW + 1] = ids
    return pl.pallas_call(
        _swa_decode_body,
        grid=(bn // bs,),
        in_specs=[
            _whole((1, WINDOW + LANES)),
            pl.BlockSpec(memory_space=pltpu.SMEM),
            pl.BlockSpec(memory_space=pltpu.SMEM),
            pl.BlockSpec((bs, H_A, width), lambda i: (i, 0, 0)),
            pl.BlockSpec((bs, 2 * width), lambda i: (i, 0)),
            pl.BlockSpec((None, bs, width, WINDOW), lambda i: (e, i, 0, 0)),
            pl.BlockSpec((None, bs, width, WINDOW), lambda i: (e, i, 0, 0)),
        ],
        out_specs=pl.BlockSpec((bs, H_A, width), lambda i: (i, 0, 0)),
        out_shape=jax.ShapeDtypeStruct((bn, H_A, width), F32),
        compiler_params=_cp("parallel"),
        name="swa_decode",
    )(jnp.asarray(bid), rel_bias, sinks, q_pad, kv_new, cache_kt, cache_vt)


def _cache_shift_body(c_ref, nt_ref, o_ref):
    o_ref[...] = pltpu.roll(c_ref[...], WINDOW - 1, axis=2)
    nt = nt_ref[...]
    for b in range(c_ref.shape[0]):
        o_ref[b, :, WINDOW - 1:WINDOW] = nt[:, b:b + 1]


def _cache_shift(cache_t, new_t):
    n, bn, width, win = cache_t.shape
    bs = new_t.shape[3]
    blk = pl.BlockSpec((None, bs, width, win), lambda e, i: (e, i, 0, 0))
    return pl.pallas_call(
        _cache_shift_body,
        grid=(n, bn // bs),
        in_specs=[blk, pl.BlockSpec((None, None, width, bs), lambda e, i: (e, i, 0, 0))],
        out_specs=blk,
        out_shape=jax.ShapeDtypeStruct(cache_t.shape, F32),
        compiler_params=_cp("parallel", "parallel"),
        name="cache_shift",
    )(cache_t, new_t)


def _conv_rows(x_ref, xbuf_ref, cw_ref, rows):
    xbuf_ref[SUBLANES:SUBLANES + rows, :] = x_ref[...]
    x = xbuf_ref[...]
    out = x * cw_ref[0:1, :]
    for i in range(1, CONV_W):
        out = pltpu.roll(out, 1, axis=0) + x * cw_ref[i:i + 1, :]
    return out[SUBLANES:]


def _conv_chunk(x_ref, xbuf_ref, cw_ref, rows):
    out = _conv_rows(x_ref, xbuf_ref, cw_ref, rows)
    xbuf_ref[0:SUBLANES, :] = xbuf_ref[rows:rows + SUBLANES, :]
    return out


def _l2norm(x):
    return x * lax.rsqrt(jnp.sum(x * x, axis=-1, keepdims=True) + EPS)


def _chunk_masks(n, cc):
    r = lax.broadcasted_iota(jnp.int32, (n, n), 0)
    c = lax.broadcasted_iota(jnp.int32, (n, n), 1)
    shift = int(math.log2(cc))
    same = lax.shift_right_logical(r, shift) == lax.shift_right_logical(c, shift)
    return same, same & (r >= c), same & (r <= c)


def _dot_bf16x3(a, b):
    a_hi = a.astype(BF16)
    a_lo = (a - a_hi.astype(F32)).astype(BF16)
    b_hi = b.astype(BF16)
    b_lo = (b - b_hi.astype(F32)).astype(BF16)
    return _dot(a_hi, b_hi) + _dot(a_hi, b_lo) + _dot(a_lo, b_hi)


def _solve_masks(n, cc, base=SUBLANES):
    r = np.arange(n)[:, None]
    c = np.arange(n)[None, :]
    masks = [-(((r // base) == (c // base)) & (r > c)).astype(np.float32)]
    s = base
    while s < cc:
        sibling = ((r // (2 * s)) == (c // (2 * s))) & ((r // s) % 2 == 1) & ((c // s) % 2 == 0)
        masks.append(sibling.astype(np.float32))
        s *= 2
    return np.stack(masks)


def _gdn_prep_body(qkv_ref, halo_ref, sm_ref, smt_ref, cw_ref, acol_ref, dtcol_ref, tri_ref,
                   u_ref, w_ref, qg_ref, kd_ref, aqk_ref, gl_ref, xbuf_ref):
    rows = qkv_ref.shape[0]
    cc = GDN_CHUNK
    xbuf_ref[0:SUBLANES, :] = jnp.where(pl.program_id(1) == 0, 0.0, halo_ref[...])
    conv = _silu(_conv_rows(qkv_ref, xbuf_ref, cw_ref, rows))
    nq = H_B * DK_B
    beta_all = jax.nn.sigmoid(sm_ref[...])
    gt_all = -jnp.exp(acol_ref[...]) * _softplus(smt_ref[...] + dtcol_ref[...])
    pad = jnp.zeros((LANES - gt_all.shape[0], rows), F32)
    g_all = jnp.concatenate([gt_all, pad], axis=0).T
    same, lower, upper = _chunk_masks(rows, cc)
    gc = _dot_hi(lower.astype(F32), g_all)
    gct = _dot_hi(gt_all, upper.astype(F32))
    gsum = _dot_hi(same.astype(F32), g_all)
    neg_outside = jnp.where(lower, 0.0, -jnp.inf)
    gc2, gct2 = gc * LOG2E, gct * LOG2E
    heads = range(H_B)
    qh, kh16, kb, decay, egc = [], [], [], [], []
    for h in heads:
        q = _l2norm(conv[:, h * DK_B:(h + 1) * DK_B]) * (DK_B ** -0.5)
        k = _l2norm(conv[:, nq + h * DK_B:nq + (h + 1) * DK_B])
        v = conv[:, 2 * nq + h * DV_B:2 * nq + (h + 1) * DV_B]
        beta = beta_all[:, h:h + 1]
        gcol = gc[:, H_B + h:H_B + h + 1]
        decay.append(jnp.exp2(gc2[:, H_B + h:H_B + h + 1] - gct2[H_B + h:H_B + h + 1, :] + neg_outside))
        eg = jnp.exp(gcol)
        qg_ref[:, h * DK_B:(h + 1) * DK_B] = (q * eg).astype(BF16)
        kd_ref[:, h * DK_B:(h + 1) * DK_B] = (k * jnp.exp(gsum[:, H_B + h:H_B + h + 1] - gcol)).astype(BF16)
        qh.append(q.astype(BF16))
        kh16.append(k.astype(BF16))
        kb.append(k * beta)
        egc.append((v * beta, eg))
    m = [_dot_nt(kb[h].astype(BF16), kh16[h]) * decay[h] for h in heads]
    for h in heads:
        aqk = _dot_nt(qh[h], kh16[h]) * decay[h]
        blocks = [aqk[i * cc:(i + 1) * cc, i * cc:(i + 1) * cc] for i in range(rows // cc)]
        aqk_ref[:, h * cc:(h + 1) * cc] = jnp.concatenate(blocks, axis=0).astype(BF16)
    mm = lambda xs, ys: [_dot(x.astype(BF16), y.astype(BF16)) for x, y in zip(xs, ys)]
    a = [m[h] * tri_ref[0] for h in heads]
    a2 = mm(a, a)
    a3 = mm(a, a2)
    a4 = mm(a2, a2)
    x1 = [a[h] + a2[h] + a3[h] for h in heads]
    x1a4 = mm(x1, a4)
    r = [x1[h] + a4[h] + x1a4[h] for h in heads]
    for level in range(1, tri_ref.shape[0]):
        c = [m[h] * tri_ref[level] for h in heads]
        rc = mm(r, c)
        y = [c[h] + rc[h] for h in heads]
        yr = mm(y, r)
        r = [r[h] - y[h] - yr[h] for h in heads]
    for h in heads:
        vb, eg = egc[h]
        rhs = jnp.concatenate([vb, kb[h] * eg], axis=1)
        sol = rhs + _dot_bf16x3(r[h], rhs)
        u_ref[:, h * DV_B:(h + 1) * DV_B] = sol[:, :DV_B]
        w_ref[:, h * DK_B:(h + 1) * DK_B] = sol[:, DV_B:].astype(BF16)
    for i in range(rows // cc):
        gl = [jnp.broadcast_to(jnp.exp(gsum[i * cc:i * cc + 1, H_B + h:H_B + h + 1]), (SUBLANES, DV_B))
              for h in heads]
        gl_ref[0, i] = jnp.concatenate(gl, axis=1)


def _gdn_scan_body(u_ref, w_ref, qg_ref, kd_ref, aqk_ref, gl_ref, z_ref, gn_ref, o_ref, s_out_ref, s_ref):
    c = pl.program_id(0)
    bn = u_ref.shape[0]
    cc = GDN_CHUNK

    @pl.when(c == 0)
    def _():
        s_ref[...] = jnp.zeros_like(s_ref)

    gn = gn_ref[...]
    group = GDN_SCAN_GROUP
    for b0 in range(0, bn, group):
        combos = [(b, h) for b in range(b0, min(b0 + group, bn)) for h in range(H_B)]
        res = {}
        for b, h in combos:
            hs = slice(h * DK_B, (h + 1) * DK_B)
            wq = jnp.concatenate([w_ref[b, :, hs], qg_ref[b, :, hs]], axis=0)
            res[b, h] = _dot(wq, s_ref[b, h].astype(BF16))
        v16 = {}
        for b, h in combos:
            hs = slice(h * DV_B, (h + 1) * DV_B)
            v16[b, h] = (u_ref[b, :, hs] - res[b, h][:cc]).astype(BF16)
        for b, h in combos:
            hs = slice(h * DV_B, (h + 1) * DV_B)
            o = res[b, h][cc:] + _dot(aqk_ref[b, :, h * cc:(h + 1) * cc], v16[b, h])
            upd = _dot_tn(kd_ref[b, :, hs], v16[b, h])
            s_ref[b, h] = s_ref[b, h] * gl_ref[b, 0, 0:1, hs] + upd
            o_ref[b, :, hs] = (_rms(o, gn) * _silu(z_ref[b, :, hs])).astype(BF16)

    @pl.when(c == pl.num_programs(0) - 1)
    def _():
        s_out_ref[...] = s_ref[...]


def _gdn_prompt(qkvb, z, small, small_t, conv_w, a_log, dt_bias, gnorm, bn, l):
    rows = GDN_PREP_CHUNKS * GDN_CHUNK
    cc = GDN_CHUNK
    nb = l // rows
    nc = l // cc
    t = bn * l
    ns = small_t.shape[0]
    nq = H_B * DK_B
    pad_col = lambda v: jnp.zeros((ns, 1), F32).at[H_B:2 * H_B, 0].set(v)
    tri = jnp.asarray(_solve_masks(rows, cc))
    tok = lambda n: pl.BlockSpec((rows, n), lambda b, i: (b * nb + i, 0))
    halo = pl.BlockSpec((SUBLANES, B_QKV), lambda b, i: (jnp.maximum((b * nb + i) * (rows // SUBLANES) - 1, 0), 0))
    u, w, qg, kd, aqk, gl = pl.pallas_call(
        _gdn_prep_body,
        grid=(bn, nb),
        in_specs=[
            tok(B_QKV), halo, tok(LANES),
            pl.BlockSpec((ns, rows), lambda b, i: (0, b * nb + i)),
            _whole((CONV_W, B_QKV)), _whole((ns, 1)), _whole((ns, 1)), _whole(tri.shape),
        ],
        out_specs=[tok(nq), tok(nq), tok(nq), tok(nq), tok(H_B * cc),
                   pl.BlockSpec((1, rows // cc, SUBLANES, nq), lambda b, i: (b, i, 0, 0))],
        out_shape=[jax.ShapeDtypeStruct((t, nq), F32)] + [jax.ShapeDtypeStruct((t, nq), BF16)] * 3
        + [jax.ShapeDtypeStruct((t, H_B * cc), BF16), jax.ShapeDtypeStruct((bn, nc, SUBLANES, nq), F32)],
        scratch_shapes=[pltpu.VMEM((rows + SUBLANES, B_QKV), F32)],
        compiler_params=_cp("parallel", "parallel"),
        name="gdn_prep",
    )(qkvb, qkvb, small, small_t, conv_w, pad_col(a_log), pad_col(dt_bias), tri)
    seq = lambda n: pl.BlockSpec((bn, cc, n), lambda c: (0, c, 0))
    r3 = lambda a: a.reshape(bn, l, a.shape[1])
    o, s_new = pl.pallas_call(
        _gdn_scan_body,
        grid=(nc,),
        in_specs=[seq(nq), seq(nq), seq(nq), seq(nq), seq(H_B * cc),
                  pl.BlockSpec((bn, 1, SUBLANES, nq), lambda c: (0, c, 0, 0)), seq(B_Z), _whole((1, DV_B))],
        out_specs=[seq(B_Z), pl.BlockSpec((bn, H_B, DK_B, DV_B), lambda c: (0, 0, 0, 0))],
        out_shape=[jax.ShapeDtypeStruct((bn, l, B_Z), BF16), jax.ShapeDtypeStruct((bn, H_B, DK_B, DV_B), F32)],
        scratch_shapes=[pltpu.VMEM((bn, H_B, DK_B, DV_B), F32)],
        compiler_params=_cp("arbitrary"),
        name="gdn_scan",
    )(r3(u), r3(w), r3(qg), r3(kd), r3(aqk), gl, r3(z), gnorm.reshape(1, DV_B))
    return o.reshape(t, B_Z), s_new


def _conv_step(x, hist_ref, cw_ref, new_ref):
    out = hist_ref[0] * cw_ref[0:1, :]
    for i in range(1, CONV_W - 1):
        out = out + hist_ref[i] * cw_ref[i:i + 1, :]
    out = out + x * cw_ref[CONV_W - 1:CONV_W, :]
    for i in range(CONV_W - 2):
        new_ref[i] = hist_ref[i + 1]
    new_ref[CONV_W - 2] = x
    return out


def _gdn_decode_pre_body(qkv_ref, hist_ref, sm_ref, cw_ref, arow_ref, dtrow_ref,
                         new_ref, w_ref, qg_ref, k_ref, u_ref, qk_ref, gl_ref):
    conv = _silu(_conv_step(qkv_ref[...], hist_ref, cw_ref, new_ref))
    nq = H_B * DK_B
    sm = sm_ref[...]
    beta_all = jax.nn.sigmoid(sm)
    g_all = -jnp.exp(arow_ref[...]) * _softplus(sm + dtrow_ref[...])
    for h in range(H_B):
        hs = slice(h * DK_B, (h + 1) * DK_B)
        qh = _l2norm(conv[:, hs]) * (DK_B ** -0.5)
        kh = _l2norm(conv[:, nq + h * DK_B:nq + (h + 1) * DK_B])
        vh = conv[:, 2 * nq + h * DV_B:2 * nq + (h + 1) * DV_B]
        beta = beta_all[:, h:h + 1]
        eg = jnp.exp(g_all[:, H_B + h:H_B + h + 1])
        w_ref[:, hs] = kh * beta * eg
        qg_ref[:, hs] = qh * eg
        k_ref[:, hs] = kh
        u_ref[:, hs] = vh * beta
        qk_ref[:, hs] = jnp.broadcast_to(jnp.sum(qh * kh, axis=-1, keepdims=True), qh.shape)
        gl_ref[:, hs] = jnp.broadcast_to(eg, qh.shape)


def _first_or_aliased(first, compute, s_out_ref):
    if not first:
        compute()
        return
    slot = pl.program_id(0)
    pl.when(slot == 0)(compute)

    @pl.when(slot != 0)
    def _():
        s_out_ref[...] = jnp.zeros_like(s_out_ref)


def _gdn_decode_state_body(s_ref, w_ref, qg_ref, k_ref, u_ref, qk_ref, gl_ref, z_ref, gn_ref, *rest, first):
    s_out_ref, o_ref = rest[-2:]
    _first_or_aliased(first, functools.partial(
        _gdn_decode_state_compute, s_ref, w_ref, qg_ref, k_ref, u_ref, qk_ref, gl_ref, z_ref, gn_ref,
        s_out_ref, o_ref), s_out_ref)


def _gdn_decode_state_compute(s_ref, w_ref, qg_ref, k_ref, u_ref, qk_ref, gl_ref, z_ref, gn_ref, s_out_ref, o_ref):
    sub = SUBLANES
    row = lax.broadcasted_iota(jnp.int32, (sub, 1), 0)
    for r0 in range(0, s_ref.shape[0], sub):
        rows = slice(r0, r0 + sub)
        for h in range(H_B):
            hs = slice(h * DK_B, (h + 1) * DK_B)
            wq = jnp.concatenate([w_ref[rows, hs], qg_ref[rows, hs]], axis=0).astype(BF16)
            ws = jnp.zeros((sub, DV_B), F32)
            qs = jnp.zeros((sub, DV_B), F32)
            for bb in range(sub):
                res = _dot(wq, s_ref[r0 + bb, h].astype(BF16))
                ws = jnp.where(row == bb, res[:sub], ws)
                qs = jnp.where(row == bb, res[sub:], qs)
            v_new = u_ref[rows, hs] - ws
            o = qs + qk_ref[rows, hs] * v_new
            o_ref[rows, hs] = _rms(o, gn_ref[...]) * _silu(z_ref[rows, hs])
            k = k_ref[rows, hs]
            v16 = v_new.astype(BF16)
            gl = gl_ref[rows, hs]
            for bb in range(sub):
                k_one = jnp.where(row == bb, k, 0.0).astype(BF16)
                s_out_ref[r0 + bb, h] = s_ref[r0 + bb, h] * gl[bb:bb + 1, :] + _dot_tn(k_one, v16)


def _gdn_decode(qkvb, hist, z, small, conv_w, a_log, dt_bias, gnorm, s_all, s_done, e):
    bn = qkvb.shape[0]
    pad_row = lambda v: jnp.zeros((1, LANES), F32).at[0, H_B:2 * H_B].set(v)
    wide = jax.ShapeDtypeStruct((bn, H_B * DK_B), F32)
    new_conv, w, qg, k, u, qk, gl = pl.pallas_call(
        _gdn_decode_pre_body,
        out_shape=[jax.ShapeDtypeStruct(hist.shape, F32)] + [wide] * 6,
        compiler_params=pltpu.CompilerParams(vmem_limit_bytes=VMEM_LIMIT_BYTES),
        name="gdn_decode_pre",
    )(qkvb, hist, small, conv_w, pad_row(a_log), pad_row(dt_bias))
    nb = GDN_STATE_SEQS
    steps = bn // nb
    first = s_done is None
    blk = (None, nb, H_B, DK_B, DV_B)
    args = [s_all, w, qg, k, u, qk, gl, z, gnorm.reshape(1, DV_B)]
    if first:
        assert e == 0
        seq = lambda s, i: jnp.where(s == 0, i, steps - 1)
        vec = pl.BlockSpec((nb, H_B * DK_B), lambda s, i: (seq(s, i), 0))
        in_specs = [pl.BlockSpec(blk, lambda s, i: (0, seq(s, i), 0, 0, 0))] + [vec] * 7
        in_specs.append(pl.BlockSpec((1, DV_B), lambda s, i: (0, 0)))
        out_specs = [pl.BlockSpec(blk, lambda s, i: (s, i, 0, 0, 0)), vec]
        grid, aliases, sem = (s_all.shape[0], steps), {}, ("arbitrary", "arbitrary")
    else:
        vec = pl.BlockSpec((nb, H_B * DK_B), lambda i: (i, 0))
        st = pl.BlockSpec(blk, lambda i: (e, i, 0, 0, 0))
        in_specs = [st] + [vec] * 7 + [_whole((1, DV_B)), pl.BlockSpec(memory_space=pl.ANY)]
        out_specs = [st, vec]
        args.append(s_done)
        grid, aliases, sem = (steps,), {len(args) - 1: 0}, ("parallel",)
    s_done, o = pl.pallas_call(
        functools.partial(_gdn_decode_state_body, first=first),
        grid=grid,
        in_specs=in_specs,
        out_specs=out_specs,
        out_shape=[jax.ShapeDtypeStruct(s_all.shape, F32), wide],
        input_output_aliases=aliases,
        compiler_params=_cp(*sem),
        name="gdn_decode_state",
    )(*args)
    return o, new_conv, s_done


def _head_expander(width):
    e = np.zeros((LANES, H_C * width), np.float32)
    for h in range(H_C):
        e[h, h * width:(h + 1) * width] = 1.0
    return jnp.asarray(np.tile(e, (3, 1)), BF16)


def _group_rms(y, g):
    gs = D_INNER // G_C
    parts = [_rms(y[:, i * gs:(i + 1) * gs], g[:, i * gs:(i + 1) * gs]) for i in range(G_C)]
    return jnp.concatenate(parts, axis=1)


def _ssd_prompt_body(z_ref, xbc_ref, sm_ref, smt_ref, cw_ref, cb_ref, dtrow_ref, arow_ref, dtcol_ref,
                     acol_ref, dx_ref, gn_ref, e_ref, o_ref, s_out_ref, xbuf_ref, s_ref):
    c = pl.program_id(1)
    rows = xbc_ref.shape[0]

    @pl.when(c == 0)
    def _():
        xbuf_ref[0:SUBLANES, :] = jnp.zeros((SUBLANES, xbuf_ref.shape[1]), F32)
        s_ref[...] = jnp.zeros_like(s_ref)

    xbc = _silu(_conv_chunk(xbc_ref, xbuf_ref, cw_ref, rows) + cb_ref[...])
    xs = xbc[:, :D_INNER]
    bm = xbc[:, D_INNER:D_INNER + G_C * N_C]
    cm = xbc[:, D_INNER + G_C * N_C:]
    e = e_ref[...]
    lower = _tri(rows, "lower")
    dt = _softplus(sm_ref[...] + dtrow_ref[...])
    acum = _dot_hi(lower.astype(F32), dt * -jnp.exp(arow_ref[...]))
    a_t = _softplus(smt_ref[...] + dtcol_ref[...]) * -jnp.exp(acol_ref[...])
    acum_t = _dot_hi(a_t, _tri(rows, "upper").astype(F32))
    xdt = xs * _expand(dt, e)
    xdte = (xdt * _expand(jnp.exp(acum[rows - 1:rows, :] - acum), e)).astype(BF16)
    scale_y = _expand(jnp.exp(acum), e)
    chunk_decay = scale_y[rows - 1:rows, :]
    xdt16 = xdt.astype(BF16)
    neg_upper = jnp.where(lower, 0.0, -jnp.inf)
    acum2, acum2_t = acum * LOG2E, acum_t * LOG2E
    gw = HPG * P_C
    ys = []
    for g in range(G_C):
        bg = bm[:, g * N_C:(g + 1) * N_C]
        cg16 = cm[:, g * N_C:(g + 1) * N_C].astype(BF16)
        cb = _dot_nt(cg16, bg.astype(BF16))
        yg = []
        for hh in range(HPG):
            h = g * HPG + hh
            lmat = jnp.exp2(acum2[:, h:h + 1] - acum2_t[h:h + 1, :] + neg_upper)
            yg.append(_dot((cb * lmat).astype(BF16), xdt16[:, h * P_C:(h + 1) * P_C]))
        gs = slice(g * gw, (g + 1) * gw)
        sg = s_ref[:, gs]
        y_off = _dot(cg16, sg.astype(BF16)) * scale_y[:, gs]
        ys.append(jnp.concatenate(yg, axis=1) + y_off)
        s_ref[:, gs] = sg * chunk_decay[:, gs] + _dot(bg.T.astype(BF16), xdte[:, gs])
    y = jnp.concatenate(ys, axis=1) + dx_ref[...] * xs
    y = y * _silu(z_ref[...])
    o_ref[...] = _group_rms(y, gn_ref[...]).astype(BF16)

    @pl.when(c == pl.num_programs(1) - 1)
    def _():
        s_out_ref[0] = s_ref[...].T.reshape(H_C, P_C, N_C)


def _ssd_small_params(dt_bias, a_log, ns):
    row = lambda v: jnp.zeros((1, LANES), F32).at[0, :H_C].set(v)
    col = lambda v: jnp.zeros((ns, 1), F32).at[:H_C, 0].set(v)
    return row(dt_bias), row(a_log), col(dt_bias), col(a_log)


def _ssd_prompt(z, xbc, small, small_t, conv_w, conv_b, dt_bias, a_log, d_skip, gnorm, bn, l):
    rows = SSD_CHUNK
    nc = l // rows
    t = bn * l
    ns = small_t.shape[0]
    dtrow, arow, dtcol, acol = _ssd_small_params(dt_bias, a_log, ns)
    tok = lambda n: pl.BlockSpec((rows, n), lambda b, c: (b * nc + c, 0))
    return pl.pallas_call(
        _ssd_prompt_body,
        grid=(bn, nc),
        in_specs=[
            tok(D_INNER), tok(SSD_CONV_CH), tok(LANES),
            pl.BlockSpec((ns, rows), lambda b, c: (0, b * nc + c)),
            _whole((CONV_W, SSD_CONV_CH)), _whole((1, SSD_CONV_CH)),
            _whole((1, LANES)), _whole((1, LANES)), _whole((ns, 1)), _whole((ns, 1)),
            _whole((1, D_INNER)), _whole((1, D_INNER)), _whole((3 * LANES, D_INNER)),
        ],
        out_specs=[tok(D_INNER), pl.BlockSpec((1, H_C, P_C, N_C), lambda b, c: (b, 0, 0, 0))],
        out_shape=[jax.ShapeDtypeStruct((t, D_INNER), BF16), jax.ShapeDtypeStruct((bn, H_C, P_C, N_C), F32)],
        scratch_shapes=[pltpu.VMEM((rows + SUBLANES, SSD_CONV_CH), F32), pltpu.VMEM((N_C, D_INNER), F32)],
        compiler_params=_cp("arbitrary", "arbitrary"),
        name="ssd_prompt",
    )(z, xbc, small, small_t, conv_w, conv_b.reshape(1, SSD_CONV_CH), dtrow, arow, dtcol, acol,
      jnp.repeat(d_skip, P_C).reshape(1, D_INNER), gnorm.reshape(1, D_INNER), _head_expander(P_C))


def _ssd_decode_pre_body(xbc_ref, hist_ref, sm_ref, cw_ref, cb_ref, dtrow_ref, arow_ref, e_ref, en_ref,
                         new_ref, xs_ref, xdt_ref, b_ref, c_ref, dax_ref, dan_ref):
    xbc = _silu(_conv_step(xbc_ref[...], hist_ref, cw_ref, new_ref) + cb_ref[...])
    xs = xbc[:, :D_INNER]
    dt = _softplus(sm_ref[...] + dtrow_ref[...])
    a = dt * -jnp.exp(arow_ref[...])
    xs_ref[...] = xs
    xdt_ref[...] = xs * _expand(dt, e_ref[...])
    b_ref[...] = xbc[:, D_INNER:D_INNER + G_C * N_C]
    c_ref[...] = xbc[:, D_INNER + G_C * N_C:]
    dax_ref[...] = jnp.exp(_expand(a, e_ref[...]))
    dan_ref[...] = jnp.exp(_expand(a, en_ref[...]))


def _ssd_decode_state_body(s_ref, xdt_ref, b_ref, c_ref, dan_ref, *rest, first):
    s_out_ref, yoff_ref = rest[-2:]
    _first_or_aliased(first, functools.partial(
        _ssd_decode_state_compute, s_ref, xdt_ref, b_ref, c_ref, dan_ref, s_out_ref, yoff_ref), s_out_ref)


def _ssd_decode_state_compute(s_ref, xdt_ref, b_ref, c_ref, dan_ref, s_out_ref, yoff_ref):
    gw = HPG * P_C
    sub = SUBLANES
    row = lax.broadcasted_iota(jnp.int32, (sub, 1), 0)
    for r0 in range(0, s_ref.shape[0], sub):
        rows = slice(r0, r0 + sub)
        xdt = xdt_ref[rows, :]
        b16 = b_ref[rows, :].astype(BF16)
        c16 = c_ref[rows, :].astype(BF16)
        dan = dan_ref[rows, :]
        yoff = jnp.zeros((sub, gw), F32)
        for bb in range(sub):
            s = s_ref[r0 + bb].reshape(gw, N_C)
            res = _dot_nt(c16, s.astype(BF16))
            yoff = jnp.where(row == bb, res, yoff)
            x_one = jnp.where(row == bb, xdt, 0.0).astype(BF16)
            upd = _dot_tn(x_one, b16)
            for hh in range(HPG):
                rs = slice(hh * P_C, (hh + 1) * P_C)
                s_out_ref[r0 + bb, hh] = s[rs] * dan[bb:bb + 1, hh * N_C:(hh + 1) * N_C] + upd[rs]
        yoff_ref[rows, :] = yoff


def _ssd_decode_post_body(yoff_ref, dax_ref, xdt_ref, xs_ref, b_ref, c_ref, z_ref, dx_ref, gn_ref, o_ref):
    gw = HPG * P_C
    bc = b_ref[...] * c_ref[...]
    cbx = [jnp.broadcast_to(jnp.sum(bc[:, g * N_C:(g + 1) * N_C], axis=-1, keepdims=True), (bc.shape[0], gw))
           for g in range(G_C)]
    y = yoff_ref[...] * dax_ref[...] + jnp.concatenate(cbx, axis=1) * xdt_ref[...]
    y = y + dx_ref[...] * xs_ref[...]
    y = y * _silu(z_ref[...])
    o_ref[...] = _group_rms(y, gn_ref[...])


def _ssd_decode(z, xbc, hist, small, conv_w, conv_b, dt_bias, a_log, d_skip, gnorm, s_all, s_done, e):
    bn = xbc.shape[0]
    dtrow, arow, _, _ = _ssd_small_params(dt_bias, a_log, H_C)
    wide = jax.ShapeDtypeStruct((bn, D_INNER), F32)
    grp = jax.ShapeDtypeStruct((bn, G_C * N_C), F32)
    plain = pltpu.CompilerParams(vmem_limit_bytes=VMEM_LIMIT_BYTES)
    new_conv, xs, xdt, bm, cm, dax, dan = pl.pallas_call(
        _ssd_decode_pre_body,
        out_shape=[jax.ShapeDtypeStruct(hist.shape, F32), wide, wide, grp, grp, wide,
                   jax.ShapeDtypeStruct((bn, H_C * N_C), F32)],
        compiler_params=plain,
        name="ssd_decode_pre",
    )(xbc, hist, small, conv_w, conv_b.reshape(1, SSD_CONV_CH), dtrow, arow,
      _head_expander(P_C), _head_expander(N_C))
    nb = SSD_STATE_SEQS
    gw = HPG * P_C
    steps = bn // nb
    first = s_done is None
    blk = (None, nb, HPG, P_C, N_C)
    widths = (gw, N_C, N_C, HPG * N_C)
    args = [s_all, xdt, bm, cm, dan]
    if first:
        assert e == 0
        seq = lambda s, i: jnp.where(s == 0, i, steps - 1)
        grp_of = lambda s, g: jnp.where(s == 0, g, G_C - 1)
        vec = lambda n: pl.BlockSpec((nb, n), lambda s, i, g: (seq(s, i), grp_of(s, g)))
        in_specs = [pl.BlockSpec(blk, lambda s, i, g: (0, seq(s, i), grp_of(s, g), 0, 0))] + [vec(n) for n in widths]
        out_specs = [pl.BlockSpec(blk, lambda s, i, g: (s, i, g, 0, 0)), vec(gw)]
        grid, aliases, sem = (s_all.shape[0], steps, G_C), {}, ("arbitrary",) * 3
    else:
        vec = lambda n: pl.BlockSpec((nb, n), lambda i, g: (i, g))
        st = pl.BlockSpec(blk, lambda i, g: (e, i, g, 0, 0))
        in_specs = [st] + [vec(n) for n in widths] + [pl.BlockSpec(memory_space=pl.ANY)]
        out_specs = [st, vec(gw)]
        args.append(s_done)
        grid, aliases, sem = (steps, G_C), {len(args) - 1: 0}, ("parallel", "parallel")
    s_done, yoff = pl.pallas_call(
        functools.partial(_ssd_decode_state_body, first=first),
        grid=grid,
        in_specs=in_specs,
        out_specs=out_specs,
        out_shape=[jax.ShapeDtypeStruct(s_all.shape, F32), wide],
        input_output_aliases=aliases,
        compiler_params=_cp(*sem),
        name="ssd_decode_state",
    )(*args)
    mix = pl.pallas_call(
        _ssd_decode_post_body,
        out_shape=wide,
        compiler_params=plain,
        name="ssd_decode_post",
    )(yoff, dax, xdt, xs, bm, cm, z, jnp.repeat(d_skip, P_C).reshape(1, D_INNER), gnorm.reshape(1, D_INNER))
    return mix, new_conv, s_done


def _narrow_weights(w_small):
    n = w_small.shape[1]
    return jnp.zeros((D_MODEL, LANES), BF16).at[:, :n].set(w_small.astype(BF16))


def _even_q_pad(qa):
    bn = qa.shape[0]
    q = qa.reshape(bn, KV_A, G_A, HD_A)
    out = jnp.zeros((bn, KV_A, G_A, KV_A, HD_A), F32)
    for kv in range(KV_A):
        out = out.at[:, kv, :, kv, :].set(q[:, kv])
    return out.reshape(bn, H_A, KV_A * HD_A)


def _even_o_unpad(o8):
    bn = o8.shape[0]
    o = o8.reshape(bn, KV_A, G_A, KV_A, HD_A)
    return jnp.concatenate([o[:, kv, :, kv, :].reshape(bn, G_A * HD_A) for kv in range(KV_A)], axis=1)


def _trunk(x3, states, wts):
    (rel_bias, norm_ff1, norm_mix, norm_ff2, norm_final,
     ff1, ff2, even_w_in, even_w_out, swa_sinks, gdn_conv_w, gdn_A_log, gdn_dt_bias, gdn_norm,
     ssd_w_in, ssd_w_out, ssd_conv_w, ssd_conv_b, ssd_dt_bias, ssd_A_log, ssd_D, ssd_norm) = wts
    bn, l, d = x3.shape
    t = bn * l
    x = x3.reshape(t, d)
    decode = states is not None
    ks, vs, gconv, gssm, sconv, sssm = [], [], [], [], [], []
    depth = norm_ff1.shape[0]
    width = KV_A * HD_A
    if decode:
        gssm_done = sssm_done = None
        n_even = states[0].shape[0]
        cache_t = lambda c: c.transpose(0, 1, 3, 4, 2).reshape(n_even, bn, width, WINDOW)
        ckt, cvt = cache_t(states[0]), cache_t(states[1])
        ghist, shist = states[2].transpose(0, 2, 1, 3), states[4].transpose(0, 2, 1, 3)
    for layer in range(depth):
        x = _ffn(x, norm_ff1[layer], ff1, layer)
        e = layer // 2
        if layer % 2 == 0:
            w_all, ws = even_w_in
            qa, kv, qkvb, z, small, small_t = _inproj(x, norm_mix[layer], w_all, e, ws[e], 2 * SUBLANES,
                                                      (A_Q, 2 * width, B_QKV, B_Z))
            if decode:
                o8 = _swa_decode(_even_q_pad(qa), kv, ckt, cvt, e, rel_bias, swa_sinks[e])
                o_a = _even_o_unpad(o8)
                o_b, new_conv, gssm_done = _gdn_decode(qkvb, ghist[e], z, small, gdn_conv_w[e], gdn_A_log[e],
                                                       gdn_dt_bias[e], gdn_norm[e], states[3], gssm_done, e)
                new_k, new_v = kv[:, :width], kv[:, width:]
                new_conv = new_conv.transpose(1, 0, 2)
            else:
                o_a = _swa_prompt(qa, kv, rel_bias, swa_sinks[e], bn, l)
                o_b, s_new = _gdn_prompt(qkvb, z, small, small_t, gdn_conv_w[e], gdn_A_log[e],
                                         gdn_dt_bias[e], gdn_norm[e], bn, l)
                kv3 = kv.reshape(bn, l, 2 * width)
                new_k = kv3[:, l - WINDOW:, :width]
                new_v = kv3[:, l - WINDOW:, width:]
                new_conv = qkvb.reshape(bn, l, B_QKV)[:, l - (CONV_W - 1):]
                gssm.append(s_new)
            ks.append(new_k)
            vs.append(new_v)
            gconv.append(new_conv)
            mixes = [(o_a, even_w_out, e, 0), (o_b, even_w_out, e, 1)]
        else:
            w_all, ws = ssd_w_in
            z, xbc, small, small_t = _inproj(x, norm_mix[layer], w_all, e, ws[e], H_C, (D_INNER, SSD_CONV_CH))
            if decode:
                mix, new_conv, sssm_done = _ssd_decode(z, xbc, shist[e], small, ssd_conv_w[e], ssd_conv_b[e],
                                                       ssd_dt_bias[e], ssd_A_log[e], ssd_D[e], ssd_norm[e],
                                                       states[5], sssm_done, e)
                new_conv = new_conv.transpose(1, 0, 2)
            else:
                mix, s_new = _ssd_prompt(z, xbc, small, small_t, ssd_conv_w[e], ssd_conv_b[e], ssd_dt_bias[e],
                                         ssd_A_log[e], ssd_D[e], ssd_norm[e], bn, l)
                new_conv = xbc.reshape(bn, l, SSD_CONV_CH)[:, l - (CONV_W - 1):]
                sssm.append(s_new)
            sconv.append(new_conv)
            mixes = [(mix, ssd_w_out, e, 0)]
        x = _ffn(x, norm_ff2[layer], ff2, layer, mixes=mixes,
                 g_final=norm_final if layer == depth - 1 else None)
    if decode:
        bs = DECODE_SEQ_BLOCK
        new_t = lambda rows: jnp.stack(rows).reshape(n_even, bn // bs, bs, width).transpose(0, 1, 3, 2)
        back = lambda c: c.reshape(n_even, bn, KV_A, HD_A, WINDOW).transpose(0, 1, 4, 2, 3)
        ks = back(_cache_shift(ckt, new_t(ks)))
        vs = back(_cache_shift(cvt, new_t(vs)))
        gssm, sssm = gssm_done, sssm_done
    else:
        kv5 = (len(ks), bn, WINDOW, KV_A, HD_A)
        ks, vs = jnp.stack(ks).reshape(kv5), jnp.stack(vs).reshape(kv5)
        gssm, sssm = jnp.stack(gssm), jnp.stack(sssm)
    return (x.reshape(bn, l, d), ks, vs, jnp.stack(gconv), gssm, jnp.stack(sconv), sssm)


def kernel(x_prompt, x_sample, cache_swa_k, cache_swa_v, state_gdn_conv, state_gdn_ssm, state_ssd_conv, state_ssd_ssm, rel_bias, norm_ff1, norm_mix, norm_ff2, norm_final, ff1_gate, ff1_up, ff1_down, ff2_gate, ff2_up, ff2_down, even_w_in, even_w_out, swa_sinks, gdn_conv_w, gdn_A_log, gdn_dt_bias, gdn_norm, ssd_w_in, ssd_w_out, ssd_conv_w, ssd_conv_b, ssd_dt_bias, ssd_A_log, ssd_D, ssd_norm):
    depth = norm_ff1.shape[0]
    ff1 = (ff1_gate.astype(BF16), ff1_up.astype(BF16), ff1_down.astype(BF16))
    ff2 = (ff2_gate.astype(BF16), ff2_up.astype(BF16), ff2_down.astype(BF16))
    n_even_main = A_Q + 2 * KV_A * HD_A + B_QKV + B_Z
    even_in = (even_w_in.astype(BF16),
               [_narrow_weights(even_w_in[e][:, n_even_main:]) for e in range(even_w_in.shape[0])])
    n_odd_main = D_INNER + SSD_CONV_CH
    odd_in = (ssd_w_in.astype(BF16),
              [_narrow_weights(ssd_w_in[e][:, n_odd_main:]) for e in range(ssd_w_in.shape[0])])
    wts = (rel_bias, norm_ff1, norm_mix, norm_ff2, norm_final, ff1, ff2,
           even_in, even_w_out.astype(BF16), swa_sinks, gdn_conv_w, gdn_A_log, gdn_dt_bias, gdn_norm,
           odd_in, ssd_w_out.astype(BF16), ssd_conv_w, ssd_conv_b, ssd_dt_bias, ssd_A_log, ssd_D, ssd_norm)
    y_p, p_k, p_v, p_gconv, p_gssm, p_sconv, p_sssm = _trunk(x_prompt, None, wts)
    states = (cache_swa_k, cache_swa_v, state_gdn_conv, state_gdn_ssm, state_ssd_conv, state_ssd_ssm)
    y_s, s_k, s_v, s_gconv, s_gssm, s_sconv, s_sssm = _trunk(x_sample, states, wts)
    return (y_p, y_s, p_k, p_v, p_gconv, p_gssm, p_sconv, p_sssm,
            s_k, s_v, s_gconv, s_gssm, s_sconv, s_sssm)
```

```python
import functools
import math

import numpy as np
import jax
import jax.numpy as jnp
from jax import lax
from jax.experimental import pallas as pl
from jax.experimental.pallas import tpu as pltpu

F32 = jnp.float32
BF16 = jnp.bfloat16
HI = lax.Precision.HIGHEST

EPS = 1e-6
NEG_INF = -1e30
LOG2E = math.log2(math.e)
D_MODEL = 1024
WINDOW = 128
BLOCK = 128
H_A, KV_A, G_A, HD_A = 8, 2, 4, 64
N_BUCKETS, MAX_DIST = 32, 128
H_B, DK_B, DV_B = 4, 128, 128
CONV_W = 4
GDN_CHUNK = 64
D_INNER = 2048
P_C, H_C, N_C, G_C = 64, 32, 128, 4
HPG = H_C // G_C
SSD_CHUNK = 128
A_Q = H_A * HD_A
B_QKV = 3 * H_B * DK_B
B_Z = H_B * DV_B
SSD_CONV_CH = D_INNER + 2 * G_C * N_C
LANES = 128
SUBLANES = 8
VMEM_LIMIT_BYTES = 56 * 1024 * 1024
ROW_TILE = 512
SWA_QUERY_BLOCKS = 4
GDN_PREP_CHUNKS = 4
GDN_SCAN_GROUP = 2
DECODE_SEQ_BLOCK = 32
SSD_STATE_SEQS = 4 * SUBLANES
GDN_STATE_SEQS = 2 * SUBLANES


def _cp(*sem):
    return pltpu.CompilerParams(dimension_semantics=sem, vmem_limit_bytes=VMEM_LIMIT_BYTES)


def _whole(shape):
    nd = len(shape)
    return pl.BlockSpec(shape, lambda *_: (0,) * nd, pipeline_mode=pl.Buffered(1))


def _rms(x, g):
    return x * lax.rsqrt(jnp.mean(x * x, axis=-1, keepdims=True) + EPS) * g


def _silu(x):
    h = 0.5 * x
    return h * jnp.tanh(h) + h


def _softplus(x):
    return jnp.maximum(x, 0.0) + jnp.log1p(jnp.exp(-jnp.abs(x)))


def _dot(a, b):
    return jnp.dot(a, b, preferred_element_type=F32)


def _dot_nt(a, b):
    return lax.dot_general(a, b, (((1,), (1,)), ((), ())), preferred_element_type=F32)


def _dot_tn(a, b):
    return lax.dot_general(a, b, (((0,), (0,)), ((), ())), preferred_element_type=F32)


def _dot_hi(a, b):
    return jnp.dot(a, b, precision=HI, preferred_element_type=F32)


def _expand(x, e3):
    hi = x.astype(BF16)
    r = x - hi.astype(F32)
    mid = r.astype(BF16)
    lo = (r - mid.astype(F32)).astype(BF16)
    return _dot(jnp.concatenate([hi, mid, lo], axis=1), e3)


def _tri(n, kind):
    r = lax.broadcasted_iota(jnp.int32, (n, n), 0)
    c = lax.broadcasted_iota(jnp.int32, (n, n), 1)
    return {"lower": r >= c, "strict_lower": r > c, "upper": r <= c}[kind]


def _ffn_body(*refs, n_mix, final):
    refs = list(refs)
    x = refs.pop(0)[...]
    for _ in range(n_mix):
        m_ref, wo_ref = refs.pop(0), refs.pop(0)
        x = x + _dot(m_ref[...].astype(BF16), wo_ref[...])
    g_ref, wg_ref, wu_ref, wd_ref = refs[:4]
    o_ref = refs[-1]
    hn = _rms(x, g_ref[...]).astype(BF16)
    act = (_silu(_dot(hn, wg_ref[...])) * _dot(hn, wu_ref[...])).astype(BF16)
    y = x + 0.5 * _dot(act, wd_ref[...])
    if final:
        y = _rms(y, refs[4][...])
    o_ref[...] = y


def _layer_slab(w_all, layer, rows=None, row_block=0):
    shape = (rows or w_all.shape[1], w_all.shape[2])
    return pl.BlockSpec((None,) + shape, lambda i: (layer, row_block, 0), pipeline_mode=pl.Buffered(1))


def _ffn(x, g, ffw, layer, mixes=(), g_final=None):
    t, d = x.shape
    tm = min(t, ROW_TILE)
    final = g_final is not None
    row = pl.BlockSpec((tm, d), lambda i: (i, 0))
    in_specs, args = [row], [x]
    for m, w_all, e, row_block in mixes:
        in_specs += [pl.BlockSpec((tm, m.shape[1]), lambda i: (i, 0)), _layer_slab(w_all, e, m.shape[1], row_block)]
        args += [m, w_all]
    in_specs += [_whole((1, d))] + [_layer_slab(w, layer) for w in ffw]
    args += [g.reshape(1, d), *ffw]
    if final:
        in_specs.append(_whole((1, d)))
        args.append(g_final.reshape(1, d))
    return pl.pallas_call(
        functools.partial(_ffn_body, n_mix=len(mixes), final=final),
        grid=(t // tm,),
        in_specs=in_specs,
        out_specs=row,
        out_shape=jax.ShapeDtypeStruct((t, d), F32),
        compiler_params=_cp("parallel"),
        name="ffn",
    )(*args)


def _inproj_body(x_ref, g_ref, w_ref, ws_ref, *outs, splits):
    hn = _rms(x_ref[...], g_ref[...]).astype(BF16)
    off = 0
    for o_ref, n in zip(outs[:-2], splits):
        o_ref[...] = _dot(hn, w_ref[:, off:off + n])
        off += n
    small = _dot(hn, ws_ref[...])
    outs[-2][...] = small
    outs[-1][...] = small.T[:outs[-1].shape[0]]


def _inproj(x, g, w_all, e, ws, ns, splits):
    t, d = x.shape
    w_spec = _layer_slab(w_all, e)
    tm = min(t, ROW_TILE)
    row = pl.BlockSpec((tm, d), lambda i: (i, 0))
    out_specs = [pl.BlockSpec((tm, n), lambda i: (i, 0)) for n in splits]
    out_specs += [pl.BlockSpec((tm, LANES), lambda i: (i, 0)), pl.BlockSpec((ns, tm), lambda i: (0, i))]
    out_shape = [jax.ShapeDtypeStruct((t, n), F32) for n in splits]
    out_shape += [jax.ShapeDtypeStruct((t, LANES), F32), jax.ShapeDtypeStruct((ns, t), F32)]
    return pl.pallas_call(
        functools.partial(_inproj_body, splits=splits),
        grid=(t // tm,),
        in_specs=[row, _whole((1, d)), w_spec, _whole(ws.shape)],
        out_specs=out_specs,
        out_shape=out_shape,
        compiler_params=_cp("parallel"),
        name="inproj",
    )(x, g.reshape(1, d), w_all, ws)


def _t5_bucket_np(dist):
    max_exact = N_BUCKETS // 2
    df = np.maximum(dist, max_exact).astype(np.float32)
    large = max_exact + (np.log(df / np.float32(max_exact)) / np.float32(math.log(MAX_DIST / max_exact))
                         * np.float32(N_BUCKETS - max_exact)).astype(np.int32)
    return np.where(dist < max_exact, dist, np.minimum(large, N_BUCKETS - 1)).astype(np.int32)


def _band_bucket_ids(n_q, n_k, offset):
    d = offset + np.arange(n_q)[:, None] - np.arange(n_k)[None, :]
    valid = (d >= 0) & (d <= WINDOW)
    return np.where(valid, _t5_bucket_np(np.clip(d, 0, WINDOW)), -1).astype(np.int32)


def _bias_from_buckets(bid, rb_ref, h):
    acc = jnp.full(bid.shape, NEG_INF, F32)
    for bk in range(N_BUCKETS):
        acc = jnp.where(bid == bk, rb_ref[bk, h], acc)
    return acc


def _swa_prompt_body(bid_ref, rb_ref, sk_ref, q_ref, kvp_ref, kvc_ref, o_ref, bias_ref):
    first_step = (pl.program_id(0) == 0) & (pl.program_id(1) == 0)

    @pl.when(first_step)
    def _():
        bid = bid_ref[...]
        col = lax.broadcasted_iota(jnp.int32, (BLOCK, 2 * BLOCK), 1)
        for h in range(H_A):
            bias = _bias_from_buckets(bid, rb_ref, h)
            bias_ref[h] = bias
            bias_ref[H_A + h] = jnp.where(col < BLOCK, NEG_INF, bias)

    kvp = kvp_ref[...]
    kvc = kvc_ref[...]
    scale = HD_A ** -0.5
    heads = range(H_A)
    for j in range(q_ref.shape[0] // BLOCK):
        rows = slice(j * BLOCK, (j + 1) * BLOCK)
        prev = kvp if j == 0 else kvc[(j - 1) * BLOCK:j * BLOCK]
        first_block = jnp.where(pl.program_id(1) == 0, H_A, 0) if j == 0 else 0
        k, v = [], []
        for kv in range(KV_A):
            ks = slice(kv * HD_A, (kv + 1) * HD_A)
            vs = slice(KV_A * HD_A + kv * HD_A, KV_A * HD_A + (kv + 1) * HD_A)
            k.append(jnp.concatenate([prev[:, ks], kvc[rows, ks]], axis=0).astype(BF16))
            v.append(jnp.concatenate([prev[:, vs], kvc[rows, vs]], axis=0).astype(BF16))
        s = [_dot_nt((q_ref[rows, h * HD_A:(h + 1) * HD_A] * scale).astype(BF16), k[h // G_A])
             + bias_ref[first_block + h] for h in heads]
        m = [jnp.maximum(jnp.max(s[h], axis=-1, keepdims=True), sk_ref[h]) for h in heads]
        p = [jnp.exp(s[h] - m[h]) for h in heads]
        denom = [jnp.sum(p[h], axis=-1, keepdims=True) + jnp.exp(sk_ref[h] - m[h]) for h in heads]
        outs = [_dot(p[h].astype(BF16), v[h // G_A]) / denom[h] for h in heads]
        o_ref[rows, :] = jnp.concatenate(outs, axis=1).astype(BF16)


def _swa_prompt(qa, kv, rel_bias, sinks, bn, l):
    qb = SWA_QUERY_BLOCKS
    nb = l // (qb * BLOCK)
    t = bn * l
    bid = jnp.asarray(_band_bucket_ids(BLOCK, 2 * BLOCK, BLOCK))
    width = 2 * KV_A * HD_A
    return pl.pallas_call(
        _swa_prompt_body,
        grid=(bn, nb),
        in_specs=[
            _whole((BLOCK, 2 * BLOCK)),
            pl.BlockSpec(memory_space=pltpu.SMEM),
            pl.BlockSpec(memory_space=pltpu.SMEM),
            pl.BlockSpec((qb * BLOCK, A_Q), lambda b, i: (b * nb + i, 0)),
            pl.BlockSpec((BLOCK, width), lambda b, i: (jnp.maximum((b * nb + i) * qb - 1, 0), 0)),
            pl.BlockSpec((qb * BLOCK, width), lambda b, i: (b * nb + i, 0)),
        ],
        out_specs=pl.BlockSpec((qb * BLOCK, A_Q), lambda b, i: (b * nb + i, 0)),
        out_shape=jax.ShapeDtypeStruct((t, A_Q), BF16),
        scratch_shapes=[pltpu.VMEM((2 * H_A, BLOCK, 2 * BLOCK), F32)],
        compiler_params=_cp("arbitrary", "arbitrary"),
        name="swa_prompt",
    )(bid, rel_bias, sinks, qa, kv, kv)


def _swa_decode_body(bid_ref, rb_ref, sk_ref, q_ref, kvn_ref, ckt_ref, cvt_ref, o_ref):
    bid = bid_ref[...]
    row = lax.broadcasted_iota(jnp.int32, (H_A, 1), 0)
    bias = jnp.zeros((H_A, bid.shape[1]), F32)
    sk = jnp.zeros((H_A, 1), F32)
    for h in range(H_A):
        bias = jnp.where(row == h, _bias_from_buckets(bid, rb_ref, h), bias)
        sk = jnp.where(row == h, sk_ref[h], sk)
    bias_c = bias[:, :WINDOW]
    bias_n = bias[:, WINDOW:WINDOW + 1]
    scale = HD_A ** -0.5
    q = q_ref[...]
    kvn = kvn_ref[...]
    width = KV_A * HD_A
    k_new = kvn[:, None, :width]
    v_new = kvn[:, None, width:]
    s = lax.dot_general(q.astype(BF16), ckt_ref[...].astype(BF16), (((2,), (1,)), ((0,), (0,))),
                        preferred_element_type=F32) * scale + bias_c[None]
    s_n = jnp.sum(q * k_new, axis=-1, keepdims=True) * scale + bias_n[None]
    m = jnp.maximum(jnp.maximum(jnp.max(s, axis=-1, keepdims=True), s_n), sk[None])
    p = jnp.exp(s - m)
    p_n = jnp.exp(s_n - m)
    denom = jnp.sum(p, axis=-1, keepdims=True) + p_n + jnp.exp(sk[None] - m)
    o = lax.dot_general((p / denom).astype(BF16), cvt_ref[...].astype(BF16), (((2,), (2,)), ((0,), (0,))),
                        preferred_element_type=F32)
    o_ref[...] = o + (p_n / denom) * v_new


def _swa_decode(q_pad, kv_new, cache_kt, cache_vt, e, rel_bias, sinks):
    bn = q_pad.shape[0]
    bs = DECODE_SEQ_BLOCK
    width = KV_A * HD_A
    ids = _band_bucket_ids(1, WINDOW + 1, WINDOW)
    bid = np.full((1, WINDOW + LANES), -1, np.int32)
    bid[:, :WINDOW + 1] = ids
    return pl.pallas_call(
        _swa_decode_body,
        grid=(bn // bs,),
        in_specs=[
            _whole((1, WINDOW + LANES)),
            pl.BlockSpec(memory_space=pltpu.SMEM),
            pl.BlockSpec(memory_space=pltpu.SMEM),
            pl.BlockSpec((bs, H_A, width), lambda i: (i, 0, 0)),
            pl.BlockSpec((bs, 2 * width), lambda i: (i, 0)),
            pl.BlockSpec((None, bs, width, WINDOW), lambda i: (e, i, 0, 0)),
            pl.BlockSpec((None, bs, width, WINDOW), lambda i: (e, i, 0, 0)),
        ],
        out_specs=pl.BlockSpec((bs, H_A, width), lambda i: (i, 0, 0)),
        out_shape=jax.ShapeDtypeStruct((bn, H_A, width), F32),
        compiler_params=_cp("parallel"),
        name="swa_decode",
    )(jnp.asarray(bid), rel_bias, sinks, q_pad, kv_new, cache_kt, cache_vt)


def _cache_shift_body(c_ref, nt_ref, o_ref):
    o_ref[...] = pltpu.roll(c_ref[...], WINDOW - 1, axis=2)
    nt = nt_ref[...]
    for b in range(c_ref.shape[0]):
        o_ref[b, :, WINDOW - 1:WINDOW] = nt[:, b:b + 1]


def _cache_shift(cache_t, new_t):
    n, bn, width, win = cache_t.shape
    bs = new_t.shape[3]
    blk = pl.BlockSpec((None, bs, width, win), lambda e, i: (e, i, 0, 0))
    return pl.pallas_call(
        _cache_shift_body,
        grid=(n, bn // bs),
        in_specs=[blk, pl.BlockSpec((None, None, width, bs), lambda e, i: (e, i, 0, 0))],
        out_specs=blk,
        out_shape=jax.ShapeDtypeStruct(cache_t.shape, F32),
        compiler_params=_cp("parallel", "parallel"),
        name="cache_shift",
    )(cache_t, new_t)


def _conv_rows(x_ref, xbuf_ref, cw_ref, rows, cb_ref=None):
    xbuf_ref[SUBLANES:SUBLANES + rows, :] = x_ref[...]
    strips = []
    for c0 in range(0, xbuf_ref.shape[1], LANES):
        cols = slice(c0, c0 + LANES)
        x = xbuf_ref[:, cols]
        out = x * cw_ref[0:1, cols]
        for i in range(1, CONV_W):
            out = pltpu.roll(out, 1, axis=0) + x * cw_ref[i:i + 1, cols]
        out = out[SUBLANES:]
        strips.append(_silu(out if cb_ref is None else out + cb_ref[:, cols]))
    return jnp.concatenate(strips, axis=1)


def _conv_chunk(x_ref, xbuf_ref, cw_ref, rows, cb_ref=None):
    out = _conv_rows(x_ref, xbuf_ref, cw_ref, rows, cb_ref)
    xbuf_ref[0:SUBLANES, :] = xbuf_ref[rows:rows + SUBLANES, :]
    return out


def _l2norm(x):
    return x * lax.rsqrt(jnp.sum(x * x, axis=-1, keepdims=True) + EPS)


def _chunk_masks(n, cc):
    r = lax.broadcasted_iota(jnp.int32, (n, n), 0)
    c = lax.broadcasted_iota(jnp.int32, (n, n), 1)
    shift = int(math.log2(cc))
    same = lax.shift_right_logical(r, shift) == lax.shift_right_logical(c, shift)
    return same, same & (r >= c), same & (r <= c)


def _dot_bf16x3(a, b):
    a_hi = a.astype(BF16)
    a_lo = (a - a_hi.astype(F32)).astype(BF16)
    b_hi = b.astype(BF16)
    b_lo = (b - b_hi.astype(F32)).astype(BF16)
    return _dot(a_hi, b_hi) + _dot(a_hi, b_lo) + _dot(a_lo, b_hi)


def _solve_masks(n, cc, base=SUBLANES):
    r = np.arange(n)[:, None]
    c = np.arange(n)[None, :]
    masks = [-(((r // base) == (c // base)) & (r > c)).astype(np.float32)]
    s = base
    while s < cc:
        sibling = ((r // (2 * s)) == (c // (2 * s))) & ((r // s) % 2 == 1) & ((c // s) % 2 == 0)
        masks.append(sibling.astype(np.float32))
        s *= 2
    return np.stack(masks)


def _gdn_prep_body(qkv_ref, halo_ref, sm_ref, smt_ref, cw_ref, acol_ref, dtcol_ref, tri_ref,
                   u_ref, w_ref, qg_ref, kd_ref, aqk_ref, gl_ref, xbuf_ref):
    rows = qkv_ref.shape[0]
    cc = GDN_CHUNK
    xbuf_ref[0:SUBLANES, :] = jnp.where(pl.program_id(1) == 0, 0.0, halo_ref[...])
    conv = _conv_rows(qkv_ref, xbuf_ref, cw_ref, rows)
    nq = H_B * DK_B
    beta_all = jax.nn.sigmoid(sm_ref[...])
    gt_all = -jnp.exp(acol_ref[...]) * _softplus(smt_ref[...] + dtcol_ref[...])
    pad = jnp.zeros((LANES - gt_all.shape[0], rows), F32)
    g_all = jnp.concatenate([gt_all, pad], axis=0).T
    same, lower, upper = _chunk_masks(rows, cc)
    gc = _dot_hi(lower.astype(F32), g_all)
    gct = _dot_hi(gt_all, upper.astype(F32))
    gsum = _dot_hi(same.astype(F32), g_all)
    neg_outside = jnp.where(lower, 0.0, -jnp.inf)
    gc2, gct2 = gc * LOG2E, gct * LOG2E
    heads = range(H_B)
    qh, kh16, kb, decay, egc = [], [], [], [], []
    for h in heads:
        q = _l2norm(conv[:, h * DK_B:(h + 1) * DK_B]) * (DK_B ** -0.5)
        k = _l2norm(conv[:, nq + h * DK_B:nq + (h + 1) * DK_B])
        v = conv[:, 2 * nq + h * DV_B:2 * nq + (h + 1) * DV_B]
        beta = beta_all[:, h:h + 1]
        gcol = gc[:, H_B + h:H_B + h + 1]
        decay.append(jnp.exp2(gc2[:, H_B + h:H_B + h + 1] - gct2[H_B + h:H_B + h + 1, :] + neg_outside))
        eg = jnp.exp(gcol)
        qg_ref[:, h * DK_B:(h + 1) * DK_B] = (q * eg).astype(BF16)
        kd_ref[:, h * DK_B:(h + 1) * DK_B] = (k * jnp.exp(gsum[:, H_B + h:H_B + h + 1] - gcol)).astype(BF16)
        qh.append(q.astype(BF16))
        kh16.append(k.astype(BF16))
        kb.append(k * beta)
        egc.append((v * beta, eg))
    m = [_dot_nt(kb[h].astype(BF16), kh16[h]) * decay[h] for h in heads]
    for h in heads:
        aqk = _dot_nt(qh[h], kh16[h]) * decay[h]
        blocks = [aqk[i * cc:(i + 1) * cc, i * cc:(i + 1) * cc] for i in range(rows // cc)]
        aqk_ref[:, h * cc:(h + 1) * cc] = jnp.concatenate(blocks, axis=0).astype(BF16)
    mm = lambda xs, ys: [_dot(x.astype(BF16), y.astype(BF16)) for x, y in zip(xs, ys)]
    a = [m[h] * tri_ref[0] for h in heads]
    a2 = mm(a, a)
    a3 = mm(a, a2)
    a4 = mm(a2, a2)
    x1 = [a[h] + a2[h] + a3[h] for h in heads]
    x1a4 = mm(x1, a4)
    r = [x1[h] + a4[h] + x1a4[h] for h in heads]
    for level in range(1, tri_ref.shape[0]):
        c = [m[h] * tri_ref[level] for h in heads]
        rc = mm(r, c)
        y = [c[h] + rc[h] for h in heads]
        yr = mm(y, r)
        r = [r[h] - y[h] - yr[h] for h in heads]
    for h in heads:
        vb, eg = egc[h]
        rhs = jnp.concatenate([vb, kb[h] * eg], axis=1)
        sol = rhs + _dot_bf16x3(r[h], rhs)
        u_ref[:, h * DV_B:(h + 1) * DV_B] = sol[:, :DV_B]
        w_ref[:, h * DK_B:(h + 1) * DK_B] = sol[:, DV_B:].astype(BF16)
    for i in range(rows // cc):
        gl = [jnp.broadcast_to(jnp.exp(gsum[i * cc:i * cc + 1, H_B + h:H_B + h + 1]), (SUBLANES, DV_B))
              for h in heads]
        gl_ref[0, i] = jnp.concatenate(gl, axis=1)


def _gdn_scan_body(u_ref, w_ref, qg_ref, kd_ref, aqk_ref, gl_ref, z_ref, gn_ref, o_ref, s_out_ref, s_ref):
    c = pl.program_id(0)
    bn = u_ref.shape[0]
    cc = GDN_CHUNK

    @pl.when(c == 0)
    def _():
        s_ref[...] = jnp.zeros_like(s_ref)

    gn = gn_ref[...]
    group = GDN_SCAN_GROUP
    for b0 in range(0, bn, group):
        combos = [(b, h) for b in range(b0, min(b0 + group, bn)) for h in range(H_B)]
        res = {}
        for b, h in combos:
            hs = slice(h * DK_B, (h + 1) * DK_B)
            wq = jnp.concatenate([w_ref[b, :, hs], qg_ref[b, :, hs]], axis=0)
            res[b, h] = _dot(wq, s_ref[b, h].astype(BF16))
        v16 = {}
        for b, h in combos:
            hs = slice(h * DV_B, (h + 1) * DV_B)
            v16[b, h] = (u_ref[b, :, hs] - res[b, h][:cc]).astype(BF16)
        for b, h in combos:
            hs = slice(h * DV_B, (h + 1) * DV_B)
            o = res[b, h][cc:] + _dot(aqk_ref[b, :, h * cc:(h + 1) * cc], v16[b, h])
            upd = _dot_tn(kd_ref[b, :, hs], v16[b, h])
            s_ref[b, h] = s_ref[b, h] * gl_ref[b, 0, 0:1, hs] + upd
            o_ref[b, :, hs] = (_rms(o, gn) * _silu(z_ref[b, :, hs])).astype(BF16)

    @pl.when(c == pl.num_programs(0) - 1)
    def _():
        s_out_ref[...] = s_ref[...]


def _gdn_prompt(qkvb, z, small, small_t, conv_w, a_log, dt_bias, gnorm, bn, l):
    rows = GDN_PREP_CHUNKS * GDN_CHUNK
    cc = GDN_CHUNK
    nb = l // rows
    nc = l // cc
    t = bn * l
    ns = small_t.shape[0]
    nq = H_B * DK_B
    pad_col = lambda v: jnp.zeros((ns, 1), F32).at[H_B:2 * H_B, 0].set(v)
    tri = jnp.asarray(_solve_masks(rows, cc))
    tok = lambda n: pl.BlockSpec((rows, n), lambda b, i: (b * nb + i, 0))
    halo = pl.BlockSpec((SUBLANES, B_QKV), lambda b, i: (jnp.maximum((b * nb + i) * (rows // SUBLANES) - 1, 0), 0))
    u, w, qg, kd, aqk, gl = pl.pallas_call(
        _gdn_prep_body,
        grid=(bn, nb),
        in_specs=[
            tok(B_QKV), halo, tok(LANES),
            pl.BlockSpec((ns, rows), lambda b, i: (0, b * nb + i)),
            _whole((CONV_W, B_QKV)), _whole((ns, 1)), _whole((ns, 1)), _whole(tri.shape),
        ],
        out_specs=[tok(nq), tok(nq), tok(nq), tok(nq), tok(H_B * cc),
                   pl.BlockSpec((1, rows // cc, SUBLANES, nq), lambda b, i: (b, i, 0, 0))],
        out_shape=[jax.ShapeDtypeStruct((t, nq), F32)] + [jax.ShapeDtypeStruct((t, nq), BF16)] * 3
        + [jax.ShapeDtypeStruct((t, H_B * cc), BF16), jax.ShapeDtypeStruct((bn, nc, SUBLANES, nq), F32)],
        scratch_shapes=[pltpu.VMEM((rows + SUBLANES, B_QKV), F32)],
        compiler_params=_cp("parallel", "parallel"),
        name="gdn_prep",
    )(qkvb, qkvb, small, small_t, conv_w, pad_col(a_log), pad_col(dt_bias), tri)
    seq = lambda n: pl.BlockSpec((bn, cc, n), lambda c: (0, c, 0))
    r3 = lambda a: a.reshape(bn, l, a.shape[1])
    o, s_new = pl.pallas_call(
        _gdn_scan_body,
        grid=(nc,),
        in_specs=[seq(nq), seq(nq), seq(nq), seq(nq), seq(H_B * cc),
                  pl.BlockSpec((bn, 1, SUBLANES, nq), lambda c: (0, c, 0, 0)), seq(B_Z), _whole((1, DV_B))],
        out_specs=[seq(B_Z), pl.BlockSpec((bn, H_B, DK_B, DV_B), lambda c: (0, 0, 0, 0))],
        out_shape=[jax.ShapeDtypeStruct((bn, l, B_Z), BF16), jax.ShapeDtypeStruct((bn, H_B, DK_B, DV_B), F32)],
        scratch_shapes=[pltpu.VMEM((bn, H_B, DK_B, DV_B), F32)],
        compiler_params=_cp("arbitrary"),
        name="gdn_scan",
    )(r3(u), r3(w), r3(qg), r3(kd), r3(aqk), gl, r3(z), gnorm.reshape(1, DV_B))
    return o.reshape(t, B_Z), s_new


def _conv_step(x, hist_ref, cw_ref, new_ref):
    out = hist_ref[0] * cw_ref[0:1, :]
    for i in range(1, CONV_W - 1):
        out = out + hist_ref[i] * cw_ref[i:i + 1, :]
    out = out + x * cw_ref[CONV_W - 1:CONV_W, :]
    for i in range(CONV_W - 2):
        new_ref[i] = hist_ref[i + 1]
    new_ref[CONV_W - 2] = x
    return out


def _gdn_decode_pre_body(qkv_ref, hist_ref, sm_ref, cw_ref, arow_ref, dtrow_ref,
                         new_ref, w_ref, qg_ref, k_ref, u_ref, qk_ref, gl_ref):
    conv = _silu(_conv_step(qkv_ref[...], hist_ref, cw_ref, new_ref))
    nq = H_B * DK_B
    sm = sm_ref[...]
    beta_all = jax.nn.sigmoid(sm)
    g_all = -jnp.exp(arow_ref[...]) * _softplus(sm + dtrow_ref[...])
    for h in range(H_B):
        hs = slice(h * DK_B, (h + 1) * DK_B)
        qh = _l2norm(conv[:, hs]) * (DK_B ** -0.5)
        kh = _l2norm(conv[:, nq + h * DK_B:nq + (h + 1) * DK_B])
        vh = conv[:, 2 * nq + h * DV_B:2 * nq + (h + 1) * DV_B]
        beta = beta_all[:, h:h + 1]
        eg = jnp.exp(g_all[:, H_B + h:H_B + h + 1])
        w_ref[:, hs] = kh * beta * eg
        qg_ref[:, hs] = qh * eg
        k_ref[:, hs] = kh
        u_ref[:, hs] = vh * beta
        qk_ref[:, hs] = jnp.broadcast_to(jnp.sum(qh * kh, axis=-1, keepdims=True), qh.shape)
        gl_ref[:, hs] = jnp.broadcast_to(eg, qh.shape)


def _first_or_aliased(first, compute, s_out_ref):
    if not first:
        compute()
        return
    slot = pl.program_id(0)
    pl.when(slot == 0)(compute)

    @pl.when(slot != 0)
    def _():
        s_out_ref[...] = jnp.zeros_like(s_out_ref)


def _gdn_decode_state_body(s_ref, w_ref, qg_ref, k_ref, u_ref, qk_ref, gl_ref, z_ref, gn_ref, *rest, first):
    s_out_ref, o_ref = rest[-2:]
    _first_or_aliased(first, functools.partial(
        _gdn_decode_state_compute, s_ref, w_ref, qg_ref, k_ref, u_ref, qk_ref, gl_ref, z_ref, gn_ref,
        s_out_ref, o_ref), s_out_ref)


def _gdn_decode_state_compute(s_ref, w_ref, qg_ref, k_ref, u_ref, qk_ref, gl_ref, z_ref, gn_ref, s_out_ref, o_ref):
    sub = SUBLANES
    row = lax.broadcasted_iota(jnp.int32, (sub, 1), 0)
    for r0 in range(0, s_ref.shape[0], sub):
        rows = slice(r0, r0 + sub)
        for h in range(H_B):
            hs = slice(h * DK_B, (h + 1) * DK_B)
            wq = jnp.concatenate([w_ref[rows, hs], qg_ref[rows, hs]], axis=0).astype(BF16)
            ws = jnp.zeros((sub, DV_B), F32)
            qs = jnp.zeros((sub, DV_B), F32)
            for bb in range(sub):
                res = _dot(wq, s_ref[r0 + bb, h].astype(BF16))
                ws = jnp.where(row == bb, res[:sub], ws)
                qs = jnp.where(row == bb, res[sub:], qs)
            v_new = u_ref[rows, hs] - ws
            o = qs + qk_ref[rows, hs] * v_new
            o_ref[rows, hs] = _rms(o, gn_ref[...]) * _silu(z_ref[rows, hs])
            k = k_ref[rows, hs]
            v16 = v_new.astype(BF16)
            gl = gl_ref[rows, hs]
            for bb in range(sub):
                k_one = jnp.where(row == bb, k, 0.0).astype(BF16)
                s_out_ref[r0 + bb, h] = s_ref[r0 + bb, h] * gl[bb:bb + 1, :] + _dot_tn(k_one, v16)


def _gdn_decode(qkvb, hist, z, small, conv_w, a_log, dt_bias, gnorm, s_all, s_done, e):
    bn = qkvb.shape[0]
    pad_row = lambda v: jnp.zeros((1, LANES), F32).at[0, H_B:2 * H_B].set(v)
    wide = jax.ShapeDtypeStruct((bn, H_B * DK_B), F32)
    new_conv, w, qg, k, u, qk, gl = pl.pallas_call(
        _gdn_decode_pre_body,
        out_shape=[jax.ShapeDtypeStruct(hist.shape, F32)] + [wide] * 6,
        compiler_params=pltpu.CompilerParams(vmem_limit_bytes=VMEM_LIMIT_BYTES),
        name="gdn_decode_pre",
    )(qkvb, hist, small, conv_w, pad_row(a_log), pad_row(dt_bias))
    nb = GDN_STATE_SEQS
    steps = bn // nb
    first = s_done is None
    blk = (None, nb, H_B, DK_B, DV_B)
    args = [s_all, w, qg, k, u, qk, gl, z, gnorm.reshape(1, DV_B)]
    if first:
        assert e == 0
        seq = lambda s, i: jnp.where(s == 0, i, steps - 1)
        vec = pl.BlockSpec((nb, H_B * DK_B), lambda s, i: (seq(s, i), 0))
        in_specs = [pl.BlockSpec(blk, lambda s, i: (0, seq(s, i), 0, 0, 0))] + [vec] * 7
        in_specs.append(pl.BlockSpec((1, DV_B), lambda s, i: (0, 0)))
        out_specs = [pl.BlockSpec(blk, lambda s, i: (s, i, 0, 0, 0)), vec]
        grid, aliases, sem = (s_all.shape[0], steps), {}, ("arbitrary", "arbitrary")
    else:
        vec = pl.BlockSpec((nb, H_B * DK_B), lambda i: (i, 0))
        st = pl.BlockSpec(blk, lambda i: (e, i, 0, 0, 0))
        in_specs = [st] + [vec] * 7 + [_whole((1, DV_B)), pl.BlockSpec(memory_space=pl.ANY)]
        out_specs = [st, vec]
        args.append(s_done)
        grid, aliases, sem = (steps,), {len(args) - 1: 0}, ("parallel",)
    s_done, o = pl.pallas_call(
        functools.partial(_gdn_decode_state_body, first=first),
        grid=grid,
        in_specs=in_specs,
        out_specs=out_specs,
        out_shape=[jax.ShapeDtypeStruct(s_all.shape, F32), wide],
        input_output_aliases=aliases,
        compiler_params=_cp(*sem),
        name="gdn_decode_state",
    )(*args)
    return o, new_conv, s_done


def _head_expander(width):
    e = np.zeros((LANES, H_C * width), np.float32)
    for h in range(H_C):
        e[h, h * width:(h + 1) * width] = 1.0
    return jnp.asarray(np.tile(e, (3, 1)), BF16)


def _group_rms(y, g):
    gs = D_INNER // G_C
    parts = [_rms(y[:, i * gs:(i + 1) * gs], g[:, i * gs:(i + 1) * gs]) for i in range(G_C)]
    return jnp.concatenate(parts, axis=1)


def _ssd_prompt_body(z_ref, xbc_ref, sm_ref, smt_ref, cw_ref, cb_ref, dtrow_ref, arow_ref, dtcol_ref,
                     acol_ref, dx_ref, gn_ref, e_ref, o_ref, s_out_ref, xbuf_ref, s_ref):
    c = pl.program_id(1)
    rows = xbc_ref.shape[0]

    @pl.when(c == 0)
    def _():
        xbuf_ref[0:SUBLANES, :] = jnp.zeros((SUBLANES, xbuf_ref.shape[1]), F32)
        s_ref[...] = jnp.zeros_like(s_ref)

    xbc = _conv_chunk(xbc_ref, xbuf_ref, cw_ref, rows, cb_ref)
    xs = xbc[:, :D_INNER]
    bm = xbc[:, D_INNER:D_INNER + G_C * N_C]
    cm = xbc[:, D_INNER + G_C * N_C:]
    e = e_ref[...]
    lower = _tri(rows, "lower")
    dt = _softplus(sm_ref[...] + dtrow_ref[...])
    acum = _dot_hi(lower.astype(F32), dt * -jnp.exp(arow_ref[...]))
    a_t = _softplus(smt_ref[...] + dtcol_ref[...]) * -jnp.exp(acol_ref[...])
    acum_t = _dot_hi(a_t, _tri(rows, "upper").astype(F32))
    xdt = xs * _expand(dt, e)
    xdte = (xdt * _expand(jnp.exp(acum[rows - 1:rows, :] - acum), e)).astype(BF16)
    scale_y = _expand(jnp.exp(acum), e)
    chunk_decay = scale_y[rows - 1:rows, :]
    xdt16 = xdt.astype(BF16)
    neg_upper = jnp.where(lower, 0.0, -jnp.inf)
    acum2, acum2_t = acum * LOG2E, acum_t * LOG2E
    gw = HPG * P_C
    ys = []
    for g in range(G_C):
        bg = bm[:, g * N_C:(g + 1) * N_C]
        cg16 = cm[:, g * N_C:(g + 1) * N_C].astype(BF16)
        cb = _dot_nt(cg16, bg.astype(BF16))
        yg = []
        for hh in range(HPG):
            h = g * HPG + hh
            lmat = jnp.exp2(acum2[:, h:h + 1] - acum2_t[h:h + 1, :] + neg_upper)
            yg.append(_dot((cb * lmat).astype(BF16), xdt16[:, h * P_C:(h + 1) * P_C]))
        gs = slice(g * gw, (g + 1) * gw)
        sg = s_ref[:, gs]
        y_off = _dot(cg16, sg.astype(BF16)) * scale_y[:, gs]
        ys.append(jnp.concatenate(yg, axis=1) + y_off)
        s_ref[:, gs] = sg * chunk_decay[:, gs] + _dot(bg.T.astype(BF16), xdte[:, gs])
    y = jnp.concatenate(ys, axis=1) + dx_ref[...] * xs
    y = y * _silu(z_ref[...])
    o_ref[...] = _group_rms(y, gn_ref[...]).astype(BF16)

    @pl.when(c == pl.num_programs(1) - 1)
    def _():
        s_out_ref[0] = s_ref[...].T.reshape(H_C, P_C, N_C)


def _ssd_small_params(dt_bias, a_log, ns):
    row = lambda v: jnp.zeros((1, LANES), F32).at[0, :H_C].set(v)
    col = lambda v: jnp.zeros((ns, 1), F32).at[:H_C, 0].set(v)
    return row(dt_bias), row(a_log), col(dt_bias), col(a_log)


def _ssd_prompt(z, xbc, small, small_t, conv_w, conv_b, dt_bias, a_log, d_skip, gnorm, bn, l):
    rows = SSD_CHUNK
    nc = l // rows
    t = bn * l
    ns = small_t.shape[0]
    dtrow, arow, dtcol, acol = _ssd_small_params(dt_bias, a_log, ns)
    tok = lambda n: pl.BlockSpec((rows, n), lambda b, c: (b * nc + c, 0))
    return pl.pallas_call(
        _ssd_prompt_body,
        grid=(bn, nc),
        in_specs=[
            tok(D_INNER), tok(SSD_CONV_CH), tok(LANES),
            pl.BlockSpec((ns, rows), lambda b, c: (0, b * nc + c)),
            _whole((CONV_W, SSD_CONV_CH)), _whole((1, SSD_CONV_CH)),
            _whole((1, LANES)), _whole((1, LANES)), _whole((ns, 1)), _whole((ns, 1)),
            _whole((1, D_INNER)), _whole((1, D_INNER)), _whole((3 * LANES, D_INNER)),
        ],
        out_specs=[tok(D_INNER), pl.BlockSpec((1, H_C, P_C, N_C), lambda b, c: (b, 0, 0, 0))],
        out_shape=[jax.ShapeDtypeStruct((t, D_INNER), BF16), jax.ShapeDtypeStruct((bn, H_C, P_C, N_C), F32)],
        scratch_shapes=[pltpu.VMEM((rows + SUBLANES, SSD_CONV_CH), F32), pltpu.VMEM((N_C, D_INNER), F32)],
        compiler_params=_cp("arbitrary", "arbitrary"),
        name="ssd_prompt",
    )(z, xbc, small, small_t, conv_w, conv_b.reshape(1, SSD_CONV_CH), dtrow, arow, dtcol, acol,
      jnp.repeat(d_skip, P_C).reshape(1, D_INNER), gnorm.reshape(1, D_INNER), _head_expander(P_C))


def _ssd_decode_pre_body(xbc_ref, hist_ref, sm_ref, cw_ref, cb_ref, dtrow_ref, arow_ref, e_ref, en_ref,
                         new_ref, xs_ref, xdt_ref, b_ref, c_ref, dax_ref, dan_ref):
    xbc = _silu(_conv_step(xbc_ref[...], hist_ref, cw_ref, new_ref) + cb_ref[...])
    xs = xbc[:, :D_INNER]
    dt = _softplus(sm_ref[...] + dtrow_ref[...])
    a = dt * -jnp.exp(arow_ref[...])
    xs_ref[...] = xs
    xdt_ref[...] = xs * _expand(dt, e_ref[...])
    b_ref[...] = xbc[:, D_INNER:D_INNER + G_C * N_C]
    c_ref[...] = xbc[:, D_INNER + G_C * N_C:]
    dax_ref[...] = jnp.exp(_expand(a, e_ref[...]))
    dan_ref[...] = jnp.exp(_expand(a, en_ref[...]))


def _ssd_decode_state_body(s_ref, xdt_ref, b_ref, c_ref, dan_ref, *rest, first):
    s_out_ref, yoff_ref = rest[-2:]
    _first_or_aliased(first, functools.partial(
        _ssd_decode_state_compute, s_ref, xdt_ref, b_ref, c_ref, dan_ref, s_out_ref, yoff_ref), s_out_ref)


def _ssd_decode_state_compute(s_ref, xdt_ref, b_ref, c_ref, dan_ref, s_out_ref, yoff_ref):
    gw = HPG * P_C
    sub = SUBLANES
    row = lax.broadcasted_iota(jnp.int32, (sub, 1), 0)
    for r0 in range(0, s_ref.shape[0], sub):
        rows = slice(r0, r0 + sub)
        xdt = xdt_ref[rows, :]
        b16 = b_ref[rows, :].astype(BF16)
        c16 = c_ref[rows, :].astype(BF16)
        dan = dan_ref[rows, :]
        yoff = jnp.zeros((sub, gw), F32)
        for bb in range(sub):
            s = s_ref[r0 + bb].reshape(gw, N_C)
            res = _dot_nt(c16, s.astype(BF16))
            yoff = jnp.where(row == bb, res, yoff)
            x_one = jnp.where(row == bb, xdt, 0.0).astype(BF16)
            upd = _dot_tn(x_one, b16)
            for hh in range(HPG):
                rs = slice(hh * P_C, (hh + 1) * P_C)
                s_out_ref[r0 + bb, hh] = s[rs] * dan[bb:bb + 1, hh * N_C:(hh + 1) * N_C] + upd[rs]
        yoff_ref[rows, :] = yoff


def _ssd_decode_post_body(yoff_ref, dax_ref, xdt_ref, xs_ref, b_ref, c_ref, z_ref, dx_ref, gn_ref, o_ref):
    gw = HPG * P_C
    bc = b_ref[...] * c_ref[...]
    cbx = [jnp.broadcast_to(jnp.sum(bc[:, g * N_C:(g + 1) * N_C], axis=-1, keepdims=True), (bc.shape[0], gw))
           for g in range(G_C)]
    y = yoff_ref[...] * dax_ref[...] + jnp.concatenate(cbx, axis=1) * xdt_ref[...]
    y = y + dx_ref[...] * xs_ref[...]
    y = y * _silu(z_ref[...])
    o_ref[...] = _group_rms(y, gn_ref[...])


def _ssd_decode(z, xbc, hist, small, conv_w, conv_b, dt_bias, a_log, d_skip, gnorm, s_all, s_done, e):
    bn = xbc.shape[0]
    dtrow, arow, _, _ = _ssd_small_params(dt_bias, a_log, H_C)
    wide = jax.ShapeDtypeStruct((bn, D_INNER), F32)
    grp = jax.ShapeDtypeStruct((bn, G_C * N_C), F32)
    plain = pltpu.CompilerParams(vmem_limit_bytes=VMEM_LIMIT_BYTES)
    new_conv, xs, xdt, bm, cm, dax, dan = pl.pallas_call(
        _ssd_decode_pre_body,
        out_shape=[jax.ShapeDtypeStruct(hist.shape, F32), wide, wide, grp, grp, wide,
                   jax.ShapeDtypeStruct((bn, H_C * N_C), F32)],
        compiler_params=plain,
        name="ssd_decode_pre",
    )(xbc, hist, small, conv_w, conv_b.reshape(1, SSD_CONV_CH), dtrow, arow,
      _head_expander(P_C), _head_expander(N_C))
    nb = SSD_STATE_SEQS
    gw = HPG * P_C
    steps = bn // nb
    first = s_done is None
    blk = (None, nb, HPG, P_C, N_C)
    widths = (gw, N_C, N_C, HPG * N_C)
    args = [s_all, xdt, bm, cm, dan]
    if first:
        assert e == 0
        seq = lambda s, i: jnp.where(s == 0, i, steps - 1)
        grp_of = lambda s, g: jnp.where(s == 0, g, G_C - 1)
        vec = lambda n: pl.BlockSpec((nb, n), lambda s, i, g: (seq(s, i), grp_of(s, g)))
        in_specs = [pl.BlockSpec(blk, lambda s, i, g: (0, seq(s, i), grp_of(s, g), 0, 0))] + [vec(n) for n in widths]
        out_specs = [pl.BlockSpec(blk, lambda s, i, g: (s, i, g, 0, 0)), vec(gw)]
        grid, aliases, sem = (s_all.shape[0], steps, G_C), {}, ("arbitrary",) * 3
    else:
        vec = lambda n: pl.BlockSpec((nb, n), lambda i, g: (i, g))
        st = pl.BlockSpec(blk, lambda i, g: (e, i, g, 0, 0))
        in_specs = [st] + [vec(n) for n in widths] + [pl.BlockSpec(memory_space=pl.ANY)]
        out_specs = [st, vec(gw)]
        args.append(s_done)
        grid, aliases, sem = (steps, G_C), {len(args) - 1: 0}, ("parallel", "parallel")
    s_done, yoff = pl.pallas_call(
        functools.partial(_ssd_decode_state_body, first=first),
        grid=grid,
        in_specs=in_specs,
        out_specs=out_specs,
        out_shape=[jax.ShapeDtypeStruct(s_all.shape, F32), wide],
        input_output_aliases=aliases,
        compiler_params=_cp(*sem),
        name="ssd_decode_state",
    )(*args)
    mix = pl.pallas_call(
        _ssd_decode_post_body,
        out_shape=wide,
        compiler_params=plain,
        name="ssd_decode_post",
    )(yoff, dax, xdt, xs, bm, cm, z, jnp.repeat(d_skip, P_C).reshape(1, D_INNER), gnorm.reshape(1, D_INNER))
    return mix, new_conv, s_done


def _narrow_weights(w_small):
    n = w_small.shape[1]
    return jnp.zeros((D_MODEL, LANES), BF16).at[:, :n].set(w_small.astype(BF16))


def _even_q_pad(qa):
    bn = qa.shape[0]
    q = qa.reshape(bn, KV_A, G_A, HD_A)
    out = jnp.zeros((bn, KV_A, G_A, KV_A, HD_A), F32)
    for kv in range(KV_A):
        out = out.at[:, kv, :, kv, :].set(q[:, kv])
    return out.reshape(bn, H_A, KV_A * HD_A)


def _even_o_unpad(o8):
    bn = o8.shape[0]
    o = o8.reshape(bn, KV_A, G_A, KV_A, HD_A)
    return jnp.concatenate([o[:, kv, :, kv, :].reshape(bn, G_A * HD_A) for kv in range(KV_A)], axis=1)


def _trunk(x3, states, wts):
    (rel_bias, norm_ff1, norm_mix, norm_ff2, norm_final,
     ff1, ff2, even_w_in, even_w_out, swa_sinks, gdn_conv_w, gdn_A_log, gdn_dt_bias, gdn_norm,
     ssd_w_in, ssd_w_out, ssd_conv_w, ssd_conv_b, ssd_dt_bias, ssd_A_log, ssd_D, ssd_norm) = wts
    bn, l, d = x3.shape
    t = bn * l
    x = x3.reshape(t, d)
    decode = states is not None
    ks, vs, gconv, gssm, sconv, sssm = [], [], [], [], [], []
    depth = norm_ff1.shape[0]
    width = KV_A * HD_A
    if decode:
        gssm_done = sssm_done = None
        n_even = states[0].shape[0]
        cache_t = lambda c: c.transpose(0, 1, 3, 4, 2).reshape(n_even, bn, width, WINDOW)
        ckt, cvt = cache_t(states[0]), cache_t(states[1])
        ghist, shist = states[2].transpose(0, 2, 1, 3), states[4].transpose(0, 2, 1, 3)
    for layer in range(depth):
        x = _ffn(x, norm_ff1[layer], ff1, layer)
        e = layer // 2
        if layer % 2 == 0:
            w_all, ws = even_w_in
            qa, kv, qkvb, z, small, small_t = _inproj(x, norm_mix[layer], w_all, e, ws[e], 2 * SUBLANES,
                                                      (A_Q, 2 * width, B_QKV, B_Z))
            if decode:
                o8 = _swa_decode(_even_q_pad(qa), kv, ckt, cvt, e, rel_bias, swa_sinks[e])
                o_a = _even_o_unpad(o8)
                o_b, new_conv, gssm_done = _gdn_decode(qkvb, ghist[e], z, small, gdn_conv_w[e], gdn_A_log[e],
                                                       gdn_dt_bias[e], gdn_norm[e], states[3], gssm_done, e)
                new_k, new_v = kv[:, :width], kv[:, width:]
                new_conv = new_conv.transpose(1, 0, 2)
            else:
                o_a = _swa_prompt(qa, kv, rel_bias, swa_sinks[e], bn, l)
                o_b, s_new = _gdn_prompt(qkvb, z, small, small_t, gdn_conv_w[e], gdn_A_log[e],
                                         gdn_dt_bias[e], gdn_norm[e], bn, l)
                kv3 = kv.reshape(bn, l, 2 * width)
                new_k = kv3[:, l - WINDOW:, :width]
                new_v = kv3[:, l - WINDOW:, width:]
                new_conv = qkvb.reshape(bn, l, B_QKV)[:, l - (CONV_W - 1):]
                gssm.append(s_new)
            ks.append(new_k)
            vs.append(new_v)
            gconv.append(new_conv)
            mixes = [(o_a, even_w_out, e, 0), (o_b, even_w_out, e, 1)]
        else:
            w_all, ws = ssd_w_in
            z, xbc, small, small_t = _inproj(x, norm_mix[layer], w_all, e, ws[e], H_C, (D_INNER, SSD_CONV_CH))
            if decode:
                mix, new_conv, sssm_done = _ssd_decode(z, xbc, shist[e], small, ssd_conv_w[e], ssd_conv_b[e],
                                                       ssd_dt_bias[e], ssd_A_log[e], ssd_D[e], ssd_norm[e],
                                                       states[5], sssm_done, e)
                new_conv = new_conv.transpose(1, 0, 2)
            else:
                mix, s_new = _ssd_prompt(z, xbc, small, small_t, ssd_conv_w[e], ssd_conv_b[e], ssd_dt_bias[e],
                                         ssd_A_log[e], ssd_D[e], ssd_norm[e], bn, l)
                new_conv = xbc.reshape(bn, l, SSD_CONV_CH)[:, l - (CONV_W - 1):]
                sssm.append(s_new)
            sconv.append(new_conv)
            mixes = [(mix, ssd_w_out, e, 0)]
        x = _ffn(x, norm_ff2[layer], ff2, layer, mixes=mixes,
                 g_final=norm_final if layer == depth - 1 else None)
    if decode:
        bs = DECODE_SEQ_BLOCK
        new_t = lambda rows: jnp.stack(rows).reshape(n_even, bn // bs, bs, width).transpose(0, 1, 3, 2)
        back = lambda c: c.reshape(n_even, bn, KV_A, HD_A, WINDOW).transpose(0, 1, 4, 2, 3)
        ks = back(_cache_shift(ckt, new_t(ks)))
        vs = back(_cache_shift(cvt, new_t(vs)))
        gssm, sssm = gssm_done, sssm_done
    else:
        kv5 = (len(ks), bn, WINDOW, KV_A, HD_A)
        ks, vs = jnp.stack(ks).reshape(kv5), jnp.stack(vs).reshape(kv5)
        gssm, sssm = jnp.stack(gssm), jnp.stack(sssm)
    return (x.reshape(bn, l, d), ks, vs, jnp.stack(gconv), gssm, jnp.stack(sconv), sssm)


def kernel(x_prompt, x_sample, cache_swa_k, cache_swa_v, state_gdn_conv, state_gdn_ssm, state_ssd_conv, state_ssd_ssm, rel_bias, norm_ff1, norm_mix, norm_ff2, norm_final, ff1_gate, ff1_up, ff1_down, ff2_gate, ff2_up, ff2_down, even_w_in, even_w_out, swa_sinks, gdn_conv_w, gdn_A_log, gdn_dt_bias, gdn_norm, ssd_w_in, ssd_w_out, ssd_conv_w, ssd_conv_b, ssd_dt_bias, ssd_A_log, ssd_D, ssd_norm):
    depth = norm_ff1.shape[0]
    ff1 = (ff1_gate.astype(BF16), ff1_up.astype(BF16), ff1_down.astype(BF16))
    ff2 = (ff2_gate.astype(BF16), ff2_up.astype(BF16), ff2_down.astype(BF16))
    n_even_main = A_Q + 2 * KV_A * HD_A + B_QKV + B_Z
    even_in = (even_w_in.astype(BF16),
               [_narrow_weights(even_w_in[e][:, n_even_main:]) for e in range(even_w_in.shape[0])])
    n_odd_main = D_INNER + SSD_CONV_CH
    odd_in = (ssd_w_in.astype(BF16),
              [_narrow_weights(ssd_w_in[e][:, n_odd_main:]) for e in range(ssd_w_in.shape[0])])
    wts = (rel_bias, norm_ff1, norm_mix, norm_ff2, norm_final, ff1, ff2,
           even_in, even_w_out.astype(BF16), swa_sinks, gdn_conv_w, gdn_A_log, gdn_dt_bias, gdn_norm,
           odd_in, ssd_w_out.astype(BF16), ssd_conv_w, ssd_conv_b, ssd_dt_bias, ssd_A_log, ssd_D, ssd_norm)
    y_p, p_k, p_v, p_gconv, p_gssm, p_sconv, p_sssm = _trunk(x_prompt, None, wts)
    states = (cache_swa_k, cache_swa_v, state_gdn_conv, state_gdn_ssm, state_ssd_conv, state_ssd_ssm)
    y_s, s_k, s_v, s_gconv, s_gssm, s_sconv, s_sssm = _trunk(x_sample, states, wts)
    return (y_p, y_s, p_k, p_v, p_gconv, p_gssm, p_sconv, p_sssm,
            s_k, s_v, s_gconv, s_gssm, s_sconv, s_sssm)
```
